```python
import jax
import jax.numpy as jnp
from jax import lax
import numpy as np

D_MODEL = 1024
BATCH = 2
SEQ = 16384
DEPTH = 1

FOX_HEADS = 8
FOX_HEAD_DIM = 64
FOX_WIDTH = FOX_HEADS * FOX_HEAD_DIM
Q_BLOCK = 128
RWKV_HEADS = 8
RWKV_HEAD_DIM = 64
RWKV_WIDTH = RWKV_HEADS * RWKV_HEAD_DIM
DECAY_LORA = 64
AAA_LORA = 64
GATE_LORA = 128
LN_X_EPS = 64e-5
N_BRANCHES = 2
SHIFT_WIDTH = 3 * RWKV_WIDTH + DECAY_LORA + AAA_LORA + GATE_LORA
IN_SPLITS = [FOX_WIDTH, 2 * FOX_WIDTH, 3 * FOX_WIDTH, 3 * FOX_WIDTH + FOX_HEADS,
             3 * FOX_WIDTH + FOX_HEADS + SHIFT_WIDTH]
D_IN = 3 * FOX_WIDTH + FOX_HEADS + SHIFT_WIDTH + N_BRANCHES * D_MODEL
RWKV_SPLITS = [RWKV_WIDTH, 2 * RWKV_WIDTH, 3 * RWKV_WIDTH, 3 * RWKV_WIDTH + DECAY_LORA,
               3 * RWKV_WIDTH + DECAY_LORA + AAA_LORA]
N_GROUPS = 4
EXPERTS_PER_GROUP = 8
N_EXPERTS = N_GROUPS * EXPERTS_PER_GROUP
TOP_K = 2
D_EXPERT = 512
MOE_BLOCK = 128
NORM_EPS = 1e-6

kernel_name = 'fox_rwkv7_hmoe_adaln_block'


def rms_norm(x, g):
    xf = x.astype(jnp.float32)
    y = xf * lax.rsqrt(jnp.mean(xf * xf, axis=-1, keepdims=True) + NORM_EPS)
    return (y * g.astype(jnp.float32)).astype(x.dtype)


def token_shift(p, mu):
    prev = jnp.pad(p, ((0, 0), (1, 0), (0, 0)))[:, :-1]
    return p + (prev - p) * mu


def fox_attention(q, k, v, log_f):
    b, s, _ = q.shape
    nqb = s // Q_BLOCK
    scale = FOX_HEAD_DIM ** -0.5
    qh = q.reshape(b, nqb, Q_BLOCK, FOX_HEADS, FOX_HEAD_DIM).transpose(1, 0, 3, 2, 4)
    kh = k.reshape(b, s, FOX_HEADS, FOX_HEAD_DIM).transpose(0, 2, 1, 3)
    vh = v.reshape(b, s, FOX_HEADS, FOX_HEAD_DIM).transpose(0, 2, 1, 3)
    cum = jnp.cumsum(log_f, axis=1).transpose(0, 2, 1)
    cum_q = cum.reshape(b, FOX_HEADS, nqb, Q_BLOCK).transpose(2, 0, 1, 3)
    k_pos = jnp.arange(s)

    def block(args):
        q_blk, c_blk, i = args
        logits = jnp.einsum('bhqd,bhkd->bhqk', q_blk, kh).astype(jnp.float32) * scale
        logits = logits + c_blk[..., None] - cum[:, :, None, :]
        q_pos = i * Q_BLOCK + jnp.arange(Q_BLOCK)
        logits = jnp.where(k_pos[None, :] <= q_pos[:, None], logits, -jnp.inf)
        probs = jax.nn.softmax(logits, axis=-1).astype(vh.dtype)
        return jnp.einsum('bhqk,bhkd->bhqd', probs, vh)

    o = lax.map(block, (qh, cum_q, jnp.arange(nqb)))
    return o.transpose(1, 0, 3, 2, 4).reshape(b, s, FOX_WIDTH)


def rwkv7_scan(r, w, k, v, a_vec, b_vec):
    b, s, h, n = r.shape
    xs = tuple(t.astype(jnp.float32).transpose(1, 0, 2, 3) for t in (r, w, k, v, a_vec, b_vec))
    state0 = jnp.zeros((b, h, n, n), jnp.float32)

    def step(state, inp):
        r_t, w_t, k_t, v_t, a_t, b_t = inp
        sa = jnp.einsum('bhij,bhj->bhi', state, a_t)
        state = (state * w_t[:, :, None, :] + sa[..., None] * b_t[:, :, None, :]
                 + v_t[..., None] * k_t[:, :, None, :])
        return state, jnp.einsum('bhij,bhj->bhi', state, r_t)

    _, y = lax.scan(step, state0, xs)
    return y.transpose(1, 0, 2, 3)


def rwkv7_branch(p, w0, w2, a0, a2, g2, k_k, k_a, r_k, ln_g, ln_b):
    b, s, _ = p.shape
    r, k, v, wd, ad, gd = jnp.split(p, RWKV_SPLITS, axis=-1)
    log_w = -jax.nn.softplus(-(w0 + jnp.tanh(wd) @ w2).astype(jnp.float32)) - 0.5
    decay = jnp.exp(-jnp.exp(log_w))
    a = jax.nn.sigmoid(a0 + ad @ a2)
    g = jax.nn.sigmoid(gd) @ g2
    heads = lambda t: t.reshape(b, s, RWKV_HEADS, RWKV_HEAD_DIM)
    kk = heads(k * k_k).astype(jnp.float32)
    kk = kk / jnp.maximum(jnp.sqrt(jnp.sum(kk * kk, axis=-1, keepdims=True)), 1e-12)
    k = k * (1.0 + (a - 1.0) * k_a)
    rh, kh, vh, ah = heads(r), heads(k), heads(v), heads(a)
    y = rwkv7_scan(rh, heads(decay), kh, vh, -kk, kk * ah)
    mu = jnp.mean(y, axis=-1, keepdims=True)
    var = jnp.mean(jnp.square(y - mu), axis=-1, keepdims=True)
    yn = ((y - mu) * lax.rsqrt(var + LN_X_EPS)).reshape(b, s, RWKV_WIDTH) * ln_g + ln_b
    bonus = jnp.sum((rh * kh * r_k).astype(jnp.float32), axis=-1, keepdims=True) * vh.astype(jnp.float32)
    out = (yn + bonus.reshape(b, s, RWKV_WIDTH)) * g
    return out.astype(p.dtype)


def hier_moe(h, rg_w, rg_b, re_w, re_b, w1, w3, w2):
    t, d = h.shape
    hf = h.astype(jnp.float32)
    group_prob = jax.nn.softmax(hf @ rg_w.astype(jnp.float32) + rg_b, axis=-1)
    g_p, g_idx = lax.top_k(group_prob, 1)
    exp_logits = (hf @ re_w.astype(jnp.float32) + re_b).reshape(t, N_GROUPS, EXPERTS_PER_GROUP)
    sel = exp_logits[jnp.arange(t), g_idx[:, 0]]
    e_p, e_idx = lax.top_k(jax.nn.softmax(sel, axis=-1), TOP_K)
    weights = g_p * e_p / jnp.sum(e_p, axis=-1, keepdims=True)
    expert_id = g_idx * EXPERTS_PER_GROUP + e_idx

    m = t * TOP_K
    flat_e = expert_id.reshape(m)
    flat_w = weights.reshape(m)
    flat_tok = jnp.arange(m) // TOP_K
    order = jnp.argsort(flat_e)
    e_s, tok_s, w_s = flat_e[order], flat_tok[order], flat_w[order]
    counts = jnp.bincount(flat_e, length=N_EXPERTS)
    padded = (counts + MOE_BLOCK - 1) // MOE_BLOCK * MOE_BLOCK
    seg_start = jnp.cumsum(counts) - counts
    pad_end = jnp.cumsum(padded)
    pad_start = pad_end - padded
    dest = pad_start[e_s] + (jnp.arange(m) - seg_start[e_s])
    n_blocks = -(-m // MOE_BLOCK) + N_EXPERTS
    rows = n_blocks * MOE_BLOCK
    row_tok = jnp.full((rows,), t, jnp.int32).at[dest].set(tok_s.astype(jnp.int32))
    row_w = jnp.zeros((rows,), h.dtype).at[dest].set(w_s.astype(h.dtype))
    blk_e = jnp.minimum(jnp.searchsorted(pad_end, jnp.arange(n_blocks) * MOE_BLOCK, side='right'),
                        N_EXPERTS - 1)
    h_pad = jnp.concatenate([h, jnp.zeros((1, d), h.dtype)], axis=0)
    x_in = h_pad[row_tok].reshape(n_blocks, MOE_BLOCK, d)

    def expert_block(args):
        xb, e = args
        return (jax.nn.silu(xb @ w1[e]) * (xb @ w3[e])) @ w2[e]

    out = lax.map(expert_block, (x_in, blk_e)).reshape(rows, d)
    y = jax.ops.segment_sum(out * row_w[:, None], row_tok, num_segments=t + 1)
    return y[:t]


def setup_inputs(seed: int = 0) -> dict:
    key = jax.random.key(seed)
    ks = jax.random.split(key, 32)
    L, D = DEPTH, D_MODEL

    def nrm(i, shape, scale):
        return scale * jax.random.normal(ks[i], shape, jnp.float32)

    def uni(i, shape, lo, hi):
        return jax.random.uniform(ks[i], shape, jnp.float32, lo, hi)

    return {
        'x': nrm(0, (BATCH, SEQ, D), 1.0),
        'c': nrm(1, (BATCH, D), 1.0),
        'ada_w': nrm(2, (L, D, 6 * D), 0.5 * D ** -0.5),
        'ada_b': nrm(3, (L, 6 * D), 0.02),
        'norm1_g': 1.0 + nrm(4, (L, D), 0.02),
        'w_in': nrm(5, (L, D, D_IN), D ** -0.5),
        'fox_forget_b': uni(6, (L, FOX_HEADS), 1.0, 5.0),
        'shift_mu': uni(7, (L, SHIFT_WIDTH), 0.0, 1.0),
        'rwkv_w0': uni(8, (L, RWKV_WIDTH), -6.0, -1.0),
        'rwkv_w2': nrm(9, (L, DECAY_LORA, RWKV_WIDTH), 0.5 * DECAY_LORA ** -0.5),
        'rwkv_a0': nrm(10, (L, RWKV_WIDTH), 0.1),
        'rwkv_a2': nrm(11, (L, AAA_LORA, RWKV_WIDTH), AAA_LORA ** -0.5),
        'rwkv_g2': nrm(12, (L, GATE_LORA, RWKV_WIDTH), GATE_LORA ** -0.5),
        'rwkv_k_k': 0.85 + nrm(13, (L, RWKV_WIDTH), 0.02),
        'rwkv_k_a': 1.0 + nrm(14, (L, RWKV_WIDTH), 0.02),
        'rwkv_r_k': nrm(15, (L, RWKV_HEADS, RWKV_HEAD_DIM), 0.1),
        'ln_x_g': 1.0 + nrm(16, (L, RWKV_WIDTH), 0.02),
        'ln_x_b': nrm(17, (L, RWKV_WIDTH), 0.02),
        'w_out_fox': nrm(18, (L, FOX_WIDTH, D), FOX_WIDTH ** -0.5),
        'w_out_rwkv': nrm(19, (L, RWKV_WIDTH, D), RWKV_WIDTH ** -0.5),
        'w_o': nrm(20, (L, D, D), D ** -0.5),
        'norm2_g': 1.0 + nrm(21, (L, D), 0.02),
        'router_group_w': nrm(22, (L, D, N_GROUPS), D ** -0.5),
        'router_group_b': nrm(23, (L, N_GROUPS), 0.01),
        'router_expert_w': nrm(24, (L, D, N_EXPERTS), D ** -0.5),
        'router_expert_b': nrm(25, (L, N_EXPERTS), 0.01),
        'exp_w1': nrm(26, (L, N_EXPERTS, D, D_EXPERT), D ** -0.5),
        'exp_w3': nrm(27, (L, N_EXPERTS, D, D_EXPERT), D ** -0.5),
        'exp_w2': nrm(28, (L, N_EXPERTS, D_EXPERT, D), D_EXPERT ** -0.5),
        'final_g': 1.0 + nrm(29, (D,), 0.02),
    }


def reference(x, c, ada_w, ada_b, norm1_g, w_in, fox_forget_b, shift_mu, rwkv_w0, rwkv_w2,
              rwkv_a0, rwkv_a2, rwkv_g2, rwkv_k_k, rwkv_k_a, rwkv_r_k, ln_x_g, ln_x_b,
              w_out_fox, w_out_rwkv, w_o, norm2_g, router_group_w, router_group_b,
              router_expert_w, router_expert_b, exp_w1, exp_w3, exp_w2, final_g):
    b, s, d = x.shape
    for l in range(DEPTH):
        mod = jax.nn.silu(c) @ ada_w[l] + ada_b[l]
        shift1, scale1, gate1, shift2, scale2, gate2 = [m[:, None, :] for m in jnp.split(mod, 6, axis=-1)]

        h = rms_norm(x, norm1_g[l]) * (1.0 + scale1) + shift1
        p = h @ w_in[l]
        q, k, v, f_logit, p_rwkv, gate_logit = jnp.split(p, IN_SPLITS, axis=-1)
        log_f = jax.nn.log_sigmoid((f_logit + fox_forget_b[l]).astype(jnp.float32))
        o_fox = fox_attention(q, k, v, log_f)
        o_rwkv = rwkv7_branch(token_shift(p_rwkv, shift_mu[l]), rwkv_w0[l], rwkv_w2[l], rwkv_a0[l],
                              rwkv_a2[l], rwkv_g2[l], rwkv_k_k[l], rwkv_k_a[l], rwkv_r_k[l],
                              ln_x_g[l], ln_x_b[l])
        g_fox, g_rwkv = jnp.split(jax.nn.sigmoid(gate_logit), N_BRANCHES, axis=-1)
        merged = g_fox * (o_fox @ w_out_fox[l]) + g_rwkv * (o_rwkv @ w_out_rwkv[l])
        x = x + gate1 * (merged @ w_o[l])

        h2 = rms_norm(x, norm2_g[l]) * (1.0 + scale2) + shift2
        y = hier_moe(h2.reshape(b * s, d), router_group_w[l], router_group_b[l], router_expert_w[l],
                     router_expert_b[l], exp_w1[l], exp_w3[l], exp_w2[l])
        x = x + gate2 * y.reshape(b, s, d)
    return rms_norm(x, final_g)
```

```python
import functools

import jax
import jax.numpy as jnp
from jax import lax
from jax.experimental import pallas as pl
from jax.experimental.pallas import tpu as pltpu

F32 = jnp.float32
BF16 = jnp.bfloat16
HIGHEST = lax.Precision.HIGHEST

HEADS = 8
HEAD_DIM = 64
WIDTH = HEADS * HEAD_DIM
DECAY_LORA = 64
AAA_LORA = 64
GATE_LORA = 128
SHIFT_WIDTH = 3 * WIDTH + DECAY_LORA + AAA_LORA + GATE_LORA
LN_X_EPS = 64e-5
NORM_EPS = 1e-6
N_GROUPS = 4
EXPERTS_PER_GROUP = 8
N_EXPERTS = N_GROUPS * EXPERTS_PER_GROUP
TOP_K = 2

LANES = 128
CHUNK = 64
MOE_ROWS = 256
VMEM_LIMIT = 48 * 1024 * 1024


def _cparams(sem):
    return pltpu.CompilerParams(dimension_semantics=sem, vmem_limit_bytes=VMEM_LIMIT)


def _dot(a, b):
    return jnp.dot(a.astype(BF16), b.astype(BF16), preferred_element_type=F32)


def _dot_nt(a, b):
    return lax.dot_general(a.astype(BF16), b.astype(BF16), (((1,), (1,)), ((), ())),
                           preferred_element_type=F32)


def _dot_tn(a, b):
    return lax.dot_general(a.astype(BF16), b.astype(BF16), (((0,), (0,)), ((), ())),
                           preferred_element_type=F32)


def _split_dot(x, w_bf16):
    hi = x.astype(BF16)
    lo = (x - hi.astype(F32)).astype(BF16)
    return (jnp.dot(hi, w_bf16, preferred_element_type=F32)
            + jnp.dot(lo, w_bf16, preferred_element_type=F32))


def _softplus(x):
    return jnp.maximum(x, 0.0) + jnp.log1p(jnp.exp(-jnp.abs(x)))


def _mod_kernel(c_ref, w_ref, b_ref, o_ref):
    c = c_ref[...]
    sc = c * jax.nn.sigmoid(c)
    o_ref[...] = jnp.dot(sc, w_ref[...], precision=HIGHEST, preferred_element_type=F32) + b_ref[...]


def _adaln_mod(c, ada_w, ada_b):
    b, d = c.shape
    n = ada_w.shape[1]
    rows = 8
    cp = jnp.zeros((rows, d), F32).at[:b].set(c)
    tn = 1024
    out = pl.pallas_call(
        _mod_kernel,
        out_shape=jax.ShapeDtypeStruct((rows, n), F32),
        grid=(n // tn,),
        in_specs=[pl.BlockSpec((rows, d), lambda j: (0, 0)),
                  pl.BlockSpec((d, tn), lambda j: (0, j)),
                  pl.BlockSpec((1, tn), lambda j: (0, j))],
        out_specs=pl.BlockSpec((rows, tn), lambda j: (0, j)),
        compiler_params=_cparams(("arbitrary",)),
        name="adaln_mod",
    )(cp, ada_w, ada_b.reshape(1, n))
    return out[:b].reshape(b, 6, d)


def _norm_mod_kernel(x_ref, g_ref, mod_ref, o_ref, *, shift_idx, scale_idx):
    x = x_ref[0]
    ms = jnp.mean(x * x, axis=-1, keepdims=True)
    y = x * lax.rsqrt(ms + NORM_EPS) * g_ref[...]
    scale = mod_ref[0, scale_idx:scale_idx + 1, :]
    shift = mod_ref[0, shift_idx:shift_idx + 1, :]
    o_ref[0] = (y * (1.0 + scale) + shift).astype(o_ref.dtype)


def _norm_mod(x, g, mod, shift_idx, scale_idx, tm=1024):
    b, s, d = x.shape
    return pl.pallas_call(
        functools.partial(_norm_mod_kernel, shift_idx=shift_idx, scale_idx=scale_idx),
        out_shape=jax.ShapeDtypeStruct((b, s, d), BF16),
        grid=(b, s // tm),
        in_specs=[pl.BlockSpec((1, tm, d), lambda i, j: (i, j, 0)),
                  pl.BlockSpec((1, d), lambda i, j: (0, 0)),
                  pl.BlockSpec((1, 6, d), lambda i, j: (i, 0, 0))],
        out_specs=pl.BlockSpec((1, tm, d), lambda i, j: (i, j, 0)),
        compiler_params=_cparams(("arbitrary", "arbitrary")),
        name="norm1_mod",
    )(x, g.reshape(1, d), mod)


def _mm_kernel(a_ref, w_ref, o_ref, *, act):
    r = jnp.dot(a_ref[...], w_ref[...], preferred_element_type=F32)
    if act == "sigmoid":
        r = jax.nn.sigmoid(r)
    o_ref[...] = r.astype(o_ref.dtype)


def _matmul(a, w, out_dtype, act=None, tm=2048, tn=512, name="proj"):
    t, k = a.shape
    n = w.shape[1]
    tn = min(tn, n)
    return pl.pallas_call(
        functools.partial(_mm_kernel, act=act),
        out_shape=jax.ShapeDtypeStruct((t, n), out_dtype),
        grid=(t // tm, n // tn),
        in_specs=[pl.BlockSpec((tm, k), lambda i, j: (i, 0)),
                  pl.BlockSpec((k, tn), lambda i, j: (0, j))],
        out_specs=pl.BlockSpec((tm, tn), lambda i, j: (i, j)),
        compiler_params=_cparams(("arbitrary", "arbitrary")),
        name=name,
    )(a, w)


def _fcum_kernel(h_ref, wf_ref, fb_ref, o_ref, carry_ref, *, ts):
    @pl.when(pl.program_id(1) == 0)
    def _():
        carry_ref[...] = jnp.zeros_like(carry_ref)

    ft = lax.dot_general(wf_ref[...], h_ref[0], (((1,), (1,)), ((), ())),
                         preferred_element_type=F32)
    x = -_softplus(-(ft + fb_ref[...]))
    lane = lax.broadcasted_iota(jnp.int32, x.shape, 1)
    d = 1
    while d < ts:
        x = x + jnp.where(lane >= d, pltpu.roll(x, d, 1), 0.0)
        d *= 2
    x = x + carry_ref[...]
    o_ref[0] = x
    carry_ref[...] = x[:, ts - 1:ts]


def _forget_cumsum(h, wf_t, fb, ts=512):
    b, s, d = h.shape
    return pl.pallas_call(
        functools.partial(_fcum_kernel, ts=ts),
        out_shape=jax.ShapeDtypeStruct((b, HEADS, s), F32),
        grid=(b, s // ts),
        in_specs=[pl.BlockSpec((1, ts, d), lambda i, j: (i, j, 0)),
                  pl.BlockSpec((HEADS, d), lambda i, j: (0, 0)),
                  pl.BlockSpec((HEADS, 1), lambda i, j: (0, 0))],
        out_specs=pl.BlockSpec((1, HEADS, ts), lambda i, j: (i, 0, j)),
        scratch_shapes=[pltpu.VMEM((HEADS, 1), F32)],
        compiler_params=_cparams(("arbitrary", "arbitrary")),
        name="forget_cumsum",
    )(h, wf_t, fb)


def _fox_kernel(q_ref, k_ref, v_ref, f_ref, o_ref, m_ref, l_ref, acc_ref, *, bq):
    i = pl.program_id(2)
    lane = lax.broadcasted_iota(jnp.int32, (bq, LANES), 1)
    first = lane < HEAD_DIM
    qs = q_ref[0] * jnp.asarray(HEAD_DIM ** -0.5, BF16)
    zero = jnp.zeros_like(qs)
    q01 = jnp.concatenate([jnp.where(first, qs, zero), jnp.where(first, zero, qs)], axis=0)
    fref = f_ref[0, 0, i][:, 0:1]

    m_ref[...] = jnp.full_like(m_ref, -jnp.inf)
    l_ref[...] = jnp.zeros_like(l_ref)
    acc_ref[...] = jnp.zeros_like(acc_ref)

    def block(j, masked):
        start = pl.multiple_of(j * bq, bq)
        kb = k_ref[0, pl.ds(start, bq), :]
        vb = v_ref[0, pl.ds(start, bq), :]
        g = fref - f_ref[0, 0, j]
        z = lax.dot_general(q01, kb, (((1,), (1,)), ((), ())), preferred_element_type=F32)
        gb = jnp.concatenate([jnp.broadcast_to(g[0:1], (bq, bq)),
                              jnp.broadcast_to(g[1:2], (bq, bq))], axis=0)
        z = z + gb
        if masked:
            row = lax.broadcasted_iota(jnp.int32, (bq, bq), 0)
            col = lax.broadcasted_iota(jnp.int32, (bq, bq), 1)
            keep = jnp.concatenate([col <= row, col <= row], axis=0)
            z = jnp.where(keep, z, -jnp.inf)
        m_prev = m_ref[...]
        m_new = jnp.maximum(m_prev, jnp.max(z, axis=1, keepdims=True))
        alpha = jnp.exp(m_prev - m_new)
        p = jnp.exp(z - m_new)
        l_ref[...] = alpha * l_ref[...] + jnp.sum(p, axis=1, keepdims=True)
        acc_ref[...] = alpha * acc_ref[...] + jnp.dot(p.astype(BF16), vb, preferred_element_type=F32)
        m_ref[...] = m_new

    def body(j, carry):
        block(j, False)
        return carry

    lax.fori_loop(0, i, body, 0)
    block(i, True)
    acc = acc_ref[...] / l_ref[...]
    o_ref[0] = jnp.where(first, acc[:bq], acc[bq:]).astype(o_ref.dtype)


def _fox_attention(qkv, fcum, bq=512):
    b, s, _ = qkv.shape
    nblk = s // bq
    pairs = HEADS // 2
    cb = WIDTH // LANES
    f5 = fcum.reshape(b, pairs, 2, nblk, bq).transpose(0, 1, 3, 2, 4)
    return pl.pallas_call(
        functools.partial(_fox_kernel, bq=bq),
        out_shape=jax.ShapeDtypeStruct((b, s, WIDTH), BF16),
        grid=(b, pairs, nblk),
        in_specs=[pl.BlockSpec((1, bq, LANES), lambda bi, hp, i: (bi, i, hp)),
                  pl.BlockSpec((1, s, LANES), lambda bi, hp, i: (bi, 0, cb + hp)),
                  pl.BlockSpec((1, s, LANES), lambda bi, hp, i: (bi, 0, 2 * cb + hp)),
                  pl.BlockSpec((1, 1, nblk, 2, bq), lambda bi, hp, i: (bi, hp, 0, 0, 0))],
        out_specs=pl.BlockSpec((1, bq, LANES), lambda bi, hp, i: (bi, i, hp)),
        scratch_shapes=[pltpu.VMEM((2 * bq, 1), F32), pltpu.VMEM((2 * bq, 1), F32),
                        pltpu.VMEM((2 * bq, LANES), F32)],
        compiler_params=_cparams(("arbitrary", "arbitrary", "arbitrary")),
        name="fox_attention",
    )(qkv, qkv, qkv, f5)


def _head_extract(stacked, headlane, rows):
    out = stacked[0:rows]
    for h in range(1, HEADS):
        out = jnp.where(headlane == h, stacked[h * rows:(h + 1) * rows], out)
    return out


def _rwkv_kernel(p_ref, mu_ref, w0_ref, a0_ref, kk_ref, ka_ref, rk_ref, lng_ref, lnb_ref,
                 wwa_ref, g2_ref, bd_ref, o_ref, st_ref, prev_ref):
    L = CHUNK

    @pl.when(pl.program_id(1) == 0)
    def _():
        st_ref[...] = jnp.zeros_like(st_ref)
        prev_ref[...] = jnp.zeros_like(prev_ref)

    p = p_ref[0]
    rowi = lax.broadcasted_iota(jnp.int32, p.shape, 0)
    prev = jnp.where(rowi == 0, prev_ref[...], pltpu.roll(p, 1, 0))
    prev_ref[...] = p[L - 1:L, :]
    ps = p + (prev - p) * mu_ref[...]
    r = ps[:, 0:WIDTH]
    k = ps[:, WIDTH:2 * WIDTH]
    v = ps[:, 2 * WIDTH:3 * WIDTH]
    wa_in = ps[:, 3 * WIDTH:3 * WIDTH + DECAY_LORA + AAA_LORA]
    gd = ps[:, 3 * WIDTH + DECAY_LORA + AAA_LORA:]
    lane_wa = lax.broadcasted_iota(jnp.int32, wa_in.shape, 1)
    wa_act = jnp.where(lane_wa < DECAY_LORA, jnp.tanh(wa_in), wa_in)
    wa = _dot(wa_act, wwa_ref[...])
    log_w = -_softplus(-(w0_ref[...] + wa[:, :WIDTH])) - 0.5
    lw = -jnp.exp(log_w)
    a = jax.nn.sigmoid(a0_ref[...] + wa[:, WIDTH:])
    g = _dot(jax.nn.sigmoid(gd), g2_ref[...])
    bd = bd_ref[...]
    kk0 = k * kk_ref[...]
    kk = kk0 * lax.rsqrt(jnp.maximum(_split_dot(kk0 * kk0, bd), 1e-24))
    k2 = k * (1.0 + (a - 1.0) * ka_ref[...])
    av = -kk
    bv = kk * a

    ri = lax.broadcasted_iota(jnp.int32, (L, L), 0)
    ci = lax.broadcasted_iota(jnp.int32, (L, L), 1)
    tri_incl = ri >= ci
    tri_strict = ri > ci
    cl = _split_dot_left(jnp.where(tri_incl, 1.0, 0.0).astype(BF16), lw)
    cl_last = cl[L - 1:L, :]
    at = av * jnp.exp(cl - lw)
    rt = r * jnp.exp(cl)
    einv = jnp.exp(-cl)
    bt = bv * einv
    kt = k2 * einv
    edec = jnp.exp(cl_last - cl)
    b_end = bv * edec
    k_end = k2 * edec
    gam_last = jnp.exp(cl_last)

    headlane = lax.broadcasted_iota(jnp.int32, (L, WIDTH), 1) // HEAD_DIM
    parts = []
    for h in range(HEADS):
        mh = headlane == h
        parts.append(jnp.where(mh, at, 0.0))
        parts.append(jnp.where(mh, rt, 0.0))
    lhs = jnp.concatenate(parts, axis=0).astype(BF16)
    sb = _dot_nt(lhs, bt).reshape(HEADS, 2 * L, L)
    sk = _dot_nt(lhs, kt).reshape(HEADS, 2 * L, L)
    n_ab = jnp.where(tri_strict, sb[:, :L, :], 0.0)
    a_ak = jnp.where(tri_strict, sk[:, :L, :], 0.0)
    a_rb = jnp.where(tri_incl, sb[:, L:, :], 0.0)
    a_rk = jnp.where(tri_incl, sk[:, L:, :], 0.0)

    def bmm(x, y):
        return lax.dot_general(x.astype(BF16), y.astype(BF16), (((2,), (1,)), ((0,), (0,))),
                               preferred_element_type=F32)

    tinv = jnp.where(ri == ci, 1.0, 0.0) + n_ab
    pw = n_ab
    span = 2
    while span < L:
        pw = bmm(pw, pw)
        tinv = tinv + bmm(tinv, pw)
        span *= 2

    av_term = _head_extract(_dot(a_ak.reshape(HEADS * L, L), v), headlane, L)
    pq = _dot(tinv.reshape(HEADS * L, L), jnp.concatenate([at, av_term], axis=1))
    pm = _head_extract(pq[:, :WIDTH], headlane, L)
    qm = _head_extract(pq[:, WIDTH:], headlane, L)
    rkv = _head_extract(_dot(a_rk.reshape(HEADS * L, L), v), headlane, L)

    st = st_ref[...]
    pr = _dot_nt(jnp.concatenate([pm, rt], axis=0), st)
    u = pr[:L] + qm
    y = pr[L:] + _head_extract(_dot(a_rb.reshape(HEADS * L, L), u), headlane, L) + rkv
    hr = lax.broadcasted_iota(jnp.int32, (WIDTH, WIDTH), 0) // HEAD_DIM
    hc = lax.broadcasted_iota(jnp.int32, (WIDTH, WIDTH), 1) // HEAD_DIM
    upd = _dot_tn(u, b_end) + _dot_tn(v, k_end)
    st_ref[...] = st * gam_last + jnp.where(hr == hc, upd, 0.0)

    inv_n = 1.0 / HEAD_DIM
    mean = _split_dot(y, bd) * inv_n
    dlt = y - mean
    var = _split_dot(dlt * dlt, bd) * inv_n
    yn = dlt * lax.rsqrt(var + LN_X_EPS) * lng_ref[...] + lnb_ref[...]
    bonus = _split_dot(r * k2 * rk_ref[...], bd) * v
    o_ref[0] = ((yn + bonus) * g).astype(o_ref.dtype)


def _split_dot_left(w_bf16, x):
    hi = x.astype(BF16)
    r1 = x - hi.astype(F32)
    mid = r1.astype(BF16)
    lo = (r1 - mid.astype(F32)).astype(BF16)
    return (jnp.dot(w_bf16, hi, preferred_element_type=F32)
            + jnp.dot(w_bf16, mid, preferred_element_type=F32)
            + jnp.dot(w_bf16, lo, preferred_element_type=F32))


def _rwkv_branch(p_rw, mu, w0, w2, a0, a2, g2, k_k, k_a, r_k, ln_g, ln_b):
    b, s, sw = p_rw.shape
    row = lambda t: t.reshape(1, -1).astype(F32)
    wwa = jnp.zeros((DECAY_LORA + AAA_LORA, 2 * WIDTH), F32)
    wwa = wwa.at[:DECAY_LORA, :WIDTH].set(w2).at[DECAY_LORA:, WIDTH:].set(a2).astype(BF16)
    hid = jnp.arange(WIDTH) // HEAD_DIM
    bd = (hid[:, None] == hid[None, :]).astype(BF16)
    const = lambda shape: pl.BlockSpec(shape, lambda i, j: (0,) * len(shape))
    return pl.pallas_call(
        _rwkv_kernel,
        out_shape=jax.ShapeDtypeStruct((b, s, WIDTH), BF16),
        grid=(b, s // CHUNK),
        in_specs=[pl.BlockSpec((1, CHUNK, sw), lambda i, j: (i, j, 0)),
                  const((1, sw)), const((1, WIDTH)), const((1, WIDTH)), const((1, WIDTH)),
                  const((1, WIDTH)), const((1, WIDTH)), const((1, WIDTH)), const((1, WIDTH)),
                  const((DECAY_LORA + AAA_LORA, 2 * WIDTH)), const((GATE_LORA, WIDTH)),
                  const((WIDTH, WIDTH))],
        out_specs=pl.BlockSpec((1, CHUNK, WIDTH), lambda i, j: (i, j, 0)),
        scratch_shapes=[pltpu.VMEM((WIDTH, WIDTH), F32), pltpu.VMEM((1, sw), F32)],
        compiler_params=_cparams(("arbitrary", "arbitrary")),
        name="rwkv7_scan",
    )(p_rw, row(mu), row(w0), row(a0), row(k_k), row(k_a), row(r_k), row(ln_g), row(ln_b),
      wwa, g2.astype(BF16), bd)


def _out_kernel(of_ref, orw_ref, gate_ref, x_ref, mod_ref, wof_ref, wor_ref, wo_ref, n2g_ref,
                wr_ref, br_ref, x1_ref, h2_ref, route_ref):
    d = x_ref.shape[-1]
    gate = gate_ref[...].astype(F32)
    merged = (gate[:, :d] * jnp.dot(of_ref[...], wof_ref[...], preferred_element_type=F32)
              + gate[:, d:] * jnp.dot(orw_ref[...], wor_ref[...], preferred_element_type=F32))
    gate1 = mod_ref[0, 2:3, :]
    shift2 = mod_ref[0, 3:4, :]
    scale2 = mod_ref[0, 4:5, :]
    x1 = x_ref[...] + gate1 * jnp.dot(merged.astype(BF16), wo_ref[...], preferred_element_type=F32)
    x1_ref[...] = x1
    ms = jnp.mean(x1 * x1, axis=-1, keepdims=True)
    h2 = x1 * lax.rsqrt(ms + NORM_EPS) * n2g_ref[...] * (1.0 + scale2) + shift2
    h2_ref[...] = h2

    logits = jnp.dot(h2, wr_ref[...], precision=HIGHEST, preferred_element_type=F32) + br_ref[...]
    lane = lax.broadcasted_iota(jnp.int32, logits.shape, 1)
    neg = -jnp.inf
    big = jnp.int32(LANES)
    gl = jnp.where(lane < N_GROUPS, logits, neg)
    gmax = jnp.max(gl, axis=1, keepdims=True)
    gidx = jnp.min(jnp.where(gl == gmax, lane, big), axis=1, keepdims=True)
    g_p = 1.0 / jnp.sum(jnp.exp(gl - gmax), axis=1, keepdims=True)
    e_lane = lane - N_GROUPS
    in_grp = (e_lane >= 0) & (e_lane < N_EXPERTS) & ((e_lane // EXPERTS_PER_GROUP) == gidx)
    sel = jnp.where(in_grp, logits, neg)
    m1 = jnp.max(sel, axis=1, keepdims=True)
    i1 = jnp.min(jnp.where(sel == m1, lane, big), axis=1, keepdims=True)
    sel2 = jnp.where(lane == i1, neg, sel)
    m2 = jnp.max(sel2, axis=1, keepdims=True)
    i2 = jnp.min(jnp.where(sel2 == m2, lane, big), axis=1, keepdims=True)
    e21 = jnp.exp(m2 - m1)
    w_first = g_p / (1.0 + e21)
    w_second = g_p * e21 / (1.0 + e21)
    route = jnp.where(lane == 0, (i1 - N_GROUPS).astype(F32),
                      jnp.where(lane == 1, (i2 - N_GROUPS).astype(F32),
                                jnp.where(lane == 2, w_first, jnp.where(lane == 3, w_second, 0.0))))
    route_ref[...] = route


def _merge_out_router(o_fox, o_rw, gate, x, mod, wof, wor, wo, n2g, wr, br, tm=512):
    b, s, d = x.shape
    t = b * s
    spb = s // tm
    rowspec = lambda w: pl.BlockSpec((tm, w), lambda i: (i, 0))
    const = lambda shape: pl.BlockSpec(shape, lambda i: (0,) * len(shape))
    return pl.pallas_call(
        _out_kernel,
        out_shape=(jax.ShapeDtypeStruct((t, d), F32), jax.ShapeDtypeStruct((t, d), F32),
                   jax.ShapeDtypeStruct((t, LANES), F32)),
        grid=(t // tm,),
        in_specs=[rowspec(WIDTH), rowspec(WIDTH), rowspec(2 * d), rowspec(d),
                  pl.BlockSpec((1, 6, d), lambda i: (i // spb, 0, 0)),
                  const((WIDTH, d)), const((WIDTH, d)), const((d, d)), const((1, d)),
                  const((d, LANES)), const((1, LANES))],
        out_specs=(rowspec(d), rowspec(d), rowspec(LANES)),
        compiler_params=_cparams(("arbitrary",)),
        name="merge_out_router",
    )(o_fox, o_rw, gate, x.reshape(t, d), mod, wof, wor, wo, n2g.reshape(1, d), wr, br)


def _dispatch_kernel(dest_ref, h_ref, xs_in_ref, xs_ref, sem, *, tm):
    del xs_in_ref

    def issue(r, carry):
        for kk in range(TOP_K):
            dst = dest_ref[0, 0, TOP_K * r + kk]
            pltpu.make_async_copy(h_ref.at[pl.ds(r, 1)], xs_ref.at[pl.ds(dst, 1)], sem).start()
        return carry

    lax.fori_loop(0, tm, issue, 0)

    def drain(r, carry):
        for kk in range(TOP_K):
            pltpu.make_async_copy(h_ref.at[pl.ds(0, 1)], xs_ref.at[pl.ds(0, 1)], sem).wait()
        return carry

    lax.fori_loop(0, tm, drain, 0)


def _moe_dispatch(h2, dest3, rows, tm=256):
    t, d = h2.shape
    xs0 = jnp.zeros((rows, d), F32)
    return pl.pallas_call(
        functools.partial(_dispatch_kernel, tm=tm),
        out_shape=jax.ShapeDtypeStruct((rows, d), F32),
        grid=(t // tm,),
        in_specs=[pl.BlockSpec((1, 1, TOP_K * tm), lambda i: (i, 0, 0), memory_space=pltpu.SMEM),
                  pl.BlockSpec((tm, d), lambda i: (i, 0)),
                  pl.BlockSpec(memory_space=pl.ANY)],
        out_specs=pl.BlockSpec(memory_space=pl.ANY),
        scratch_shapes=[pltpu.SemaphoreType.DMA(())],
        input_output_aliases={2: 0},
        compiler_params=_cparams(("arbitrary",)),
        name="moe_dispatch",
    )(dest3, h2, xs0)


def _expert_kernel(blk_e_ref, nused_ref, xs_ref, w1_ref, w3_ref, w2_ref, o_ref):
    del blk_e_ref
    live = pl.program_id(0) * MOE_ROWS < nused_ref[0]

    @pl.when(live)
    def _():
        xb = xs_ref[...].astype(BF16)
        h1 = jnp.dot(xb, w1_ref[0], preferred_element_type=F32)
        h3 = jnp.dot(xb, w3_ref[0], preferred_element_type=F32)
        hh = (h1 * jax.nn.sigmoid(h1)) * h3
        o_ref[...] = jnp.dot(hh.astype(BF16), w2_ref[0], preferred_element_type=F32)

    @pl.when(jnp.logical_not(live))
    def _():
        o_ref[...] = jnp.zeros_like(o_ref)


def _moe_experts(xs, blk_e, nused, w1, w3, w2):
    rows, d = xs.shape
    de = w1.shape[-1]
    grid_spec = pltpu.PrefetchScalarGridSpec(
        num_scalar_prefetch=2,
        grid=(rows // MOE_ROWS,),
        in_specs=[pl.BlockSpec((MOE_ROWS, d), lambda i, be, nu: (i, 0)),
                  pl.BlockSpec((1, d, de), lambda i, be, nu: (be[i], 0, 0)),
                  pl.BlockSpec((1, d, de), lambda i, be, nu: (be[i], 0, 0)),
                  pl.BlockSpec((1, de, d), lambda i, be, nu: (be[i], 0, 0))],
        out_specs=pl.BlockSpec((MOE_ROWS, d), lambda i, be, nu: (i, 0)),
    )
    return pl.pallas_call(
        _expert_kernel,
        out_shape=jax.ShapeDtypeStruct((rows, d), F32),
        grid_spec=grid_spec,
        compiler_params=_cparams(("arbitrary",)),
        name="moe_experts",
    )(blk_e, nused, xs, w1, w3, w2)


def _final_kernel(dest_ref, route_ref, x1_ref, mod_ref, fg_ref, ys_ref, o_ref, buf_ref, sem, *, tm):
    def issue(r, carry):
        for kk in range(TOP_K):
            src = dest_ref[0, 0, TOP_K * r + kk]
            pltpu.make_async_copy(ys_ref.at[pl.ds(src, 1)], buf_ref.at[kk, pl.ds(r, 1)], sem).start()
        return carry

    lax.fori_loop(0, tm, issue, 0)

    def drain(r, carry):
        for kk in range(TOP_K):
            pltpu.make_async_copy(ys_ref.at[pl.ds(0, 1)], buf_ref.at[kk, pl.ds(0, 1)], sem).wait()
        return carry

    lax.fori_loop(0, tm, drain, 0)

    route = route_ref[...]
    y = route[:, 2:3] * buf_ref[0] + route[:, 3:4] * buf_ref[1]
    gate2 = mod_ref[0, 5:6, :]
    x2 = x1_ref[...] + gate2 * y
    ms = jnp.mean(x2 * x2, axis=-1, keepdims=True)
    o_ref[...] = x2 * lax.rsqrt(ms + NORM_EPS) * fg_ref[...]


def _moe_combine_final(dest3, route, x1, mod, final_g, ys, s, tm=256):
    t, d = x1.shape
    spb = s // tm
    return pl.pallas_call(
        functools.partial(_final_kernel, tm=tm),
        out_shape=jax.ShapeDtypeStruct((t, d), F32),
        grid=(t // tm,),
        in_specs=[pl.BlockSpec((1, 1, TOP_K * tm), lambda i: (i, 0, 0), memory_space=pltpu.SMEM),
                  pl.BlockSpec((tm, LANES), lambda i: (i, 0)),
                  pl.BlockSpec((tm, d), lambda i: (i, 0)),
                  pl.BlockSpec((1, 6, d), lambda i: (i // spb, 0, 0)),
                  pl.BlockSpec((1, d), lambda i: (0, 0)),
                  pl.BlockSpec(memory_space=pl.ANY)],
        out_specs=pl.BlockSpec((tm, d), lambda i: (i, 0)),
        scratch_shapes=[pltpu.VMEM((TOP_K, tm, d), F32), pltpu.SemaphoreType.DMA(())],
        compiler_params=_cparams(("arbitrary",)),
        name="moe_combine_final",
    )(dest3, route, x1, mod, final_g.reshape(1, d), ys)


def _moe_plan(route, tm):
    t = route.shape[0]
    m = t * TOP_K
    flat_e = route[:, :TOP_K].astype(jnp.int32).reshape(m)
    onehot = (flat_e[:, None] == jnp.arange(N_EXPERTS, dtype=jnp.int32)[None, :]).astype(jnp.int32)
    csum = jnp.cumsum(onehot, axis=0)
    rank = jnp.take_along_axis(csum, flat_e[:, None], axis=1)[:, 0] - 1
    counts = csum[-1]
    padded = (counts + MOE_ROWS - 1) // MOE_ROWS * MOE_ROWS
    pad_end = jnp.cumsum(padded)
    pad_start = pad_end - padded
    dest = (pad_start[flat_e] + rank).astype(jnp.int32)
    n_blocks = m // MOE_ROWS + N_EXPERTS
    blk_e = jnp.minimum(jnp.searchsorted(pad_end, jnp.arange(n_blocks) * MOE_ROWS, side="right"),
                        N_EXPERTS - 1).astype(jnp.int32)
    nused = pad_end[-1:].astype(jnp.int32)
    return dest.reshape(t // tm, 1, TOP_K * tm), blk_e, nused, n_blocks * MOE_ROWS


def kernel(x, c, ada_w, ada_b, norm1_g, w_in, fox_forget_b, shift_mu, rwkv_w0, rwkv_w2, rwkv_a0, rwkv_a2, rwkv_g2, rwkv_k_k, rwkv_k_a, rwkv_r_k, ln_x_g, ln_x_b, w_out_fox, w_out_rwkv, w_o, norm2_g, router_group_w, router_group_b, router_expert_w, router_expert_b, exp_w1, exp_w3, exp_w2, final_g):
    b, s, d = x.shape
    t = b * s
    depth = ada_w.shape[0]
    for l in range(depth):
        mod = _adaln_mod(c, ada_w[l], ada_b[l])

        h = _norm_mod(x, norm1_g[l], mod, shift_idx=0, scale_idx=1)
        h2d = h.reshape(t, d)
        w = w_in[l]
        o_f = 3 * WIDTH
        o_rw = o_f + HEADS
        o_g = o_rw + SHIFT_WIDTH
        qkv = _matmul(h2d, w[:, :o_f].astype(BF16), BF16, name="proj_qkv")
        p_rw = _matmul(h2d, w[:, o_rw:o_g].astype(BF16), F32, tn=896, name="proj_rwkv")
        gate = _matmul(h2d, w[:, o_g:].astype(BF16), BF16, act="sigmoid", name="proj_gate")
        fcum = _forget_cumsum(h, w[:, o_f:o_rw].T.astype(BF16), fox_forget_b[l].reshape(HEADS, 1))
        o_fox = _fox_attention(qkv.reshape(b, s, 3 * WIDTH), fcum)
        o_rwkv = _rwkv_branch(p_rw.reshape(b, s, SHIFT_WIDTH), shift_mu[l], rwkv_w0[l], rwkv_w2[l],
                              rwkv_a0[l], rwkv_a2[l], rwkv_g2[l], rwkv_k_k[l], rwkv_k_a[l],
                              rwkv_r_k[l], ln_x_g[l], ln_x_b[l])

        wr = jnp.zeros((d, LANES), F32)
        wr = wr.at[:, :N_GROUPS].set(router_group_w[l]).at[:, N_GROUPS:N_GROUPS + N_EXPERTS].set(router_expert_w[l])
        br = jnp.zeros((1, LANES), F32)
        br = br.at[0, :N_GROUPS].set(router_group_b[l]).at[0, N_GROUPS:N_GROUPS + N_EXPERTS].set(router_expert_b[l])
        x1, h2, route = _merge_out_router(
            o_fox.reshape(t, WIDTH), o_rwkv.reshape(t, WIDTH), gate, x, mod,
            w_out_fox[l].astype(BF16), w_out_rwkv[l].astype(BF16), w_o[l].astype(BF16), norm2_g[l], wr, br)

        tm_moe = 256
        dest3, blk_e, nused, rows = _moe_plan(route, tm_moe)
        xs = _moe_dispatch(h2, dest3, rows, tm=tm_moe)
        ys = _moe_experts(xs, blk_e, nused, exp_w1[l].astype(BF16), exp_w3[l].astype(BF16),
                          exp_w2[l].astype(BF16))
        last = l == depth - 1
        if last:
            out = _moe_combine_final(dest3, route, x1, mod, final_g, ys, s, tm=tm_moe)
            return out.reshape(b, s, d)
        raise NotImplementedError("only DEPTH == 1 is laid out")
```

```python
import functools

import jax
import jax.numpy as jnp
from jax import lax
from jax.experimental import pallas as pl
from jax.experimental.pallas import tpu as pltpu

F32 = jnp.float32
BF16 = jnp.bfloat16
HIGHEST = lax.Precision.HIGHEST

HEADS = 8
HEAD_DIM = 64
WIDTH = HEADS * HEAD_DIM
DECAY_LORA = 64
AAA_LORA = 64
GATE_LORA = 128
SHIFT_WIDTH = 3 * WIDTH + DECAY_LORA + AAA_LORA + GATE_LORA
LN_X_EPS = 64e-5
NORM_EPS = 1e-6
N_GROUPS = 4
EXPERTS_PER_GROUP = 8
N_EXPERTS = N_GROUPS * EXPERTS_PER_GROUP
TOP_K = 2

LANES = 128
CHUNK = 64
RWKV_TILE = 128
MOE_ROWS = 256
VMEM_LIMIT = 48 * 1024 * 1024


def _cparams(sem):
    return pltpu.CompilerParams(dimension_semantics=sem, vmem_limit_bytes=VMEM_LIMIT)


def _dot(a, b):
    return jnp.dot(a.astype(BF16), b.astype(BF16), preferred_element_type=F32)


def _dot_nt(a, b):
    return lax.dot_general(a.astype(BF16), b.astype(BF16), (((1,), (1,)), ((), ())),
                           preferred_element_type=F32)


def _dot_tn(a, b):
    return lax.dot_general(a.astype(BF16), b.astype(BF16), (((0,), (0,)), ((), ())),
                           preferred_element_type=F32)


def _split_dot(x, w_bf16):
    hi = x.astype(BF16)
    lo = (x - hi.astype(F32)).astype(BF16)
    return (jnp.dot(hi, w_bf16, preferred_element_type=F32)
            + jnp.dot(lo, w_bf16, preferred_element_type=F32))


def _softplus(x):
    return jnp.maximum(x, 0.0) + jnp.log1p(jnp.exp(-jnp.abs(x)))


def _mod_kernel(c_ref, w_ref, b_ref, o_ref):
    c = c_ref[...]
    sc = c * jax.nn.sigmoid(c)
    o_ref[...] = jnp.dot(sc, w_ref[...], precision=HIGHEST, preferred_element_type=F32) + b_ref[...]


def _adaln_mod(c, ada_w, ada_b):
    b, d = c.shape
    n = ada_w.shape[1]
    rows = 8
    cp = jnp.zeros((rows, d), F32).at[:b].set(c)
    tn = 1024
    out = pl.pallas_call(
        _mod_kernel,
        out_shape=jax.ShapeDtypeStruct((rows, n), F32),
        grid=(n // tn,),
        in_specs=[pl.BlockSpec((rows, d), lambda j: (0, 0)),
                  pl.BlockSpec((d, tn), lambda j: (0, j)),
                  pl.BlockSpec((1, tn), lambda j: (0, j))],
        out_specs=pl.BlockSpec((rows, tn), lambda j: (0, j)),
        compiler_params=_cparams(("arbitrary",)),
        name="adaln_mod",
    )(cp, ada_w, ada_b.reshape(1, n))
    return out[:b].reshape(b, 6, d)


def _norm_mod_kernel(x_ref, g_ref, mod_ref, o_ref, *, shift_idx, scale_idx):
    x = x_ref[0]
    ms = jnp.mean(x * x, axis=-1, keepdims=True)
    y = x * lax.rsqrt(ms + NORM_EPS) * g_ref[...]
    scale = mod_ref[0, scale_idx:scale_idx + 1, :]
    shift = mod_ref[0, shift_idx:shift_idx + 1, :]
    o_ref[0] = (y * (1.0 + scale) + shift).astype(o_ref.dtype)


def _norm_mod(x, g, mod, shift_idx, scale_idx, tm=1024):
    b, s, d = x.shape
    return pl.pallas_call(
        functools.partial(_norm_mod_kernel, shift_idx=shift_idx, scale_idx=scale_idx),
        out_shape=jax.ShapeDtypeStruct((b, s, d), BF16),
        grid=(b, s // tm),
        in_specs=[pl.BlockSpec((1, tm, d), lambda i, j: (i, j, 0)),
                  pl.BlockSpec((1, d), lambda i, j: (0, 0)),
                  pl.BlockSpec((1, 6, d), lambda i, j: (i, 0, 0))],
        out_specs=pl.BlockSpec((1, tm, d), lambda i, j: (i, j, 0)),
        compiler_params=_cparams(("arbitrary", "arbitrary")),
        name="norm1_mod",
    )(x, g.reshape(1, d), mod)


def _mm_kernel(a_ref, w_ref, bias_ref, o_ref, *, act):
    r = jnp.dot(a_ref[...], w_ref[...], preferred_element_type=F32) + bias_ref[...]
    if act == "sigmoid":
        r = jax.nn.sigmoid(r)
    o_ref[...] = r.astype(o_ref.dtype)


def _matmul(a, w, out_dtype, bias=None, act=None, tm=2048, tn=512, name="proj"):
    t, k = a.shape
    n = w.shape[1]
    tn = min(tn, n)
    if bias is None:
        bias = jnp.zeros((1, n), F32)
    return pl.pallas_call(
        functools.partial(_mm_kernel, act=act),
        out_shape=jax.ShapeDtypeStruct((t, n), out_dtype),
        grid=(t // tm, n // tn),
        in_specs=[pl.BlockSpec((tm, k), lambda i, j: (i, 0)),
                  pl.BlockSpec((k, tn), lambda i, j: (0, j)),
                  pl.BlockSpec((1, tn), lambda i, j: (0, j))],
        out_specs=pl.BlockSpec((tm, tn), lambda i, j: (i, j)),
        compiler_params=_cparams(("arbitrary", "arbitrary")),
        name=name,
    )(a, w, bias)


LOG2E = 1.4426950408889634


def _split3(x):
    hi = x.astype(BF16)
    r1 = x - hi.astype(F32)
    mid = r1.astype(BF16)
    lo = (r1 - mid.astype(F32)).astype(BF16)
    return hi, mid, lo


def _fcum_kernel(h_ref, wf_ref, fb_ref, sel_ref, o_ref, carry_ref, *, ts):
    @pl.when(pl.program_id(1) == 0)
    def _():
        carry_ref[...] = jnp.zeros_like(carry_ref)

    f = jnp.dot(h_ref[0], wf_ref[...], preferred_element_type=F32) + fb_ref[...]
    lf = -_softplus(-f)
    ri = lax.broadcasted_iota(jnp.int32, (ts, ts), 0)
    ci = lax.broadcasted_iota(jnp.int32, (ts, ts), 1)
    tri = jnp.where(ri >= ci, 1.0, 0.0).astype(BF16)
    cum = carry_ref[...]
    for piece in _split3(lf):
        cum = cum + jnp.dot(tri, piece, preferred_element_type=F32)
    carry_ref[...] = cum[ts - 1:ts, :]
    out = jnp.zeros((ts, sel_ref.shape[2]), F32)
    for idx, piece in enumerate(_split3(cum * (-LOG2E))):
        out = out + jnp.dot(piece, sel_ref[idx], preferred_element_type=F32)
    o_ref[0] = out.astype(o_ref.dtype)


def _forget_bias(h, wf, fb, ts=512):
    b, s, d = h.shape
    pairs = HEADS // 2
    wf_p = jnp.zeros((d, LANES), F32).at[:, :HEADS].set(wf).astype(BF16)
    fb_p = jnp.zeros((1, LANES), F32).at[0, :HEADS].set(fb)
    hh = jnp.arange(HEADS)
    sel = jnp.zeros((3, LANES, pairs * LANES), F32)
    for piece in range(3):
        sel = sel.at[piece, hh, (hh // 2) * LANES + (hh % 2) * 3 + piece].set(1.0)
    return pl.pallas_call(
        functools.partial(_fcum_kernel, ts=ts),
        out_shape=jax.ShapeDtypeStruct((b, s, pairs * LANES), BF16),
        grid=(b, s // ts),
        in_specs=[pl.BlockSpec((1, ts, d), lambda i, j: (i, j, 0)),
                  pl.BlockSpec((d, LANES), lambda i, j: (0, 0)),
                  pl.BlockSpec((1, LANES), lambda i, j: (0, 0)),
                  pl.BlockSpec((3, LANES, pairs * LANES), lambda i, j: (0, 0, 0))],
        out_specs=pl.BlockSpec((1, ts, pairs * LANES), lambda i, j: (i, j, 0)),
        scratch_shapes=[pltpu.VMEM((1, LANES), F32)],
        compiler_params=_cparams(("arbitrary", "arbitrary")),
        name="forget_bias",
    )(h, wf_p, fb_p, sel.astype(BF16))


FOX_BQ = 1024
FOX_BK = 512


def _fox_kernel(q_ref, k_ref, v_ref, a_ref, o_ref, m_ref, acc_ref, *, bq, bk):
    i = pl.program_id(2)
    lane = lax.broadcasted_iota(jnp.int32, (bq, LANES), 1)
    first = lane < HEAD_DIM
    qs = q_ref[0].astype(F32) * (HEAD_DIM ** -0.5 * LOG2E)
    aug0 = jnp.where(lane < 3, 1.0, 0.0)
    aug1 = jnp.where((lane >= 3) & (lane < 6), 1.0, 0.0)
    q01 = jnp.concatenate([jnp.concatenate([jnp.where(first, qs, 0.0), aug0], axis=1),
                           jnp.concatenate([jnp.where(first, 0.0, qs), aug1], axis=1)],
                          axis=0).astype(BF16)
    m_ref[...] = jnp.full_like(m_ref, -jnp.inf)
    acc_ref[...] = jnp.zeros_like(acc_ref)

    def logits(j):
        start = pl.multiple_of(j * bk, bk)
        kb = jnp.concatenate([k_ref[0, pl.ds(start, bk), :], a_ref[0, pl.ds(start, bk), :]], axis=1)
        return lax.dot_general(q01, kb, (((1,), (1,)), ((), ())), preferred_element_type=F32)

    def consume(j, z):
        start = pl.multiple_of(j * bk, bk)
        vb = v_ref[0, pl.ds(start, bk), :]
        m_prev = m_ref[...]
        m_new = jnp.maximum(m_prev, jnp.max(z, axis=1, keepdims=True))
        alpha = jnp.exp2(m_prev - m_new)
        p = jnp.exp2(z - jnp.concatenate([m_new] * (bk // LANES), axis=1)).astype(BF16)
        pv = jnp.concatenate([jnp.dot(p[:bq], vb[:, :LANES], preferred_element_type=F32),
                              jnp.dot(p[bq:], vb[:, LANES:], preferred_element_type=F32)], axis=0)
        acc_ref[...] = alpha * acc_ref[...] + pv
        m_ref[...] = m_new

    per_q = bq // bk
    n_full = i * per_q

    def body(j, carry):
        consume(j, logits(j))
        return carry

    lax.fori_loop(0, n_full, body, 0)
    row = lax.broadcasted_iota(jnp.int32, (bq, bk), 0)
    col = lax.broadcasted_iota(jnp.int32, (bq, bk), 1)
    for d in range(per_q):
        keep = col + d * bk <= row
        z = logits(n_full + d)
        consume(n_full + d, jnp.where(jnp.concatenate([keep, keep], axis=0), z, -jnp.inf))
    acc = acc_ref[...]
    o = acc / pltpu.roll(acc, HEAD_DIM, 1)
    o_ref[0] = jnp.where(first, o[:bq], pltpu.roll(o[bq:], HEAD_DIM, 1)).astype(o_ref.dtype)


def _fox_attention(qkv, fbias, bq=FOX_BQ, bk=FOX_BK):
    b, s, _ = qkv.shape
    pairs = HEADS // 2
    cb = WIDTH // LANES
    return pl.pallas_call(
        functools.partial(_fox_kernel, bq=bq, bk=bk),
        out_shape=jax.ShapeDtypeStruct((b, s, WIDTH), BF16),
        grid=(b, pairs, s // bq),
        in_specs=[pl.BlockSpec((1, bq, LANES), lambda bi, hp, i: (bi, i, hp)),
                  pl.BlockSpec((1, s, LANES), lambda bi, hp, i: (bi, 0, cb + hp)),
                  pl.BlockSpec((1, s, 2 * LANES), lambda bi, hp, i: (bi, 0, cb + hp)),
                  pl.BlockSpec((1, s, LANES), lambda bi, hp, i: (bi, 0, hp))],
        out_specs=pl.BlockSpec((1, bq, LANES), lambda bi, hp, i: (bi, i, hp)),
        scratch_shapes=[pltpu.VMEM((2 * bq, LANES), F32), pltpu.VMEM((2 * bq, LANES), F32)],
        compiler_params=_cparams(("arbitrary", "arbitrary", "arbitrary")),
        name="fox_attention",
    )(qkv, qkv, qkv, fbias)


PAIR = 2 * HEAD_DIM
GROUPS = WIDTH // PAIR


def _group(x, g):
    return x[:, g * PAIR:(g + 1) * PAIR]


def _pair_sum(x, bd2):
    return jnp.concatenate([_split_dot(_group(x, g), bd2) for g in range(GROUPS)], axis=1)


def _head_apply(mats, x, lane_first):
    rows = mats.shape[1]
    outs = []
    for g in range(GROUPS):
        res = _dot(mats[2 * g:2 * g + 2].reshape(2 * rows, rows), _group(x, g))
        outs.append(jnp.where(lane_first, res[:rows], res[rows:]))
    return jnp.concatenate(outs, axis=1)


def _rwkv_kernel(p_ref, mu_ref, w0_ref, a0_ref, kk_ref, ka_ref, rk_ref, lng_ref, lnb_ref,
                 wwa_ref, g2_ref, bd_ref, o_ref, st_ref, prev_ref):
    L = CHUNK

    @pl.when(pl.program_id(1) == 0)
    def _():
        st_ref[...] = jnp.zeros_like(st_ref)
        prev_ref[...] = jnp.zeros_like(prev_ref)

    p = p_ref[0]
    T = p.shape[0]
    rowi = lax.broadcasted_iota(jnp.int32, p.shape, 0)
    prev = jnp.where(rowi == 0, prev_ref[...], pltpu.roll(p, 1, 0))
    prev_ref[...] = p[T - 1:T, :]
    ps = p + (prev - p) * mu_ref[...]
    r = ps[:, 0:WIDTH]
    k = ps[:, WIDTH:2 * WIDTH]
    v = ps[:, 2 * WIDTH:3 * WIDTH]
    wa_in = ps[:, 3 * WIDTH:3 * WIDTH + DECAY_LORA + AAA_LORA]
    gd = ps[:, 3 * WIDTH + DECAY_LORA + AAA_LORA:]
    lane_wa = lax.broadcasted_iota(jnp.int32, wa_in.shape, 1)
    wa_act = jnp.where(lane_wa < DECAY_LORA, jnp.tanh(wa_in), wa_in)
    wa = _dot(wa_act, wwa_ref[...])
    log_w = -_softplus(-(w0_ref[...] + wa[:, :WIDTH])) - 0.5
    lw = -jnp.exp(log_w)
    a = jax.nn.sigmoid(a0_ref[...] + wa[:, WIDTH:])
    out_gate = _dot(jax.nn.sigmoid(gd), g2_ref[...])
    bd = bd_ref[...]
    kk0 = k * kk_ref[...]
    kk = kk0 * lax.rsqrt(jnp.maximum(_pair_sum(kk0 * kk0, bd), 1e-24))
    k2 = k * (1.0 + (a - 1.0) * ka_ref[...])
    av = -kk
    bv = kk * a

    n_sub = T // L
    rt_i = lax.broadcasted_iota(jnp.int32, (T, T), 0)
    ct_i = lax.broadcasted_iota(jnp.int32, (T, T), 1)
    tri_tile = (rt_i >= ct_i) & (rt_i // L == ct_i // L)
    cl = _split_dot_left(jnp.where(tri_tile, 1.0, 0.0).astype(BF16), lw)
    cl_end = jnp.concatenate([jnp.broadcast_to(cl[(c + 1) * L - 1:(c + 1) * L, :], (L, WIDTH))
                              for c in range(n_sub)], axis=0)
    at_all = av * jnp.exp(cl - lw)
    rt_all = r * jnp.exp(cl)
    einv = jnp.exp(-cl)
    bt_all = bv * einv
    kt_all = k2 * einv
    edec = jnp.exp(cl_end - cl)
    b_end_all = bv * edec
    k_end_all = k2 * edec

    ri = lax.broadcasted_iota(jnp.int32, (L, L), 0)
    ci = lax.broadcasted_iota(jnp.int32, (L, L), 1)
    tri_incl = ri >= ci
    tri_strict = ri > ci
    eye = jnp.where(ri == ci, 1.0, 0.0)
    lane_first = lax.broadcasted_iota(jnp.int32, (L, PAIR), 1) < HEAD_DIM
    qr = lax.broadcasted_iota(jnp.int32, (PAIR, PAIR), 0) < HEAD_DIM
    qc = lax.broadcasted_iota(jnp.int32, (PAIR, PAIR), 1) < HEAD_DIM
    same_head = qr == qc

    def bmm(x, y):
        return lax.dot_general(x.astype(BF16), y.astype(BF16), (((2,), (1,)), ((0,), (0,))),
                               preferred_element_type=F32)

    def chunk_terms(c):
        rows = slice(c * L, (c + 1) * L)
        at, rt, bt, kt, vc = at_all[rows], rt_all[rows], bt_all[rows], kt_all[rows], v[rows]
        sb_heads, sk_heads = [], []
        for g in range(GROUPS):
            at_g, rt_g = _group(at, g), _group(rt, g)
            lhs = jnp.concatenate([jnp.where(lane_first, at_g, 0.0), jnp.where(lane_first, rt_g, 0.0),
                                   jnp.where(lane_first, 0.0, at_g), jnp.where(lane_first, 0.0, rt_g)],
                                  axis=0).astype(BF16)
            sb_g = _dot_nt(lhs, _group(bt, g))
            sk_g = _dot_nt(lhs, _group(kt, g))
            for hh in range(2):
                sb_heads.append(sb_g[hh * 2 * L:(hh + 1) * 2 * L])
                sk_heads.append(sk_g[hh * 2 * L:(hh + 1) * 2 * L])
        sb = jnp.stack(sb_heads)
        sk = jnp.stack(sk_heads)
        n_ab = jnp.where(tri_strict, sb[:, :L, :], 0.0)
        a_ak = jnp.where(tri_strict, sk[:, :L, :], 0.0)
        a_rb = jnp.where(tri_incl, sb[:, L:, :], 0.0)
        a_rk = jnp.where(tri_incl, sk[:, L:, :], 0.0)
        tinv = eye + n_ab
        pw = n_ab
        span = 2
        while span < L:
            pw = bmm(pw, pw)
            tinv = tinv + bmm(tinv, pw)
            span *= 2
        av_term = _head_apply(a_ak, vc, lane_first)
        pm = _head_apply(tinv, at, lane_first)
        qm = _head_apply(tinv, av_term, lane_first)
        rkv = _head_apply(a_rk, vc, lane_first)
        return pm, qm, rkv, a_rb

    terms = [chunk_terms(c) for c in range(n_sub)]

    y_chunks = []
    for c in range(n_sub):
        rows = slice(c * L, (c + 1) * L)
        pm, qm, rkv, a_rb = terms[c]
        rt, vc, b_end, k_end = rt_all[rows], v[rows], b_end_all[rows], k_end_all[rows]
        gam_last = jnp.exp(cl[(c + 1) * L - 1:(c + 1) * L, :])
        u_parts, ys_parts = [], []
        for g in range(GROUPS):
            pr = _dot_nt(jnp.concatenate([_group(pm, g), _group(rt, g)], axis=0), st_ref[g])
            u_parts.append(pr[:L] + _group(qm, g))
            ys_parts.append(pr[L:])
        u = jnp.concatenate(u_parts, axis=1)
        y_chunks.append(jnp.concatenate(ys_parts, axis=1) + _head_apply(a_rb, u, lane_first) + rkv)
        for g in range(GROUPS):
            upd = _dot_tn(_group(u, g), _group(b_end, g)) + _dot_tn(_group(vc, g), _group(k_end, g))
            st_ref[g] = st_ref[g] * _group(gam_last, g) + jnp.where(same_head, upd, 0.0)
    y = jnp.concatenate(y_chunks, axis=0)

    inv_n = 1.0 / HEAD_DIM
    mean = _pair_sum(y, bd) * inv_n
    dlt = y - mean
    var = _pair_sum(dlt * dlt, bd) * inv_n
    yn = dlt * lax.rsqrt(var + LN_X_EPS) * lng_ref[...] + lnb_ref[...]
    bonus = _pair_sum(r * k2 * rk_ref[...], bd) * v
    o_ref[0] = ((yn + bonus) * out_gate).astype(o_ref.dtype)


def _split_dot_left(w_bf16, x):
    hi = x.astype(BF16)
    r1 = x - hi.astype(F32)
    mid = r1.astype(BF16)
    lo = (r1 - mid.astype(F32)).astype(BF16)
    return (jnp.dot(w_bf16, hi, preferred_element_type=F32)
            + jnp.dot(w_bf16, mid, preferred_element_type=F32)
            + jnp.dot(w_bf16, lo, preferred_element_type=F32))


def _rwkv_branch(p_rw, mu, w0, w2, a0, a2, g2, k_k, k_a, r_k, ln_g, ln_b):
    b, s, sw = p_rw.shape
    row = lambda t: t.reshape(1, -1).astype(F32)
    wwa = jnp.zeros((DECAY_LORA + AAA_LORA, 2 * WIDTH), F32)
    wwa = wwa.at[:DECAY_LORA, :WIDTH].set(w2).at[DECAY_LORA:, WIDTH:].set(a2).astype(BF16)
    hid = jnp.arange(PAIR) // HEAD_DIM
    bd = (hid[:, None] == hid[None, :]).astype(BF16)
    const = lambda shape: pl.BlockSpec(shape, lambda i, j: (0,) * len(shape))
    return pl.pallas_call(
        _rwkv_kernel,
        out_shape=jax.ShapeDtypeStruct((b, s, WIDTH), BF16),
        grid=(b, s // RWKV_TILE),
        in_specs=[pl.BlockSpec((1, RWKV_TILE, sw), lambda i, j: (i, j, 0)),
                  const((1, sw)), const((1, WIDTH)), const((1, WIDTH)), const((1, WIDTH)),
                  const((1, WIDTH)), const((1, WIDTH)), const((1, WIDTH)), const((1, WIDTH)),
                  const((DECAY_LORA + AAA_LORA, 2 * WIDTH)), const((GATE_LORA, WIDTH)),
                  const((PAIR, PAIR))],
        out_specs=pl.BlockSpec((1, RWKV_TILE, WIDTH), lambda i, j: (i, j, 0)),
        scratch_shapes=[pltpu.VMEM((GROUPS, PAIR, PAIR), F32), pltpu.VMEM((1, sw), F32)],
        compiler_params=_cparams(("arbitrary", "arbitrary")),
        name="rwkv7_scan",
    )(p_rw, row(mu), row(w0), row(a0), row(k_k), row(k_a), row(r_k), row(ln_g), row(ln_b),
      wwa, g2.astype(BF16), bd)


def _out_kernel(of_ref, orw_ref, gate_ref, x_ref, mod_ref, wof_ref, wor_ref, wo_ref, n2g_ref,
                wr_ref, br_ref, x1_ref, h2_ref, route_ref):
    d = x_ref.shape[-1]
    gate = gate_ref[...].astype(F32)
    merged = (gate[:, :d] * jnp.dot(of_ref[...], wof_ref[...], preferred_element_type=F32)
              + gate[:, d:] * jnp.dot(orw_ref[...], wor_ref[...], preferred_element_type=F32))
    gate1 = mod_ref[0, 2:3, :]
    shift2 = mod_ref[0, 3:4, :]
    scale2 = mod_ref[0, 4:5, :]
    x1 = x_ref[...] + gate1 * jnp.dot(merged.astype(BF16), wo_ref[...], preferred_element_type=F32)
    x1_ref[...] = x1
    ms = jnp.mean(x1 * x1, axis=-1, keepdims=True)
    h2 = x1 * lax.rsqrt(ms + NORM_EPS) * n2g_ref[...] * (1.0 + scale2) + shift2
    h2_ref[...] = h2

    logits = jnp.dot(h2, wr_ref[...], precision=HIGHEST, preferred_element_type=F32) + br_ref[...]
    lane = lax.broadcasted_iota(jnp.int32, logits.shape, 1)
    neg = -jnp.inf
    big = jnp.int32(LANES)
    gl = jnp.where(lane < N_GROUPS, logits, neg)
    gmax = jnp.max(gl, axis=1, keepdims=True)
    gidx = jnp.min(jnp.where(gl == gmax, lane, big), axis=1, keepdims=True)
    g_p = 1.0 / jnp.sum(jnp.exp(gl - gmax), axis=1, keepdims=True)
    e_lane = lane - N_GROUPS
    in_grp = (e_lane >= 0) & (e_lane < N_EXPERTS) & ((e_lane // EXPERTS_PER_GROUP) == gidx)
    sel = jnp.where(in_grp, logits, neg)
    m1 = jnp.max(sel, axis=1, keepdims=True)
    i1 = jnp.min(jnp.where(sel == m1, lane, big), axis=1, keepdims=True)
    sel2 = jnp.where(lane == i1, neg, sel)
    m2 = jnp.max(sel2, axis=1, keepdims=True)
    i2 = jnp.min(jnp.where(sel2 == m2, lane, big), axis=1, keepdims=True)
    e21 = jnp.exp(m2 - m1)
    w_first = g_p / (1.0 + e21)
    w_second = g_p * e21 / (1.0 + e21)
    route = jnp.where(lane == 0, (i1 - N_GROUPS).astype(F32),
                      jnp.where(lane == 1, (i2 - N_GROUPS).astype(F32),
                                jnp.where(lane == 2, w_first, jnp.where(lane == 3, w_second, 0.0))))
    route_ref[...] = route


def _merge_out_router(o_fox, o_rw, gate, x, mod, wof, wor, wo, n2g, wr, br, tm=512):
    b, s, d = x.shape
    t = b * s
    spb = s // tm
    rowspec = lambda w: pl.BlockSpec((tm, w), lambda i: (i, 0))
    const = lambda shape: pl.BlockSpec(shape, lambda i: (0,) * len(shape))
    return pl.pallas_call(
        _out_kernel,
        out_shape=(jax.ShapeDtypeStruct((t, d), F32), jax.ShapeDtypeStruct((t, d), F32),
                   jax.ShapeDtypeStruct((t, LANES), F32)),
        grid=(t // tm,),
        in_specs=[rowspec(WIDTH), rowspec(WIDTH), rowspec(2 * d), rowspec(d),
                  pl.BlockSpec((1, 6, d), lambda i: (i // spb, 0, 0)),
                  const((WIDTH, d)), const((WIDTH, d)), const((d, d)), const((1, d)),
                  const((d, LANES)), const((1, LANES))],
        out_specs=(rowspec(d), rowspec(d), rowspec(LANES)),
        compiler_params=_cparams(("arbitrary",)),
        name="merge_out_router",
    )(o_fox, o_rw, gate, x.reshape(t, d), mod, wof, wor, wo, n2g.reshape(1, d), wr, br)


def _dispatch_kernel(dest_ref, h_ref, xs_in_ref, xs_ref, sem, *, tm):
    del xs_in_ref

    def issue(r, carry):
        for kk in range(TOP_K):
            dst = dest_ref[0, 0, TOP_K * r + kk]
            pltpu.make_async_copy(h_ref.at[pl.ds(r, 1)], xs_ref.at[pl.ds(dst, 1)], sem).start()
        return carry

    lax.fori_loop(0, tm, issue, 0, unroll=8)
    for kk in range(TOP_K):
        pltpu.make_async_copy(h_ref, xs_ref.at[pl.ds(0, tm)], sem).wait()


def _moe_dispatch(h2, dest3, rows, tm=256):
    t, d = h2.shape
    xs0 = jnp.zeros((rows, d), F32)
    return pl.pallas_call(
        functools.partial(_dispatch_kernel, tm=tm),
        out_shape=jax.ShapeDtypeStruct((rows, d), F32),
        grid=(t // tm,),
        in_specs=[pl.BlockSpec((1, 1, TOP_K * tm), lambda i: (i, 0, 0), memory_space=pltpu.SMEM),
                  pl.BlockSpec((tm, d), lambda i: (i, 0)),
                  pl.BlockSpec(memory_space=pl.ANY)],
        out_specs=pl.BlockSpec(memory_space=pl.ANY),
        scratch_shapes=[pltpu.SemaphoreType.DMA(())],
        input_output_aliases={2: 0},
        compiler_params=_cparams(("arbitrary",)),
        name="moe_dispatch",
    )(dest3, h2, xs0)


def _expert_kernel(blk_e_ref, nused_ref, xs_ref, w1_ref, w3_ref, w2_ref, o_ref):
    del blk_e_ref
    live = pl.program_id(0) * MOE_ROWS < nused_ref[0]

    @pl.when(live)
    def _():
        xb = xs_ref[...].astype(BF16)
        h1 = jnp.dot(xb, w1_ref[0], preferred_element_type=F32)
        h3 = jnp.dot(xb, w3_ref[0], preferred_element_type=F32)
        hh = (h1 * jax.nn.sigmoid(h1)) * h3
        o_ref[...] = jnp.dot(hh.astype(BF16), w2_ref[0], preferred_element_type=F32)

    @pl.when(jnp.logical_not(live))
    def _():
        o_ref[...] = jnp.zeros_like(o_ref)


def _moe_experts(xs, blk_e, nused, w1, w3, w2):
    rows, d = xs.shape
    de = w1.shape[-1]
    grid_spec = pltpu.PrefetchScalarGridSpec(
        num_scalar_prefetch=2,
        grid=(rows // MOE_ROWS,),
        in_specs=[pl.BlockSpec((MOE_ROWS, d), lambda i, be, nu: (i, 0)),
                  pl.BlockSpec((1, d, de), lambda i, be, nu: (be[i], 0, 0)),
                  pl.BlockSpec((1, d, de), lambda i, be, nu: (be[i], 0, 0)),
                  pl.BlockSpec((1, de, d), lambda i, be, nu: (be[i], 0, 0))],
        out_specs=pl.BlockSpec((MOE_ROWS, d), lambda i, be, nu: (i, 0)),
    )
    return pl.pallas_call(
        _expert_kernel,
        out_shape=jax.ShapeDtypeStruct((rows, d), F32),
        grid_spec=grid_spec,
        compiler_params=_cparams(("arbitrary",)),
        name="moe_experts",
    )(blk_e, nused, xs, w1, w3, w2)


def _final_kernel(dest_ref, route_ref, x1_ref, mod_ref, fg_ref, ys_ref, o_ref, buf_ref, sem, *, tm):
    def issue(r, carry):
        for kk in range(TOP_K):
            src = dest_ref[0, 0, TOP_K * r + kk]
            pltpu.make_async_copy(ys_ref.at[pl.ds(src, 1)], buf_ref.at[kk, pl.ds(r, 1)], sem).start()
        return carry

    lax.fori_loop(0, tm, issue, 0, unroll=8)
    for kk in range(TOP_K):
        pltpu.make_async_copy(ys_ref.at[pl.ds(0, tm)], buf_ref.at[kk], sem).wait()

    route = route_ref[...]
    y = route[:, 2:3] * buf_ref[0] + route[:, 3:4] * buf_ref[1]
    gate2 = mod_ref[0, 5:6, :]
    x2 = x1_ref[...] + gate2 * y
    ms = jnp.mean(x2 * x2, axis=-1, keepdims=True)
    o_ref[...] = x2 * lax.rsqrt(ms + NORM_EPS) * fg_ref[...]


def _moe_combine_final(dest3, route, x1, mod, final_g, ys, s, tm=256):
    t, d = x1.shape
    spb = s // tm
    return pl.pallas_call(
        functools.partial(_final_kernel, tm=tm),
        out_shape=jax.ShapeDtypeStruct((t, d), F32),
        grid=(t // tm,),
        in_specs=[pl.BlockSpec((1, 1, TOP_K * tm), lambda i: (i, 0, 0), memory_space=pltpu.SMEM),
                  pl.BlockSpec((tm, LANES), lambda i: (i, 0)),
                  pl.BlockSpec((tm, d), lambda i: (i, 0)),
                  pl.BlockSpec((1, 6, d), lambda i: (i // spb, 0, 0)),
                  pl.BlockSpec((1, d), lambda i: (0, 0)),
                  pl.BlockSpec(memory_space=pl.ANY)],
        out_specs=pl.BlockSpec((tm, d), lambda i: (i, 0)),
        scratch_shapes=[pltpu.VMEM((TOP_K, tm, d), F32), pltpu.SemaphoreType.DMA(())],
        compiler_params=_cparams(("arbitrary",)),
        name="moe_combine_final",
    )(dest3, route, x1, mod, final_g.reshape(1, d), ys)


def _moe_plan(route, tm):
    t = route.shape[0]
    m = t * TOP_K
    flat_e = route[:, :TOP_K].astype(jnp.int32).reshape(m)
    onehot = (flat_e[:, None] == jnp.arange(N_EXPERTS, dtype=jnp.int32)[None, :]).astype(jnp.int32)
    csum = jnp.cumsum(onehot, axis=0)
    rank = jnp.take_along_axis(csum, flat_e[:, None], axis=1)[:, 0] - 1
    counts = csum[-1]
    padded = (counts + MOE_ROWS - 1) // MOE_ROWS * MOE_ROWS
    pad_end = jnp.cumsum(padded)
    pad_start = pad_end - padded
    dest = (pad_start[flat_e] + rank).astype(jnp.int32)
    n_blocks = m // MOE_ROWS + N_EXPERTS
    blk_e = jnp.minimum(jnp.searchsorted(pad_end, jnp.arange(n_blocks) * MOE_ROWS, side="right"),
                        N_EXPERTS - 1).astype(jnp.int32)
    nused = pad_end[-1:].astype(jnp.int32)
    return dest.reshape(t // tm, 1, TOP_K * tm), blk_e, nused, n_blocks * MOE_ROWS


def kernel(x, c, ada_w, ada_b, norm1_g, w_in, fox_forget_b, shift_mu, rwkv_w0, rwkv_w2, rwkv_a0, rwkv_a2, rwkv_g2, rwkv_k_k, rwkv_k_a, rwkv_r_k, ln_x_g, ln_x_b, w_out_fox, w_out_rwkv, w_o, norm2_g, router_group_w, router_group_b, router_expert_w, router_expert_b, exp_w1, exp_w3, exp_w2, final_g):
    b, s, d = x.shape
    t = b * s
    assert ada_w.shape[0] == 1, "the final norm is fused into the last layer's combine; one layer is laid out"
    for l in range(1):
        mod = _adaln_mod(c, ada_w[l], ada_b[l])

        h = _norm_mod(x, norm1_g[l], mod, shift_idx=0, scale_idx=1)
        h2d = h.reshape(t, d)
        w = w_in[l]
        o_f = 3 * WIDTH
        o_rw = o_f + HEADS
        o_g = o_rw + SHIFT_WIDTH
        wv = w[:, 2 * WIDTH:o_f].reshape(d, HEADS, HEAD_DIM)
        wv = jnp.concatenate([wv, jnp.zeros_like(wv)], axis=2).reshape(d, 2 * WIDTH)
        ones_cols = (jnp.arange(2 * WIDTH) % LANES >= HEAD_DIM).astype(F32)
        qkv_bias = jnp.concatenate([jnp.zeros((2 * WIDTH,), F32), ones_cols]).reshape(1, 4 * WIDTH)
        qkv = _matmul(h2d, jnp.concatenate([w[:, :2 * WIDTH], wv], axis=1).astype(BF16), BF16,
                      bias=qkv_bias, name="proj_qkv")
        p_rw = _matmul(h2d, w[:, o_rw:o_g].astype(BF16), F32, tn=896, name="proj_rwkv")
        gate = _matmul(h2d, w[:, o_g:].astype(BF16), BF16, act="sigmoid", name="proj_gate")
        fbias = _forget_bias(h, w[:, o_f:o_rw], fox_forget_b[l])
        o_fox = _fox_attention(qkv.reshape(b, s, 4 * WIDTH), fbias)
        o_rwkv = _rwkv_branch(p_rw.reshape(b, s, SHIFT_WIDTH), shift_mu[l], rwkv_w0[l], rwkv_w2[l],
                              rwkv_a0[l], rwkv_a2[l], rwkv_g2[l], rwkv_k_k[l], rwkv_k_a[l],
                              rwkv_r_k[l], ln_x_g[l], ln_x_b[l])

        wr = jnp.zeros((d, LANES), F32)
        wr = wr.at[:, :N_GROUPS].set(router_group_w[l]).at[:, N_GROUPS:N_GROUPS + N_EXPERTS].set(router_expert_w[l])
        br = jnp.zeros((1, LANES), F32)
        br = br.at[0, :N_GROUPS].set(router_group_b[l]).at[0, N_GROUPS:N_GROUPS + N_EXPERTS].set(router_expert_b[l])
        x1, h2, route = _merge_out_router(
            o_fox.reshape(t, WIDTH), o_rwkv.reshape(t, WIDTH), gate, x, mod,
            w_out_fox[l].astype(BF16), w_out_rwkv[l].astype(BF16), w_o[l].astype(BF16), norm2_g[l], wr, br)

        tm_moe = 256
        dest3, blk_e, nused, rows = _moe_plan(route, tm_moe)
        xs = _moe_dispatch(h2, dest3, rows, tm=tm_moe)
        ys = _moe_experts(xs, blk_e, nused, exp_w1[l].astype(BF16), exp_w3[l].astype(BF16),
                          exp_w2[l].astype(BF16))
        out = _moe_combine_final(dest3, route, x1, mod, final_g, ys, s, tm=tm_moe)
    return out.reshape(b, s, d)
```

```python
import functools

import jax
import jax.numpy as jnp
from jax import lax
from jax.experimental import pallas as pl
from jax.experimental.pallas import tpu as pltpu

F32 = jnp.float32
BF16 = jnp.bfloat16
HIGHEST = lax.Precision.HIGHEST

HEADS = 8
HEAD_DIM = 64
WIDTH = HEADS * HEAD_DIM
DECAY_LORA = 64
AAA_LORA = 64
GATE_LORA = 128
SHIFT_WIDTH = 3 * WIDTH + DECAY_LORA + AAA_LORA + GATE_LORA
LN_X_EPS = 64e-5
NORM_EPS = 1e-6
N_GROUPS = 4
EXPERTS_PER_GROUP = 8
N_EXPERTS = N_GROUPS * EXPERTS_PER_GROUP
TOP_K = 2

LANES = 128
CHUNK = 64
RWKV_TILE = 128
MOE_ROWS = 256
VMEM_LIMIT = 48 * 1024 * 1024


def _cparams(sem):
    return pltpu.CompilerParams(dimension_semantics=sem, vmem_limit_bytes=VMEM_LIMIT)


def _dot(a, b):
    return jnp.dot(a.astype(BF16), b.astype(BF16), preferred_element_type=F32)


def _dot_nt(a, b):
    return lax.dot_general(a.astype(BF16), b.astype(BF16), (((1,), (1,)), ((), ())),
                           preferred_element_type=F32)


def _dot_tn(a, b):
    return lax.dot_general(a.astype(BF16), b.astype(BF16), (((0,), (0,)), ((), ())),
                           preferred_element_type=F32)


def _split_dot(x, w_bf16):
    hi = x.astype(BF16)
    lo = (x - hi.astype(F32)).astype(BF16)
    return (jnp.dot(hi, w_bf16, preferred_element_type=F32)
            + jnp.dot(lo, w_bf16, preferred_element_type=F32))


def _softplus(x):
    return jnp.maximum(x, 0.0) + jnp.log1p(jnp.exp(-jnp.abs(x)))


def _mod_kernel(c_ref, w_ref, b_ref, o_ref):
    c = c_ref[...]
    sc = c * jax.nn.sigmoid(c)
    o_ref[...] = jnp.dot(sc, w_ref[...], precision=HIGHEST, preferred_element_type=F32) + b_ref[...]


def _adaln_mod(c, ada_w, ada_b):
    b, d = c.shape
    n = ada_w.shape[1]
    rows = 8
    cp = jnp.zeros((rows, d), F32).at[:b].set(c)
    tn = 1024
    out = pl.pallas_call(
        _mod_kernel,
        out_shape=jax.ShapeDtypeStruct((rows, n), F32),
        grid=(n // tn,),
        in_specs=[pl.BlockSpec((rows, d), lambda j: (0, 0)),
                  pl.BlockSpec((d, tn), lambda j: (0, j)),
                  pl.BlockSpec((1, tn), lambda j: (0, j))],
        out_specs=pl.BlockSpec((rows, tn), lambda j: (0, j)),
        compiler_params=_cparams(("arbitrary",)),
        name="adaln_mod",
    )(cp, ada_w, ada_b.reshape(1, n))
    return out[:b].reshape(b, 6, d)


def _norm_mod_kernel(x_ref, g_ref, mod_ref, o_ref, *, shift_idx, scale_idx):
    x = x_ref[0]
    ms = jnp.mean(x * x, axis=-1, keepdims=True)
    y = x * lax.rsqrt(ms + NORM_EPS) * g_ref[...]
    scale = mod_ref[0, scale_idx:scale_idx + 1, :]
    shift = mod_ref[0, shift_idx:shift_idx + 1, :]
    o_ref[0] = (y * (1.0 + scale) + shift).astype(o_ref.dtype)


def _norm_mod(x, g, mod, shift_idx, scale_idx, tm=1024):
    b, s, d = x.shape
    return pl.pallas_call(
        functools.partial(_norm_mod_kernel, shift_idx=shift_idx, scale_idx=scale_idx),
        out_shape=jax.ShapeDtypeStruct((b, s, d), BF16),
        grid=(b, s // tm),
        in_specs=[pl.BlockSpec((1, tm, d), lambda i, j: (i, j, 0)),
                  pl.BlockSpec((1, d), lambda i, j: (0, 0)),
                  pl.BlockSpec((1, 6, d), lambda i, j: (i, 0, 0))],
        out_specs=pl.BlockSpec((1, tm, d), lambda i, j: (i, j, 0)),
        compiler_params=_cparams(("arbitrary", "arbitrary")),
        name="norm1_mod",
    )(x, g.reshape(1, d), mod)


def _mm_kernel(a_ref, w_ref, bias_ref, o_ref, *, act):
    r = jnp.dot(a_ref[...], w_ref[...], preferred_element_type=F32) + bias_ref[...]
    if act == "sigmoid":
        r = jax.nn.sigmoid(r)
    o_ref[...] = r.astype(o_ref.dtype)


def _matmul(a, w, out_dtype, bias=None, act=None, tm=2048, tn=512, name="proj"):
    t, k = a.shape
    n = w.shape[1]
    tn = min(tn, n)
    if bias is None:
        bias = jnp.zeros((1, n), F32)
    return pl.pallas_call(
        functools.partial(_mm_kernel, act=act),
        out_shape=jax.ShapeDtypeStruct((t, n), out_dtype),
        grid=(t // tm, n // tn),
        in_specs=[pl.BlockSpec((tm, k), lambda i, j: (i, 0)),
                  pl.BlockSpec((k, tn), lambda i, j: (0, j)),
                  pl.BlockSpec((1, tn), lambda i, j: (0, j))],
        out_specs=pl.BlockSpec((tm, tn), lambda i, j: (i, j)),
        compiler_params=_cparams(("arbitrary", "arbitrary")),
        name=name,
    )(a, w, bias)


LOG2E = 1.4426950408889634


def _split3(x):
    hi = x.astype(BF16)
    r1 = x - hi.astype(F32)
    mid = r1.astype(BF16)
    lo = (r1 - mid.astype(F32)).astype(BF16)
    return hi, mid, lo


def _fcum_kernel(h_ref, wf_ref, fb_ref, sel_ref, o_ref, carry_ref, *, ts):
    @pl.when(pl.program_id(1) == 0)
    def _():
        carry_ref[...] = jnp.zeros_like(carry_ref)

    f = jnp.dot(h_ref[0], wf_ref[...], preferred_element_type=F32) + fb_ref[...]
    lf = -_softplus(-f)
    ri = lax.broadcasted_iota(jnp.int32, (ts, ts), 0)
    ci = lax.broadcasted_iota(jnp.int32, (ts, ts), 1)
    tri = jnp.where(ri >= ci, 1.0, 0.0).astype(BF16)
    cum = carry_ref[...]
    for piece in _split3(lf):
        cum = cum + jnp.dot(tri, piece, preferred_element_type=F32)
    carry_ref[...] = cum[ts - 1:ts, :]
    out = jnp.zeros((ts, sel_ref.shape[2]), F32)
    for idx, piece in enumerate(_split3(cum * (-LOG2E))):
        out = out + jnp.dot(piece, sel_ref[idx], preferred_element_type=F32)
    o_ref[0] = out.astype(o_ref.dtype)


def _forget_bias(h, wf, fb, ts=512):
    b, s, d = h.shape
    pairs = HEADS // 2
    wf_p = jnp.zeros((d, LANES), F32).at[:, :HEADS].set(wf).astype(BF16)
    fb_p = jnp.zeros((1, LANES), F32).at[0, :HEADS].set(fb)
    hh = jnp.arange(HEADS)
    sel = jnp.zeros((3, LANES, pairs * LANES), F32)
    for piece in range(3):
        sel = sel.at[piece, hh, (hh // 2) * LANES + (hh % 2) * 3 + piece].set(1.0)
    return pl.pallas_call(
        functools.partial(_fcum_kernel, ts=ts),
        out_shape=jax.ShapeDtypeStruct((b, s, pairs * LANES), BF16),
        grid=(b, s // ts),
        in_specs=[pl.BlockSpec((1, ts, d), lambda i, j: (i, j, 0)),
                  pl.BlockSpec((d, LANES), lambda i, j: (0, 0)),
                  pl.BlockSpec((1, LANES), lambda i, j: (0, 0)),
                  pl.BlockSpec((3, LANES, pairs * LANES), lambda i, j: (0, 0, 0))],
        out_specs=pl.BlockSpec((1, ts, pairs * LANES), lambda i, j: (i, j, 0)),
        scratch_shapes=[pltpu.VMEM((1, LANES), F32)],
        compiler_params=_cparams(("arbitrary", "arbitrary")),
        name="forget_bias",
    )(h, wf_p, fb_p, sel.astype(BF16))


FOX_BQ = 1024
FOX_BK = 512


def _fox_kernel(first_ref, q_ref, k_ref, v_ref, a_ref, o_ref, m_ref, acc_ref, *, bq, bk):
    i = pl.program_id(2)
    lane = lax.broadcasted_iota(jnp.int32, (bq, LANES), 1)
    first = lane < HEAD_DIM
    qs = q_ref[0].astype(F32) * (HEAD_DIM ** -0.5 * LOG2E)
    aug0 = jnp.where(lane < 3, 1.0, 0.0)
    aug1 = jnp.where((lane >= 3) & (lane < 6), 1.0, 0.0)
    q01 = jnp.concatenate([jnp.concatenate([jnp.where(first, qs, 0.0), aug0], axis=1),
                           jnp.concatenate([jnp.where(first, 0.0, qs), aug1], axis=1)],
                          axis=0).astype(BF16)
    m_ref[...] = jnp.full_like(m_ref, -jnp.inf)
    acc_ref[...] = jnp.zeros_like(acc_ref)

    def logits(j):
        start = pl.multiple_of(j * bk, bk)
        kb = jnp.concatenate([k_ref[0, pl.ds(start, bk), :], a_ref[0, pl.ds(start, bk), :]], axis=1)
        return lax.dot_general(q01, kb, (((1,), (1,)), ((), ())), preferred_element_type=F32)

    def consume(j, z):
        start = pl.multiple_of(j * bk, bk)
        vb = v_ref[0, pl.ds(start, bk), :]
        m_prev = m_ref[...]
        m_new = jnp.maximum(m_prev, jnp.max(z, axis=1, keepdims=True))
        alpha = jnp.exp2(m_prev - m_new)
        p = jnp.exp2(z - jnp.concatenate([m_new] * (bk // LANES), axis=1)).astype(BF16)
        pv = jnp.concatenate([jnp.dot(p[:bq], vb[:, :LANES], preferred_element_type=F32),
                              jnp.dot(p[bq:], vb[:, LANES:], preferred_element_type=F32)], axis=0)
        acc_ref[...] = alpha * acc_ref[...] + pv
        m_ref[...] = m_new

    per_q = bq // bk
    n_full = i * per_q

    def body(j, carry):
        consume(j, logits(j))
        return carry

    lax.fori_loop(first_ref[pl.program_id(0), pl.program_id(1), i], n_full, body, 0)
    row = lax.broadcasted_iota(jnp.int32, (bq, bk), 0)
    col = lax.broadcasted_iota(jnp.int32, (bq, bk), 1)
    for d in range(per_q):
        keep = col + d * bk <= row
        z = logits(n_full + d)
        consume(n_full + d, jnp.where(jnp.concatenate([keep, keep], axis=0), z, -jnp.inf))
    acc = acc_ref[...]
    o = acc / pltpu.roll(acc, HEAD_DIM, 1)
    o_ref[0] = jnp.where(first, o[:bq], pltpu.roll(o[bq:], HEAD_DIM, 1)).astype(o_ref.dtype)


def _norm_bound_kernel(x_ref, sel_ref, o_ref):
    x = x_ref[...].astype(F32)
    ssq = _split_dot(x * x, sel_ref[...])
    o_ref[0] = jnp.broadcast_to(jnp.max(ssq, axis=0, keepdims=True), o_ref.shape[1:])


def _fox_first_block(qkv, fbias, bq, bk):
    b, s, _ = qkv.shape
    t = b * s
    nq, nk = s // bq, s // bk
    n_heads2 = 2 * HEADS
    sel = (jnp.arange(2 * WIDTH)[:, None] // HEAD_DIM == jnp.arange(LANES)[None, :]).astype(BF16)
    bounds = pl.pallas_call(
        _norm_bound_kernel,
        out_shape=jax.ShapeDtypeStruct((t // bk, 8, LANES), F32),
        grid=(t // bk,),
        in_specs=[pl.BlockSpec((bk, 2 * WIDTH), lambda i: (i, 0)),
                  pl.BlockSpec((2 * WIDTH, LANES), lambda i: (0, 0))],
        out_specs=pl.BlockSpec((1, 8, LANES), lambda i: (i, 0, 0)),
        compiler_params=_cparams(("arbitrary",)),
        name="fox_norm_bounds",
    )(qkv.reshape(t, -1), sel)
    nrm = jnp.sqrt(bounds[:, 0, :n_heads2]).reshape(b, nk, n_heads2) * 1.001
    qn = nrm[..., :HEADS] * (HEAD_DIM ** -0.5 * LOG2E * 1.01)
    kn = nrm[..., HEADS:]
    per_q = bq // bk
    qn_i = qn.reshape(b, nq, per_q, HEADS).max(axis=2)
    kn_i = kn.reshape(b, nq, per_q, HEADS).max(axis=2)
    kn_pre = lax.cummax(kn, axis=1)
    pairs = HEADS // 2
    fb = fbias.astype(F32).reshape(b, s, pairs, LANES)[..., :6].reshape(b, s, pairs, 2, 3).sum(-1)
    nb = fb.reshape(b, s, HEADS)
    nb_end = nb[:, bk - 1::bk]
    nb_start = nb[:, ::bq]
    gap = nb_start[:, :, None, :] - nb_end[:, None, :, :]
    need = qn_i[:, :, None, :] * (kn_pre[:, None, :, :] + kn_i[:, :, None, :]) + 152.0
    skip = (gap > need).reshape(b, nq, nk, pairs, 2).all(axis=-1)
    n_full = jnp.arange(nq) * per_q
    skip = skip & (jnp.arange(nk)[None, None, :, None] < n_full[None, :, None, None])
    first = jnp.argmin(skip, axis=2)
    return first.transpose(0, 2, 1).astype(jnp.int32)


def _fox_attention(qkv, fbias, bq=FOX_BQ, bk=FOX_BK):
    b, s, _ = qkv.shape
    pairs = HEADS // 2
    cb = WIDTH // LANES
    first = _fox_first_block(qkv, fbias, bq, bk)
    grid_spec = pltpu.PrefetchScalarGridSpec(
        num_scalar_prefetch=1,
        grid=(b, pairs, s // bq),
        in_specs=[pl.BlockSpec((1, bq, LANES), lambda bi, hp, i, fr: (bi, i, hp)),
                  pl.BlockSpec((1, s, LANES), lambda bi, hp, i, fr: (bi, 0, cb + hp)),
                  pl.BlockSpec((1, s, 2 * LANES), lambda bi, hp, i, fr: (bi, 0, cb + hp)),
                  pl.BlockSpec((1, s, LANES), lambda bi, hp, i, fr: (bi, 0, hp))],
        out_specs=pl.BlockSpec((1, bq, LANES), lambda bi, hp, i, fr: (bi, i, hp)),
        scratch_shapes=[pltpu.VMEM((2 * bq, LANES), F32), pltpu.VMEM((2 * bq, LANES), F32)],
    )
    return pl.pallas_call(
        functools.partial(_fox_kernel, bq=bq, bk=bk),
        out_shape=jax.ShapeDtypeStruct((b, s, WIDTH), BF16),
        grid_spec=grid_spec,
        compiler_params=_cparams(("arbitrary", "arbitrary", "arbitrary")),
        name="fox_attention",
    )(first, qkv, qkv, qkv, fbias)


PAIR = 2 * HEAD_DIM
GROUPS = WIDTH // PAIR


def _group(x, g):
    return x[:, g * PAIR:(g + 1) * PAIR]


def _pair_sum(x, bd2):
    return jnp.concatenate([_split_dot(_group(x, g), bd2) for g in range(GROUPS)], axis=1)


def _head_apply(mats, x, lane_first):
    rows = mats.shape[1]
    outs = []
    for g in range(GROUPS):
        res = _dot(mats[2 * g:2 * g + 2].reshape(2 * rows, rows), _group(x, g))
        outs.append(jnp.where(lane_first, res[:rows], res[rows:]))
    return jnp.concatenate(outs, axis=1)


def _rwkv_kernel(p_ref, mu_ref, w0_ref, a0_ref, kk_ref, ka_ref, rk_ref, lng_ref, lnb_ref,
                 wwa_ref, g2_ref, bd_ref, o_ref, st_ref, prev_ref):
    L = CHUNK

    @pl.when(pl.program_id(1) == 0)
    def _():
        st_ref[...] = jnp.zeros_like(st_ref)
        prev_ref[...] = jnp.zeros_like(prev_ref)

    p = p_ref[0]
    T = p.shape[0]
    rowi = lax.broadcasted_iota(jnp.int32, p.shape, 0)
    prev = jnp.where(rowi == 0, prev_ref[...], pltpu.roll(p, 1, 0))
    prev_ref[...] = p[T - 1:T, :]
    ps = p + (prev - p) * mu_ref[...]
    r = ps[:, 0:WIDTH]
    k = ps[:, WIDTH:2 * WIDTH]
    v = ps[:, 2 * WIDTH:3 * WIDTH]
    wa_in = ps[:, 3 * WIDTH:3 * WIDTH + DECAY_LORA + AAA_LORA]
    gd = ps[:, 3 * WIDTH + DECAY_LORA + AAA_LORA:]
    lane_wa = lax.broadcasted_iota(jnp.int32, wa_in.shape, 1)
    wa_act = jnp.where(lane_wa < DECAY_LORA, jnp.tanh(wa_in), wa_in)
    wa = _dot(wa_act, wwa_ref[...])
    log_w = -_softplus(-(w0_ref[...] + wa[:, :WIDTH])) - 0.5
    lw = -jnp.exp(log_w)
    a = jax.nn.sigmoid(a0_ref[...] + wa[:, WIDTH:])
    out_gate = _dot(jax.nn.sigmoid(gd), g2_ref[...])
    bd = bd_ref[...]
    kk0 = k * kk_ref[...]
    kk = kk0 * lax.rsqrt(jnp.maximum(_pair_sum(kk0 * kk0, bd), 1e-24))
    k2 = k * (1.0 + (a - 1.0) * ka_ref[...])
    av = -kk
    bv = kk * a

    n_sub = T // L
    rt_i = lax.broadcasted_iota(jnp.int32, (T, T), 0)
    ct_i = lax.broadcasted_iota(jnp.int32, (T, T), 1)
    tri_tile = (rt_i >= ct_i) & (rt_i // L == ct_i // L)
    cl = _split_dot_left(jnp.where(tri_tile, 1.0, 0.0).astype(BF16), lw)
    cl_end = jnp.concatenate([jnp.broadcast_to(cl[(c + 1) * L - 1:(c + 1) * L, :], (L, WIDTH))
                              for c in range(n_sub)], axis=0)
    at_all = av * jnp.exp(cl - lw)
    rt_all = r * jnp.exp(cl)
    einv = jnp.exp(-cl)
    bt_all = bv * einv
    kt_all = k2 * einv
    edec = jnp.exp(cl_end - cl)
    b_end_all = bv * edec
    k_end_all = k2 * edec

    ri = lax.broadcasted_iota(jnp.int32, (L, L), 0)
    ci = lax.broadcasted_iota(jnp.int32, (L, L), 1)
    tri_incl = ri >= ci
    tri_strict = ri > ci
    eye = jnp.where(ri == ci, 1.0, 0.0)
    lane_first = lax.broadcasted_iota(jnp.int32, (L, PAIR), 1) < HEAD_DIM
    qr = lax.broadcasted_iota(jnp.int32, (PAIR, PAIR), 0) < HEAD_DIM
    qc = lax.broadcasted_iota(jnp.int32, (PAIR, PAIR), 1) < HEAD_DIM
    same_head = qr == qc

    def bmm(x, y):
        return lax.dot_general(x.astype(BF16), y.astype(BF16), (((2,), (1,)), ((0,), (0,))),
                               preferred_element_type=F32)

    def chunk_terms(c):
        rows = slice(c * L, (c + 1) * L)
        at, rt, bt, kt, vc = at_all[rows], rt_all[rows], bt_all[rows], kt_all[rows], v[rows]
        sb_heads, sk_heads = [], []
        for g in range(GROUPS):
            at_g, rt_g = _group(at, g), _group(rt, g)
            lhs = jnp.concatenate([jnp.where(lane_first, at_g, 0.0), jnp.where(lane_first, rt_g, 0.0),
                                   jnp.where(lane_first, 0.0, at_g), jnp.where(lane_first, 0.0, rt_g)],
                                  axis=0).astype(BF16)
            sb_g = _dot_nt(lhs, _group(bt, g))
            sk_g = _dot_nt(lhs, _group(kt, g))
            for hh in range(2):
                sb_heads.append(sb_g[hh * 2 * L:(hh + 1) * 2 * L])
                sk_heads.append(sk_g[hh * 2 * L:(hh + 1) * 2 * L])
        sb = jnp.stack(sb_heads)
        sk = jnp.stack(sk_heads)
        n_ab = jnp.where(tri_strict, sb[:, :L, :], 0.0)
        a_ak = jnp.where(tri_strict, sk[:, :L, :], 0.0)
        a_rb = jnp.where(tri_incl, sb[:, L:, :], 0.0)
        a_rk = jnp.where(tri_incl, sk[:, L:, :], 0.0)
        tinv = eye + n_ab
        pw = n_ab
        span = 2
        while span < L:
            pw = bmm(pw, pw)
            tinv = tinv + bmm(tinv, pw)
            span *= 2
        av_term = _head_apply(a_ak, vc, lane_first)
        pm = _head_apply(tinv, at, lane_first)
        qm = _head_apply(tinv, av_term, lane_first)
        rkv = _head_apply(a_rk, vc, lane_first)
        return pm, qm, rkv, a_rb

    terms = [chunk_terms(c) for c in range(n_sub)]

    y_chunks = []
    for c in range(n_sub):
        rows = slice(c * L, (c + 1) * L)
        pm, qm, rkv, a_rb = terms[c]
        rt, vc, b_end, k_end = rt_all[rows], v[rows], b_end_all[rows], k_end_all[rows]
        gam_last = jnp.exp(cl[(c + 1) * L - 1:(c + 1) * L, :])
        u_parts, ys_parts = [], []
        for g in range(GROUPS):
            pr = _dot_nt(jnp.concatenate([_group(pm, g), _group(rt, g)], axis=0), st_ref[g])
            u_parts.append(pr[:L] + _group(qm, g))
            ys_parts.append(pr[L:])
        u = jnp.concatenate(u_parts, axis=1)
        y_chunks.append(jnp.concatenate(ys_parts, axis=1) + _head_apply(a_rb, u, lane_first) + rkv)
        for g in range(GROUPS):
            upd = _dot_tn(_group(u, g), _group(b_end, g)) + _dot_tn(_group(vc, g), _group(k_end, g))
            st_ref[g] = st_ref[g] * _group(gam_last, g) + jnp.where(same_head, upd, 0.0)
    y = jnp.concatenate(y_chunks, axis=0)

    inv_n = 1.0 / HEAD_DIM
    mean = _pair_sum(y, bd) * inv_n
    dlt = y - mean
    var = _pair_sum(dlt * dlt, bd) * inv_n
    yn = dlt * lax.rsqrt(var + LN_X_EPS) * lng_ref[...] + lnb_ref[...]
    bonus = _pair_sum(r * k2 * rk_ref[...], bd) * v
    o_ref[0] = ((yn + bonus) * out_gate).astype(o_ref.dtype)


def _split_dot_left(w_bf16, x):
    hi = x.astype(BF16)
    r1 = x - hi.astype(F32)
    mid = r1.astype(BF16)
    lo = (r1 - mid.astype(F32)).astype(BF16)
    return (jnp.dot(w_bf16, hi, preferred_element_type=F32)
            + jnp.dot(w_bf16, mid, preferred_element_type=F32)
            + jnp.dot(w_bf16, lo, preferred_element_type=F32))


def _rwkv_branch(p_rw, mu, w0, w2, a0, a2, g2, k_k, k_a, r_k, ln_g, ln_b):
    b, s, sw = p_rw.shape
    row = lambda t: t.reshape(1, -1).astype(F32)
    wwa = jnp.zeros((DECAY_LORA + AAA_LORA, 2 * WIDTH), F32)
    wwa = wwa.at[:DECAY_LORA, :WIDTH].set(w2).at[DECAY_LORA:, WIDTH:].set(a2).astype(BF16)
    hid = jnp.arange(PAIR) // HEAD_DIM
    bd = (hid[:, None] == hid[None, :]).astype(BF16)
    const = lambda shape: pl.BlockSpec(shape, lambda i, j: (0,) * len(shape))
    return pl.pallas_call(
        _rwkv_kernel,
        out_shape=jax.ShapeDtypeStruct((b, s, WIDTH), BF16),
        grid=(b, s // RWKV_TILE),
        in_specs=[pl.BlockSpec((1, RWKV_TILE, sw), lambda i, j: (i, j, 0)),
                  const((1, sw)), const((1, WIDTH)), const((1, WIDTH)), const((1, WIDTH)),
                  const((1, WIDTH)), const((1, WIDTH)), const((1, WIDTH)), const((1, WIDTH)),
                  const((DECAY_LORA + AAA_LORA, 2 * WIDTH)), const((GATE_LORA, WIDTH)),
                  const((PAIR, PAIR))],
        out_specs=pl.BlockSpec((1, RWKV_TILE, WIDTH), lambda i, j: (i, j, 0)),
        scratch_shapes=[pltpu.VMEM((GROUPS, PAIR, PAIR), F32), pltpu.VMEM((1, sw), F32)],
        compiler_params=_cparams(("arbitrary", "arbitrary")),
        name="rwkv7_scan",
    )(p_rw, row(mu), row(w0), row(a0), row(k_k), row(k_a), row(r_k), row(ln_g), row(ln_b),
      wwa, g2.astype(BF16), bd)


def _out_kernel(of_ref, orw_ref, gate_ref, x_ref, mod_ref, wof_ref, wor_ref, wo_ref, n2g_ref,
                wr_ref, br_ref, x1_ref, h2_ref, route_ref, counts_ref, cnt_ref):
    d = x_ref.shape[-1]
    gate = gate_ref[...].astype(F32)
    merged = (gate[:, :d] * jnp.dot(of_ref[...], wof_ref[...], preferred_element_type=F32)
              + gate[:, d:] * jnp.dot(orw_ref[...], wor_ref[...], preferred_element_type=F32))
    gate1 = mod_ref[0, 2:3, :]
    shift2 = mod_ref[0, 3:4, :]
    scale2 = mod_ref[0, 4:5, :]
    x1 = x_ref[...] + gate1 * jnp.dot(merged.astype(BF16), wo_ref[...], preferred_element_type=F32)
    x1_ref[...] = x1
    ms = jnp.mean(x1 * x1, axis=-1, keepdims=True)
    h2 = x1 * lax.rsqrt(ms + NORM_EPS) * n2g_ref[...] * (1.0 + scale2) + shift2
    h2_ref[...] = h2

    h2_hi = h2.astype(BF16)
    h2_lo = (h2 - h2_hi.astype(F32)).astype(BF16)
    logits = (jnp.dot(h2_hi, wr_ref[0], preferred_element_type=F32)
              + jnp.dot(h2_lo, wr_ref[0], preferred_element_type=F32)
              + jnp.dot(h2_hi, wr_ref[1], preferred_element_type=F32)) + br_ref[...]
    lane = lax.broadcasted_iota(jnp.int32, logits.shape, 1)
    neg = -jnp.inf
    big = jnp.int32(LANES)
    gl = jnp.where(lane < N_GROUPS, logits, neg)
    gmax = jnp.max(gl, axis=1, keepdims=True)
    gidx = jnp.min(jnp.where(gl == gmax, lane, big), axis=1, keepdims=True)
    g_p = 1.0 / jnp.sum(jnp.exp(gl - gmax), axis=1, keepdims=True)
    e_lane = lane - N_GROUPS
    in_grp = (e_lane >= 0) & (e_lane < N_EXPERTS) & ((e_lane // EXPERTS_PER_GROUP) == gidx)
    sel = jnp.where(in_grp, logits, neg)
    m1 = jnp.max(sel, axis=1, keepdims=True)
    i1 = jnp.min(jnp.where(sel == m1, lane, big), axis=1, keepdims=True)
    sel2 = jnp.where(lane == i1, neg, sel)
    m2 = jnp.max(sel2, axis=1, keepdims=True)
    i2 = jnp.min(jnp.where(sel2 == m2, lane, big), axis=1, keepdims=True)
    e21 = jnp.exp(m2 - m1)
    w_first = g_p / (1.0 + e21)
    w_second = g_p * e21 / (1.0 + e21)
    @pl.when(pl.program_id(0) == 0)
    def _():
        cnt_ref[...] = jnp.zeros_like(cnt_ref)

    tm = logits.shape[0]
    oh1 = lane == i1
    oh2 = lane == i2
    both = jnp.where(oh1 | oh2, 1.0, 0.0)
    before = (lax.broadcasted_iota(jnp.int32, (tm, tm), 0)
              > lax.broadcasted_iota(jnp.int32, (tm, tm), 1))
    seen = jnp.dot(jnp.where(before, 1.0, 0.0).astype(BF16), both.astype(BF16),
                   preferred_element_type=F32) + cnt_ref[...]
    rank1 = jnp.sum(jnp.where(oh1, seen, 0.0), axis=1, keepdims=True)
    rank2 = jnp.sum(jnp.where(oh2, seen, 0.0), axis=1, keepdims=True)
    cnt_ref[...] = cnt_ref[...] + jnp.sum(both, axis=0, keepdims=True)
    counts_ref[...] = jnp.broadcast_to(cnt_ref[...], counts_ref.shape)

    route = jnp.where(lane == 0, (i1 - N_GROUPS).astype(F32),
                      jnp.where(lane == 1, (i2 - N_GROUPS).astype(F32),
                                jnp.where(lane == 2, w_first,
                                          jnp.where(lane == 3, w_second,
                                                    jnp.where(lane == 4, rank1,
                                                              jnp.where(lane == 5, rank2, 0.0))))))
    route_ref[...] = route


def _merge_out_router(o_fox, o_rw, gate, x, mod, wof, wor, wo, n2g, wr, br, tm=512):
    b, s, d = x.shape
    t = b * s
    spb = s // tm
    rowspec = lambda w: pl.BlockSpec((tm, w), lambda i: (i, 0))
    const = lambda shape: pl.BlockSpec(shape, lambda i: (0,) * len(shape))
    return pl.pallas_call(
        _out_kernel,
        out_shape=(jax.ShapeDtypeStruct((t, d), F32), jax.ShapeDtypeStruct((t, d), F32),
                   jax.ShapeDtypeStruct((t, LANES), F32), jax.ShapeDtypeStruct((8, LANES), F32)),
        grid=(t // tm,),
        in_specs=[rowspec(WIDTH), rowspec(WIDTH), rowspec(2 * d), rowspec(d),
                  pl.BlockSpec((1, 6, d), lambda i: (i // spb, 0, 0)),
                  const((WIDTH, d)), const((WIDTH, d)), const((d, d)), const((1, d)),
                  const((2, d, LANES)), const((1, LANES))],
        out_specs=(rowspec(d), rowspec(d), rowspec(LANES), const((8, LANES))),
        scratch_shapes=[pltpu.VMEM((1, LANES), F32)],
        compiler_params=_cparams(("arbitrary",)),
        name="merge_out_router",
    )(o_fox, o_rw, gate, x.reshape(t, d), mod, wof, wor, wo, n2g.reshape(1, d), wr, br)


def _dispatch_kernel(zstart_ref, dest_ref, h_ref, xs_ref, zero_ref, sem, *, tm):
    @pl.when(pl.program_id(0) == 0)
    def _():
        zero_ref[...] = jnp.zeros_like(zero_ref)
        for e in range(N_EXPERTS):
            zrow = pl.multiple_of(zstart_ref[e], MOE_ROWS)
            pltpu.make_async_copy(zero_ref, xs_ref.at[pl.ds(zrow, MOE_ROWS)], sem).start()
        for e in range(N_EXPERTS):
            pltpu.make_async_copy(zero_ref, xs_ref.at[pl.ds(0, MOE_ROWS)], sem).wait()
        for phase in ("start", "wait"):
            for e in range(N_EXPERTS):
                trow = pl.multiple_of(zstart_ref[N_EXPERTS] + e * MOE_ROWS, MOE_ROWS)

                @pl.when(trow < xs_ref.shape[0])
                def _():
                    tail = pltpu.make_async_copy(zero_ref, xs_ref.at[pl.ds(trow, MOE_ROWS)], sem)
                    tail.start() if phase == "start" else tail.wait()

    def issue(r, carry):
        for kk in range(TOP_K):
            dst = dest_ref[0, 0, TOP_K * r + kk]
            pltpu.make_async_copy(h_ref.at[pl.ds(r, 1)], xs_ref.at[pl.ds(dst, 1)], sem).start()
        return carry

    lax.fori_loop(0, tm, issue, 0, unroll=8)
    for kk in range(TOP_K):
        pltpu.make_async_copy(h_ref, xs_ref.at[pl.ds(0, tm)], sem).wait()


def _moe_dispatch(h2, dest3, zstart, rows, tm=256):
    t, d = h2.shape
    grid_spec = pltpu.PrefetchScalarGridSpec(
        num_scalar_prefetch=1,
        grid=(t // tm,),
        in_specs=[pl.BlockSpec((1, 1, TOP_K * tm), lambda i, zs: (i, 0, 0), memory_space=pltpu.SMEM),
                  pl.BlockSpec((tm, d), lambda i, zs: (i, 0))],
        out_specs=pl.BlockSpec(memory_space=pl.ANY),
        scratch_shapes=[pltpu.VMEM((MOE_ROWS, d), F32), pltpu.SemaphoreType.DMA(())],
    )
    return pl.pallas_call(
        functools.partial(_dispatch_kernel, tm=tm),
        out_shape=jax.ShapeDtypeStruct((rows, d), F32),
        grid_spec=grid_spec,
        compiler_params=_cparams(("arbitrary",)),
        name="moe_dispatch",
    )(zstart, dest3, h2)


def _expert_kernel(blk_e_ref, nused_ref, xs_ref, w1_ref, w3_ref, w2_ref, o_ref):
    del blk_e_ref
    live = pl.program_id(0) * MOE_ROWS < nused_ref[0]

    @pl.when(live)
    def _():
        xb = xs_ref[...].astype(BF16)
        h1 = jnp.dot(xb, w1_ref[0], preferred_element_type=F32)
        h3 = jnp.dot(xb, w3_ref[0], preferred_element_type=F32)
        hh = (h1 * jax.nn.sigmoid(h1)) * h3
        o_ref[...] = jnp.dot(hh.astype(BF16), w2_ref[0], preferred_element_type=F32)

    @pl.when(jnp.logical_not(live))
    def _():
        o_ref[...] = jnp.zeros_like(o_ref)


def _moe_experts(xs, blk_e, nused, w1, w3, w2):
    rows, d = xs.shape
    de = w1.shape[-1]
    grid_spec = pltpu.PrefetchScalarGridSpec(
        num_scalar_prefetch=2,
        grid=(rows // MOE_ROWS,),
        in_specs=[pl.BlockSpec((MOE_ROWS, d), lambda i, be, nu: (i, 0)),
                  pl.BlockSpec((1, d, de), lambda i, be, nu: (be[i], 0, 0)),
                  pl.BlockSpec((1, d, de), lambda i, be, nu: (be[i], 0, 0)),
                  pl.BlockSpec((1, de, d), lambda i, be, nu: (be[i], 0, 0))],
        out_specs=pl.BlockSpec((MOE_ROWS, d), lambda i, be, nu: (i, 0)),
    )
    return pl.pallas_call(
        _expert_kernel,
        out_shape=jax.ShapeDtypeStruct((rows, d), F32),
        grid_spec=grid_spec,
        compiler_params=_cparams(("arbitrary",)),
        name="moe_experts",
    )(blk_e, nused, xs, w1, w3, w2)


def _final_kernel(dest_ref, route_ref, x1_ref, mod_ref, fg_ref, ys_ref, o_ref, buf_ref, sem, *, tm):
    def issue(r, carry):
        for kk in range(TOP_K):
            src = dest_ref[0, 0, TOP_K * r + kk]
            pltpu.make_async_copy(ys_ref.at[pl.ds(src, 1)], buf_ref.at[kk, pl.ds(r, 1)], sem).start()
        return carry

    lax.fori_loop(0, tm, issue, 0, unroll=8)
    for kk in range(TOP_K):
        pltpu.make_async_copy(ys_ref.at[pl.ds(0, tm)], buf_ref.at[kk], sem).wait()

    route = route_ref[...]
    y = route[:, 2:3] * buf_ref[0] + route[:, 3:4] * buf_ref[1]
    gate2 = mod_ref[0, 5:6, :]
    x2 = x1_ref[...] + gate2 * y
    ms = jnp.mean(x2 * x2, axis=-1, keepdims=True)
    o_ref[...] = x2 * lax.rsqrt(ms + NORM_EPS) * fg_ref[...]


def _moe_combine_final(dest3, route, x1, mod, final_g, ys, s, tm=256):
    t, d = x1.shape
    spb = s // tm
    return pl.pallas_call(
        functools.partial(_final_kernel, tm=tm),
        out_shape=jax.ShapeDtypeStruct((t, d), F32),
        grid=(t // tm,),
        in_specs=[pl.BlockSpec((1, 1, TOP_K * tm), lambda i: (i, 0, 0), memory_space=pltpu.SMEM),
                  pl.BlockSpec((tm, LANES), lambda i: (i, 0)),
                  pl.BlockSpec((tm, d), lambda i: (i, 0)),
                  pl.BlockSpec((1, 6, d), lambda i: (i // spb, 0, 0)),
                  pl.BlockSpec((1, d), lambda i: (0, 0)),
                  pl.BlockSpec(memory_space=pl.ANY)],
        out_specs=pl.BlockSpec((tm, d), lambda i: (i, 0)),
        scratch_shapes=[pltpu.VMEM((TOP_K, tm, d), F32), pltpu.SemaphoreType.DMA(())],
        compiler_params=_cparams(("arbitrary",)),
        name="moe_combine_final",
    )(dest3, route, x1, mod, final_g.reshape(1, d), ys)


def _moe_plan(route, counts, tm):
    t = route.shape[0]
    m = t * TOP_K
    flat_e = route[:, :TOP_K].astype(jnp.int32).reshape(m)
    rank = route[:, 4:4 + TOP_K].astype(jnp.int32).reshape(m)
    counts = counts[0, N_GROUPS:N_GROUPS + N_EXPERTS].astype(jnp.int32)
    padded = (counts + MOE_ROWS - 1) // MOE_ROWS * MOE_ROWS
    pad_end = jnp.cumsum(padded)
    pad_start = pad_end - padded
    experts = jnp.arange(N_EXPERTS, dtype=jnp.int32)
    start_of = jnp.sum(jnp.where(flat_e[:, None] == experts[None, :], pad_start[None, :], 0), axis=1)
    dest = (start_of + rank).astype(jnp.int32)
    n_blocks = m // MOE_ROWS + N_EXPERTS
    blk_start = jnp.arange(n_blocks, dtype=jnp.int32) * MOE_ROWS
    blk_e = jnp.minimum(jnp.sum(pad_end[None, :] <= blk_start[:, None], axis=1), N_EXPERTS - 1).astype(jnp.int32)
    nused = pad_end[-1:].astype(jnp.int32)
    zstart = jnp.concatenate([jnp.maximum(pad_end - MOE_ROWS, 0), pad_end[-1:]]).astype(jnp.int32)
    return dest.reshape(t // tm, 1, TOP_K * tm), blk_e, nused, zstart, n_blocks * MOE_ROWS


def kernel(x, c, ada_w, ada_b, norm1_g, w_in, fox_forget_b, shift_mu, rwkv_w0, rwkv_w2, rwkv_a0, rwkv_a2, rwkv_g2, rwkv_k_k, rwkv_k_a, rwkv_r_k, ln_x_g, ln_x_b, w_out_fox, w_out_rwkv, w_o, norm2_g, router_group_w, router_group_b, router_expert_w, router_expert_b, exp_w1, exp_w3, exp_w2, final_g):
    b, s, d = x.shape
    t = b * s
    assert ada_w.shape[0] == 1, "the final norm is fused into the last layer's combine; one layer is laid out"
    for l in range(1):
        mod = _adaln_mod(c, ada_w[l], ada_b[l])

        h = _norm_mod(x, norm1_g[l], mod, shift_idx=0, scale_idx=1)
        h2d = h.reshape(t, d)
        w = w_in[l]
        o_f = 3 * WIDTH
        o_rw = o_f + HEADS
        o_g = o_rw + SHIFT_WIDTH
        perm = jnp.argsort(fox_forget_b[l])
        by_head = lambda m: m.reshape(d, HEADS, HEAD_DIM)[:, perm]
        wq = by_head(w[:, :WIDTH]).reshape(d, WIDTH)
        wk = by_head(w[:, WIDTH:2 * WIDTH]).reshape(d, WIDTH)
        wv = by_head(w[:, 2 * WIDTH:o_f])
        wv = jnp.concatenate([wv, jnp.zeros_like(wv)], axis=2).reshape(d, 2 * WIDTH)
        ones_cols = (jnp.arange(2 * WIDTH) % LANES >= HEAD_DIM).astype(F32)
        qkv_bias = jnp.concatenate([jnp.zeros((2 * WIDTH,), F32), ones_cols]).reshape(1, 4 * WIDTH)
        qkv = _matmul(h2d, jnp.concatenate([wq, wk, wv], axis=1).astype(BF16), BF16,
                      bias=qkv_bias, name="proj_qkv")
        p_rw = _matmul(h2d, w[:, o_rw:o_g].astype(BF16), F32, tn=896, name="proj_rwkv")
        gate = _matmul(h2d, w[:, o_g:].astype(BF16), BF16, act="sigmoid", name="proj_gate")
        fbias = _forget_bias(h, w[:, o_f:o_rw][:, perm], fox_forget_b[l][perm])
        o_fox = _fox_attention(qkv.reshape(b, s, 4 * WIDTH), fbias)
        w_of = w_out_fox[l].reshape(HEADS, HEAD_DIM, d)[perm].reshape(WIDTH, d)
        o_rwkv = _rwkv_branch(p_rw.reshape(b, s, SHIFT_WIDTH), shift_mu[l], rwkv_w0[l], rwkv_w2[l],
                              rwkv_a0[l], rwkv_a2[l], rwkv_g2[l], rwkv_k_k[l], rwkv_k_a[l],
                              rwkv_r_k[l], ln_x_g[l], ln_x_b[l])

        wr = jnp.zeros((d, LANES), F32)
        wr = wr.at[:, :N_GROUPS].set(router_group_w[l]).at[:, N_GROUPS:N_GROUPS + N_EXPERTS].set(router_expert_w[l])
        br = jnp.zeros((1, LANES), F32)
        br = br.at[0, :N_GROUPS].set(router_group_b[l]).at[0, N_GROUPS:N_GROUPS + N_EXPERTS].set(router_expert_b[l])
        wr_hi = wr.astype(BF16)
        wr_lo = (wr - wr_hi.astype(F32)).astype(BF16)
        x1, h2, route, counts = _merge_out_router(
            o_fox.reshape(t, WIDTH), o_rwkv.reshape(t, WIDTH), gate, x, mod,
            w_of.astype(BF16), w_out_rwkv[l].astype(BF16), w_o[l].astype(BF16), norm2_g[l],
            jnp.stack([wr_hi, wr_lo]), br)

        tm_moe = 256
        dest3, blk_e, nused, zstart, rows = _moe_plan(route, counts, tm_moe)
        xs = _moe_dispatch(h2, dest3, zstart, rows, tm=tm_moe)
        ys = _moe_experts(xs, blk_e, nused, exp_w1[l].astype(BF16), exp_w3[l].astype(BF16),
                          exp_w2[l].astype(BF16))
        out = _moe_combine_final(dest3, route, x1, mod, final_g, ys, s, tm=tm_moe)
    return out.reshape(b, s, d)
```

```python
import functools

import jax
import jax.numpy as jnp
from jax import lax
from jax.experimental import pallas as pl
from jax.experimental.pallas import tpu as pltpu

F32 = jnp.float32
BF16 = jnp.bfloat16
HIGHEST = lax.Precision.HIGHEST

HEADS = 8
HEAD_DIM = 64
WIDTH = HEADS * HEAD_DIM
DECAY_LORA = 64
AAA_LORA = 64
GATE_LORA = 128
SHIFT_WIDTH = 3 * WIDTH + DECAY_LORA + AAA_LORA + GATE_LORA
LN_X_EPS = 64e-5
NORM_EPS = 1e-6
N_GROUPS = 4
EXPERTS_PER_GROUP = 8
N_EXPERTS = N_GROUPS * EXPERTS_PER_GROUP
TOP_K = 2

LANES = 128
CHUNK = 64
RWKV_TILE = 128
MOE_ROWS = 256
VMEM_LIMIT = 48 * 1024 * 1024


def _cparams(sem):
    return pltpu.CompilerParams(dimension_semantics=sem, vmem_limit_bytes=VMEM_LIMIT)


def _dot(a, b):
    return jnp.dot(a.astype(BF16), b.astype(BF16), preferred_element_type=F32)


def _dot_nt(a, b):
    return lax.dot_general(a.astype(BF16), b.astype(BF16), (((1,), (1,)), ((), ())),
                           preferred_element_type=F32)


def _dot_tn(a, b):
    return lax.dot_general(a.astype(BF16), b.astype(BF16), (((0,), (0,)), ((), ())),
                           preferred_element_type=F32)


def _split_dot(x, w_bf16):
    hi = x.astype(BF16)
    lo = (x - hi.astype(F32)).astype(BF16)
    return (jnp.dot(hi, w_bf16, preferred_element_type=F32)
            + jnp.dot(lo, w_bf16, preferred_element_type=F32))


def _softplus(x):
    return jnp.maximum(x, 0.0) + jnp.log1p(jnp.exp(-jnp.abs(x)))


def _mod_kernel(c_ref, w_ref, b_ref, o_ref):
    c = c_ref[...]
    sc = c * jax.nn.sigmoid(c)
    o_ref[...] = jnp.dot(sc, w_ref[...], precision=HIGHEST, preferred_element_type=F32) + b_ref[...]


def _adaln_mod(c, ada_w, ada_b):
    b, d = c.shape
    n = ada_w.shape[1]
    rows = 8
    cp = jnp.zeros((rows, d), F32).at[:b].set(c)
    tn = 1024
    out = pl.pallas_call(
        _mod_kernel,
        out_shape=jax.ShapeDtypeStruct((rows, n), F32),
        grid=(n // tn,),
        in_specs=[pl.BlockSpec((rows, d), lambda j: (0, 0)),
                  pl.BlockSpec((d, tn), lambda j: (0, j)),
                  pl.BlockSpec((1, tn), lambda j: (0, j))],
        out_specs=pl.BlockSpec((rows, tn), lambda j: (0, j)),
        compiler_params=_cparams(("arbitrary",)),
        name="adaln_mod",
    )(cp, ada_w, ada_b.reshape(1, n))
    return out[:b].reshape(b, 6, d)


def _norm_mod_kernel(x_ref, g_ref, mod_ref, o_ref, *, shift_idx, scale_idx):
    x = x_ref[0]
    ms = jnp.mean(x * x, axis=-1, keepdims=True)
    y = x * lax.rsqrt(ms + NORM_EPS) * g_ref[...]
    scale = mod_ref[0, scale_idx:scale_idx + 1, :]
    shift = mod_ref[0, shift_idx:shift_idx + 1, :]
    o_ref[0] = (y * (1.0 + scale) + shift).astype(o_ref.dtype)


def _norm_mod(x, g, mod, shift_idx, scale_idx, tm=1024):
    b, s, d = x.shape
    return pl.pallas_call(
        functools.partial(_norm_mod_kernel, shift_idx=shift_idx, scale_idx=scale_idx),
        out_shape=jax.ShapeDtypeStruct((b, s, d), BF16),
        grid=(b, s // tm),
        in_specs=[pl.BlockSpec((1, tm, d), lambda i, j: (i, j, 0)),
                  pl.BlockSpec((1, d), lambda i, j: (0, 0)),
                  pl.BlockSpec((1, 6, d), lambda i, j: (i, 0, 0))],
        out_specs=pl.BlockSpec((1, tm, d), lambda i, j: (i, j, 0)),
        compiler_params=_cparams(("arbitrary", "arbitrary")),
        name="norm1_mod",
    )(x, g.reshape(1, d), mod)


def _mm_kernel(a_ref, w_ref, bias_ref, o_ref, *, act):
    r = jnp.dot(a_ref[...], w_ref[...], preferred_element_type=F32) + bias_ref[...]
    if act == "sigmoid":
        r = jax.nn.sigmoid(r)
    o_ref[...] = r.astype(o_ref.dtype)


def _matmul(a, w, out_dtype, bias=None, act=None, tm=2048, tn=512, name="proj"):
    t, k = a.shape
    n = w.shape[1]
    tn = min(tn, n)
    if bias is None:
        bias = jnp.zeros((1, n), F32)
    return pl.pallas_call(
        functools.partial(_mm_kernel, act=act),
        out_shape=jax.ShapeDtypeStruct((t, n), out_dtype),
        grid=(t // tm, n // tn),
        in_specs=[pl.BlockSpec((tm, k), lambda i, j: (i, 0)),
                  pl.BlockSpec((k, tn), lambda i, j: (0, j)),
                  pl.BlockSpec((1, tn), lambda i, j: (0, j))],
        out_specs=pl.BlockSpec((tm, tn), lambda i, j: (i, j)),
        compiler_params=_cparams(("arbitrary", "arbitrary")),
        name=name,
    )(a, w, bias)


LOG2E = 1.4426950408889634


def _split3(x):
    hi = x.astype(BF16)
    r1 = x - hi.astype(F32)
    mid = r1.astype(BF16)
    lo = (r1 - mid.astype(F32)).astype(BF16)
    return hi, mid, lo


def _fcum_kernel(h_ref, wf_ref, fb_ref, sel_ref, o_ref, carry_ref, *, ts):
    @pl.when(pl.program_id(1) == 0)
    def _():
        carry_ref[...] = jnp.zeros_like(carry_ref)

    f = jnp.dot(h_ref[0], wf_ref[...], preferred_element_type=F32) + fb_ref[...]
    lf = -_softplus(-f)
    ri = lax.broadcasted_iota(jnp.int32, (ts, ts), 0)
    ci = lax.broadcasted_iota(jnp.int32, (ts, ts), 1)
    tri = jnp.where(ri >= ci, 1.0, 0.0).astype(BF16)
    cum = carry_ref[...]
    for piece in _split3(lf):
        cum = cum + jnp.dot(tri, piece, preferred_element_type=F32)
    carry_ref[...] = cum[ts - 1:ts, :]
    out = jnp.zeros((ts, sel_ref.shape[2]), F32)
    for idx, piece in enumerate(_split3(cum * (-LOG2E))):
        out = out + jnp.dot(piece, sel_ref[idx], preferred_element_type=F32)
    o_ref[0] = out.astype(o_ref.dtype)


def _forget_bias(h, wf, fb, ts=512):
    b, s, d = h.shape
    pairs = HEADS // 2
    wf_p = jnp.zeros((d, LANES), F32).at[:, :HEADS].set(wf).astype(BF16)
    fb_p = jnp.zeros((1, LANES), F32).at[0, :HEADS].set(fb)
    hh = jnp.arange(HEADS)
    sel = jnp.zeros((3, LANES, pairs * LANES), F32)
    for piece in range(3):
        sel = sel.at[piece, hh, (hh // 2) * LANES + (hh % 2) * 3 + piece].set(1.0)
    return pl.pallas_call(
        functools.partial(_fcum_kernel, ts=ts),
        out_shape=jax.ShapeDtypeStruct((b, s, pairs * LANES), BF16),
        grid=(b, s // ts),
        in_specs=[pl.BlockSpec((1, ts, d), lambda i, j: (i, j, 0)),
                  pl.BlockSpec((d, LANES), lambda i, j: (0, 0)),
                  pl.BlockSpec((1, LANES), lambda i, j: (0, 0)),
                  pl.BlockSpec((3, LANES, pairs * LANES), lambda i, j: (0, 0, 0))],
        out_specs=pl.BlockSpec((1, ts, pairs * LANES), lambda i, j: (i, j, 0)),
        scratch_shapes=[pltpu.VMEM((1, LANES), F32)],
        compiler_params=_cparams(("arbitrary", "arbitrary")),
        name="forget_bias",
    )(h, wf_p, fb_p, sel.astype(BF16))


FOX_BQ = 1024
FOX_BK = 512


def _fox_kernel(first_ref, q_ref, k_ref, v_ref, a_ref, o_ref, m_ref, acc_ref, *, bq, bk):
    i = pl.program_id(2)
    lane = lax.broadcasted_iota(jnp.int32, (bq, LANES), 1)
    first = lane < HEAD_DIM
    qs = q_ref[0].astype(F32) * (HEAD_DIM ** -0.5 * LOG2E)
    aug0 = jnp.where(lane < 3, 1.0, 0.0)
    aug1 = jnp.where((lane >= 3) & (lane < 6), 1.0, 0.0)
    q01 = jnp.concatenate([jnp.concatenate([jnp.where(first, qs, 0.0), aug0], axis=1),
                           jnp.concatenate([jnp.where(first, 0.0, qs), aug1], axis=1)],
                          axis=0).astype(BF16)
    m_ref[...] = jnp.full_like(m_ref, -jnp.inf)
    acc_ref[...] = jnp.zeros_like(acc_ref)

    def logits(j):
        start = pl.multiple_of(j * bk, bk)
        kb = jnp.concatenate([k_ref[0, pl.ds(start, bk), :], a_ref[0, pl.ds(start, bk), :]], axis=1)
        return lax.dot_general(q01, kb, (((1,), (1,)), ((), ())), preferred_element_type=F32)

    def consume(j, z):
        start = pl.multiple_of(j * bk, bk)
        vb = v_ref[0, pl.ds(start, bk), :]
        m_prev = m_ref[...]
        m_new = jnp.maximum(m_prev, jnp.max(z, axis=1, keepdims=True))
        alpha = jnp.exp2(m_prev - m_new)
        p = jnp.exp2(z - jnp.concatenate([m_new] * (bk // LANES), axis=1)).astype(BF16)
        pv = jnp.concatenate([jnp.dot(p[:bq], vb[:, :LANES], preferred_element_type=F32),
                              jnp.dot(p[bq:], vb[:, LANES:], preferred_element_type=F32)], axis=0)
        acc_ref[...] = alpha * acc_ref[...] + pv
        m_ref[...] = m_new

    per_q = bq // bk
    n_full = i * per_q

    def body(j, carry):
        consume(j, logits(j))
        return carry

    lax.fori_loop(first_ref[pl.program_id(0), pl.program_id(1), i], n_full, body, 0)
    row = lax.broadcasted_iota(jnp.int32, (bq, bk), 0)
    col = lax.broadcasted_iota(jnp.int32, (bq, bk), 1)
    for d in range(per_q):
        keep = col + d * bk <= row
        z = logits(n_full + d)
        consume(n_full + d, jnp.where(jnp.concatenate([keep, keep], axis=0), z, -jnp.inf))
    acc = acc_ref[...]
    o = acc / pltpu.roll(acc, HEAD_DIM, 1)
    o_ref[0] = jnp.where(first, o[:bq], pltpu.roll(o[bq:], HEAD_DIM, 1)).astype(o_ref.dtype)


def _norm_bound_kernel(x_ref, sel_ref, o_ref):
    x = x_ref[...].astype(F32)
    ssq = _split_dot(x * x, sel_ref[...])
    o_ref[0] = jnp.broadcast_to(jnp.max(ssq, axis=0, keepdims=True), o_ref.shape[1:])


def _fox_first_block(qkv, fbias, bq, bk):
    b, s, _ = qkv.shape
    t = b * s
    nq, nk = s // bq, s // bk
    n_heads2 = 2 * HEADS
    sel = (jnp.arange(2 * WIDTH)[:, None] // HEAD_DIM == jnp.arange(LANES)[None, :]).astype(BF16)
    bounds = pl.pallas_call(
        _norm_bound_kernel,
        out_shape=jax.ShapeDtypeStruct((t // bk, 8, LANES), F32),
        grid=(t // bk,),
        in_specs=[pl.BlockSpec((bk, 2 * WIDTH), lambda i: (i, 0)),
                  pl.BlockSpec((2 * WIDTH, LANES), lambda i: (0, 0))],
        out_specs=pl.BlockSpec((1, 8, LANES), lambda i: (i, 0, 0)),
        compiler_params=_cparams(("arbitrary",)),
        name="fox_norm_bounds",
    )(qkv.reshape(t, -1), sel)
    nrm = jnp.sqrt(bounds[:, 0, :n_heads2]).reshape(b, nk, n_heads2) * 1.001
    qn = nrm[..., :HEADS] * (HEAD_DIM ** -0.5 * LOG2E * 1.01)
    kn = nrm[..., HEADS:]
    per_q = bq // bk
    qn_i = qn.reshape(b, nq, per_q, HEADS).max(axis=2)
    kn_i = kn.reshape(b, nq, per_q, HEADS).max(axis=2)
    kn_pre = lax.cummax(kn, axis=1)
    pairs = HEADS // 2
    def bias_rows(rows):
        pieces = rows.astype(F32).reshape(b, -1, pairs, LANES)[..., :6].reshape(b, -1, pairs, 2, 3)
        return pieces.sum(-1).reshape(b, -1, HEADS)

    nb_end = bias_rows(fbias[:, bk - 1::bk])
    nb_start = bias_rows(fbias[:, ::bq])
    gap = nb_start[:, :, None, :] - nb_end[:, None, :, :]
    need = qn_i[:, :, None, :] * (kn_pre[:, None, :, :] + kn_i[:, :, None, :]) + 152.0
    skip = (gap > need).reshape(b, nq, nk, pairs, 2).all(axis=-1)
    n_full = jnp.arange(nq) * per_q
    skip = skip & (jnp.arange(nk)[None, None, :, None] < n_full[None, :, None, None])
    first = jnp.argmin(skip, axis=2)
    return first.transpose(0, 2, 1).astype(jnp.int32)


def _fox_attention(qkv, fbias, bq=FOX_BQ, bk=FOX_BK):
    b, s, _ = qkv.shape
    pairs = HEADS // 2
    cb = WIDTH // LANES
    first = _fox_first_block(qkv, fbias, bq, bk)
    grid_spec = pltpu.PrefetchScalarGridSpec(
        num_scalar_prefetch=1,
        grid=(b, pairs, s // bq),
        in_specs=[pl.BlockSpec((1, bq, LANES), lambda bi, hp, i, fr: (bi, i, hp)),
                  pl.BlockSpec((1, s, LANES), lambda bi, hp, i, fr: (bi, 0, cb + hp)),
                  pl.BlockSpec((1, s, 2 * LANES), lambda bi, hp, i, fr: (bi, 0, cb + hp)),
                  pl.BlockSpec((1, s, LANES), lambda bi, hp, i, fr: (bi, 0, hp))],
        out_specs=pl.BlockSpec((1, bq, LANES), lambda bi, hp, i, fr: (bi, i, hp)),
        scratch_shapes=[pltpu.VMEM((2 * bq, LANES), F32), pltpu.VMEM((2 * bq, LANES), F32)],
    )
    return pl.pallas_call(
        functools.partial(_fox_kernel, bq=bq, bk=bk),
        out_shape=jax.ShapeDtypeStruct((b, s, WIDTH), BF16),
        grid_spec=grid_spec,
        compiler_params=_cparams(("arbitrary", "arbitrary", "arbitrary")),
        name="fox_attention",
    )(first, qkv, qkv, qkv, fbias)


PAIR = 2 * HEAD_DIM
GROUPS = WIDTH // PAIR


def _group(x, g):
    return x[:, g * PAIR:(g + 1) * PAIR]


def _head_sum(x, bd2):
    return jnp.concatenate([_dot(_group(x, g), bd2) for g in range(GROUPS)], axis=1)


def _head_apply(mats, x, lane_first):
    rows = mats.shape[1]
    outs = []
    for g in range(GROUPS):
        res = _dot(mats[2 * g:2 * g + 2].reshape(2 * rows, rows), _group(x, g))
        outs.append(jnp.where(lane_first, res[:rows], res[rows:]))
    return jnp.concatenate(outs, axis=1)


def _rwkv_kernel(p_ref, mu_ref, w0_ref, a0_ref, kk_ref, ka_ref, rk_ref, lng_ref, lnb_ref,
                 wwa_ref, g2_ref, bd_ref, o_ref, st_ref, prev_ref):
    L = CHUNK

    @pl.when(pl.program_id(1) == 0)
    def _():
        st_ref[...] = jnp.zeros_like(st_ref)
        prev_ref[...] = jnp.zeros_like(prev_ref)

    p = p_ref[0]
    T = p.shape[0]
    rowi = lax.broadcasted_iota(jnp.int32, p.shape, 0)
    prev = jnp.where(rowi == 0, prev_ref[...], pltpu.roll(p, 1, 0))
    prev_ref[...] = p[T - 1:T, :]
    ps = p + (prev - p) * mu_ref[...]
    r = ps[:, 0:WIDTH]
    k = ps[:, WIDTH:2 * WIDTH]
    v = ps[:, 2 * WIDTH:3 * WIDTH]
    wa_in = ps[:, 3 * WIDTH:3 * WIDTH + DECAY_LORA + AAA_LORA]
    gd = ps[:, 3 * WIDTH + DECAY_LORA + AAA_LORA:]
    lane_wa = lax.broadcasted_iota(jnp.int32, wa_in.shape, 1)
    wa_act = jnp.where(lane_wa < DECAY_LORA, jnp.tanh(wa_in), wa_in)
    wa = _dot(wa_act, wwa_ref[...])
    log_w = -_softplus(-(w0_ref[...] + wa[:, :WIDTH])) - 0.5
    lw = -jnp.exp(log_w)
    a = jax.nn.sigmoid(a0_ref[...] + wa[:, WIDTH:])
    out_gate = _dot(jax.nn.sigmoid(gd), g2_ref[...])
    bd = bd_ref[...]
    kk0 = k * kk_ref[...]
    kk = kk0 * lax.rsqrt(jnp.maximum(_head_sum(kk0 * kk0, bd), 1e-24))
    k2 = k * (1.0 + (a - 1.0) * ka_ref[...])
    av = -kk
    bv = kk * a

    n_sub = T // L
    rt_i = lax.broadcasted_iota(jnp.int32, (T, T), 0)
    ct_i = lax.broadcasted_iota(jnp.int32, (T, T), 1)
    tri_tile = (rt_i >= ct_i) & (rt_i // L == ct_i // L)
    cl = _split_dot_left(jnp.where(tri_tile, 1.0, 0.0).astype(BF16), lw)
    cl_end = jnp.concatenate([jnp.broadcast_to(cl[(c + 1) * L - 1:(c + 1) * L, :], (L, WIDTH))
                              for c in range(n_sub)], axis=0)
    at_all = av * jnp.exp(cl - lw)
    rt_all = r * jnp.exp(cl)
    einv = jnp.exp(-cl)
    bt_all = bv * einv
    kt_all = k2 * einv
    edec = jnp.exp(cl_end - cl)
    b_end_all = bv * edec
    k_end_all = k2 * edec

    ri = lax.broadcasted_iota(jnp.int32, (L, L), 0)
    ci = lax.broadcasted_iota(jnp.int32, (L, L), 1)
    tri_incl = ri >= ci
    tri_strict = ri > ci
    eye = jnp.where(ri == ci, 1.0, 0.0)
    lane_first = lax.broadcasted_iota(jnp.int32, (L, PAIR), 1) < HEAD_DIM
    qr = lax.broadcasted_iota(jnp.int32, (PAIR, PAIR), 0) < HEAD_DIM
    qc = lax.broadcasted_iota(jnp.int32, (PAIR, PAIR), 1) < HEAD_DIM
    same_head = qr == qc

    def bmm(x, y):
        return lax.dot_general(x.astype(BF16), y.astype(BF16), (((2,), (1,)), ((0,), (0,))),
                               preferred_element_type=F32)

    def chunk_terms(c):
        rows = slice(c * L, (c + 1) * L)
        at, rt, bt, kt, vc = at_all[rows], rt_all[rows], bt_all[rows], kt_all[rows], v[rows]
        sb_heads, sk_heads = [], []
        for g in range(GROUPS):
            at_g, rt_g = _group(at, g), _group(rt, g)
            lhs = jnp.concatenate([jnp.where(lane_first, at_g, 0.0), jnp.where(lane_first, rt_g, 0.0),
                                   jnp.where(lane_first, 0.0, at_g), jnp.where(lane_first, 0.0, rt_g)],
                                  axis=0).astype(BF16)
            sb_g = _dot_nt(lhs, _group(bt, g))
            sk_g = _dot_nt(lhs, _group(kt, g))
            for hh in range(2):
                sb_heads.append(sb_g[hh * 2 * L:(hh + 1) * 2 * L])
                sk_heads.append(sk_g[hh * 2 * L:(hh + 1) * 2 * L])
        sb = jnp.stack(sb_heads)
        sk = jnp.stack(sk_heads)
        n_ab = jnp.where(tri_strict, sb[:, :L, :], 0.0)
        a_ak = jnp.where(tri_strict, sk[:, :L, :], 0.0)
        a_rb = jnp.where(tri_incl, sb[:, L:, :], 0.0)
        a_rk = jnp.where(tri_incl, sk[:, L:, :], 0.0)
        tinv = eye + n_ab
        pw = bmm(n_ab, n_ab)
        span = 2
        while 2 * span < L:
            both = bmm(jnp.concatenate([tinv, pw], axis=1), pw)
            tinv = tinv + both[:, :L, :]
            pw = both[:, L:, :]
            span *= 2
        tinv = tinv + bmm(tinv, pw)
        av_term = _head_apply(a_ak, vc, lane_first)
        pm = _head_apply(tinv, at, lane_first)
        qm = _head_apply(tinv, av_term, lane_first)
        rkv = _head_apply(a_rk, vc, lane_first)
        return pm, qm, rkv, a_rb

    terms = [chunk_terms(c) for c in range(n_sub)]

    y_chunks = []
    for c in range(n_sub):
        rows = slice(c * L, (c + 1) * L)
        pm, qm, rkv, a_rb = terms[c]
        rt, vc, b_end, k_end = rt_all[rows], v[rows], b_end_all[rows], k_end_all[rows]
        gam_last = jnp.exp(cl[(c + 1) * L - 1:(c + 1) * L, :])
        u_parts, ys_parts = [], []
        for g in range(GROUPS):
            pr = _dot_nt(jnp.concatenate([_group(pm, g), _group(rt, g)], axis=0), st_ref[g])
            u_parts.append(pr[:L] + _group(qm, g))
            ys_parts.append(pr[L:])
        u = jnp.concatenate(u_parts, axis=1)
        y_chunks.append(jnp.concatenate(ys_parts, axis=1) + _head_apply(a_rb, u, lane_first) + rkv)
        for g in range(GROUPS):
            upd = _dot_tn(_group(u, g), _group(b_end, g)) + _dot_tn(_group(vc, g), _group(k_end, g))
            st_ref[g] = st_ref[g] * _group(gam_last, g) + jnp.where(same_head, upd, 0.0)
    y = jnp.concatenate(y_chunks, axis=0)

    inv_n = 1.0 / HEAD_DIM
    mean = _head_sum(y, bd) * inv_n
    dlt = y - mean
    var = _head_sum(dlt * dlt, bd) * inv_n
    yn = dlt * lax.rsqrt(var + LN_X_EPS) * lng_ref[...] + lnb_ref[...]
    bonus = _head_sum(r * k2 * rk_ref[...], bd) * v
    o_ref[0] = ((yn + bonus) * out_gate).astype(o_ref.dtype)


def _split_dot_left(w_bf16, x):
    hi = x.astype(BF16)
    r1 = x - hi.astype(F32)
    mid = r1.astype(BF16)
    lo = (r1 - mid.astype(F32)).astype(BF16)
    return (jnp.dot(w_bf16, hi, preferred_element_type=F32)
            + jnp.dot(w_bf16, mid, preferred_element_type=F32)
            + jnp.dot(w_bf16, lo, preferred_element_type=F32))


def _rwkv_branch(p_rw, mu, w0, w2, a0, a2, g2, k_k, k_a, r_k, ln_g, ln_b):
    b, s, sw = p_rw.shape
    row = lambda t: t.reshape(1, -1).astype(F32)
    wwa = jnp.zeros((DECAY_LORA + AAA_LORA, 2 * WIDTH), F32)
    wwa = wwa.at[:DECAY_LORA, :WIDTH].set(w2).at[DECAY_LORA:, WIDTH:].set(a2).astype(BF16)
    hid = jnp.arange(PAIR) // HEAD_DIM
    bd = (hid[:, None] == hid[None, :]).astype(BF16)
    const = lambda shape: pl.BlockSpec(shape, lambda i, j: (0,) * len(shape))
    return pl.pallas_call(
        _rwkv_kernel,
        out_shape=jax.ShapeDtypeStruct((b, s, WIDTH), BF16),
        grid=(b, s // RWKV_TILE),
        in_specs=[pl.BlockSpec((1, RWKV_TILE, sw), lambda i, j: (i, j, 0)),
                  const((1, sw)), const((1, WIDTH)), const((1, WIDTH)), const((1, WIDTH)),
                  const((1, WIDTH)), const((1, WIDTH)), const((1, WIDTH)), const((1, WIDTH)),
                  const((DECAY_LORA + AAA_LORA, 2 * WIDTH)), const((GATE_LORA, WIDTH)),
                  const((PAIR, PAIR))],
        out_specs=pl.BlockSpec((1, RWKV_TILE, WIDTH), lambda i, j: (i, j, 0)),
        scratch_shapes=[pltpu.VMEM((GROUPS, PAIR, PAIR), F32), pltpu.VMEM((1, sw), F32)],
        compiler_params=_cparams(("arbitrary", "arbitrary")),
        name="rwkv7_scan",
    )(p_rw, row(mu), row(w0), row(a0), row(k_k), row(k_a), row(r_k), row(ln_g), row(ln_b),
      wwa, g2.astype(BF16), bd)


def _out_kernel(of_ref, orw_ref, gate_ref, x_ref, mod_ref, wof_ref, wor_ref, wo_ref, n2g_ref,
                wr_ref, br_ref, x1_ref, h2_ref, route_ref, counts_ref, cnt_ref):
    d = x_ref.shape[-1]
    gate = gate_ref[...].astype(F32)
    merged = (gate[:, :d] * jnp.dot(of_ref[...], wof_ref[...], preferred_element_type=F32)
              + gate[:, d:] * jnp.dot(orw_ref[...], wor_ref[...], preferred_element_type=F32))
    gate1 = mod_ref[0, 2:3, :]
    shift2 = mod_ref[0, 3:4, :]
    scale2 = mod_ref[0, 4:5, :]
    x1 = x_ref[...] + gate1 * jnp.dot(merged.astype(BF16), wo_ref[...], preferred_element_type=F32)
    x1_ref[...] = x1
    ms = jnp.mean(x1 * x1, axis=-1, keepdims=True)
    h2 = x1 * lax.rsqrt(ms + NORM_EPS) * n2g_ref[...] * (1.0 + scale2) + shift2
    h2_ref[...] = h2

    h2_hi = h2.astype(BF16)
    h2_lo = (h2 - h2_hi.astype(F32)).astype(BF16)
    logits = (jnp.dot(h2_hi, wr_ref[0], preferred_element_type=F32)
              + jnp.dot(h2_lo, wr_ref[0], preferred_element_type=F32)
              + jnp.dot(h2_hi, wr_ref[1], preferred_element_type=F32)) + br_ref[...]
    lane = lax.broadcasted_iota(jnp.int32, logits.shape, 1)
    neg = -jnp.inf
    big = jnp.int32(LANES)
    gl = jnp.where(lane < N_GROUPS, logits, neg)
    gmax = jnp.max(gl, axis=1, keepdims=True)
    gidx = jnp.min(jnp.where(gl == gmax, lane, big), axis=1, keepdims=True)
    g_p = 1.0 / jnp.sum(jnp.exp(gl - gmax), axis=1, keepdims=True)
    e_lane = lane - N_GROUPS
    in_grp = (e_lane >= 0) & (e_lane < N_EXPERTS) & ((e_lane // EXPERTS_PER_GROUP) == gidx)
    sel = jnp.where(in_grp, logits, neg)
    m1 = jnp.max(sel, axis=1, keepdims=True)
    i1 = jnp.min(jnp.where(sel == m1, lane, big), axis=1, keepdims=True)
    sel2 = jnp.where(lane == i1, neg, sel)
    m2 = jnp.max(sel2, axis=1, keepdims=True)
    i2 = jnp.min(jnp.where(sel2 == m2, lane, big), axis=1, keepdims=True)
    e21 = jnp.exp(m2 - m1)
    w_first = g_p / (1.0 + e21)
    w_second = g_p * e21 / (1.0 + e21)
    @pl.when(pl.program_id(0) == 0)
    def _():
        cnt_ref[...] = jnp.zeros_like(cnt_ref)

    tm = logits.shape[0]
    oh1 = lane == i1
    oh2 = lane == i2
    both = jnp.where(oh1 | oh2, 1.0, 0.0)
    before = (lax.broadcasted_iota(jnp.int32, (tm, tm), 0)
              > lax.broadcasted_iota(jnp.int32, (tm, tm), 1))
    seen = jnp.dot(jnp.where(before, 1.0, 0.0).astype(BF16), both.astype(BF16),
                   preferred_element_type=F32) + cnt_ref[...]
    rank1 = jnp.sum(jnp.where(oh1, seen, 0.0), axis=1, keepdims=True)
    rank2 = jnp.sum(jnp.where(oh2, seen, 0.0), axis=1, keepdims=True)
    cnt_ref[...] = cnt_ref[...] + jnp.sum(both, axis=0, keepdims=True)
    counts_ref[...] = jnp.broadcast_to(cnt_ref[...], counts_ref.shape)

    route = jnp.where(lane == 0, (i1 - N_GROUPS).astype(F32),
                      jnp.where(lane == 1, (i2 - N_GROUPS).astype(F32),
                                jnp.where(lane == 2, w_first,
                                          jnp.where(lane == 3, w_second,
                                                    jnp.where(lane == 4, rank1,
                                                              jnp.where(lane == 5, rank2, 0.0))))))
    route_ref[...] = route


def _merge_out_router(o_fox, o_rw, gate, x, mod, wof, wor, wo, n2g, wr, br, tm=512):
    b, s, d = x.shape
    t = b * s
    spb = s // tm
    rowspec = lambda w: pl.BlockSpec((tm, w), lambda i: (i, 0))
    const = lambda shape: pl.BlockSpec(shape, lambda i: (0,) * len(shape))
    return pl.pallas_call(
        _out_kernel,
        out_shape=(jax.ShapeDtypeStruct((t, d), F32), jax.ShapeDtypeStruct((t, d), F32),
                   jax.ShapeDtypeStruct((t, LANES), F32), jax.ShapeDtypeStruct((8, LANES), F32)),
        grid=(t // tm,),
        in_specs=[rowspec(WIDTH), rowspec(WIDTH), rowspec(2 * d), rowspec(d),
                  pl.BlockSpec((1, 6, d), lambda i: (i // spb, 0, 0)),
                  const((WIDTH, d)), const((WIDTH, d)), const((d, d)), const((1, d)),
                  const((2, d, LANES)), const((1, LANES))],
        out_specs=(rowspec(d), rowspec(d), rowspec(LANES), const((8, LANES))),
        scratch_shapes=[pltpu.VMEM((1, LANES), F32)],
        compiler_params=_cparams(("arbitrary",)),
        name="merge_out_router",
    )(o_fox, o_rw, gate, x.reshape(t, d), mod, wof, wor, wo, n2g.reshape(1, d), wr, br)


def _dispatch_kernel(zstart_ref, dest_ref, h_ref, xs_ref, zero_ref, sem, *, tm):
    @pl.when(pl.program_id(0) == 0)
    def _():
        zero_ref[...] = jnp.zeros_like(zero_ref)
        for e in range(N_EXPERTS):
            zrow = pl.multiple_of(zstart_ref[e], MOE_ROWS)
            pltpu.make_async_copy(zero_ref, xs_ref.at[pl.ds(zrow, MOE_ROWS)], sem).start()
        for e in range(N_EXPERTS):
            pltpu.make_async_copy(zero_ref, xs_ref.at[pl.ds(0, MOE_ROWS)], sem).wait()
        for phase in ("start", "wait"):
            for e in range(N_EXPERTS):
                trow = pl.multiple_of(zstart_ref[N_EXPERTS] + e * MOE_ROWS, MOE_ROWS)

                @pl.when(trow < xs_ref.shape[0])
                def _():
                    tail = pltpu.make_async_copy(zero_ref, xs_ref.at[pl.ds(trow, MOE_ROWS)], sem)
                    tail.start() if phase == "start" else tail.wait()

    def issue(r, carry):
        for kk in range(TOP_K):
            dst = dest_ref[0, 0, TOP_K * r + kk]
            pltpu.make_async_copy(h_ref.at[pl.ds(r, 1)], xs_ref.at[pl.ds(dst, 1)], sem).start(priority=kk)
        return carry

    lax.fori_loop(0, tm, issue, 0, unroll=8)
    for kk in range(TOP_K):
        pltpu.make_async_copy(h_ref, xs_ref.at[pl.ds(0, tm)], sem).wait()


def _moe_dispatch(h2, dest3, zstart, rows, tm=256):
    t, d = h2.shape
    grid_spec = pltpu.PrefetchScalarGridSpec(
        num_scalar_prefetch=1,
        grid=(t // tm,),
        in_specs=[pl.BlockSpec((1, 1, TOP_K * tm), lambda i, zs: (i, 0, 0), memory_space=pltpu.SMEM),
                  pl.BlockSpec((tm, d), lambda i, zs: (i, 0))],
        out_specs=pl.BlockSpec(memory_space=pl.ANY),
        scratch_shapes=[pltpu.VMEM((MOE_ROWS, d), F32), pltpu.SemaphoreType.DMA(())],
    )
    return pl.pallas_call(
        functools.partial(_dispatch_kernel, tm=tm),
        out_shape=jax.ShapeDtypeStruct((rows, d), F32),
        grid_spec=grid_spec,
        compiler_params=_cparams(("arbitrary",)),
        name="moe_dispatch",
    )(zstart, dest3, h2)


def _expert_kernel(blk_e_ref, nused_ref, xs_ref, w1_ref, w3_ref, w2_ref, o_ref):
    del blk_e_ref
    live = pl.program_id(0) * MOE_ROWS < nused_ref[0]

    @pl.when(live)
    def _():
        xb = xs_ref[...].astype(BF16)
        h1 = jnp.dot(xb, w1_ref[0].astype(BF16), preferred_element_type=F32)
        h3 = jnp.dot(xb, w3_ref[0].astype(BF16), preferred_element_type=F32)
        hh = (h1 * jax.nn.sigmoid(h1)) * h3
        o_ref[...] = jnp.dot(hh.astype(BF16), w2_ref[0].astype(BF16), preferred_element_type=F32)

    @pl.when(jnp.logical_not(live))
    def _():
        o_ref[...] = jnp.zeros_like(o_ref)


def _moe_experts(xs, blk_e, nused, w1, w3, w2):
    rows, d = xs.shape
    de = w1.shape[-1]
    grid_spec = pltpu.PrefetchScalarGridSpec(
        num_scalar_prefetch=2,
        grid=(rows // MOE_ROWS,),
        in_specs=[pl.BlockSpec((MOE_ROWS, d), lambda i, be, nu: (i, 0)),
                  pl.BlockSpec((1, d, de), lambda i, be, nu: (be[i], 0, 0)),
                  pl.BlockSpec((1, d, de), lambda i, be, nu: (be[i], 0, 0)),
                  pl.BlockSpec((1, de, d), lambda i, be, nu: (be[i], 0, 0))],
        out_specs=pl.BlockSpec((MOE_ROWS, d), lambda i, be, nu: (i, 0)),
    )
    return pl.pallas_call(
        _expert_kernel,
        out_shape=jax.ShapeDtypeStruct((rows, d), F32),
        grid_spec=grid_spec,
        compiler_params=_cparams(("arbitrary",)),
        name="moe_experts",
    )(blk_e, nused, xs, w1, w3, w2)


def _final_kernel(dest_ref, route_ref, x1_ref, mod_ref, fg_ref, ys_ref, o_ref, buf_ref, sem, *, tm):
    def issue(r, carry):
        for kk in range(TOP_K):
            src = dest_ref[0, 0, TOP_K * r + kk]
            pltpu.make_async_copy(ys_ref.at[pl.ds(src, 1)], buf_ref.at[kk, pl.ds(r, 1)], sem).start(priority=kk)
        return carry

    lax.fori_loop(0, tm, issue, 0, unroll=8)
    for kk in range(TOP_K):
        pltpu.make_async_copy(ys_ref.at[pl.ds(0, tm)], buf_ref.at[kk], sem).wait()

    route = route_ref[...]
    y = route[:, 2:3] * buf_ref[0] + route[:, 3:4] * buf_ref[1]
    gate2 = mod_ref[0, 5:6, :]
    x2 = x1_ref[...] + gate2 * y
    ms = jnp.mean(x2 * x2, axis=-1, keepdims=True)
    o_ref[...] = x2 * lax.rsqrt(ms + NORM_EPS) * fg_ref[...]


def _moe_combine_final(dest3, route, x1, mod, final_g, ys, s, tm=256):
    t, d = x1.shape
    spb = s // tm
    return pl.pallas_call(
        functools.partial(_final_kernel, tm=tm),
        out_shape=jax.ShapeDtypeStruct((t, d), F32),
        grid=(t // tm,),
        in_specs=[pl.BlockSpec((1, 1, TOP_K * tm), lambda i: (i, 0, 0), memory_space=pltpu.SMEM),
                  pl.BlockSpec((tm, LANES), lambda i: (i, 0)),
                  pl.BlockSpec((tm, d), lambda i: (i, 0)),
                  pl.BlockSpec((1, 6, d), lambda i: (i // spb, 0, 0)),
                  pl.BlockSpec((1, d), lambda i: (0, 0)),
                  pl.BlockSpec(memory_space=pl.ANY)],
        out_specs=pl.BlockSpec((tm, d), lambda i: (i, 0)),
        scratch_shapes=[pltpu.VMEM((TOP_K, tm, d), F32), pltpu.SemaphoreType.DMA(())],
        compiler_params=_cparams(("arbitrary",)),
        name="moe_combine_final",
    )(dest3, route, x1, mod, final_g.reshape(1, d), ys)


def _moe_plan(route, counts, tm):
    t = route.shape[0]
    m = t * TOP_K
    flat_e = route[:, :TOP_K].astype(jnp.int32).reshape(m)
    rank = route[:, 4:4 + TOP_K].astype(jnp.int32).reshape(m)
    counts = counts[0, N_GROUPS:N_GROUPS + N_EXPERTS].astype(jnp.int32)
    padded = (counts + MOE_ROWS - 1) // MOE_ROWS * MOE_ROWS
    pad_end = jnp.cumsum(padded)
    pad_start = pad_end - padded
    experts = jnp.arange(N_EXPERTS, dtype=jnp.int32)
    start_of = jnp.sum(jnp.where(flat_e[:, None] == experts[None, :], pad_start[None, :], 0), axis=1)
    dest = (start_of + rank).astype(jnp.int32)
    n_blocks = m // MOE_ROWS + N_EXPERTS
    blk_start = jnp.arange(n_blocks, dtype=jnp.int32) * MOE_ROWS
    blk_e = jnp.minimum(jnp.sum(pad_end[None, :] <= blk_start[:, None], axis=1), N_EXPERTS - 1).astype(jnp.int32)
    nused = pad_end[-1:].astype(jnp.int32)
    zstart = jnp.concatenate([jnp.maximum(pad_end - MOE_ROWS, 0), pad_end[-1:]]).astype(jnp.int32)
    return dest.reshape(t // tm, 1, TOP_K * tm), blk_e, nused, zstart, n_blocks * MOE_ROWS


def kernel(x, c, ada_w, ada_b, norm1_g, w_in, fox_forget_b, shift_mu, rwkv_w0, rwkv_w2, rwkv_a0, rwkv_a2, rwkv_g2, rwkv_k_k, rwkv_k_a, rwkv_r_k, ln_x_g, ln_x_b, w_out_fox, w_out_rwkv, w_o, norm2_g, router_group_w, router_group_b, router_expert_w, router_expert_b, exp_w1, exp_w3, exp_w2, final_g):
    b, s, d = x.shape
    t = b * s
    assert ada_w.shape[0] == 1, "the final norm is fused into the last layer's combine; one layer is laid out"
    for l in range(1):
        mod = _adaln_mod(c, ada_w[l], ada_b[l])

        h = _norm_mod(x, norm1_g[l], mod, shift_idx=0, scale_idx=1)
        h2d = h.reshape(t, d)
        w = w_in[l]
        o_f = 3 * WIDTH
        o_rw = o_f + HEADS
        o_g = o_rw + SHIFT_WIDTH
        perm = jnp.argsort(fox_forget_b[l])
        by_head = lambda m: m.reshape(d, HEADS, HEAD_DIM)[:, perm]
        wq = by_head(w[:, :WIDTH]).reshape(d, WIDTH)
        wk = by_head(w[:, WIDTH:2 * WIDTH]).reshape(d, WIDTH)
        wv = by_head(w[:, 2 * WIDTH:o_f])
        wv = jnp.concatenate([wv, jnp.zeros_like(wv)], axis=2).reshape(d, 2 * WIDTH)
        ones_cols = (jnp.arange(2 * WIDTH) % LANES >= HEAD_DIM).astype(F32)
        qkv_bias = jnp.concatenate([jnp.zeros((2 * WIDTH,), F32), ones_cols]).reshape(1, 4 * WIDTH)
        qkv = _matmul(h2d, jnp.concatenate([wq, wk, wv], axis=1).astype(BF16), BF16,
                      bias=qkv_bias, name="proj_qkv")
        p_rw = _matmul(h2d, w[:, o_rw:o_g].astype(BF16), F32, tn=896, name="proj_rwkv")
        gate = _matmul(h2d, w[:, o_g:].astype(BF16), BF16, act="sigmoid", name="proj_gate")
        fbias = _forget_bias(h, w[:, o_f:o_rw][:, perm], fox_forget_b[l][perm])
        o_fox = _fox_attention(qkv.reshape(b, s, 4 * WIDTH), fbias)
        w_of = w_out_fox[l].reshape(HEADS, HEAD_DIM, d)[perm].reshape(WIDTH, d)
        o_rwkv = _rwkv_branch(p_rw.reshape(b, s, SHIFT_WIDTH), shift_mu[l], rwkv_w0[l], rwkv_w2[l],
                              rwkv_a0[l], rwkv_a2[l], rwkv_g2[l], rwkv_k_k[l], rwkv_k_a[l],
                              rwkv_r_k[l], ln_x_g[l], ln_x_b[l])

        wr = jnp.zeros((d, LANES), F32)
        wr = wr.at[:, :N_GROUPS].set(router_group_w[l]).at[:, N_GROUPS:N_GROUPS + N_EXPERTS].set(router_expert_w[l])
        br = jnp.zeros((1, LANES), F32)
        br = br.at[0, :N_GROUPS].set(router_group_b[l]).at[0, N_GROUPS:N_GROUPS + N_EXPERTS].set(router_expert_b[l])
        wr_hi = wr.astype(BF16)
        wr_lo = (wr - wr_hi.astype(F32)).astype(BF16)
        x1, h2, route, counts = _merge_out_router(
            o_fox.reshape(t, WIDTH), o_rwkv.reshape(t, WIDTH), gate, x, mod,
            w_of.astype(BF16), w_out_rwkv[l].astype(BF16), w_o[l].astype(BF16), norm2_g[l],
            jnp.stack([wr_hi, wr_lo]), br)

        tm_moe = 256
        dest3, blk_e, nused, zstart, rows = _moe_plan(route, counts, tm_moe)
        xs = _moe_dispatch(h2, dest3, zstart, rows, tm=tm_moe)
        ys = _moe_experts(xs, blk_e, nused, exp_w1[l], exp_w3[l], exp_w2[l])
        out = _moe_combine_final(dest3, route, x1, mod, final_g, ys, s, tm=tm_moe)
    return out.reshape(b, s, d)
```

```python
import functools

import jax
import jax.numpy as jnp
from jax import lax
from jax.experimental import pallas as pl
from jax.experimental.pallas import tpu as pltpu

F32 = jnp.float32
BF16 = jnp.bfloat16
HIGHEST = lax.Precision.HIGHEST

HEADS = 8
HEAD_DIM = 64
WIDTH = HEADS * HEAD_DIM
DECAY_LORA = 64
AAA_LORA = 64
GATE_LORA = 128
SHIFT_WIDTH = 3 * WIDTH + DECAY_LORA + AAA_LORA + GATE_LORA
LN_X_EPS = 64e-5
NORM_EPS = 1e-6
N_GROUPS = 4
EXPERTS_PER_GROUP = 8
N_EXPERTS = N_GROUPS * EXPERTS_PER_GROUP
TOP_K = 2

LANES = 128
CHUNK = 64
RWKV_TILE = 128
MOE_ROWS = 256
VMEM_LIMIT = 48 * 1024 * 1024


def _cparams(sem):
    return pltpu.CompilerParams(dimension_semantics=sem, vmem_limit_bytes=VMEM_LIMIT)


def _dot(a, b):
    return jnp.dot(a.astype(BF16), b.astype(BF16), preferred_element_type=F32)


def _dot_nt(a, b):
    return lax.dot_general(a.astype(BF16), b.astype(BF16), (((1,), (1,)), ((), ())),
                           preferred_element_type=F32)


def _dot_tn(a, b):
    return lax.dot_general(a.astype(BF16), b.astype(BF16), (((0,), (0,)), ((), ())),
                           preferred_element_type=F32)


def _split_dot(x, w_bf16):
    hi = x.astype(BF16)
    lo = (x - hi.astype(F32)).astype(BF16)
    return (jnp.dot(hi, w_bf16, preferred_element_type=F32)
            + jnp.dot(lo, w_bf16, preferred_element_type=F32))


def _softplus(x):
    return jnp.maximum(x, 0.0) + jnp.log1p(jnp.exp(-jnp.abs(x)))


def _mod_kernel(c_ref, w_ref, b_ref, o_ref):
    c = c_ref[...]
    sc = c * jax.nn.sigmoid(c)
    o_ref[...] = jnp.dot(sc, w_ref[...], precision=HIGHEST, preferred_element_type=F32) + b_ref[...]


def _adaln_mod(c, ada_w, ada_b):
    b, d = c.shape
    n = ada_w.shape[1]
    rows = 8
    cp = jnp.zeros((rows, d), F32).at[:b].set(c)
    tn = 1024
    out = pl.pallas_call(
        _mod_kernel,
        out_shape=jax.ShapeDtypeStruct((rows, n), F32),
        grid=(n // tn,),
        in_specs=[pl.BlockSpec((rows, d), lambda j: (0, 0)),
                  pl.BlockSpec((d, tn), lambda j: (0, j)),
                  pl.BlockSpec((1, tn), lambda j: (0, j))],
        out_specs=pl.BlockSpec((rows, tn), lambda j: (0, j)),
        compiler_params=_cparams(("arbitrary",)),
        name="adaln_mod",
    )(cp, ada_w, ada_b.reshape(1, n))
    return out[:b].reshape(b, 6, d)


def _norm_mod_kernel(x_ref, g_ref, mod_ref, o_ref, *, shift_idx, scale_idx):
    x = x_ref[0]
    ms = jnp.mean(x * x, axis=-1, keepdims=True)
    y = x * lax.rsqrt(ms + NORM_EPS) * g_ref[...]
    scale = mod_ref[0, scale_idx:scale_idx + 1, :]
    shift = mod_ref[0, shift_idx:shift_idx + 1, :]
    o_ref[0] = (y * (1.0 + scale) + shift).astype(o_ref.dtype)


def _norm_mod(x, g, mod, shift_idx, scale_idx, tm=1024):
    b, s, d = x.shape
    return pl.pallas_call(
        functools.partial(_norm_mod_kernel, shift_idx=shift_idx, scale_idx=scale_idx),
        out_shape=jax.ShapeDtypeStruct((b, s, d), BF16),
        grid=(b, s // tm),
        in_specs=[pl.BlockSpec((1, tm, d), lambda i, j: (i, j, 0)),
                  pl.BlockSpec((1, d), lambda i, j: (0, 0)),
                  pl.BlockSpec((1, 6, d), lambda i, j: (i, 0, 0))],
        out_specs=pl.BlockSpec((1, tm, d), lambda i, j: (i, j, 0)),
        compiler_params=_cparams(("arbitrary", "arbitrary")),
        name="norm1_mod",
    )(x, g.reshape(1, d), mod)


def _mm_kernel(a_ref, w_ref, *rest, act, has_bias):
    o_ref = rest[-1]
    r = jnp.dot(a_ref[...], w_ref[...], preferred_element_type=F32)
    if has_bias:
        r = r + rest[0][...]
    if act == "sigmoid":
        r = 0.5 * jnp.tanh(0.5 * r) + 0.5
    o_ref[...] = r.astype(o_ref.dtype)


def _matmul(a, w, out_dtype, bias=None, act=None, tm=2048, tn=512, name="proj"):
    t, k = a.shape
    n = w.shape[1]
    tn = min(tn, n)
    in_specs = [pl.BlockSpec((tm, k), lambda i, j: (i, 0)),
                pl.BlockSpec((k, tn), lambda i, j: (0, j))]
    args = [a, w]
    if bias is not None:
        in_specs.append(pl.BlockSpec((1, tn), lambda i, j: (0, j)))
        args.append(bias)
    return pl.pallas_call(
        functools.partial(_mm_kernel, act=act, has_bias=bias is not None),
        out_shape=jax.ShapeDtypeStruct((t, n), out_dtype),
        grid=(t // tm, n // tn),
        in_specs=in_specs,
        out_specs=pl.BlockSpec((tm, tn), lambda i, j: (i, j)),
        compiler_params=_cparams(("arbitrary", "arbitrary")),
        name=name,
    )(*args)


LOG2E = 1.4426950408889634


def _split3(x):
    hi = x.astype(BF16)
    r1 = x - hi.astype(F32)
    mid = r1.astype(BF16)
    lo = (r1 - mid.astype(F32)).astype(BF16)
    return hi, mid, lo


def _fcum_kernel(h_ref, wf_ref, fb_ref, sel_ref, o_ref, carry_ref, *, ts):
    @pl.when(pl.program_id(0) == 0)
    def _():
        carry_ref[...] = jnp.zeros_like(carry_ref)

    ri = lax.broadcasted_iota(jnp.int32, (ts, ts), 0)
    ci = lax.broadcasted_iota(jnp.int32, (ts, ts), 1)
    tri = jnp.where(ri >= ci, 1.0, 0.0).astype(BF16)
    for bi in range(h_ref.shape[0]):
        f = jnp.dot(h_ref[bi], wf_ref[...], preferred_element_type=F32) + fb_ref[...]
        lf = -_softplus(-f)
        cum = carry_ref[bi]
        for piece in _split3(lf):
            cum = cum + jnp.dot(tri, piece, preferred_element_type=F32)
        carry_ref[bi] = cum[ts - 1:ts, :]
        out = jnp.zeros((ts, sel_ref.shape[2]), F32)
        for idx, piece in enumerate(_split3(cum * (-LOG2E))):
            out = out + jnp.dot(piece, sel_ref[idx], preferred_element_type=F32)
        o_ref[bi] = out.astype(o_ref.dtype)


def _forget_bias(h, wf, fb, ts=512):
    b, s, d = h.shape
    pairs = HEADS // 2
    wf_p = jnp.zeros((d, LANES), F32).at[:, :HEADS].set(wf).astype(BF16)
    fb_p = jnp.zeros((1, LANES), F32).at[0, :HEADS].set(fb)
    hh = jnp.arange(HEADS)
    sel = jnp.zeros((3, LANES, pairs * LANES), F32)
    for piece in range(3):
        sel = sel.at[piece, hh, (hh // 2) * LANES + (hh % 2) * 3 + piece].set(1.0)
    return pl.pallas_call(
        functools.partial(_fcum_kernel, ts=ts),
        out_shape=jax.ShapeDtypeStruct((b, s, pairs * LANES), BF16),
        grid=(s // ts,),
        in_specs=[pl.BlockSpec((b, ts, d), lambda j: (0, j, 0)),
                  pl.BlockSpec((d, LANES), lambda j: (0, 0)),
                  pl.BlockSpec((1, LANES), lambda j: (0, 0)),
                  pl.BlockSpec((3, LANES, pairs * LANES), lambda j: (0, 0, 0))],
        out_specs=pl.BlockSpec((b, ts, pairs * LANES), lambda j: (0, j, 0)),
        scratch_shapes=[pltpu.VMEM((b, 1, LANES), F32)],
        compiler_params=_cparams(("arbitrary",)),
        name="forget_bias",
    )(h, wf_p, fb_p, sel.astype(BF16))


FOX_BQ = 1024
FOX_BK = 512


def _fox_kernel(first_ref, q_ref, k_ref, v_ref, a_ref, o_ref, m_ref, acc_ref, *, bq, bk):
    i = pl.program_id(2)
    lane = lax.broadcasted_iota(jnp.int32, (bq, LANES), 1)
    first = lane < HEAD_DIM
    qs = q_ref[0].astype(F32) * (HEAD_DIM ** -0.5 * LOG2E)
    aug0 = jnp.where(lane < 3, 1.0, 0.0)
    aug1 = jnp.where((lane >= 3) & (lane < 6), 1.0, 0.0)
    q01 = jnp.concatenate([jnp.concatenate([jnp.where(first, qs, 0.0), aug0], axis=1),
                           jnp.concatenate([jnp.where(first, 0.0, qs), aug1], axis=1)],
                          axis=0).astype(BF16)
    m_ref[...] = jnp.full_like(m_ref, -jnp.inf)
    acc_ref[...] = jnp.zeros_like(acc_ref)

    def both_heads(x, r0):
        if r0 == 0:
            return x[...]
        return jnp.concatenate([x[r0:bq], x[bq + r0:2 * bq]], axis=0)

    def logits(j, r0=0):
        start = pl.multiple_of(j * bk, bk)
        kb = jnp.concatenate([k_ref[0, pl.ds(start, bk), :], a_ref[0, pl.ds(start, bk), :]], axis=1)
        return lax.dot_general(both_heads(q01, r0), kb, (((1,), (1,)), ((), ())),
                               preferred_element_type=F32)

    def consume(j, z, r0=0):
        nr = bq - r0
        start = pl.multiple_of(j * bk, bk)
        vb = v_ref[0, pl.ds(start, bk), :]
        m_prev = both_heads(m_ref, r0)
        m_new = jnp.maximum(m_prev, jnp.max(z, axis=1, keepdims=True))
        alpha = jnp.exp2(m_prev - m_new)
        p = jnp.exp2(z - jnp.concatenate([m_new] * (bk // LANES), axis=1)).astype(BF16)
        pv = jnp.concatenate([jnp.dot(p[:nr], vb[:, :LANES], preferred_element_type=F32),
                              jnp.dot(p[nr:], vb[:, LANES:], preferred_element_type=F32)], axis=0)
        acc_new = alpha * both_heads(acc_ref, r0) + pv
        if r0 == 0:
            acc_ref[...] = acc_new
            m_ref[...] = m_new
        else:
            for half, dst in ((slice(0, nr), slice(r0, bq)), (slice(nr, 2 * nr), slice(bq + r0, 2 * bq))):
                acc_ref[dst] = acc_new[half]
                m_ref[dst] = m_new[half]

    per_q = bq // bk
    n_full = i * per_q

    def body(j, carry):
        consume(j, logits(j))
        return carry

    lax.fori_loop(first_ref[pl.program_id(0), pl.program_id(1), i], n_full, body, 0)
    for d in range(per_q):
        r0 = d * bk
        row = lax.broadcasted_iota(jnp.int32, (bq - r0, bk), 0)
        col = lax.broadcasted_iota(jnp.int32, (bq - r0, bk), 1)
        keep = col <= row
        z = logits(n_full + d, r0)
        consume(n_full + d, jnp.where(jnp.concatenate([keep, keep], axis=0), z, -jnp.inf), r0)
    acc = acc_ref[...]
    o = acc / pltpu.roll(acc, HEAD_DIM, 1)
    o_ref[0] = jnp.where(first, o[:bq], pltpu.roll(o[bq:], HEAD_DIM, 1)).astype(o_ref.dtype)


def _norm_bound_kernel(x_ref, sel_ref, o_ref):
    x = x_ref[...].astype(F32)
    ssq = _dot(x * x, sel_ref[...])
    o_ref[0] = jnp.broadcast_to(jnp.max(ssq, axis=0, keepdims=True), o_ref.shape[1:])


def _fox_first_block(qkv, fbias, bq, bk):
    b, s, _ = qkv.shape
    t = b * s
    nq, nk = s // bq, s // bk
    n_heads2 = 2 * HEADS
    sel = (jnp.arange(2 * WIDTH)[:, None] // HEAD_DIM == jnp.arange(LANES)[None, :]).astype(BF16)
    bounds = pl.pallas_call(
        _norm_bound_kernel,
        out_shape=jax.ShapeDtypeStruct((t // bk, 8, LANES), F32),
        grid=(t // bk,),
        in_specs=[pl.BlockSpec((bk, 2 * WIDTH), lambda i: (i, 0)),
                  pl.BlockSpec((2 * WIDTH, LANES), lambda i: (0, 0))],
        out_specs=pl.BlockSpec((1, 8, LANES), lambda i: (i, 0, 0)),
        compiler_params=_cparams(("arbitrary",)),
        name="fox_norm_bounds",
    )(qkv.reshape(t, -1), sel)
    nrm = jnp.sqrt(bounds[:, 0, :n_heads2]).reshape(b, nk, n_heads2) * 1.01
    qn = nrm[..., :HEADS] * (HEAD_DIM ** -0.5 * LOG2E * 1.01)
    kn = nrm[..., HEADS:]
    per_q = bq // bk
    qn_i = qn.reshape(b, nq, per_q, HEADS).max(axis=2)
    kn_i = kn.reshape(b, nq, per_q, HEADS).max(axis=2)
    kn_pre = lax.cummax(kn, axis=1)
    pairs = HEADS // 2
    def bias_rows(rows):
        pieces = rows.astype(F32).reshape(b, -1, pairs, LANES)[..., :6].reshape(b, -1, pairs, 2, 3)
        return pieces.sum(-1).reshape(b, -1, HEADS)

    nb_end = bias_rows(fbias[:, bk - 1::bk])
    nb_start = bias_rows(fbias[:, ::bq])
    gap = nb_start[:, :, None, :] - nb_end[:, None, :, :]
    need = qn_i[:, :, None, :] * (kn_pre[:, None, :, :] + kn_i[:, :, None, :]) + 152.0
    skip = (gap > need).reshape(b, nq, nk, pairs, 2).all(axis=-1)
    n_full = jnp.arange(nq) * per_q
    skip = skip & (jnp.arange(nk)[None, None, :, None] < n_full[None, :, None, None])
    first = jnp.argmin(skip, axis=2)
    return first.transpose(0, 2, 1).astype(jnp.int32)


def _fox_attention(qkv, fbias, bq=FOX_BQ, bk=FOX_BK):
    b, s, _ = qkv.shape
    pairs = HEADS // 2
    cb = WIDTH // LANES
    first = _fox_first_block(qkv, fbias, bq, bk)
    grid_spec = pltpu.PrefetchScalarGridSpec(
        num_scalar_prefetch=1,
        grid=(b, pairs, s // bq),
        in_specs=[pl.BlockSpec((1, bq, LANES), lambda bi, hp, i, fr: (bi, i, hp)),
                  pl.BlockSpec((1, s, LANES), lambda bi, hp, i, fr: (bi, 0, cb + hp)),
                  pl.BlockSpec((1, s, 2 * LANES), lambda bi, hp, i, fr: (bi, 0, cb + hp)),
                  pl.BlockSpec((1, s, LANES), lambda bi, hp, i, fr: (bi, 0, hp))],
        out_specs=pl.BlockSpec((1, bq, LANES), lambda bi, hp, i, fr: (bi, i, hp)),
        scratch_shapes=[pltpu.VMEM((2 * bq, LANES), F32), pltpu.VMEM((2 * bq, LANES), F32)],
    )
    return pl.pallas_call(
        functools.partial(_fox_kernel, bq=bq, bk=bk),
        out_shape=jax.ShapeDtypeStruct((b, s, WIDTH), BF16),
        grid_spec=grid_spec,
        compiler_params=_cparams(("arbitrary", "arbitrary", "arbitrary")),
        name="fox_attention",
    )(first, qkv, qkv, qkv, fbias)


PAIR = 2 * HEAD_DIM
GROUPS = WIDTH // PAIR


def _group(x, g):
    return x[:, g * PAIR:(g + 1) * PAIR]


def _head_sum(x, bd2):
    return jnp.concatenate([_dot(_group(x, g), bd2) for g in range(GROUPS)], axis=1)


def _head_apply(mats, x, lane_first):
    rows = mats.shape[1]
    outs = []
    for g in range(GROUPS):
        res = _dot(mats[2 * g:2 * g + 2].reshape(2 * rows, rows), _group(x, g))
        outs.append(jnp.where(lane_first, res[:rows], res[rows:]))
    return jnp.concatenate(outs, axis=1)


def _rwkv_kernel(p_ref, mu_ref, w0_ref, a0_ref, kk_ref, ka_ref, rk_ref, lng_ref, lnb_ref,
                 wwa_ref, g2_ref, bd_ref, o_ref, st_ref, prev_ref):
    L = CHUNK

    @pl.when(pl.program_id(1) == 0)
    def _():
        st_ref[...] = jnp.zeros_like(st_ref)
        prev_ref[...] = jnp.zeros_like(prev_ref)

    p = p_ref[0]
    T = p.shape[0]
    rowi = lax.broadcasted_iota(jnp.int32, p.shape, 0)
    prev = jnp.where(rowi == 0, prev_ref[...], pltpu.roll(p, 1, 0))
    prev_ref[...] = p[T - 1:T, :]
    ps = p + (prev - p) * mu_ref[...]
    r = ps[:, 0:WIDTH]
    k = ps[:, WIDTH:2 * WIDTH]
    v = ps[:, 2 * WIDTH:3 * WIDTH]
    wa_in = ps[:, 3 * WIDTH:3 * WIDTH + DECAY_LORA + AAA_LORA]
    gd = ps[:, 3 * WIDTH + DECAY_LORA + AAA_LORA:]
    lane_wa = lax.broadcasted_iota(jnp.int32, wa_in.shape, 1)
    wa_act = jnp.where(lane_wa < DECAY_LORA, jnp.tanh(wa_in), wa_in)
    wa = _dot(wa_act, wwa_ref[...])
    log_w = -_softplus(-(w0_ref[...] + wa[:, :WIDTH])) - 0.5
    lw = -jnp.exp(log_w)
    a = jax.nn.sigmoid(a0_ref[...] + wa[:, WIDTH:])
    out_gate = _dot(jax.nn.sigmoid(gd), g2_ref[...])
    bd = bd_ref[...]
    kk0 = k * kk_ref[...]
    kk = kk0 * lax.rsqrt(jnp.maximum(_head_sum(kk0 * kk0, bd), 1e-24))
    k2 = k * (1.0 + (a - 1.0) * ka_ref[...])
    av = -kk
    bv = kk * a

    n_sub = T // L
    rt_i = lax.broadcasted_iota(jnp.int32, (T, T), 0)
    ct_i = lax.broadcasted_iota(jnp.int32, (T, T), 1)
    tri_tile = (rt_i >= ct_i) & (rt_i // L == ct_i // L)
    cl = _split_dot_left(jnp.where(tri_tile, 1.0, 0.0).astype(BF16), lw)
    cl_end = jnp.concatenate([jnp.broadcast_to(cl[(c + 1) * L - 1:(c + 1) * L, :], (L, WIDTH))
                              for c in range(n_sub)], axis=0)
    at_all = av * jnp.exp(cl - lw)
    rt_all = r * jnp.exp(cl)
    einv = jnp.exp(-cl)
    bt_all = bv * einv
    kt_all = k2 * einv
    edec = jnp.exp(cl_end - cl)
    b_end_all = bv * edec
    k_end_all = k2 * edec

    ri = lax.broadcasted_iota(jnp.int32, (L, L), 0)
    ci = lax.broadcasted_iota(jnp.int32, (L, L), 1)
    tri_incl = ri >= ci
    tri_strict = ri > ci
    eye = jnp.where(ri == ci, 1.0, 0.0)
    lane_first = lax.broadcasted_iota(jnp.int32, (L, PAIR), 1) < HEAD_DIM
    qr = lax.broadcasted_iota(jnp.int32, (PAIR, PAIR), 0) < HEAD_DIM
    qc = lax.broadcasted_iota(jnp.int32, (PAIR, PAIR), 1) < HEAD_DIM
    same_head = qr == qc

    def bmm(x, y):
        return lax.dot_general(x.astype(BF16), y.astype(BF16), (((2,), (1,)), ((0,), (0,))),
                               preferred_element_type=F32)

    def chunk_terms(c):
        rows = slice(c * L, (c + 1) * L)
        at, rt, bt, kt, vc = at_all[rows], rt_all[rows], bt_all[rows], kt_all[rows], v[rows]
        sb_heads, sk_heads = [], []
        for g in range(GROUPS):
            at_g, rt_g = _group(at, g), _group(rt, g)
            lhs = jnp.concatenate([jnp.where(lane_first, at_g, 0.0), jnp.where(lane_first, rt_g, 0.0),
                                   jnp.where(lane_first, 0.0, at_g), jnp.where(lane_first, 0.0, rt_g)],
                                  axis=0).astype(BF16)
            sb_g = _dot_nt(lhs, _group(bt, g))
            sk_g = _dot_nt(lhs, _group(kt, g))
            for hh in range(2):
                sb_heads.append(sb_g[hh * 2 * L:(hh + 1) * 2 * L])
                sk_heads.append(sk_g[hh * 2 * L:(hh + 1) * 2 * L])
        sb = jnp.stack(sb_heads)
        sk = jnp.stack(sk_heads)
        n_ab = jnp.where(tri_strict, sb[:, :L, :], 0.0)
        a_ak = jnp.where(tri_strict, sk[:, :L, :], 0.0)
        a_rb = jnp.where(tri_incl, sb[:, L:, :], 0.0)
        a_rk = jnp.where(tri_incl, sk[:, L:, :], 0.0)
        tinv = eye + n_ab
        pw = bmm(n_ab, n_ab)
        span = 2
        while 2 * span < L:
            both = bmm(jnp.concatenate([tinv, pw], axis=1), pw)
            tinv = tinv + both[:, :L, :]
            pw = both[:, L:, :]
            span *= 2
        tinv = tinv + bmm(tinv, pw)
        av_term = _head_apply(a_ak, vc, lane_first)
        pm = _head_apply(tinv, at, lane_first)
        qm = _head_apply(tinv, av_term, lane_first)
        rkv = _head_apply(a_rk, vc, lane_first)
        return pm, qm, rkv, a_rb

    terms = [chunk_terms(c) for c in range(n_sub)]

    y_chunks = []
    for c in range(n_sub):
        rows = slice(c * L, (c + 1) * L)
        pm, qm, rkv, a_rb = terms[c]
        rt, vc, b_end, k_end = rt_all[rows], v[rows], b_end_all[rows], k_end_all[rows]
        gam_last = jnp.exp(cl[(c + 1) * L - 1:(c + 1) * L, :])
        u_parts, ys_parts = [], []
        for g in range(GROUPS):
            pr = _dot_nt(jnp.concatenate([_group(pm, g), _group(rt, g)], axis=0), st_ref[g])
            u_parts.append(pr[:L] + _group(qm, g))
            ys_parts.append(pr[L:])
        u = jnp.concatenate(u_parts, axis=1)
        y_chunks.append(jnp.concatenate(ys_parts, axis=1) + _head_apply(a_rb, u, lane_first) + rkv)
        for g in range(GROUPS):
            upd = _dot_tn(_group(u, g), _group(b_end, g)) + _dot_tn(_group(vc, g), _group(k_end, g))
            st_ref[g] = st_ref[g] * _group(gam_last, g) + jnp.where(same_head, upd, 0.0)
    y = jnp.concatenate(y_chunks, axis=0)

    inv_n = 1.0 / HEAD_DIM
    mean = _head_sum(y, bd) * inv_n
    dlt = y - mean
    var = _head_sum(dlt * dlt, bd) * inv_n
    yn = dlt * lax.rsqrt(var + LN_X_EPS) * lng_ref[...] + lnb_ref[...]
    bonus = _head_sum(r * k2 * rk_ref[...], bd) * v
    o_ref[0] = ((yn + bonus) * out_gate).astype(o_ref.dtype)


def _split_dot_left(w_bf16, x):
    hi = x.astype(BF16)
    r1 = x - hi.astype(F32)
    mid = r1.astype(BF16)
    lo = (r1 - mid.astype(F32)).astype(BF16)
    return (jnp.dot(w_bf16, hi, preferred_element_type=F32)
            + jnp.dot(w_bf16, mid, preferred_element_type=F32)
            + jnp.dot(w_bf16, lo, preferred_element_type=F32))


def _rwkv_branch(p_rw, mu, w0, w2, a0, a2, g2, k_k, k_a, r_k, ln_g, ln_b):
    b, s, sw = p_rw.shape
    row = lambda t: t.reshape(1, -1).astype(F32)
    wwa = jnp.zeros((DECAY_LORA + AAA_LORA, 2 * WIDTH), F32)
    wwa = wwa.at[:DECAY_LORA, :WIDTH].set(w2).at[DECAY_LORA:, WIDTH:].set(a2).astype(BF16)
    hid = jnp.arange(PAIR) // HEAD_DIM
    bd = (hid[:, None] == hid[None, :]).astype(BF16)
    const = lambda shape: pl.BlockSpec(shape, lambda i, j: (0,) * len(shape))
    return pl.pallas_call(
        _rwkv_kernel,
        out_shape=jax.ShapeDtypeStruct((b, s, WIDTH), BF16),
        grid=(b, s // RWKV_TILE),
        in_specs=[pl.BlockSpec((1, RWKV_TILE, sw), lambda i, j: (i, j, 0)),
                  const((1, sw)), const((1, WIDTH)), const((1, WIDTH)), const((1, WIDTH)),
                  const((1, WIDTH)), const((1, WIDTH)), const((1, WIDTH)), const((1, WIDTH)),
                  const((DECAY_LORA + AAA_LORA, 2 * WIDTH)), const((GATE_LORA, WIDTH)),
                  const((PAIR, PAIR))],
        out_specs=pl.BlockSpec((1, RWKV_TILE, WIDTH), lambda i, j: (i, j, 0)),
        scratch_shapes=[pltpu.VMEM((GROUPS, PAIR, PAIR), F32), pltpu.VMEM((1, sw), F32)],
        compiler_params=_cparams(("arbitrary", "arbitrary")),
        name="rwkv7_scan",
    )(p_rw, row(mu), row(w0), row(a0), row(k_k), row(k_a), row(r_k), row(ln_g), row(ln_b),
      wwa, g2.astype(BF16), bd)


def _out_kernel(of_ref, orw_ref, gate_ref, x_ref, mod_ref, wof_ref, wor_ref, wo_ref, n2g_ref,
                wr_ref, br_ref, x1_ref, h2_ref, route_ref, counts_ref, cnt_ref):
    d = x_ref.shape[-1]
    gate = gate_ref[...].astype(F32)
    merged = (gate[:, :d] * jnp.dot(of_ref[...], wof_ref[...], preferred_element_type=F32)
              + gate[:, d:] * jnp.dot(orw_ref[...], wor_ref[...], preferred_element_type=F32))
    gate1 = mod_ref[0, 2:3, :]
    shift2 = mod_ref[0, 3:4, :]
    scale2 = mod_ref[0, 4:5, :]
    x1 = x_ref[...] + gate1 * jnp.dot(merged.astype(BF16), wo_ref[...], preferred_element_type=F32)
    x1_ref[...] = x1
    ms = jnp.mean(x1 * x1, axis=-1, keepdims=True)
    h2 = x1 * lax.rsqrt(ms + NORM_EPS) * n2g_ref[...] * (1.0 + scale2) + shift2
    h2_ref[...] = h2

    h2_hi = h2.astype(BF16)
    h2_lo = (h2 - h2_hi.astype(F32)).astype(BF16)
    logits = (jnp.dot(h2_hi, wr_ref[0], preferred_element_type=F32)
              + jnp.dot(h2_lo, wr_ref[0], preferred_element_type=F32)
              + jnp.dot(h2_hi, wr_ref[1], preferred_element_type=F32)) + br_ref[...]
    lane = lax.broadcasted_iota(jnp.int32, logits.shape, 1)
    neg = -jnp.inf
    big = jnp.int32(LANES)
    gl = jnp.where(lane < N_GROUPS, logits, neg)
    gmax = jnp.max(gl, axis=1, keepdims=True)
    gidx = jnp.min(jnp.where(gl == gmax, lane, big), axis=1, keepdims=True)
    g_p = 1.0 / jnp.sum(jnp.exp(gl - gmax), axis=1, keepdims=True)
    e_lane = lane - N_GROUPS
    in_grp = (e_lane >= 0) & (e_lane < N_EXPERTS) & ((e_lane // EXPERTS_PER_GROUP) == gidx)
    sel = jnp.where(in_grp, logits, neg)
    m1 = jnp.max(sel, axis=1, keepdims=True)
    i1 = jnp.min(jnp.where(sel == m1, lane, big), axis=1, keepdims=True)
    sel2 = jnp.where(lane == i1, neg, sel)
    m2 = jnp.max(sel2, axis=1, keepdims=True)
    i2 = jnp.min(jnp.where(sel2 == m2, lane, big), axis=1, keepdims=True)
    e21 = jnp.exp(m2 - m1)
    w_first = g_p / (1.0 + e21)
    w_second = g_p * e21 / (1.0 + e21)
    @pl.when(pl.program_id(0) == 0)
    def _():
        cnt_ref[...] = jnp.zeros_like(cnt_ref)

    tm = logits.shape[0]
    oh1 = lane == i1
    oh2 = lane == i2
    both = jnp.where(oh1 | oh2, 1.0, 0.0)
    before = (lax.broadcasted_iota(jnp.int32, (tm, tm), 0)
              > lax.broadcasted_iota(jnp.int32, (tm, tm), 1))
    seen = jnp.dot(jnp.where(before, 1.0, 0.0).astype(BF16), both.astype(BF16),
                   preferred_element_type=F32) + cnt_ref[...]
    rank1 = jnp.sum(jnp.where(oh1, seen, 0.0), axis=1, keepdims=True)
    rank2 = jnp.sum(jnp.where(oh2, seen, 0.0), axis=1, keepdims=True)
    cnt_ref[...] = cnt_ref[...] + jnp.sum(both, axis=0, keepdims=True)
    counts_ref[...] = jnp.broadcast_to(cnt_ref[...], counts_ref.shape)

    route = jnp.where(lane == 0, (i1 - N_GROUPS).astype(F32),
                      jnp.where(lane == 1, (i2 - N_GROUPS).astype(F32),
                                jnp.where(lane == 2, w_first,
                                          jnp.where(lane == 3, w_second,
                                                    jnp.where(lane == 4, rank1,
                                                              jnp.where(lane == 5, rank2, 0.0))))))
    route_ref[...] = route


def _merge_out_router(o_fox, o_rw, gate, x, mod, wof, wor, wo, n2g, wr, br, tm=512):
    b, s, d = x.shape
    t = b * s
    spb = s // tm
    rowspec = lambda w: pl.BlockSpec((tm, w), lambda i: (i, 0))
    const = lambda shape: pl.BlockSpec(shape, lambda i: (0,) * len(shape))
    return pl.pallas_call(
        _out_kernel,
        out_shape=(jax.ShapeDtypeStruct((t, d), F32), jax.ShapeDtypeStruct((t, d), F32),
                   jax.ShapeDtypeStruct((t, LANES), F32), jax.ShapeDtypeStruct((8, LANES), F32)),
        grid=(t // tm,),
        in_specs=[rowspec(WIDTH), rowspec(WIDTH), rowspec(2 * d), rowspec(d),
                  pl.BlockSpec((1, 6, d), lambda i: (i // spb, 0, 0)),
                  const((WIDTH, d)), const((WIDTH, d)), const((d, d)), const((1, d)),
                  const((2, d, LANES)), const((1, LANES))],
        out_specs=(rowspec(d), rowspec(d), rowspec(LANES), const((8, LANES))),
        scratch_shapes=[pltpu.VMEM((1, LANES), F32)],
        compiler_params=_cparams(("arbitrary",)),
        name="merge_out_router",
    )(o_fox, o_rw, gate, x.reshape(t, d), mod, wof, wor, wo, n2g.reshape(1, d), wr, br)


def _dispatch_kernel(zstart_ref, dest_ref, h_ref, xs_ref, zero_ref, sem, *, tm):
    @pl.when(pl.program_id(0) == 0)
    def _():
        zero_ref[...] = jnp.zeros_like(zero_ref)
        for e in range(N_EXPERTS):
            zrow = pl.multiple_of(zstart_ref[e], MOE_ROWS)
            pltpu.make_async_copy(zero_ref, xs_ref.at[pl.ds(zrow, MOE_ROWS)], sem).start()
        for e in range(N_EXPERTS):
            pltpu.make_async_copy(zero_ref, xs_ref.at[pl.ds(0, MOE_ROWS)], sem).wait()
        for phase in ("start", "wait"):
            for e in range(N_EXPERTS):
                trow = pl.multiple_of(zstart_ref[N_EXPERTS] + e * MOE_ROWS, MOE_ROWS)

                @pl.when(trow < xs_ref.shape[0])
                def _():
                    tail = pltpu.make_async_copy(zero_ref, xs_ref.at[pl.ds(trow, MOE_ROWS)], sem)
                    tail.start() if phase == "start" else tail.wait()

    def issue(r, carry):
        for kk in range(TOP_K):
            dst = dest_ref[0, 0, TOP_K * r + kk]
            pltpu.make_async_copy(h_ref.at[pl.ds(r, 1)], xs_ref.at[pl.ds(dst, 1)], sem).start(priority=kk)
        return carry

    lax.fori_loop(0, tm, issue, 0, unroll=8)
    for kk in range(TOP_K):
        pltpu.make_async_copy(h_ref, xs_ref.at[pl.ds(0, tm)], sem).wait()


def _moe_dispatch(h2, dest3, zstart, rows, tm=256):
    t, d = h2.shape
    grid_spec = pltpu.PrefetchScalarGridSpec(
        num_scalar_prefetch=1,
        grid=(t // tm,),
        in_specs=[pl.BlockSpec((1, 1, TOP_K * tm), lambda i, zs: (i, 0, 0), memory_space=pltpu.SMEM),
                  pl.BlockSpec((tm, d), lambda i, zs: (i, 0))],
        out_specs=pl.BlockSpec(memory_space=pl.ANY),
        scratch_shapes=[pltpu.VMEM((MOE_ROWS, d), F32), pltpu.SemaphoreType.DMA(())],
    )
    return pl.pallas_call(
        functools.partial(_dispatch_kernel, tm=tm),
        out_shape=jax.ShapeDtypeStruct((rows, d), F32),
        grid_spec=grid_spec,
        compiler_params=_cparams(("arbitrary",)),
        name="moe_dispatch",
    )(zstart, dest3, h2)


def _expert_kernel(blk_e_ref, nused_ref, xs_ref, w1_ref, w3_ref, w2_ref, o_ref):
    del blk_e_ref
    live = pl.program_id(0) * MOE_ROWS < nused_ref[0]

    @pl.when(live)
    def _():
        xb = xs_ref[...].astype(BF16)
        h1 = jnp.dot(xb, w1_ref[0].astype(BF16), preferred_element_type=F32)
        h3 = jnp.dot(xb, w3_ref[0].astype(BF16), preferred_element_type=F32)
        hh = (h1 * jax.nn.sigmoid(h1)) * h3
        o_ref[...] = jnp.dot(hh.astype(BF16), w2_ref[0].astype(BF16), preferred_element_type=F32)

    @pl.when(jnp.logical_not(live))
    def _():
        o_ref[...] = jnp.zeros_like(o_ref)


def _moe_experts(xs, blk_e, nused, w1, w3, w2):
    rows, d = xs.shape
    de = w1.shape[-1]
    grid_spec = pltpu.PrefetchScalarGridSpec(
        num_scalar_prefetch=2,
        grid=(rows // MOE_ROWS,),
        in_specs=[pl.BlockSpec((MOE_ROWS, d), lambda i, be, nu: (i, 0)),
                  pl.BlockSpec((1, d, de), lambda i, be, nu: (be[i], 0, 0)),
                  pl.BlockSpec((1, d, de), lambda i, be, nu: (be[i], 0, 0)),
                  pl.BlockSpec((1, de, d), lambda i, be, nu: (be[i], 0, 0))],
        out_specs=pl.BlockSpec((MOE_ROWS, d), lambda i, be, nu: (i, 0)),
    )
    return pl.pallas_call(
        _expert_kernel,
        out_shape=jax.ShapeDtypeStruct((rows, d), F32),
        grid_spec=grid_spec,
        compiler_params=_cparams(("arbitrary",)),
        name="moe_experts",
    )(blk_e, nused, xs, w1, w3, w2)


def _final_kernel(dest_ref, route_ref, x1_ref, mod_ref, fg_ref, ys_ref, o_ref, buf_ref, sem, *, tm):
    def issue(r, carry):
        for kk in range(TOP_K):
            src = dest_ref[0, 0, TOP_K * r + kk]
            pltpu.make_async_copy(ys_ref.at[pl.ds(src, 1)], buf_ref.at[kk, pl.ds(r, 1)], sem).start(priority=kk)
        return carry

    lax.fori_loop(0, tm, issue, 0, unroll=8)
    for kk in range(TOP_K):
        pltpu.make_async_copy(ys_ref.at[pl.ds(0, tm)], buf_ref.at[kk], sem).wait()

    route = route_ref[...]
    y = route[:, 2:3] * buf_ref[0] + route[:, 3:4] * buf_ref[1]
    gate2 = mod_ref[0, 5:6, :]
    x2 = x1_ref[...] + gate2 * y
    ms = jnp.mean(x2 * x2, axis=-1, keepdims=True)
    o_ref[...] = x2 * lax.rsqrt(ms + NORM_EPS) * fg_ref[...]


def _moe_combine_final(dest3, route, x1, mod, final_g, ys, s, tm=256):
    t, d = x1.shape
    spb = s // tm
    return pl.pallas_call(
        functools.partial(_final_kernel, tm=tm),
        out_shape=jax.ShapeDtypeStruct((t, d), F32),
        grid=(t // tm,),
        in_specs=[pl.BlockSpec((1, 1, TOP_K * tm), lambda i: (i, 0, 0), memory_space=pltpu.SMEM),
                  pl.BlockSpec((tm, LANES), lambda i: (i, 0)),
                  pl.BlockSpec((tm, d), lambda i: (i, 0)),
                  pl.BlockSpec((1, 6, d), lambda i: (i // spb, 0, 0)),
                  pl.BlockSpec((1, d), lambda i: (0, 0)),
                  pl.BlockSpec(memory_space=pl.ANY)],
        out_specs=pl.BlockSpec((tm, d), lambda i: (i, 0)),
        scratch_shapes=[pltpu.VMEM((TOP_K, tm, d), F32), pltpu.SemaphoreType.DMA(())],
        compiler_params=_cparams(("arbitrary",)),
        name="moe_combine_final",
    )(dest3, route, x1, mod, final_g.reshape(1, d), ys)


def _moe_plan(route, counts, tm):
    t = route.shape[0]
    m = t * TOP_K
    flat_e = route[:, :TOP_K].astype(jnp.int32).reshape(m)
    rank = route[:, 4:4 + TOP_K].astype(jnp.int32).reshape(m)
    counts = counts[0, N_GROUPS:N_GROUPS + N_EXPERTS].astype(jnp.int32)
    padded = (counts + MOE_ROWS - 1) // MOE_ROWS * MOE_ROWS
    pad_end = jnp.cumsum(padded)
    pad_start = pad_end - padded
    experts = jnp.arange(N_EXPERTS, dtype=jnp.int32)
    start_of = jnp.sum(jnp.where(flat_e[:, None] == experts[None, :], pad_start[None, :], 0), axis=1)
    dest = (start_of + rank).astype(jnp.int32)
    n_blocks = m // MOE_ROWS + N_EXPERTS
    blk_start = jnp.arange(n_blocks, dtype=jnp.int32) * MOE_ROWS
    blk_e = jnp.minimum(jnp.sum(pad_end[None, :] <= blk_start[:, None], axis=1), N_EXPERTS - 1).astype(jnp.int32)
    nused = pad_end[-1:].astype(jnp.int32)
    zstart = jnp.concatenate([jnp.maximum(pad_end - MOE_ROWS, 0), pad_end[-1:]]).astype(jnp.int32)
    return dest.reshape(t // tm, 1, TOP_K * tm), blk_e, nused, zstart, n_blocks * MOE_ROWS


def kernel(x, c, ada_w, ada_b, norm1_g, w_in, fox_forget_b, shift_mu, rwkv_w0, rwkv_w2, rwkv_a0, rwkv_a2, rwkv_g2, rwkv_k_k, rwkv_k_a, rwkv_r_k, ln_x_g, ln_x_b, w_out_fox, w_out_rwkv, w_o, norm2_g, router_group_w, router_group_b, router_expert_w, router_expert_b, exp_w1, exp_w3, exp_w2, final_g):
    b, s, d = x.shape
    t = b * s
    assert ada_w.shape[0] == 1, "the final norm is fused into the last layer's combine; one layer is laid out"
    for l in range(1):
        mod = _adaln_mod(c, ada_w[l], ada_b[l])

        h = _norm_mod(x, norm1_g[l], mod, shift_idx=0, scale_idx=1)
        h2d = h.reshape(t, d)
        w = w_in[l]
        o_f = 3 * WIDTH
        o_rw = o_f + HEADS
        o_g = o_rw + SHIFT_WIDTH
        perm = jnp.argsort(fox_forget_b[l])
        by_head = lambda m: m.reshape(d, HEADS, HEAD_DIM)[:, perm]
        wq = by_head(w[:, :WIDTH]).reshape(d, WIDTH)
        wk = by_head(w[:, WIDTH:2 * WIDTH]).reshape(d, WIDTH)
        wv = by_head(w[:, 2 * WIDTH:o_f])
        wv = jnp.concatenate([wv, jnp.zeros_like(wv)], axis=2).reshape(d, 2 * WIDTH)
        ones_cols = (jnp.arange(2 * WIDTH) % LANES >= HEAD_DIM).astype(F32)
        qkv_bias = jnp.concatenate([jnp.zeros((2 * WIDTH,), F32), ones_cols]).reshape(1, 4 * WIDTH)
        qkv = _matmul(h2d, jnp.concatenate([wq, wk, wv], axis=1).astype(BF16), BF16,
                      bias=qkv_bias, name="proj_qkv")
        p_rw = _matmul(h2d, w[:, o_rw:o_g].astype(BF16), F32, tn=896, name="proj_rwkv")
        gate = _matmul(h2d, w[:, o_g:].astype(BF16), BF16, act="sigmoid", name="proj_gate")
        fbias = _forget_bias(h, w[:, o_f:o_rw][:, perm], fox_forget_b[l][perm])
        o_fox = _fox_attention(qkv.reshape(b, s, 4 * WIDTH), fbias)
        w_of = w_out_fox[l].reshape(HEADS, HEAD_DIM, d)[perm].reshape(WIDTH, d)
        o_rwkv = _rwkv_branch(p_rw.reshape(b, s, SHIFT_WIDTH), shift_mu[l], rwkv_w0[l], rwkv_w2[l],
                              rwkv_a0[l], rwkv_a2[l], rwkv_g2[l], rwkv_k_k[l], rwkv_k_a[l],
                              rwkv_r_k[l], ln_x_g[l], ln_x_b[l])

        wr = jnp.zeros((d, LANES), F32)
        wr = wr.at[:, :N_GROUPS].set(router_group_w[l]).at[:, N_GROUPS:N_GROUPS + N_EXPERTS].set(router_expert_w[l])
        br = jnp.zeros((1, LANES), F32)
        br = br.at[0, :N_GROUPS].set(router_group_b[l]).at[0, N_GROUPS:N_GROUPS + N_EXPERTS].set(router_expert_b[l])
        wr_hi = wr.astype(BF16)
        wr_lo = (wr - wr_hi.astype(F32)).astype(BF16)
        x1, h2, route, counts = _merge_out_router(
            o_fox.reshape(t, WIDTH), o_rwkv.reshape(t, WIDTH), gate, x, mod,
            w_of.astype(BF16), w_out_rwkv[l].astype(BF16), w_o[l].astype(BF16), norm2_g[l],
            jnp.stack([wr_hi, wr_lo]), br)

        tm_moe = 256
        dest3, blk_e, nused, zstart, rows = _moe_plan(route, counts, tm_moe)
        xs = _moe_dispatch(h2, dest3, zstart, rows, tm=tm_moe)
        ys = _moe_experts(xs, blk_e, nused, exp_w1[l], exp_w3[l], exp_w2[l])
        out = _moe_combine_final(dest3, route, x1, mod, final_g, ys, s, tm=tm_moe)
    return out.reshape(b, s, d)
```

```python
import functools

import jax
import jax.numpy as jnp
from jax import lax
from jax.experimental import pallas as pl
from jax.experimental.pallas import tpu as pltpu

F32 = jnp.float32
BF16 = jnp.bfloat16
HIGHEST = lax.Precision.HIGHEST

HEADS = 8
HEAD_DIM = 64
WIDTH = HEADS * HEAD_DIM
DECAY_LORA = 64
AAA_LORA = 64
GATE_LORA = 128
SHIFT_WIDTH = 3 * WIDTH + DECAY_LORA + AAA_LORA + GATE_LORA
LN_X_EPS = 64e-5
NORM_EPS = 1e-6
N_GROUPS = 4
EXPERTS_PER_GROUP = 8
N_EXPERTS = N_GROUPS * EXPERTS_PER_GROUP
TOP_K = 2

LANES = 128
CHUNK = 64
RWKV_TILE = 256
MOE_ROWS = 512
VMEM_LIMIT = 48 * 1024 * 1024


def _cparams(sem):
    return pltpu.CompilerParams(dimension_semantics=sem, vmem_limit_bytes=VMEM_LIMIT)


def _dot(a, b):
    return jnp.dot(a.astype(BF16), b.astype(BF16), preferred_element_type=F32)


def _dot_nt(a, b):
    return lax.dot_general(a.astype(BF16), b.astype(BF16), (((1,), (1,)), ((), ())),
                           preferred_element_type=F32)


def _dot_tn(a, b):
    return lax.dot_general(a.astype(BF16), b.astype(BF16), (((0,), (0,)), ((), ())),
                           preferred_element_type=F32)


def _split_dot(x, w_bf16):
    hi = x.astype(BF16)
    lo = (x - hi.astype(F32)).astype(BF16)
    return (jnp.dot(hi, w_bf16, preferred_element_type=F32)
            + jnp.dot(lo, w_bf16, preferred_element_type=F32))


def _softplus(x):
    return jnp.maximum(x, 0.0) + jnp.log1p(jnp.exp(-jnp.abs(x)))


def _mod_kernel(c_ref, w_ref, b_ref, o_ref):
    c = c_ref[...]
    sc = c * jax.nn.sigmoid(c)
    o_ref[...] = jnp.dot(sc, w_ref[...], precision=HIGHEST, preferred_element_type=F32) + b_ref[...]


def _adaln_mod(c, ada_w, ada_b):
    b, d = c.shape
    n = ada_w.shape[1]
    rows = 8
    cp = jnp.zeros((rows, d), F32).at[:b].set(c)
    tn = 1024
    out = pl.pallas_call(
        _mod_kernel,
        out_shape=jax.ShapeDtypeStruct((rows, n), F32),
        grid=(n // tn,),
        in_specs=[pl.BlockSpec((rows, d), lambda j: (0, 0)),
                  pl.BlockSpec((d, tn), lambda j: (0, j)),
                  pl.BlockSpec((1, tn), lambda j: (0, j))],
        out_specs=pl.BlockSpec((rows, tn), lambda j: (0, j)),
        compiler_params=_cparams(("arbitrary",)),
        name="adaln_mod",
    )(cp, ada_w, ada_b.reshape(1, n))
    return out[:b].reshape(b, 6, d)


def _norm_mod_kernel(x_ref, g_ref, mod_ref, o_ref, *, shift_idx, scale_idx):
    x = x_ref[0]
    ms = jnp.mean(x * x, axis=-1, keepdims=True)
    y = x * lax.rsqrt(ms + NORM_EPS) * g_ref[...]
    scale = mod_ref[0, scale_idx:scale_idx + 1, :]
    shift = mod_ref[0, shift_idx:shift_idx + 1, :]
    o_ref[0] = (y * (1.0 + scale) + shift).astype(o_ref.dtype)


def _norm_mod(x, g, mod, shift_idx, scale_idx, tm=1024):
    b, s, d = x.shape
    return pl.pallas_call(
        functools.partial(_norm_mod_kernel, shift_idx=shift_idx, scale_idx=scale_idx),
        out_shape=jax.ShapeDtypeStruct((b, s, d), BF16),
        grid=(b, s // tm),
        in_specs=[pl.BlockSpec((1, tm, d), lambda i, j: (i, j, 0)),
                  pl.BlockSpec((1, d), lambda i, j: (0, 0)),
                  pl.BlockSpec((1, 6, d), lambda i, j: (i, 0, 0))],
        out_specs=pl.BlockSpec((1, tm, d), lambda i, j: (i, j, 0)),
        compiler_params=_cparams(("arbitrary", "arbitrary")),
        name="norm1_mod",
    )(x, g.reshape(1, d), mod)


def _mm_kernel(a_ref, w_ref, *rest, act, has_bias):
    o_ref = rest[-1]
    r = jnp.dot(a_ref[...], w_ref[...], preferred_element_type=F32)
    if has_bias:
        r = r + rest[0][...]
    if act == "sigmoid":
        r = 0.5 * jnp.tanh(0.5 * r) + 0.5
    o_ref[...] = r.astype(o_ref.dtype)


def _matmul(a, w, out_dtype, bias=None, act=None, tm=2048, tn=512, name="proj"):
    t, k = a.shape
    n = w.shape[1]
    tn = min(tn, n)
    in_specs = [pl.BlockSpec((tm, k), lambda i, j: (i, 0)),
                pl.BlockSpec((k, tn), lambda i, j: (0, j))]
    args = [a, w]
    if bias is not None:
        in_specs.append(pl.BlockSpec((1, tn), lambda i, j: (0, j)))
        args.append(bias)
    return pl.pallas_call(
        functools.partial(_mm_kernel, act=act, has_bias=bias is not None),
        out_shape=jax.ShapeDtypeStruct((t, n), out_dtype),
        grid=(t // tm, n // tn),
        in_specs=in_specs,
        out_specs=pl.BlockSpec((tm, tn), lambda i, j: (i, j)),
        compiler_params=_cparams(("arbitrary", "arbitrary")),
        name=name,
    )(*args)


LOG2E = 1.4426950408889634


def _split3(x):
    hi = x.astype(BF16)
    r1 = x - hi.astype(F32)
    mid = r1.astype(BF16)
    lo = (r1 - mid.astype(F32)).astype(BF16)
    return hi, mid, lo


def _fcum_kernel(h_ref, wf_ref, fb_ref, sel_ref, o_ref, carry_ref, *, ts):
    @pl.when(pl.program_id(0) == 0)
    def _():
        carry_ref[...] = jnp.zeros_like(carry_ref)

    ri = lax.broadcasted_iota(jnp.int32, (ts, ts), 0)
    ci = lax.broadcasted_iota(jnp.int32, (ts, ts), 1)
    tri = jnp.where(ri >= ci, 1.0, 0.0).astype(BF16)
    for bi in range(h_ref.shape[0]):
        f = jnp.dot(h_ref[bi], wf_ref[...], preferred_element_type=F32) + fb_ref[...]
        lf = -_softplus(-f)
        cum = carry_ref[bi]
        for piece in _split3(lf):
            cum = cum + jnp.dot(tri, piece, preferred_element_type=F32)
        carry_ref[bi] = cum[ts - 1:ts, :]
        out = jnp.zeros((ts, sel_ref.shape[2]), F32)
        for idx, piece in enumerate(_split3(cum * (-LOG2E))):
            out = out + jnp.dot(piece, sel_ref[idx], preferred_element_type=F32)
        o_ref[bi] = out.astype(o_ref.dtype)


def _forget_bias(h, wf, fb, ts=512):
    b, s, d = h.shape
    pairs = HEADS // 2
    wf_p = jnp.zeros((d, LANES), F32).at[:, :HEADS].set(wf).astype(BF16)
    fb_p = jnp.zeros((1, LANES), F32).at[0, :HEADS].set(fb)
    hh = jnp.arange(HEADS)
    sel = jnp.zeros((3, LANES, pairs * LANES), F32)
    for piece in range(3):
        sel = sel.at[piece, hh, (hh // 2) * LANES + (hh % 2) * 3 + piece].set(1.0)
    return pl.pallas_call(
        functools.partial(_fcum_kernel, ts=ts),
        out_shape=jax.ShapeDtypeStruct((b, s, pairs * LANES), BF16),
        grid=(s // ts,),
        in_specs=[pl.BlockSpec((b, ts, d), lambda j: (0, j, 0)),
                  pl.BlockSpec((d, LANES), lambda j: (0, 0)),
                  pl.BlockSpec((1, LANES), lambda j: (0, 0)),
                  pl.BlockSpec((3, LANES, pairs * LANES), lambda j: (0, 0, 0))],
        out_specs=pl.BlockSpec((b, ts, pairs * LANES), lambda j: (0, j, 0)),
        scratch_shapes=[pltpu.VMEM((b, 1, LANES), F32)],
        compiler_params=_cparams(("arbitrary",)),
        name="forget_bias",
    )(h, wf_p, fb_p, sel.astype(BF16))


FOX_BQ = 1024
FOX_BK = 512


def _fox_kernel(first_ref, q_ref, k_ref, v_ref, a_ref, o_ref, m_ref, acc_ref, *, bq, bk):
    i = pl.program_id(2)
    lane = lax.broadcasted_iota(jnp.int32, (bq, LANES), 1)
    first = lane < HEAD_DIM
    qs = q_ref[0].astype(F32) * (HEAD_DIM ** -0.5 * LOG2E)
    aug0 = jnp.where(lane < 3, 1.0, 0.0)
    aug1 = jnp.where((lane >= 3) & (lane < 6), 1.0, 0.0)
    q01 = jnp.concatenate([jnp.concatenate([jnp.where(first, qs, 0.0), aug0], axis=1),
                           jnp.concatenate([jnp.where(first, 0.0, qs), aug1], axis=1)],
                          axis=0).astype(BF16)
    m_ref[...] = jnp.full_like(m_ref, -jnp.inf)
    acc_ref[...] = jnp.zeros_like(acc_ref)

    def both_heads(x, r0):
        if r0 == 0:
            return x[...]
        return jnp.concatenate([x[r0:bq], x[bq + r0:2 * bq]], axis=0)

    def logits(j, r0=0):
        start = pl.multiple_of(j * bk, bk)
        kb = jnp.concatenate([k_ref[0, pl.ds(start, bk), :], a_ref[0, pl.ds(start, bk), :]], axis=1)
        return lax.dot_general(both_heads(q01, r0), kb, (((1,), (1,)), ((), ())),
                               preferred_element_type=F32)

    def consume(j, z, r0=0):
        nr = bq - r0
        start = pl.multiple_of(j * bk, bk)
        vb = v_ref[0, pl.ds(start, bk), :]
        m_prev = both_heads(m_ref, r0)
        m_new = jnp.maximum(m_prev, jnp.max(z, axis=1, keepdims=True))
        alpha = jnp.exp2(m_prev - m_new)
        p = jnp.exp2(z - jnp.concatenate([m_new] * (bk // LANES), axis=1)).astype(BF16)
        pv = jnp.concatenate([jnp.dot(p[:nr], vb[:, :LANES], preferred_element_type=F32),
                              jnp.dot(p[nr:], vb[:, LANES:], preferred_element_type=F32)], axis=0)
        acc_new = alpha * both_heads(acc_ref, r0) + pv
        if r0 == 0:
            acc_ref[...] = acc_new
            m_ref[...] = m_new
        else:
            for half, dst in ((slice(0, nr), slice(r0, bq)), (slice(nr, 2 * nr), slice(bq + r0, 2 * bq))):
                acc_ref[dst] = acc_new[half]
                m_ref[dst] = m_new[half]

    per_q = bq // bk
    n_full = i * per_q

    def body(j, carry):
        consume(j, logits(j))
        return carry

    lax.fori_loop(first_ref[pl.program_id(0), pl.program_id(1), i], n_full, body, 0)
    for d in range(per_q):
        r0 = d * bk
        row = lax.broadcasted_iota(jnp.int32, (bq - r0, bk), 0)
        col = lax.broadcasted_iota(jnp.int32, (bq - r0, bk), 1)
        keep = col <= row
        z = logits(n_full + d, r0)
        consume(n_full + d, jnp.where(jnp.concatenate([keep, keep], axis=0), z, -jnp.inf), r0)
    acc = acc_ref[...]
    o = acc / pltpu.roll(acc, HEAD_DIM, 1)
    o_ref[0] = jnp.where(first, o[:bq], pltpu.roll(o[bq:], HEAD_DIM, 1)).astype(o_ref.dtype)


def _norm_bound_kernel(x_ref, sel_ref, o_ref):
    x = x_ref[...].astype(F32)
    ssq = _dot(x * x, sel_ref[...])
    o_ref[0] = jnp.broadcast_to(jnp.max(ssq, axis=0, keepdims=True), o_ref.shape[1:])


def _fox_first_block(qkv, fbias, bq, bk):
    b, s, _ = qkv.shape
    t = b * s
    nq, nk = s // bq, s // bk
    n_heads2 = 2 * HEADS
    sel = (jnp.arange(2 * WIDTH)[:, None] // HEAD_DIM == jnp.arange(LANES)[None, :]).astype(BF16)
    bounds = pl.pallas_call(
        _norm_bound_kernel,
        out_shape=jax.ShapeDtypeStruct((t // bk, 8, LANES), F32),
        grid=(t // bk,),
        in_specs=[pl.BlockSpec((bk, 2 * WIDTH), lambda i: (i, 0)),
                  pl.BlockSpec((2 * WIDTH, LANES), lambda i: (0, 0))],
        out_specs=pl.BlockSpec((1, 8, LANES), lambda i: (i, 0, 0)),
        compiler_params=_cparams(("arbitrary",)),
        name="fox_norm_bounds",
    )(qkv.reshape(t, -1), sel)
    nrm = jnp.sqrt(bounds[:, 0, :n_heads2]).reshape(b, nk, n_heads2) * 1.01
    qn = nrm[..., :HEADS] * (HEAD_DIM ** -0.5 * LOG2E * 1.01)
    kn = nrm[..., HEADS:]
    per_q = bq // bk
    qn_i = qn.reshape(b, nq, per_q, HEADS).max(axis=2)
    kn_i = kn.reshape(b, nq, per_q, HEADS).max(axis=2)
    kn_pre = lax.cummax(kn, axis=1)
    pairs = HEADS // 2
    def bias_rows(rows):
        pieces = rows.astype(F32).reshape(b, -1, pairs, LANES)[..., :6].reshape(b, -1, pairs, 2, 3)
        return pieces.sum(-1).reshape(b, -1, HEADS)

    nb_end = bias_rows(fbias[:, bk - 1::bk])
    nb_start = bias_rows(fbias[:, ::bq])
    gap = nb_start[:, :, None, :] - nb_end[:, None, :, :]
    need = qn_i[:, :, None, :] * (kn_pre[:, None, :, :] + kn_i[:, :, None, :]) + 152.0
    skip = (gap > need).reshape(b, nq, nk, pairs, 2).all(axis=-1)
    n_full = jnp.arange(nq) * per_q
    skip = skip & (jnp.arange(nk)[None, None, :, None] < n_full[None, :, None, None])
    first = jnp.argmin(skip, axis=2)
    return first.transpose(0, 2, 1).astype(jnp.int32)


def _fox_attention(qkv, fbias, bq=FOX_BQ, bk=FOX_BK):
    b, s, _ = qkv.shape
    pairs = HEADS // 2
    cb = WIDTH // LANES
    first = _fox_first_block(qkv, fbias, bq, bk)
    grid_spec = pltpu.PrefetchScalarGridSpec(
        num_scalar_prefetch=1,
        grid=(b, pairs, s // bq),
        in_specs=[pl.BlockSpec((1, bq, LANES), lambda bi, hp, i, fr: (bi, i, hp)),
                  pl.BlockSpec((1, s, LANES), lambda bi, hp, i, fr: (bi, 0, cb + hp)),
                  pl.BlockSpec((1, s, 2 * LANES), lambda bi, hp, i, fr: (bi, 0, cb + hp)),
                  pl.BlockSpec((1, s, LANES), lambda bi, hp, i, fr: (bi, 0, hp))],
        out_specs=pl.BlockSpec((1, bq, LANES), lambda bi, hp, i, fr: (bi, i, hp)),
        scratch_shapes=[pltpu.VMEM((2 * bq, LANES), F32), pltpu.VMEM((2 * bq, LANES), F32)],
    )
    return pl.pallas_call(
        functools.partial(_fox_kernel, bq=bq, bk=bk),
        out_shape=jax.ShapeDtypeStruct((b, s, WIDTH), BF16),
        grid_spec=grid_spec,
        compiler_params=_cparams(("arbitrary", "arbitrary", "arbitrary")),
        name="fox_attention",
    )(first, qkv, qkv, qkv, fbias)


PAIR = 2 * HEAD_DIM
GROUPS = WIDTH // PAIR


def _group(x, g):
    return x[:, g * PAIR:(g + 1) * PAIR]


def _head_sum(x, bd2):
    return jnp.concatenate([_dot(_group(x, g), bd2) for g in range(GROUPS)], axis=1)


def _head_apply(mats, x, lane_first):
    rows = mats.shape[1]
    outs = []
    for g in range(GROUPS):
        res = _dot(mats[2 * g:2 * g + 2].reshape(2 * rows, rows), _group(x, g))
        outs.append(jnp.where(lane_first, res[:rows], res[rows:]))
    return jnp.concatenate(outs, axis=1)


def _rwkv_kernel(p_ref, mu_ref, w0_ref, a0_ref, kk_ref, ka_ref, rk_ref, lng_ref, lnb_ref,
                 wwa_ref, g2_ref, bd_ref, o_ref, st_ref, prev_ref):
    L = CHUNK

    @pl.when(pl.program_id(1) == 0)
    def _():
        st_ref[...] = jnp.zeros_like(st_ref)
        prev_ref[...] = jnp.zeros_like(prev_ref)

    p = p_ref[0]
    T = p.shape[0]
    rowi = lax.broadcasted_iota(jnp.int32, p.shape, 0)
    prev = jnp.where(rowi == 0, prev_ref[...], pltpu.roll(p, 1, 0))
    prev_ref[...] = p[T - 1:T, :]
    ps = p + (prev - p) * mu_ref[...]
    r = ps[:, 0:WIDTH]
    k = ps[:, WIDTH:2 * WIDTH]
    v = ps[:, 2 * WIDTH:3 * WIDTH]
    wa_in = ps[:, 3 * WIDTH:3 * WIDTH + DECAY_LORA + AAA_LORA]
    gd = ps[:, 3 * WIDTH + DECAY_LORA + AAA_LORA:]
    lane_wa = lax.broadcasted_iota(jnp.int32, wa_in.shape, 1)
    wa_act = jnp.where(lane_wa < DECAY_LORA, jnp.tanh(wa_in), wa_in)
    wa = _dot(wa_act, wwa_ref[...])
    log_w = -_softplus(-(w0_ref[...] + wa[:, :WIDTH])) - 0.5
    lw = -jnp.exp(log_w)
    a = jax.nn.sigmoid(a0_ref[...] + wa[:, WIDTH:])
    out_gate = _dot(jax.nn.sigmoid(gd), g2_ref[...])
    bd = bd_ref[...]
    kk0 = k * kk_ref[...]
    kk = kk0 * lax.rsqrt(jnp.maximum(_head_sum(kk0 * kk0, bd), 1e-24))
    k2 = k * (1.0 + (a - 1.0) * ka_ref[...])
    av = -kk
    bv = kk * a

    n_sub = T // L
    rt_i = lax.broadcasted_iota(jnp.int32, (T, T), 0)
    ct_i = lax.broadcasted_iota(jnp.int32, (T, T), 1)
    tri_tile = (rt_i >= ct_i) & (rt_i // L == ct_i // L)
    cl = _split_dot_left(jnp.where(tri_tile, 1.0, 0.0).astype(BF16), lw)
    cl_end = jnp.concatenate([jnp.broadcast_to(cl[(c + 1) * L - 1:(c + 1) * L, :], (L, WIDTH))
                              for c in range(n_sub)], axis=0)
    at_all = av * jnp.exp(cl - lw)
    rt_all = r * jnp.exp(cl)
    einv = jnp.exp(-cl)
    bt_all = bv * einv
    kt_all = k2 * einv
    edec = jnp.exp(cl_end - cl)
    b_end_all = bv * edec
    k_end_all = k2 * edec

    ri = lax.broadcasted_iota(jnp.int32, (L, L), 0)
    ci = lax.broadcasted_iota(jnp.int32, (L, L), 1)
    tri_incl = ri >= ci
    tri_strict = ri > ci
    eye = jnp.where(ri == ci, 1.0, 0.0)
    lane_first = lax.broadcasted_iota(jnp.int32, (L, PAIR), 1) < HEAD_DIM
    qr = lax.broadcasted_iota(jnp.int32, (PAIR, PAIR), 0) < HEAD_DIM
    qc = lax.broadcasted_iota(jnp.int32, (PAIR, PAIR), 1) < HEAD_DIM
    same_head = qr == qc

    def bmm(x, y):
        return lax.dot_general(x.astype(BF16), y.astype(BF16), (((2,), (1,)), ((0,), (0,))),
                               preferred_element_type=F32)

    def chunk_terms(c):
        rows = slice(c * L, (c + 1) * L)
        at, rt, bt, kt, vc = at_all[rows], rt_all[rows], bt_all[rows], kt_all[rows], v[rows]
        sb_heads, sk_heads = [], []
        for g in range(GROUPS):
            at_g, rt_g = _group(at, g), _group(rt, g)
            lhs = jnp.concatenate([jnp.where(lane_first, at_g, 0.0), jnp.where(lane_first, rt_g, 0.0),
                                   jnp.where(lane_first, 0.0, at_g), jnp.where(lane_first, 0.0, rt_g)],
                                  axis=0).astype(BF16)
            sb_g = _dot_nt(lhs, _group(bt, g))
            sk_g = _dot_nt(lhs, _group(kt, g))
            for hh in range(2):
                sb_heads.append(sb_g[hh * 2 * L:(hh + 1) * 2 * L])
                sk_heads.append(sk_g[hh * 2 * L:(hh + 1) * 2 * L])
        sb = jnp.stack(sb_heads)
        sk = jnp.stack(sk_heads)
        n_ab = jnp.where(tri_strict, sb[:, :L, :], 0.0)
        a_ak = jnp.where(tri_strict, sk[:, :L, :], 0.0)
        a_rb = jnp.where(tri_incl, sb[:, L:, :], 0.0)
        a_rk = jnp.where(tri_incl, sk[:, L:, :], 0.0)
        tinv = eye + n_ab
        pw = bmm(n_ab, n_ab)
        span = 2
        while 2 * span < L:
            both = bmm(jnp.concatenate([tinv, pw], axis=1), pw)
            tinv = tinv + both[:, :L, :]
            pw = both[:, L:, :]
            span *= 2
        tinv = tinv + bmm(tinv, pw)
        av_term = _head_apply(a_ak, vc, lane_first)
        pm = _head_apply(tinv, at, lane_first)
        qm = _head_apply(tinv, av_term, lane_first)
        rkv = _head_apply(a_rk, vc, lane_first)
        return pm, qm, rkv, a_rb

    terms = [chunk_terms(c) for c in range(n_sub)]

    y_chunks = []
    for c in range(n_sub):
        rows = slice(c * L, (c + 1) * L)
        pm, qm, rkv, a_rb = terms[c]
        rt, vc, b_end, k_end = rt_all[rows], v[rows], b_end_all[rows], k_end_all[rows]
        gam_last = jnp.exp(cl[(c + 1) * L - 1:(c + 1) * L, :])
        u_parts, ys_parts = [], []
        for g in range(GROUPS):
            pr = _dot_nt(jnp.concatenate([_group(pm, g), _group(rt, g)], axis=0), st_ref[g])
            u_parts.append(pr[:L] + _group(qm, g))
            ys_parts.append(pr[L:])
        u = jnp.concatenate(u_parts, axis=1)
        y_chunks.append(jnp.concatenate(ys_parts, axis=1) + _head_apply(a_rb, u, lane_first) + rkv)
        for g in range(GROUPS):
            upd = _dot_tn(_group(u, g), _group(b_end, g)) + _dot_tn(_group(vc, g), _group(k_end, g))
            st_ref[g] = st_ref[g] * _group(gam_last, g) + jnp.where(same_head, upd, 0.0)
    y = jnp.concatenate(y_chunks, axis=0)

    inv_n = 1.0 / HEAD_DIM
    mean = _head_sum(y, bd) * inv_n
    dlt = y - mean
    var = _head_sum(dlt * dlt, bd) * inv_n
    yn = dlt * lax.rsqrt(var + LN_X_EPS) * lng_ref[...] + lnb_ref[...]
    bonus = _head_sum(r * k2 * rk_ref[...], bd) * v
    o_ref[0] = ((yn + bonus) * out_gate).astype(o_ref.dtype)


def _split_dot_left(w_bf16, x):
    hi = x.astype(BF16)
    r1 = x - hi.astype(F32)
    mid = r1.astype(BF16)
    lo = (r1 - mid.astype(F32)).astype(BF16)
    return (jnp.dot(w_bf16, hi, preferred_element_type=F32)
            + jnp.dot(w_bf16, mid, preferred_element_type=F32)
            + jnp.dot(w_bf16, lo, preferred_element_type=F32))


def _rwkv_branch(p_rw, mu, w0, w2, a0, a2, g2, k_k, k_a, r_k, ln_g, ln_b):
    b, s, sw = p_rw.shape
    row = lambda t: t.reshape(1, -1).astype(F32)
    wwa = jnp.zeros((DECAY_LORA + AAA_LORA, 2 * WIDTH), F32)
    wwa = wwa.at[:DECAY_LORA, :WIDTH].set(w2).at[DECAY_LORA:, WIDTH:].set(a2).astype(BF16)
    hid = jnp.arange(PAIR) // HEAD_DIM
    bd = (hid[:, None] == hid[None, :]).astype(BF16)
    const = lambda shape: pl.BlockSpec(shape, lambda i, j: (0,) * len(shape))
    return pl.pallas_call(
        _rwkv_kernel,
        out_shape=jax.ShapeDtypeStruct((b, s, WIDTH), BF16),
        grid=(b, s // RWKV_TILE),
        in_specs=[pl.BlockSpec((1, RWKV_TILE, sw), lambda i, j: (i, j, 0)),
                  const((1, sw)), const((1, WIDTH)), const((1, WIDTH)), const((1, WIDTH)),
                  const((1, WIDTH)), const((1, WIDTH)), const((1, WIDTH)), const((1, WIDTH)),
                  const((DECAY_LORA + AAA_LORA, 2 * WIDTH)), const((GATE_LORA, WIDTH)),
                  const((PAIR, PAIR))],
        out_specs=pl.BlockSpec((1, RWKV_TILE, WIDTH), lambda i, j: (i, j, 0)),
        scratch_shapes=[pltpu.VMEM((GROUPS, PAIR, PAIR), F32), pltpu.VMEM((1, sw), F32)],
        compiler_params=_cparams(("arbitrary", "arbitrary")),
        name="rwkv7_scan",
    )(p_rw, row(mu), row(w0), row(a0), row(k_k), row(k_a), row(r_k), row(ln_g), row(ln_b),
      wwa, g2.astype(BF16), bd)


def _out_kernel(of_ref, orw_ref, gate_ref, x_ref, mod_ref, wof_ref, wor_ref, wo_ref, n2g_ref,
                wr_ref, br_ref, x1_ref, h2_ref, route_ref, counts_ref, cnt_ref):
    d = x_ref.shape[-1]
    gate = gate_ref[...].astype(F32)
    merged = (gate[:, :d] * jnp.dot(of_ref[...], wof_ref[...], preferred_element_type=F32)
              + gate[:, d:] * jnp.dot(orw_ref[...], wor_ref[...], preferred_element_type=F32))
    gate1 = mod_ref[0, 2:3, :]
    shift2 = mod_ref[0, 3:4, :]
    scale2 = mod_ref[0, 4:5, :]
    x1 = x_ref[...] + gate1 * jnp.dot(merged.astype(BF16), wo_ref[...], preferred_element_type=F32)
    x1_ref[...] = x1
    ms = jnp.mean(x1 * x1, axis=-1, keepdims=True)
    h2 = x1 * lax.rsqrt(ms + NORM_EPS) * n2g_ref[...] * (1.0 + scale2) + shift2
    h2_ref[...] = h2

    h2_hi = h2.astype(BF16)
    h2_lo = (h2 - h2_hi.astype(F32)).astype(BF16)
    logits = (jnp.dot(h2_hi, wr_ref[0], preferred_element_type=F32)
              + jnp.dot(h2_lo, wr_ref[0], preferred_element_type=F32)
              + jnp.dot(h2_hi, wr_ref[1], preferred_element_type=F32)) + br_ref[...]
    lane = lax.broadcasted_iota(jnp.int32, logits.shape, 1)
    neg = -jnp.inf
    big = jnp.int32(LANES)
    gl = jnp.where(lane < N_GROUPS, logits, neg)
    gmax = jnp.max(gl, axis=1, keepdims=True)
    gidx = jnp.min(jnp.where(gl == gmax, lane, big), axis=1, keepdims=True)
    g_p = 1.0 / jnp.sum(jnp.exp(gl - gmax), axis=1, keepdims=True)
    e_lane = lane - N_GROUPS
    in_grp = (e_lane >= 0) & (e_lane < N_EXPERTS) & ((e_lane // EXPERTS_PER_GROUP) == gidx)
    sel = jnp.where(in_grp, logits, neg)
    m1 = jnp.max(sel, axis=1, keepdims=True)
    i1 = jnp.min(jnp.where(sel == m1, lane, big), axis=1, keepdims=True)
    sel2 = jnp.where(lane == i1, neg, sel)
    m2 = jnp.max(sel2, axis=1, keepdims=True)
    i2 = jnp.min(jnp.where(sel2 == m2, lane, big), axis=1, keepdims=True)
    e21 = jnp.exp(m2 - m1)
    w_first = g_p / (1.0 + e21)
    w_second = g_p * e21 / (1.0 + e21)
    @pl.when(pl.program_id(0) == 0)
    def _():
        cnt_ref[...] = jnp.zeros_like(cnt_ref)

    tm = logits.shape[0]
    oh1 = lane == i1
    oh2 = lane == i2
    both = jnp.where(oh1 | oh2, 1.0, 0.0)
    before = (lax.broadcasted_iota(jnp.int32, (tm, tm), 0)
              > lax.broadcasted_iota(jnp.int32, (tm, tm), 1))
    seen = jnp.dot(jnp.where(before, 1.0, 0.0).astype(BF16), both.astype(BF16),
                   preferred_element_type=F32) + cnt_ref[...]
    rank1 = jnp.sum(jnp.where(oh1, seen, 0.0), axis=1, keepdims=True)
    rank2 = jnp.sum(jnp.where(oh2, seen, 0.0), axis=1, keepdims=True)
    cnt_ref[...] = cnt_ref[...] + jnp.sum(both, axis=0, keepdims=True)
    counts_ref[...] = jnp.broadcast_to(cnt_ref[...], counts_ref.shape)

    route = jnp.where(lane == 0, (i1 - N_GROUPS).astype(F32),
                      jnp.where(lane == 1, (i2 - N_GROUPS).astype(F32),
                                jnp.where(lane == 2, w_first,
                                          jnp.where(lane == 3, w_second,
                                                    jnp.where(lane == 4, rank1,
                                                              jnp.where(lane == 5, rank2, 0.0))))))
    route_ref[...] = route


def _merge_out_router(o_fox, o_rw, gate, x, mod, wof, wor, wo, n2g, wr, br, tm=512):
    b, s, d = x.shape
    t = b * s
    spb = s // tm
    rowspec = lambda w: pl.BlockSpec((tm, w), lambda i: (i, 0))
    const = lambda shape: pl.BlockSpec(shape, lambda i: (0,) * len(shape))
    return pl.pallas_call(
        _out_kernel,
        out_shape=(jax.ShapeDtypeStruct((t, d), F32), jax.ShapeDtypeStruct((t, d), F32),
                   jax.ShapeDtypeStruct((t, LANES), F32), jax.ShapeDtypeStruct((8, LANES), F32)),
        grid=(t // tm,),
        in_specs=[rowspec(WIDTH), rowspec(WIDTH), rowspec(2 * d), rowspec(d),
                  pl.BlockSpec((1, 6, d), lambda i: (i // spb, 0, 0)),
                  const((WIDTH, d)), const((WIDTH, d)), const((d, d)), const((1, d)),
                  const((2, d, LANES)), const((1, LANES))],
        out_specs=(rowspec(d), rowspec(d), rowspec(LANES), const((8, LANES))),
        scratch_shapes=[pltpu.VMEM((1, LANES), F32)],
        compiler_params=_cparams(("arbitrary",)),
        name="merge_out_router",
    )(o_fox, o_rw, gate, x.reshape(t, d), mod, wof, wor, wo, n2g.reshape(1, d), wr, br)


def _dispatch_kernel(zstart_ref, dest_ref, h_ref, xs_ref, zero_ref, sem, *, tm):
    @pl.when(pl.program_id(0) == 0)
    def _():
        zero_ref[...] = jnp.zeros_like(zero_ref)
        for e in range(N_EXPERTS):
            zrow = pl.multiple_of(zstart_ref[e], MOE_ROWS)
            pltpu.make_async_copy(zero_ref, xs_ref.at[pl.ds(zrow, MOE_ROWS)], sem).start()
        for e in range(N_EXPERTS):
            pltpu.make_async_copy(zero_ref, xs_ref.at[pl.ds(0, MOE_ROWS)], sem).wait()
        for phase in ("start", "wait"):
            for e in range(N_EXPERTS):
                trow = pl.multiple_of(zstart_ref[N_EXPERTS] + e * MOE_ROWS, MOE_ROWS)

                @pl.when(trow < xs_ref.shape[0])
                def _():
                    tail = pltpu.make_async_copy(zero_ref, xs_ref.at[pl.ds(trow, MOE_ROWS)], sem)
                    tail.start() if phase == "start" else tail.wait()

    def issue(r, carry):
        for kk in range(TOP_K):
            dst = dest_ref[0, 0, TOP_K * r + kk]
            pltpu.make_async_copy(h_ref.at[pl.ds(r, 1)], xs_ref.at[pl.ds(dst, 1)], sem).start(priority=kk)
        return carry

    lax.fori_loop(0, tm, issue, 0, unroll=8)
    for kk in range(TOP_K):
        pltpu.make_async_copy(h_ref, xs_ref.at[pl.ds(0, tm)], sem).wait()


def _moe_dispatch(h2, dest3, zstart, rows, tm=256):
    t, d = h2.shape
    grid_spec = pltpu.PrefetchScalarGridSpec(
        num_scalar_prefetch=1,
        grid=(t // tm,),
        in_specs=[pl.BlockSpec((1, 1, TOP_K * tm), lambda i, zs: (i, 0, 0), memory_space=pltpu.SMEM),
                  pl.BlockSpec((tm, d), lambda i, zs: (i, 0))],
        out_specs=pl.BlockSpec(memory_space=pl.ANY),
        scratch_shapes=[pltpu.VMEM((MOE_ROWS, d), F32), pltpu.SemaphoreType.DMA(())],
    )
    return pl.pallas_call(
        functools.partial(_dispatch_kernel, tm=tm),
        out_shape=jax.ShapeDtypeStruct((rows, d), F32),
        grid_spec=grid_spec,
        compiler_params=_cparams(("arbitrary",)),
        name="moe_dispatch",
    )(zstart, dest3, h2)


def _expert_kernel(blk_e_ref, nused_ref, xs_ref, w1_ref, w3_ref, w2_ref, o_ref):
    del blk_e_ref
    live = pl.program_id(0) * MOE_ROWS < nused_ref[0]

    @pl.when(live)
    def _():
        xb = xs_ref[...].astype(BF16)
        h1 = jnp.dot(xb, w1_ref[0].astype(BF16), preferred_element_type=F32)
        h3 = jnp.dot(xb, w3_ref[0].astype(BF16), preferred_element_type=F32)
        hh = (h1 * jax.nn.sigmoid(h1)) * h3
        o_ref[...] = jnp.dot(hh.astype(BF16), w2_ref[0].astype(BF16), preferred_element_type=F32)

    @pl.when(jnp.logical_not(live))
    def _():
        o_ref[...] = jnp.zeros_like(o_ref)


def _moe_experts(xs, blk_e, nused, w1, w3, w2):
    rows, d = xs.shape
    de = w1.shape[-1]
    grid_spec = pltpu.PrefetchScalarGridSpec(
        num_scalar_prefetch=2,
        grid=(rows // MOE_ROWS,),
        in_specs=[pl.BlockSpec((MOE_ROWS, d), lambda i, be, nu: (i, 0)),
                  pl.BlockSpec((1, d, de), lambda i, be, nu: (be[i], 0, 0)),
                  pl.BlockSpec((1, d, de), lambda i, be, nu: (be[i], 0, 0)),
                  pl.BlockSpec((1, de, d), lambda i, be, nu: (be[i], 0, 0))],
        out_specs=pl.BlockSpec((MOE_ROWS, d), lambda i, be, nu: (i, 0)),
    )
    return pl.pallas_call(
        _expert_kernel,
        out_shape=jax.ShapeDtypeStruct((rows, d), F32),
        grid_spec=grid_spec,
        compiler_params=_cparams(("arbitrary",)),
        name="moe_experts",
    )(blk_e, nused, xs, w1, w3, w2)


def _final_kernel(dest_ref, dest_next_ref, route_ref, x1_ref, mod_ref, fg_ref, ys_ref, o_ref, buf_ref, sem,
                  *, tm):
    i = pl.program_id(0)
    half = i & 1

    def gather(idx_ref, into):
        def issue(r, carry):
            for kk in range(TOP_K):
                src = idx_ref[0, 0, TOP_K * r + kk]
                pltpu.make_async_copy(ys_ref.at[pl.ds(src, 1)], buf_ref.at[into, kk, pl.ds(r, 1)],
                                      sem.at[into]).start(priority=kk)
            return carry

        lax.fori_loop(0, tm, issue, 0, unroll=8)

    @pl.when(i == 0)
    def _():
        gather(dest_ref, 0)

    @pl.when(i + 1 < pl.num_programs(0))
    def _():
        gather(dest_next_ref, 1 - half)

    for kk in range(TOP_K):
        pltpu.make_async_copy(ys_ref.at[pl.ds(0, tm)], buf_ref.at[half, kk], sem.at[half]).wait()

    route = route_ref[...]
    y = route[:, 2:3] * buf_ref[half, 0] + route[:, 3:4] * buf_ref[half, 1]
    gate2 = mod_ref[0, 5:6, :]
    x2 = x1_ref[...] + gate2 * y
    ms = jnp.mean(x2 * x2, axis=-1, keepdims=True)
    o_ref[...] = x2 * lax.rsqrt(ms + NORM_EPS) * fg_ref[...]


def _moe_combine_final(dest3, route, x1, mod, final_g, ys, s, tm=256):
    t, d = x1.shape
    spb = s // tm
    return pl.pallas_call(
        functools.partial(_final_kernel, tm=tm),
        out_shape=jax.ShapeDtypeStruct((t, d), F32),
        grid=(t // tm,),
        in_specs=[pl.BlockSpec((1, 1, TOP_K * tm), lambda i: (i, 0, 0), memory_space=pltpu.SMEM),
                  pl.BlockSpec((1, 1, TOP_K * tm), lambda i: (jnp.minimum(i + 1, t // tm - 1), 0, 0),
                               memory_space=pltpu.SMEM),
                  pl.BlockSpec((tm, LANES), lambda i: (i, 0)),
                  pl.BlockSpec((tm, d), lambda i: (i, 0)),
                  pl.BlockSpec((1, 6, d), lambda i: (i // spb, 0, 0)),
                  pl.BlockSpec((1, d), lambda i: (0, 0)),
                  pl.BlockSpec(memory_space=pl.ANY)],
        out_specs=pl.BlockSpec((tm, d), lambda i: (i, 0)),
        scratch_shapes=[pltpu.VMEM((2, TOP_K, tm, d), F32), pltpu.SemaphoreType.DMA((2,))],
        compiler_params=_cparams(("arbitrary",)),
        name="moe_combine_final",
    )(dest3, dest3, route, x1, mod, final_g.reshape(1, d), ys)


def _moe_plan(route, counts, tm):
    t = route.shape[0]
    m = t * TOP_K
    flat_e = route[:, :TOP_K].astype(jnp.int32).reshape(m)
    rank = route[:, 4:4 + TOP_K].astype(jnp.int32).reshape(m)
    counts = counts[0, N_GROUPS:N_GROUPS + N_EXPERTS].astype(jnp.int32)
    padded = (counts + MOE_ROWS - 1) // MOE_ROWS * MOE_ROWS
    pad_end = jnp.cumsum(padded)
    pad_start = pad_end - padded
    experts = jnp.arange(N_EXPERTS, dtype=jnp.int32)
    start_of = jnp.sum(jnp.where(flat_e[:, None] == experts[None, :], pad_start[None, :], 0), axis=1)
    dest = (start_of + rank).astype(jnp.int32)
    n_blocks = m // MOE_ROWS + N_EXPERTS
    blk_start = jnp.arange(n_blocks, dtype=jnp.int32) * MOE_ROWS
    blk_e = jnp.minimum(jnp.sum(pad_end[None, :] <= blk_start[:, None], axis=1), N_EXPERTS - 1).astype(jnp.int32)
    nused = pad_end[-1:].astype(jnp.int32)
    zstart = jnp.concatenate([jnp.maximum(pad_end - MOE_ROWS, 0), pad_end[-1:]]).astype(jnp.int32)
    return dest.reshape(t // tm, 1, TOP_K * tm), blk_e, nused, zstart, n_blocks * MOE_ROWS


def kernel(x, c, ada_w, ada_b, norm1_g, w_in, fox_forget_b, shift_mu, rwkv_w0, rwkv_w2, rwkv_a0, rwkv_a2, rwkv_g2, rwkv_k_k, rwkv_k_a, rwkv_r_k, ln_x_g, ln_x_b, w_out_fox, w_out_rwkv, w_o, norm2_g, router_group_w, router_group_b, router_expert_w, router_expert_b, exp_w1, exp_w3, exp_w2, final_g):
    b, s, d = x.shape
    t = b * s
    assert ada_w.shape[0] == 1, "the final norm is fused into the last layer's combine; one layer is laid out"
    for l in range(1):
        mod = _adaln_mod(c, ada_w[l], ada_b[l])

        h = _norm_mod(x, norm1_g[l], mod, shift_idx=0, scale_idx=1)
        h2d = h.reshape(t, d)
        w = w_in[l]
        o_f = 3 * WIDTH
        o_rw = o_f + HEADS
        o_g = o_rw + SHIFT_WIDTH
        perm = jnp.argsort(fox_forget_b[l])
        by_head = lambda m: m.reshape(d, HEADS, HEAD_DIM)[:, perm]
        wq = by_head(w[:, :WIDTH]).reshape(d, WIDTH)
        wk = by_head(w[:, WIDTH:2 * WIDTH]).reshape(d, WIDTH)
        wv = by_head(w[:, 2 * WIDTH:o_f])
        wv = jnp.concatenate([wv, jnp.zeros_like(wv)], axis=2).reshape(d, 2 * WIDTH)
        ones_cols = (jnp.arange(2 * WIDTH) % LANES >= HEAD_DIM).astype(F32)
        qkv_bias = jnp.concatenate([jnp.zeros((2 * WIDTH,), F32), ones_cols]).reshape(1, 4 * WIDTH)
        qkv = _matmul(h2d, jnp.concatenate([wq, wk, wv], axis=1).astype(BF16), BF16,
                      bias=qkv_bias, name="proj_qkv")
        p_rw = _matmul(h2d, w[:, o_rw:o_g].astype(BF16), F32, tn=896, name="proj_rwkv")
        gate = _matmul(h2d, w[:, o_g:].astype(BF16), BF16, act="sigmoid", name="proj_gate")
        fbias = _forget_bias(h, w[:, o_f:o_rw][:, perm], fox_forget_b[l][perm])
        o_fox = _fox_attention(qkv.reshape(b, s, 4 * WIDTH), fbias)
        w_of = w_out_fox[l].reshape(HEADS, HEAD_DIM, d)[perm].reshape(WIDTH, d)
        o_rwkv = _rwkv_branch(p_rw.reshape(b, s, SHIFT_WIDTH), shift_mu[l], rwkv_w0[l], rwkv_w2[l],
                              rwkv_a0[l], rwkv_a2[l], rwkv_g2[l], rwkv_k_k[l], rwkv_k_a[l],
                              rwkv_r_k[l], ln_x_g[l], ln_x_b[l])

        wr = jnp.zeros((d, LANES), F32)
        wr = wr.at[:, :N_GROUPS].set(router_group_w[l]).at[:, N_GROUPS:N_GROUPS + N_EXPERTS].set(router_expert_w[l])
        br = jnp.zeros((1, LANES), F32)
        br = br.at[0, :N_GROUPS].set(router_group_b[l]).at[0, N_GROUPS:N_GROUPS + N_EXPERTS].set(router_expert_b[l])
        wr_hi = wr.astype(BF16)
        wr_lo = (wr - wr_hi.astype(F32)).astype(BF16)
        x1, h2, route, counts = _merge_out_router(
            o_fox.reshape(t, WIDTH), o_rwkv.reshape(t, WIDTH), gate, x, mod,
            w_of.astype(BF16), w_out_rwkv[l].astype(BF16), w_o[l].astype(BF16), norm2_g[l],
            jnp.stack([wr_hi, wr_lo]), br)

        tm_moe = 256
        dest3, blk_e, nused, zstart, rows = _moe_plan(route, counts, tm_moe)
        xs = _moe_dispatch(h2, dest3, zstart, rows, tm=tm_moe)
        ys = _moe_experts(xs, blk_e, nused, exp_w1[l], exp_w3[l], exp_w2[l])
        out = _moe_combine_final(dest3, route, x1, mod, final_g, ys, s, tm=tm_moe)
    return out.reshape(b, s, d)
```

```python
import functools

import jax
import jax.numpy as jnp
from jax import lax
from jax.experimental import pallas as pl
from jax.experimental.pallas import tpu as pltpu

F32 = jnp.float32
BF16 = jnp.bfloat16
HIGHEST = lax.Precision.HIGHEST

HEADS = 8
HEAD_DIM = 64
WIDTH = HEADS * HEAD_DIM
DECAY_LORA = 64
AAA_LORA = 64
GATE_LORA = 128
SHIFT_WIDTH = 3 * WIDTH + DECAY_LORA + AAA_LORA + GATE_LORA
LN_X_EPS = 64e-5
NORM_EPS = 1e-6
N_GROUPS = 4
EXPERTS_PER_GROUP = 8
N_EXPERTS = N_GROUPS * EXPERTS_PER_GROUP
TOP_K = 2

LANES = 128
CHUNK = 64
RWKV_TILE = 256
MOE_ROWS = 512
VMEM_LIMIT = 48 * 1024 * 1024


def _cparams(sem):
    return pltpu.CompilerParams(dimension_semantics=sem, vmem_limit_bytes=VMEM_LIMIT)


def _dot(a, b):
    return jnp.dot(a.astype(BF16), b.astype(BF16), preferred_element_type=F32)


def _dot_nt(a, b):
    return lax.dot_general(a.astype(BF16), b.astype(BF16), (((1,), (1,)), ((), ())),
                           preferred_element_type=F32)


def _dot_tn(a, b):
    return lax.dot_general(a.astype(BF16), b.astype(BF16), (((0,), (0,)), ((), ())),
                           preferred_element_type=F32)


def _split_dot(x, w_bf16):
    hi = x.astype(BF16)
    lo = (x - hi.astype(F32)).astype(BF16)
    return (jnp.dot(hi, w_bf16, preferred_element_type=F32)
            + jnp.dot(lo, w_bf16, preferred_element_type=F32))


def _softplus(x):
    return jnp.maximum(x, 0.0) + jnp.log1p(jnp.exp(-jnp.abs(x)))


def _mod_kernel(c_ref, w_ref, b_ref, o_ref):
    c = c_ref[...]
    sc = c * jax.nn.sigmoid(c)
    o_ref[...] = jnp.dot(sc, w_ref[...], precision=HIGHEST, preferred_element_type=F32) + b_ref[...]


def _adaln_mod(c, ada_w, ada_b):
    b, d = c.shape
    n = ada_w.shape[1]
    rows = 8
    cp = jnp.zeros((rows, d), F32).at[:b].set(c)
    tn = 1024
    out = pl.pallas_call(
        _mod_kernel,
        out_shape=jax.ShapeDtypeStruct((rows, n), F32),
        grid=(n // tn,),
        in_specs=[pl.BlockSpec((rows, d), lambda j: (0, 0)),
                  pl.BlockSpec((d, tn), lambda j: (0, j)),
                  pl.BlockSpec((1, tn), lambda j: (0, j))],
        out_specs=pl.BlockSpec((rows, tn), lambda j: (0, j)),
        compiler_params=_cparams(("arbitrary",)),
        name="adaln_mod",
    )(cp, ada_w, ada_b.reshape(1, n))
    return out[:b].reshape(b, 6, d)


def _norm_mod_kernel(x_ref, g_ref, mod_ref, o_ref, *, shift_idx, scale_idx):
    x = x_ref[0]
    ms = jnp.mean(x * x, axis=-1, keepdims=True)
    y = x * lax.rsqrt(ms + NORM_EPS) * g_ref[...]
    scale = mod_ref[0, scale_idx:scale_idx + 1, :]
    shift = mod_ref[0, shift_idx:shift_idx + 1, :]
    o_ref[0] = (y * (1.0 + scale) + shift).astype(o_ref.dtype)


def _norm_mod(x, g, mod, shift_idx, scale_idx, tm=1024):
    b, s, d = x.shape
    return pl.pallas_call(
        functools.partial(_norm_mod_kernel, shift_idx=shift_idx, scale_idx=scale_idx),
        out_shape=jax.ShapeDtypeStruct((b, s, d), BF16),
        grid=(b, s // tm),
        in_specs=[pl.BlockSpec((1, tm, d), lambda i, j: (i, j, 0)),
                  pl.BlockSpec((1, d), lambda i, j: (0, 0)),
                  pl.BlockSpec((1, 6, d), lambda i, j: (i, 0, 0))],
        out_specs=pl.BlockSpec((1, tm, d), lambda i, j: (i, j, 0)),
        compiler_params=_cparams(("arbitrary", "arbitrary")),
        name="norm1_mod",
    )(x, g.reshape(1, d), mod)


def _mm_kernel(a_ref, w_ref, *rest, act, has_bias):
    o_ref = rest[-1]
    r = jnp.dot(a_ref[...], w_ref[...], preferred_element_type=F32)
    if has_bias:
        r = r + rest[0][...]
    if act == "sigmoid":
        r = 0.5 * jnp.tanh(0.5 * r) + 0.5
    o_ref[...] = r.astype(o_ref.dtype)


def _matmul(a, w, out_dtype, bias=None, act=None, tm=2048, tn=512, name="proj"):
    t, k = a.shape
    n = w.shape[1]
    tn = min(tn, n)
    in_specs = [pl.BlockSpec((tm, k), lambda i, j: (i, 0)),
                pl.BlockSpec((k, tn), lambda i, j: (0, j))]
    args = [a, w]
    if bias is not None:
        in_specs.append(pl.BlockSpec((1, tn), lambda i, j: (0, j)))
        args.append(bias)
    return pl.pallas_call(
        functools.partial(_mm_kernel, act=act, has_bias=bias is not None),
        out_shape=jax.ShapeDtypeStruct((t, n), out_dtype),
        grid=(t // tm, n // tn),
        in_specs=in_specs,
        out_specs=pl.BlockSpec((tm, tn), lambda i, j: (i, j)),
        compiler_params=_cparams(("arbitrary", "arbitrary")),
        name=name,
    )(*args)


LOG2E = 1.4426950408889634


def _split3(x):
    hi = x.astype(BF16)
    r1 = x - hi.astype(F32)
    mid = r1.astype(BF16)
    lo = (r1 - mid.astype(F32)).astype(BF16)
    return hi, mid, lo


def _fcum_kernel(h_ref, wf_ref, fb_ref, sel_ref, o_ref, carry_ref, *, ts):
    @pl.when(pl.program_id(0) == 0)
    def _():
        carry_ref[...] = jnp.zeros_like(carry_ref)

    ri = lax.broadcasted_iota(jnp.int32, (ts, ts), 0)
    ci = lax.broadcasted_iota(jnp.int32, (ts, ts), 1)
    tri = jnp.where(ri >= ci, 1.0, 0.0).astype(BF16)
    for bi in range(h_ref.shape[0]):
        f = jnp.dot(h_ref[bi], wf_ref[...], preferred_element_type=F32) + fb_ref[...]
        lf = -_softplus(-f)
        cum = carry_ref[bi]
        for piece in _split3(lf):
            cum = cum + jnp.dot(tri, piece, preferred_element_type=F32)
        carry_ref[bi] = cum[ts - 1:ts, :]
        out = jnp.zeros((ts, sel_ref.shape[2]), F32)
        for idx, piece in enumerate(_split3(cum * (-LOG2E))):
            out = out + jnp.dot(piece, sel_ref[idx], preferred_element_type=F32)
        o_ref[bi] = out.astype(o_ref.dtype)


def _forget_bias(h, wf, fb, ts=512):
    b, s, d = h.shape
    pairs = HEADS // 2
    wf_p = jnp.zeros((d, LANES), F32).at[:, :HEADS].set(wf).astype(BF16)
    fb_p = jnp.zeros((1, LANES), F32).at[0, :HEADS].set(fb)
    hh = jnp.arange(HEADS)
    sel = jnp.zeros((3, LANES, pairs * LANES), F32)
    for piece in range(3):
        sel = sel.at[piece, hh, (hh // 2) * LANES + (hh % 2) * 3 + piece].set(1.0)
    return pl.pallas_call(
        functools.partial(_fcum_kernel, ts=ts),
        out_shape=jax.ShapeDtypeStruct((b, s, pairs * LANES), BF16),
        grid=(s // ts,),
        in_specs=[pl.BlockSpec((b, ts, d), lambda j: (0, j, 0)),
                  pl.BlockSpec((d, LANES), lambda j: (0, 0)),
                  pl.BlockSpec((1, LANES), lambda j: (0, 0)),
                  pl.BlockSpec((3, LANES, pairs * LANES), lambda j: (0, 0, 0))],
        out_specs=pl.BlockSpec((b, ts, pairs * LANES), lambda j: (0, j, 0)),
        scratch_shapes=[pltpu.VMEM((b, 1, LANES), F32)],
        compiler_params=_cparams(("arbitrary",)),
        name="forget_bias",
    )(h, wf_p, fb_p, sel.astype(BF16))


FOX_BQ = 1024
FOX_BK = 512


def _fox_kernel(first_ref, q_ref, k_ref, v_ref, a_ref, o_ref, m_ref, acc_ref, *, bq, bk):
    i = pl.program_id(2)
    lane = lax.broadcasted_iota(jnp.int32, (bq, LANES), 1)
    first = lane < HEAD_DIM
    qs = q_ref[0].astype(F32) * (HEAD_DIM ** -0.5 * LOG2E)
    aug0 = jnp.where(lane < 3, 1.0, 0.0)
    aug1 = jnp.where((lane >= 3) & (lane < 6), 1.0, 0.0)
    q01 = jnp.concatenate([jnp.concatenate([jnp.where(first, qs, 0.0), aug0], axis=1),
                           jnp.concatenate([jnp.where(first, 0.0, qs), aug1], axis=1)],
                          axis=0).astype(BF16)
    m_ref[...] = jnp.full_like(m_ref, -jnp.inf)
    acc_ref[...] = jnp.zeros_like(acc_ref)
    lane_k = lax.broadcasted_iota(jnp.int32, (bk, LANES), 1)
    keep_first = jnp.where(lane_k < HEAD_DIM, 1.0, 0.0).astype(BF16)
    keep_second = jnp.where(lane_k < HEAD_DIM, 0.0, 1.0).astype(BF16)

    def both_heads(x, r0):
        if r0 == 0:
            return x[...]
        return jnp.concatenate([x[r0:bq], x[bq + r0:2 * bq]], axis=0)

    def logits(j, r0=0):
        start = pl.multiple_of(j * bk, bk)
        kb = jnp.concatenate([k_ref[0, pl.ds(start, bk), :], a_ref[0, pl.ds(start, bk), :]], axis=1)
        return lax.dot_general(both_heads(q01, r0), kb, (((1,), (1,)), ((), ())),
                               preferred_element_type=F32)

    def consume(j, z, r0=0):
        nr = bq - r0
        start = pl.multiple_of(j * bk, bk)
        vb = v_ref[0, pl.ds(start, bk), :]
        m_prev = both_heads(m_ref, r0)
        m_new = jnp.maximum(m_prev, jnp.max(z, axis=1, keepdims=True))
        alpha = jnp.exp2(m_prev - m_new)
        p = jnp.exp2(z - jnp.concatenate([m_new] * (bk // LANES), axis=1)).astype(BF16)
        pv = jnp.concatenate(
            [jnp.dot(p[:nr], vb * keep_first + keep_second, preferred_element_type=F32),
             jnp.dot(p[nr:], vb * keep_second + keep_first, preferred_element_type=F32)], axis=0)
        acc_new = alpha * both_heads(acc_ref, r0) + pv
        if r0 == 0:
            acc_ref[...] = acc_new
            m_ref[...] = m_new
        else:
            for half, dst in ((slice(0, nr), slice(r0, bq)), (slice(nr, 2 * nr), slice(bq + r0, 2 * bq))):
                acc_ref[dst] = acc_new[half]
                m_ref[dst] = m_new[half]

    per_q = bq // bk
    n_full = i * per_q

    j_first = first_ref[pl.program_id(0), pl.program_id(1), i]
    odd = (n_full - j_first) & 1

    @pl.when(odd == 1)
    def _():
        consume(j_first, logits(j_first))

    def body(step, carry):
        j = j_first + odd + 2 * step
        z_a = logits(j)
        z_b = logits(j + 1)
        consume(j, z_a)
        consume(j + 1, z_b)
        return carry

    lax.fori_loop(0, (n_full - j_first) // 2, body, 0)
    for d in range(per_q):
        r0 = d * bk
        row = lax.broadcasted_iota(jnp.int32, (bq - r0, bk), 0)
        col = lax.broadcasted_iota(jnp.int32, (bq - r0, bk), 1)
        keep = col <= row
        z = logits(n_full + d, r0)
        consume(n_full + d, jnp.where(jnp.concatenate([keep, keep], axis=0), z, -jnp.inf), r0)
    acc = acc_ref[...]
    o = acc / pltpu.roll(acc, HEAD_DIM, 1)
    o_ref[0] = jnp.where(first, o[:bq], o[bq:]).astype(o_ref.dtype)


def _norm_bound_kernel(x_ref, sel_ref, o_ref):
    x = x_ref[...].astype(F32)
    ssq = _dot(x * x, sel_ref[...])
    o_ref[0] = jnp.broadcast_to(jnp.max(ssq, axis=0, keepdims=True), o_ref.shape[1:])


def _fox_first_block(qkv, fbias, bq, bk):
    b, s, _ = qkv.shape
    t = b * s
    nq, nk = s // bq, s // bk
    n_heads2 = 2 * HEADS
    sel = (jnp.arange(2 * WIDTH)[:, None] // HEAD_DIM == jnp.arange(LANES)[None, :]).astype(BF16)
    bounds = pl.pallas_call(
        _norm_bound_kernel,
        out_shape=jax.ShapeDtypeStruct((t // bk, 8, LANES), F32),
        grid=(t // bk,),
        in_specs=[pl.BlockSpec((bk, 2 * WIDTH), lambda i: (i, 0)),
                  pl.BlockSpec((2 * WIDTH, LANES), lambda i: (0, 0))],
        out_specs=pl.BlockSpec((1, 8, LANES), lambda i: (i, 0, 0)),
        compiler_params=_cparams(("arbitrary",)),
        name="fox_norm_bounds",
    )(qkv.reshape(t, -1), sel)
    nrm = jnp.sqrt(bounds[:, 0, :n_heads2]).reshape(b, nk, n_heads2) * 1.01
    qn = nrm[..., :HEADS] * (HEAD_DIM ** -0.5 * LOG2E * 1.01)
    kn = nrm[..., HEADS:]
    per_q = bq // bk
    qn_i = qn.reshape(b, nq, per_q, HEADS).max(axis=2)
    kn_i = kn.reshape(b, nq, per_q, HEADS).max(axis=2)
    kn_pre = lax.cummax(kn, axis=1)
    pairs = HEADS // 2
    def bias_rows(rows):
        pieces = rows.astype(F32).reshape(b, -1, pairs, LANES)[..., :6].reshape(b, -1, pairs, 2, 3)
        return pieces.sum(-1).reshape(b, -1, HEADS)

    nb_end = bias_rows(fbias[:, bk - 1::bk])
    nb_start = bias_rows(fbias[:, ::bq])
    gap = nb_start[:, :, None, :] - nb_end[:, None, :, :]
    need = qn_i[:, :, None, :] * (kn_pre[:, None, :, :] + kn_i[:, :, None, :]) + 152.0
    skip = (gap > need).reshape(b, nq, nk, pairs, 2).all(axis=-1)
    n_full = jnp.arange(nq) * per_q
    skip = skip & (jnp.arange(nk)[None, None, :, None] < n_full[None, :, None, None])
    first = jnp.argmin(skip, axis=2)
    return first.transpose(0, 2, 1).astype(jnp.int32)


def _fox_attention(qkv, fbias, bq=FOX_BQ, bk=FOX_BK):
    b, s, _ = qkv.shape
    pairs = HEADS // 2
    cb = WIDTH // LANES
    first = _fox_first_block(qkv, fbias, bq, bk)
    grid_spec = pltpu.PrefetchScalarGridSpec(
        num_scalar_prefetch=1,
        grid=(b, pairs, s // bq),
        in_specs=[pl.BlockSpec((1, bq, LANES), lambda bi, hp, i, fr: (bi, i, hp)),
                  pl.BlockSpec((1, s, LANES), lambda bi, hp, i, fr: (bi, 0, cb + hp)),
                  pl.BlockSpec((1, s, LANES), lambda bi, hp, i, fr: (bi, 0, 2 * cb + hp)),
                  pl.BlockSpec((1, s, LANES), lambda bi, hp, i, fr: (bi, 0, hp))],
        out_specs=pl.BlockSpec((1, bq, LANES), lambda bi, hp, i, fr: (bi, i, hp)),
        scratch_shapes=[pltpu.VMEM((2 * bq, LANES), F32), pltpu.VMEM((2 * bq, LANES), F32)],
    )
    return pl.pallas_call(
        functools.partial(_fox_kernel, bq=bq, bk=bk),
        out_shape=jax.ShapeDtypeStruct((b, s, WIDTH), BF16),
        grid_spec=grid_spec,
        compiler_params=_cparams(("arbitrary", "arbitrary", "arbitrary")),
        name="fox_attention",
    )(first, qkv, qkv, qkv, fbias)


PAIR = 2 * HEAD_DIM
GROUPS = WIDTH // PAIR


def _group(x, g):
    return x[:, g * PAIR:(g + 1) * PAIR]


def _head_sum(x, bd2):
    return jnp.concatenate([_dot(_group(x, g), bd2) for g in range(GROUPS)], axis=1)


def _head_apply(mats, x, lane_first):
    rows = mats.shape[1]
    outs = []
    for g in range(GROUPS):
        res = _dot(mats[2 * g:2 * g + 2].reshape(2 * rows, rows), _group(x, g))
        outs.append(jnp.where(lane_first, res[:rows], res[rows:]))
    return jnp.concatenate(outs, axis=1)


def _rwkv_kernel(p_ref, mu_ref, w0_ref, a0_ref, kk_ref, ka_ref, rk_ref, lng_ref, lnb_ref,
                 wwa_ref, g2_ref, bd_ref, o_ref, st_ref, prev_ref):
    L = CHUNK

    @pl.when(pl.program_id(1) == 0)
    def _():
        st_ref[...] = jnp.zeros_like(st_ref)
        prev_ref[...] = jnp.zeros_like(prev_ref)

    p = p_ref[0]
    T = p.shape[0]
    rowi = lax.broadcasted_iota(jnp.int32, p.shape, 0)
    prev = jnp.where(rowi == 0, prev_ref[...], pltpu.roll(p, 1, 0))
    prev_ref[...] = p[T - 1:T, :]
    ps = p + (prev - p) * mu_ref[...]
    r = ps[:, 0:WIDTH]
    k = ps[:, WIDTH:2 * WIDTH]
    v = ps[:, 2 * WIDTH:3 * WIDTH]
    wa_in = ps[:, 3 * WIDTH:3 * WIDTH + DECAY_LORA + AAA_LORA]
    gd = ps[:, 3 * WIDTH + DECAY_LORA + AAA_LORA:]
    lane_wa = lax.broadcasted_iota(jnp.int32, wa_in.shape, 1)
    wa_act = jnp.where(lane_wa < DECAY_LORA, jnp.tanh(wa_in), wa_in)
    wa = _dot(wa_act, wwa_ref[...])
    log_w = -_softplus(-(w0_ref[...] + wa[:, :WIDTH])) - 0.5
    lw = -jnp.exp(log_w)
    a = jax.nn.sigmoid(a0_ref[...] + wa[:, WIDTH:])
    out_gate = _dot(jax.nn.sigmoid(gd), g2_ref[...])
    bd = bd_ref[...]
    kk0 = k * kk_ref[...]
    kk = kk0 * lax.rsqrt(jnp.maximum(_head_sum(kk0 * kk0, bd), 1e-24))
    k2 = k * (1.0 + (a - 1.0) * ka_ref[...])
    av = -kk
    bv = kk * a

    n_sub = T // L
    rt_i = lax.broadcasted_iota(jnp.int32, (T, T), 0)
    ct_i = lax.broadcasted_iota(jnp.int32, (T, T), 1)
    tri_tile = (rt_i >= ct_i) & (rt_i // L == ct_i // L)
    cl = _split_dot_left(jnp.where(tri_tile, 1.0, 0.0).astype(BF16), lw)
    cl_end = jnp.concatenate([jnp.broadcast_to(cl[(c + 1) * L - 1:(c + 1) * L, :], (L, WIDTH))
                              for c in range(n_sub)], axis=0)
    at_all = av * jnp.exp(cl - lw)
    rt_all = r * jnp.exp(cl)
    einv = jnp.exp(-cl)
    bt_all = bv * einv
    kt_all = k2 * einv
    edec = jnp.exp(cl_end - cl)
    b_end_all = bv * edec
    k_end_all = k2 * edec

    ri = lax.broadcasted_iota(jnp.int32, (L, L), 0)
    ci = lax.broadcasted_iota(jnp.int32, (L, L), 1)
    tri_incl = ri >= ci
    tri_strict = ri > ci
    eye = jnp.where(ri == ci, 1.0, 0.0)
    lane_first = lax.broadcasted_iota(jnp.int32, (L, PAIR), 1) < HEAD_DIM
    qr = lax.broadcasted_iota(jnp.int32, (PAIR, PAIR), 0) < HEAD_DIM
    qc = lax.broadcasted_iota(jnp.int32, (PAIR, PAIR), 1) < HEAD_DIM
    same_head = qr == qc

    def bmm(x, y):
        return lax.dot_general(x.astype(BF16), y.astype(BF16), (((2,), (1,)), ((0,), (0,))),
                               preferred_element_type=F32)

    def chunk_terms(c):
        rows = slice(c * L, (c + 1) * L)
        at, rt, bt, kt, vc = at_all[rows], rt_all[rows], bt_all[rows], kt_all[rows], v[rows]
        sb_heads, sk_heads = [], []
        for g in range(GROUPS):
            at_g, rt_g = _group(at, g), _group(rt, g)
            lhs = jnp.concatenate([jnp.where(lane_first, at_g, 0.0), jnp.where(lane_first, rt_g, 0.0),
                                   jnp.where(lane_first, 0.0, at_g), jnp.where(lane_first, 0.0, rt_g)],
                                  axis=0).astype(BF16)
            sb_g = _dot_nt(lhs, _group(bt, g))
            sk_g = _dot_nt(lhs, _group(kt, g))
            for hh in range(2):
                sb_heads.append(sb_g[hh * 2 * L:(hh + 1) * 2 * L])
                sk_heads.append(sk_g[hh * 2 * L:(hh + 1) * 2 * L])
        sb = jnp.stack(sb_heads)
        sk = jnp.stack(sk_heads)
        n_ab = jnp.where(tri_strict, sb[:, :L, :], 0.0)
        a_ak = jnp.where(tri_strict, sk[:, :L, :], 0.0)
        a_rb = jnp.where(tri_incl, sb[:, L:, :], 0.0)
        a_rk = jnp.where(tri_incl, sk[:, L:, :], 0.0)
        tinv = eye + n_ab
        pw = bmm(n_ab, n_ab)
        span = 2
        while 2 * span < L:
            both = bmm(jnp.concatenate([tinv, pw], axis=1), pw)
            tinv = tinv + both[:, :L, :]
            pw = both[:, L:, :]
            span *= 2
        tinv = tinv + bmm(tinv, pw)
        av_term = _head_apply(a_ak, vc, lane_first)
        pm = _head_apply(tinv, at, lane_first)
        qm = _head_apply(tinv, av_term, lane_first)
        rkv = _head_apply(a_rk, vc, lane_first)
        return pm, qm, rkv, a_rb

    terms = [chunk_terms(c) for c in range(n_sub)]

    y_chunks = []
    for c in range(n_sub):
        rows = slice(c * L, (c + 1) * L)
        pm, qm, rkv, a_rb = terms[c]
        rt, vc, b_end, k_end = rt_all[rows], v[rows], b_end_all[rows], k_end_all[rows]
        gam_last = jnp.exp(cl[(c + 1) * L - 1:(c + 1) * L, :])
        u_parts, ys_parts = [], []
        for g in range(GROUPS):
            pr = _dot_nt(jnp.concatenate([_group(pm, g), _group(rt, g)], axis=0), st_ref[g])
            u_parts.append(pr[:L] + _group(qm, g))
            ys_parts.append(pr[L:])
        u = jnp.concatenate(u_parts, axis=1)
        y_chunks.append(jnp.concatenate(ys_parts, axis=1) + _head_apply(a_rb, u, lane_first) + rkv)
        for g in range(GROUPS):
            upd = _dot_tn(_group(u, g), _group(b_end, g)) + _dot_tn(_group(vc, g), _group(k_end, g))
            st_ref[g] = st_ref[g] * _group(gam_last, g) + jnp.where(same_head, upd, 0.0)
    y = jnp.concatenate(y_chunks, axis=0)

    inv_n = 1.0 / HEAD_DIM
    mean = _head_sum(y, bd) * inv_n
    dlt = y - mean
    var = _head_sum(dlt * dlt, bd) * inv_n
    yn = dlt * lax.rsqrt(var + LN_X_EPS) * lng_ref[...] + lnb_ref[...]
    bonus = _head_sum(r * k2 * rk_ref[...], bd) * v
    o_ref[0] = ((yn + bonus) * out_gate).astype(o_ref.dtype)


def _split_dot_left(w_bf16, x):
    hi = x.astype(BF16)
    r1 = x - hi.astype(F32)
    mid = r1.astype(BF16)
    lo = (r1 - mid.astype(F32)).astype(BF16)
    return (jnp.dot(w_bf16, hi, preferred_element_type=F32)
            + jnp.dot(w_bf16, mid, preferred_element_type=F32)
            + jnp.dot(w_bf16, lo, preferred_element_type=F32))


def _rwkv_branch(p_rw, mu, w0, w2, a0, a2, g2, k_k, k_a, r_k, ln_g, ln_b):
    b, s, sw = p_rw.shape
    row = lambda t: t.reshape(1, -1).astype(F32)
    wwa = jnp.zeros((DECAY_LORA + AAA_LORA, 2 * WIDTH), F32)
    wwa = wwa.at[:DECAY_LORA, :WIDTH].set(w2).at[DECAY_LORA:, WIDTH:].set(a2).astype(BF16)
    hid = jnp.arange(PAIR) // HEAD_DIM
    bd = (hid[:, None] == hid[None, :]).astype(BF16)
    const = lambda shape: pl.BlockSpec(shape, lambda i, j: (0,) * len(shape))
    return pl.pallas_call(
        _rwkv_kernel,
        out_shape=jax.ShapeDtypeStruct((b, s, WIDTH), BF16),
        grid=(b, s // RWKV_TILE),
        in_specs=[pl.BlockSpec((1, RWKV_TILE, sw), lambda i, j: (i, j, 0)),
                  const((1, sw)), const((1, WIDTH)), const((1, WIDTH)), const((1, WIDTH)),
                  const((1, WIDTH)), const((1, WIDTH)), const((1, WIDTH)), const((1, WIDTH)),
                  const((DECAY_LORA + AAA_LORA, 2 * WIDTH)), const((GATE_LORA, WIDTH)),
                  const((PAIR, PAIR))],
        out_specs=pl.BlockSpec((1, RWKV_TILE, WIDTH), lambda i, j: (i, j, 0)),
        scratch_shapes=[pltpu.VMEM((GROUPS, PAIR, PAIR), F32), pltpu.VMEM((1, sw), F32)],
        compiler_params=_cparams(("arbitrary", "arbitrary")),
        name="rwkv7_scan",
    )(p_rw, row(mu), row(w0), row(a0), row(k_k), row(k_a), row(r_k), row(ln_g), row(ln_b),
      wwa, g2.astype(BF16), bd)


def _out_kernel(of_ref, orw_ref, gate_ref, x_ref, mod_ref, wof_ref, wor_ref, wo_ref, n2g_ref,
                wr_ref, br_ref, x1_ref, h2_ref, route_ref, counts_ref, cnt_ref):
    d = x_ref.shape[-1]
    gate = gate_ref[...].astype(F32)
    merged = (gate[:, :d] * jnp.dot(of_ref[...], wof_ref[...], preferred_element_type=F32)
              + gate[:, d:] * jnp.dot(orw_ref[...], wor_ref[...], preferred_element_type=F32))
    gate1 = mod_ref[0, 2:3, :]
    shift2 = mod_ref[0, 3:4, :]
    scale2 = mod_ref[0, 4:5, :]
    x1 = x_ref[...] + gate1 * jnp.dot(merged.astype(BF16), wo_ref[...], preferred_element_type=F32)
    x1_ref[...] = x1
    ms = jnp.mean(x1 * x1, axis=-1, keepdims=True)
    h2 = x1 * lax.rsqrt(ms + NORM_EPS) * n2g_ref[...] * (1.0 + scale2) + shift2
    h2_ref[...] = h2

    h2_hi = h2.astype(BF16)
    h2_lo = (h2 - h2_hi.astype(F32)).astype(BF16)
    logits = (jnp.dot(h2_hi, wr_ref[0], preferred_element_type=F32)
              + jnp.dot(h2_lo, wr_ref[0], preferred_element_type=F32)
              + jnp.dot(h2_hi, wr_ref[1], preferred_element_type=F32)) + br_ref[...]
    lane = lax.broadcasted_iota(jnp.int32, logits.shape, 1)
    neg = -jnp.inf
    big = jnp.int32(LANES)
    gl = jnp.where(lane < N_GROUPS, logits, neg)
    gmax = jnp.max(gl, axis=1, keepdims=True)
    gidx = jnp.min(jnp.where(gl == gmax, lane, big), axis=1, keepdims=True)
    g_p = 1.0 / jnp.sum(jnp.exp(gl - gmax), axis=1, keepdims=True)
    e_lane = lane - N_GROUPS
    in_grp = (e_lane >= 0) & (e_lane < N_EXPERTS) & ((e_lane // EXPERTS_PER_GROUP) == gidx)
    sel = jnp.where(in_grp, logits, neg)
    m1 = jnp.max(sel, axis=1, keepdims=True)
    i1 = jnp.min(jnp.where(sel == m1, lane, big), axis=1, keepdims=True)
    sel2 = jnp.where(lane == i1, neg, sel)
    m2 = jnp.max(sel2, axis=1, keepdims=True)
    i2 = jnp.min(jnp.where(sel2 == m2, lane, big), axis=1, keepdims=True)
    e21 = jnp.exp(m2 - m1)
    w_first = g_p / (1.0 + e21)
    w_second = g_p * e21 / (1.0 + e21)
    @pl.when(pl.program_id(0) == 0)
    def _():
        cnt_ref[...] = jnp.zeros_like(cnt_ref)

    tm = logits.shape[0]
    oh1 = lane == i1
    oh2 = lane == i2
    both = jnp.where(oh1 | oh2, 1.0, 0.0)
    before = (lax.broadcasted_iota(jnp.int32, (tm, tm), 0)
              > lax.broadcasted_iota(jnp.int32, (tm, tm), 1))
    seen = jnp.dot(jnp.where(before, 1.0, 0.0).astype(BF16), both.astype(BF16),
                   preferred_element_type=F32) + cnt_ref[...]
    rank1 = jnp.sum(jnp.where(oh1, seen, 0.0), axis=1, keepdims=True)
    rank2 = jnp.sum(jnp.where(oh2, seen, 0.0), axis=1, keepdims=True)
    cnt_ref[...] = cnt_ref[...] + jnp.sum(both, axis=0, keepdims=True)
    counts_ref[...] = jnp.broadcast_to(cnt_ref[...], counts_ref.shape)

    route = jnp.where(lane == 0, (i1 - N_GROUPS).astype(F32),
                      jnp.where(lane == 1, (i2 - N_GROUPS).astype(F32),
                                jnp.where(lane == 2, w_first,
                                          jnp.where(lane == 3, w_second,
                                                    jnp.where(lane == 4, rank1,
                                                              jnp.where(lane == 5, rank2, 0.0))))))
    route_ref[...] = route


def _merge_out_router(o_fox, o_rw, gate, x, mod, wof, wor, wo, n2g, wr, br, tm=512):
    b, s, d = x.shape
    t = b * s
    spb = s // tm
    rowspec = lambda w: pl.BlockSpec((tm, w), lambda i: (i, 0))
    const = lambda shape: pl.BlockSpec(shape, lambda i: (0,) * len(shape))
    return pl.pallas_call(
        _out_kernel,
        out_shape=(jax.ShapeDtypeStruct((t, d), F32), jax.ShapeDtypeStruct((t, d), F32),
                   jax.ShapeDtypeStruct((t, LANES), F32), jax.ShapeDtypeStruct((8, LANES), F32)),
        grid=(t // tm,),
        in_specs=[rowspec(WIDTH), rowspec(WIDTH), rowspec(2 * d), rowspec(d),
                  pl.BlockSpec((1, 6, d), lambda i: (i // spb, 0, 0)),
                  const((WIDTH, d)), const((WIDTH, d)), const((d, d)), const((1, d)),
                  const((2, d, LANES)), const((1, LANES))],
        out_specs=(rowspec(d), rowspec(d), rowspec(LANES), const((8, LANES))),
        scratch_shapes=[pltpu.VMEM((1, LANES), F32)],
        compiler_params=_cparams(("arbitrary",)),
        name="merge_out_router",
    )(o_fox, o_rw, gate, x.reshape(t, d), mod, wof, wor, wo, n2g.reshape(1, d), wr, br)


def _dispatch_kernel(zstart_ref, dest_ref, h_ref, xs_ref, zero_ref, sem, *, tm):
    @pl.when(pl.program_id(0) == 0)
    def _():
        zero_ref[...] = jnp.zeros_like(zero_ref)
        for e in range(N_EXPERTS):
            zrow = pl.multiple_of(zstart_ref[e], MOE_ROWS)
            pltpu.make_async_copy(zero_ref, xs_ref.at[pl.ds(zrow, MOE_ROWS)], sem).start()
        for e in range(N_EXPERTS):
            pltpu.make_async_copy(zero_ref, xs_ref.at[pl.ds(0, MOE_ROWS)], sem).wait()
        for phase in ("start", "wait"):
            for e in range(N_EXPERTS):
                trow = pl.multiple_of(zstart_ref[N_EXPERTS] + e * MOE_ROWS, MOE_ROWS)

                @pl.when(trow < xs_ref.shape[0])
                def _():
                    tail = pltpu.make_async_copy(zero_ref, xs_ref.at[pl.ds(trow, MOE_ROWS)], sem)
                    tail.start() if phase == "start" else tail.wait()

    def issue(r, carry):
        for kk in range(TOP_K):
            dst = dest_ref[0, 0, TOP_K * r + kk]
            pltpu.make_async_copy(h_ref.at[pl.ds(r, 1)], xs_ref.at[pl.ds(dst, 1)], sem).start(priority=kk)
        return carry

    lax.fori_loop(0, tm, issue, 0, unroll=8)
    for kk in range(TOP_K):
        pltpu.make_async_copy(h_ref, xs_ref.at[pl.ds(0, tm)], sem).wait()


def _moe_dispatch(h2, dest3, zstart, rows, tm=256):
    t, d = h2.shape
    grid_spec = pltpu.PrefetchScalarGridSpec(
        num_scalar_prefetch=1,
        grid=(t // tm,),
        in_specs=[pl.BlockSpec((1, 1, TOP_K * tm), lambda i, zs: (i, 0, 0), memory_space=pltpu.SMEM),
                  pl.BlockSpec((tm, d), lambda i, zs: (i, 0))],
        out_specs=pl.BlockSpec(memory_space=pl.ANY),
        scratch_shapes=[pltpu.VMEM((MOE_ROWS, d), F32), pltpu.SemaphoreType.DMA(())],
    )
    return pl.pallas_call(
        functools.partial(_dispatch_kernel, tm=tm),
        out_shape=jax.ShapeDtypeStruct((rows, d), F32),
        grid_spec=grid_spec,
        compiler_params=_cparams(("arbitrary",)),
        name="moe_dispatch",
    )(zstart, dest3, h2)


def _expert_kernel(blk_e_ref, nused_ref, xs_ref, w1_ref, w3_ref, w2_ref, o_ref, w1b_ref, w3b_ref, w2b_ref):
    i = pl.program_id(0)
    live = i * MOE_ROWS < nused_ref[0]
    new_expert = jnp.logical_or(i == 0, blk_e_ref[i] != blk_e_ref[jnp.maximum(i - 1, 0)])

    @pl.when(new_expert)
    def _():
        w1b_ref[...] = w1_ref[0].astype(BF16)
        w3b_ref[...] = w3_ref[0].astype(BF16)
        w2b_ref[...] = w2_ref[0].astype(BF16)

    @pl.when(live)
    def _():
        xb = xs_ref[...].astype(BF16)
        h1 = jnp.dot(xb, w1b_ref[...], preferred_element_type=F32)
        h3 = jnp.dot(xb, w3b_ref[...], preferred_element_type=F32)
        hh = (h1 * jax.nn.sigmoid(h1)) * h3
        o_ref[...] = jnp.dot(hh.astype(BF16), w2b_ref[...], preferred_element_type=F32)

    @pl.when(jnp.logical_not(live))
    def _():
        o_ref[...] = jnp.zeros_like(o_ref)


def _moe_experts(xs, blk_e, nused, w1, w3, w2):
    rows, d = xs.shape
    de = w1.shape[-1]
    grid_spec = pltpu.PrefetchScalarGridSpec(
        num_scalar_prefetch=2,
        grid=(rows // MOE_ROWS,),
        in_specs=[pl.BlockSpec((MOE_ROWS, d), lambda i, be, nu: (i, 0)),
                  pl.BlockSpec((1, d, de), lambda i, be, nu: (be[i], 0, 0)),
                  pl.BlockSpec((1, d, de), lambda i, be, nu: (be[i], 0, 0)),
                  pl.BlockSpec((1, de, d), lambda i, be, nu: (be[i], 0, 0))],
        out_specs=pl.BlockSpec((MOE_ROWS, d), lambda i, be, nu: (i, 0)),
        scratch_shapes=[pltpu.VMEM((d, de), BF16), pltpu.VMEM((d, de), BF16), pltpu.VMEM((de, d), BF16)],
    )
    return pl.pallas_call(
        _expert_kernel,
        out_shape=jax.ShapeDtypeStruct((rows, d), F32),
        grid_spec=grid_spec,
        compiler_params=_cparams(("arbitrary",)),
        name="moe_experts",
    )(blk_e, nused, xs, w1, w3, w2)


def _final_kernel(dest_ref, dest_next_ref, route_ref, x1_ref, mod_ref, fg_ref, ys_ref, o_ref, buf_ref, sem,
                  *, tm):
    i = pl.program_id(0)
    half = i & 1

    def gather(idx_ref, into):
        def issue(r, carry):
            for kk in range(TOP_K):
                src = idx_ref[0, 0, TOP_K * r + kk]
                pltpu.make_async_copy(ys_ref.at[pl.ds(src, 1)], buf_ref.at[into, kk, pl.ds(r, 1)],
                                      sem.at[into]).start(priority=kk)
            return carry

        lax.fori_loop(0, tm, issue, 0, unroll=8)

    @pl.when(i == 0)
    def _():
        gather(dest_ref, 0)

    @pl.when(i + 1 < pl.num_programs(0))
    def _():
        gather(dest_next_ref, 1 - half)

    for kk in range(TOP_K):
        pltpu.make_async_copy(ys_ref.at[pl.ds(0, tm)], buf_ref.at[half, kk], sem.at[half]).wait()

    route = route_ref[...]
    y = route[:, 2:3] * buf_ref[half, 0] + route[:, 3:4] * buf_ref[half, 1]
    gate2 = mod_ref[0, 5:6, :]
    x2 = x1_ref[...] + gate2 * y
    ms = jnp.mean(x2 * x2, axis=-1, keepdims=True)
    o_ref[...] = x2 * lax.rsqrt(ms + NORM_EPS) * fg_ref[...]


def _moe_combine_final(dest3, route, x1, mod, final_g, ys, s, tm=256):
    t, d = x1.shape
    spb = s // tm
    return pl.pallas_call(
        functools.partial(_final_kernel, tm=tm),
        out_shape=jax.ShapeDtypeStruct((t, d), F32),
        grid=(t // tm,),
        in_specs=[pl.BlockSpec((1, 1, TOP_K * tm), lambda i: (i, 0, 0), memory_space=pltpu.SMEM),
                  pl.BlockSpec((1, 1, TOP_K * tm), lambda i: (jnp.minimum(i + 1, t // tm - 1), 0, 0),
                               memory_space=pltpu.SMEM),
                  pl.BlockSpec((tm, LANES), lambda i: (i, 0)),
                  pl.BlockSpec((tm, d), lambda i: (i, 0)),
                  pl.BlockSpec((1, 6, d), lambda i: (i // spb, 0, 0)),
                  pl.BlockSpec((1, d), lambda i: (0, 0)),
                  pl.BlockSpec(memory_space=pl.ANY)],
        out_specs=pl.BlockSpec((tm, d), lambda i: (i, 0)),
        scratch_shapes=[pltpu.VMEM((2, TOP_K, tm, d), F32), pltpu.SemaphoreType.DMA((2,))],
        compiler_params=_cparams(("arbitrary",)),
        name="moe_combine_final",
    )(dest3, dest3, route, x1, mod, final_g.reshape(1, d), ys)


def _moe_plan(route, counts, tm):
    t = route.shape[0]
    m = t * TOP_K
    flat_e = route[:, :TOP_K].astype(jnp.int32).reshape(m)
    rank = route[:, 4:4 + TOP_K].astype(jnp.int32).reshape(m)
    counts = counts[0, N_GROUPS:N_GROUPS + N_EXPERTS].astype(jnp.int32)
    padded = (counts + MOE_ROWS - 1) // MOE_ROWS * MOE_ROWS
    pad_end = jnp.cumsum(padded)
    pad_start = pad_end - padded
    experts = jnp.arange(N_EXPERTS, dtype=jnp.int32)
    start_of = jnp.sum(jnp.where(flat_e[:, None] == experts[None, :], pad_start[None, :], 0), axis=1)
    dest = (start_of + rank).astype(jnp.int32)
    n_blocks = m // MOE_ROWS + N_EXPERTS
    blk_start = jnp.arange(n_blocks, dtype=jnp.int32) * MOE_ROWS
    blk_e = jnp.minimum(jnp.sum(pad_end[None, :] <= blk_start[:, None], axis=1), N_EXPERTS - 1).astype(jnp.int32)
    nused = pad_end[-1:].astype(jnp.int32)
    zstart = jnp.concatenate([jnp.maximum(pad_end - MOE_ROWS, 0), pad_end[-1:]]).astype(jnp.int32)
    return dest.reshape(t // tm, 1, TOP_K * tm), blk_e, nused, zstart, n_blocks * MOE_ROWS


def kernel(x, c, ada_w, ada_b, norm1_g, w_in, fox_forget_b, shift_mu, rwkv_w0, rwkv_w2, rwkv_a0, rwkv_a2, rwkv_g2, rwkv_k_k, rwkv_k_a, rwkv_r_k, ln_x_g, ln_x_b, w_out_fox, w_out_rwkv, w_o, norm2_g, router_group_w, router_group_b, router_expert_w, router_expert_b, exp_w1, exp_w3, exp_w2, final_g):
    b, s, d = x.shape
    t = b * s
    assert ada_w.shape[0] == 1, "the final norm is fused into the last layer's combine; one layer is laid out"
    for l in range(1):
        mod = _adaln_mod(c, ada_w[l], ada_b[l])

        h = _norm_mod(x, norm1_g[l], mod, shift_idx=0, scale_idx=1)
        h2d = h.reshape(t, d)
        w = w_in[l]
        o_f = 3 * WIDTH
        o_rw = o_f + HEADS
        o_g = o_rw + SHIFT_WIDTH
        perm = jnp.argsort(fox_forget_b[l])
        by_head = lambda m: m.reshape(d, HEADS, HEAD_DIM)[:, perm]
        wq = by_head(w[:, :WIDTH]).reshape(d, WIDTH)
        wk = by_head(w[:, WIDTH:2 * WIDTH]).reshape(d, WIDTH)
        wv = by_head(w[:, 2 * WIDTH:o_f]).reshape(d, WIDTH)
        qkv = _matmul(h2d, jnp.concatenate([wq, wk, wv], axis=1).astype(BF16), BF16, name="proj_qkv")
        p_rw = _matmul(h2d, w[:, o_rw:o_g].astype(BF16), F32, tn=896, name="proj_rwkv")
        gate = _matmul(h2d, w[:, o_g:].astype(BF16), BF16, act="sigmoid", name="proj_gate")
        fbias = _forget_bias(h, w[:, o_f:o_rw][:, perm], fox_forget_b[l][perm])
        o_fox = _fox_attention(qkv.reshape(b, s, 3 * WIDTH), fbias)
        w_of = w_out_fox[l].reshape(HEADS, HEAD_DIM, d)[perm].reshape(WIDTH, d)
        o_rwkv = _rwkv_branch(p_rw.reshape(b, s, SHIFT_WIDTH), shift_mu[l], rwkv_w0[l], rwkv_w2[l],
                              rwkv_a0[l], rwkv_a2[l], rwkv_g2[l], rwkv_k_k[l], rwkv_k_a[l],
                              rwkv_r_k[l], ln_x_g[l], ln_x_b[l])

        wr = jnp.zeros((d, LANES), F32)
        wr = wr.at[:, :N_GROUPS].set(router_group_w[l]).at[:, N_GROUPS:N_GROUPS + N_EXPERTS].set(router_expert_w[l])
        br = jnp.zeros((1, LANES), F32)
        br = br.at[0, :N_GROUPS].set(router_group_b[l]).at[0, N_GROUPS:N_GROUPS + N_EXPERTS].set(router_expert_b[l])
        wr_hi = wr.astype(BF16)
        wr_lo = (wr - wr_hi.astype(F32)).astype(BF16)
        x1, h2, route, counts = _merge_out_router(
            o_fox.reshape(t, WIDTH), o_rwkv.reshape(t, WIDTH), gate, x, mod,
            w_of.astype(BF16), w_out_rwkv[l].astype(BF16), w_o[l].astype(BF16), norm2_g[l],
            jnp.stack([wr_hi, wr_lo]), br)

        tm_moe = 256
        dest3, blk_e, nused, zstart, rows = _moe_plan(route, counts, tm_moe)
        xs = _moe_dispatch(h2, dest3, zstart, rows, tm=tm_moe)
        ys = _moe_experts(xs, blk_e, nused, exp_w1[l], exp_w3[l], exp_w2[l])
        out = _moe_combine_final(dest3, route, x1, mod, final_g, ys, s, tm=tm_moe)
    return out.reshape(b, s, d)
```

```python
import functools

import jax
import jax.numpy as jnp
from jax import lax
from jax.experimental import pallas as pl
from jax.experimental.pallas import tpu as pltpu

F32 = jnp.float32
BF16 = jnp.bfloat16
HIGHEST = lax.Precision.HIGHEST

HEADS = 8
HEAD_DIM = 64
WIDTH = HEADS * HEAD_DIM
DECAY_LORA = 64
AAA_LORA = 64
GATE_LORA = 128
SHIFT_WIDTH = 3 * WIDTH + DECAY_LORA + AAA_LORA + GATE_LORA
LN_X_EPS = 64e-5
NORM_EPS = 1e-6
N_GROUPS = 4
EXPERTS_PER_GROUP = 8
N_EXPERTS = N_GROUPS * EXPERTS_PER_GROUP
TOP_K = 2

LANES = 128
CHUNK = 64
RWKV_TILE = 256
MOE_ROWS = 512
VMEM_LIMIT = 48 * 1024 * 1024


def _cparams(sem):
    return pltpu.CompilerParams(dimension_semantics=sem, vmem_limit_bytes=VMEM_LIMIT)


def _dot(a, b):
    return jnp.dot(a.astype(BF16), b.astype(BF16), preferred_element_type=F32)


def _dot_nt(a, b):
    return lax.dot_general(a.astype(BF16), b.astype(BF16), (((1,), (1,)), ((), ())),
                           preferred_element_type=F32)


def _dot_tn(a, b):
    return lax.dot_general(a.astype(BF16), b.astype(BF16), (((0,), (0,)), ((), ())),
                           preferred_element_type=F32)


def _split_dot(x, w_bf16):
    hi = x.astype(BF16)
    lo = (x - hi.astype(F32)).astype(BF16)
    return (jnp.dot(hi, w_bf16, preferred_element_type=F32)
            + jnp.dot(lo, w_bf16, preferred_element_type=F32))


def _pack_pairs(x):
    n = x.shape[1] // 2
    lo = lax.bitcast_convert_type(x[:, :n].astype(BF16).astype(F32), jnp.uint32) >> 16
    hi = lax.bitcast_convert_type(x[:, n:].astype(BF16).astype(F32), jnp.uint32) & jnp.uint32(0xFFFF0000)
    return lo | hi


def _unpack_pairs(u):
    lo = lax.bitcast_convert_type(u << 16, F32)
    hi = lax.bitcast_convert_type(u & jnp.uint32(0xFFFF0000), F32)
    return jnp.concatenate([lo, hi], axis=1)


def _softplus(x):
    return jnp.maximum(x, 0.0) + jnp.log1p(jnp.exp(-jnp.abs(x)))


def _mod_kernel(c_ref, w_ref, b_ref, o_ref):
    c = c_ref[...]
    sc = c * jax.nn.sigmoid(c)
    o_ref[...] = jnp.dot(sc, w_ref[...], precision=HIGHEST, preferred_element_type=F32) + b_ref[...]


def _adaln_mod(c, ada_w, ada_b):
    b, d = c.shape
    n = ada_w.shape[1]
    rows = 8
    cp = jnp.zeros((rows, d), F32).at[:b].set(c)
    tn = 1024
    out = pl.pallas_call(
        _mod_kernel,
        out_shape=jax.ShapeDtypeStruct((rows, n), F32),
        grid=(n // tn,),
        in_specs=[pl.BlockSpec((rows, d), lambda j: (0, 0)),
                  pl.BlockSpec((d, tn), lambda j: (0, j)),
                  pl.BlockSpec((1, tn), lambda j: (0, j))],
        out_specs=pl.BlockSpec((rows, tn), lambda j: (0, j)),
        compiler_params=_cparams(("arbitrary",)),
        name="adaln_mod",
    )(cp, ada_w, ada_b.reshape(1, n))
    return out[:b].reshape(b, 6, d)


def _norm_mod_kernel(x_ref, g_ref, mod_ref, o_ref, *, shift_idx, scale_idx):
    x = x_ref[0]
    ms = jnp.mean(x * x, axis=-1, keepdims=True)
    y = x * lax.rsqrt(ms + NORM_EPS) * g_ref[...]
    scale = mod_ref[0, scale_idx:scale_idx + 1, :]
    shift = mod_ref[0, shift_idx:shift_idx + 1, :]
    o_ref[0] = (y * (1.0 + scale) + shift).astype(o_ref.dtype)


def _norm_mod(x, g, mod, shift_idx, scale_idx, tm=1024):
    b, s, d = x.shape
    return pl.pallas_call(
        functools.partial(_norm_mod_kernel, shift_idx=shift_idx, scale_idx=scale_idx),
        out_shape=jax.ShapeDtypeStruct((b, s, d), BF16),
        grid=(b, s // tm),
        in_specs=[pl.BlockSpec((1, tm, d), lambda i, j: (i, j, 0)),
                  pl.BlockSpec((1, d), lambda i, j: (0, 0)),
                  pl.BlockSpec((1, 6, d), lambda i, j: (i, 0, 0))],
        out_specs=pl.BlockSpec((1, tm, d), lambda i, j: (i, j, 0)),
        compiler_params=_cparams(("arbitrary", "arbitrary")),
        name="norm1_mod",
    )(x, g.reshape(1, d), mod)


def _mm_kernel(a_ref, w_ref, *rest, act, has_bias):
    o_ref = rest[-1]
    r = jnp.dot(a_ref[...], w_ref[...], preferred_element_type=F32)
    if has_bias:
        r = r + rest[0][...]
    if act == "sigmoid":
        r = 0.5 * jnp.tanh(0.5 * r) + 0.5
    o_ref[...] = r.astype(o_ref.dtype)


def _matmul(a, w, out_dtype, bias=None, act=None, tm=2048, tn=512, name="proj"):
    t, k = a.shape
    n = w.shape[1]
    tn = min(tn, n)
    in_specs = [pl.BlockSpec((tm, k), lambda i, j: (i, 0)),
                pl.BlockSpec((k, tn), lambda i, j: (0, j))]
    args = [a, w]
    if bias is not None:
        in_specs.append(pl.BlockSpec((1, tn), lambda i, j: (0, j)))
        args.append(bias)
    return pl.pallas_call(
        functools.partial(_mm_kernel, act=act, has_bias=bias is not None),
        out_shape=jax.ShapeDtypeStruct((t, n), out_dtype),
        grid=(t // tm, n // tn),
        in_specs=in_specs,
        out_specs=pl.BlockSpec((tm, tn), lambda i, j: (i, j)),
        compiler_params=_cparams(("arbitrary", "arbitrary")),
        name=name,
    )(*args)


LOG2E = 1.4426950408889634


def _split3(x):
    hi = x.astype(BF16)
    r1 = x - hi.astype(F32)
    mid = r1.astype(BF16)
    lo = (r1 - mid.astype(F32)).astype(BF16)
    return hi, mid, lo


def _fcum_kernel(h_ref, wf_ref, fb_ref, sel_ref, o_ref, carry_ref, *, ts):
    @pl.when(pl.program_id(0) == 0)
    def _():
        carry_ref[...] = jnp.zeros_like(carry_ref)

    ri = lax.broadcasted_iota(jnp.int32, (ts, ts), 0)
    ci = lax.broadcasted_iota(jnp.int32, (ts, ts), 1)
    tri = jnp.where(ri >= ci, 1.0, 0.0).astype(BF16)
    for bi in range(h_ref.shape[0]):
        f = jnp.dot(h_ref[bi], wf_ref[...], preferred_element_type=F32) + fb_ref[...]
        lf = -_softplus(-f)
        cum = carry_ref[bi]
        for piece in _split3(lf):
            cum = cum + jnp.dot(tri, piece, preferred_element_type=F32)
        carry_ref[bi] = cum[ts - 1:ts, :]
        out = jnp.zeros((ts, sel_ref.shape[2]), F32)
        for idx, piece in enumerate(_split3(cum * (-LOG2E))):
            out = out + jnp.dot(piece, sel_ref[idx], preferred_element_type=F32)
        o_ref[bi] = out.astype(o_ref.dtype)


def _forget_bias(h, wf, fb, ts=512):
    b, s, d = h.shape
    pairs = HEADS // 2
    wf_p = jnp.zeros((d, LANES), F32).at[:, :HEADS].set(wf).astype(BF16)
    fb_p = jnp.zeros((1, LANES), F32).at[0, :HEADS].set(fb)
    hh = jnp.arange(HEADS)
    sel = jnp.zeros((3, LANES, pairs * LANES), F32)
    for piece in range(3):
        sel = sel.at[piece, hh, (hh // 2) * LANES + (hh % 2) * 3 + piece].set(1.0)
    return pl.pallas_call(
        functools.partial(_fcum_kernel, ts=ts),
        out_shape=jax.ShapeDtypeStruct((b, s, pairs * LANES), BF16),
        grid=(s // ts,),
        in_specs=[pl.BlockSpec((b, ts, d), lambda j: (0, j, 0)),
                  pl.BlockSpec((d, LANES), lambda j: (0, 0)),
                  pl.BlockSpec((1, LANES), lambda j: (0, 0)),
                  pl.BlockSpec((3, LANES, pairs * LANES), lambda j: (0, 0, 0))],
        out_specs=pl.BlockSpec((b, ts, pairs * LANES), lambda j: (0, j, 0)),
        scratch_shapes=[pltpu.VMEM((b, 1, LANES), F32)],
        compiler_params=_cparams(("arbitrary",)),
        name="forget_bias",
    )(h, wf_p, fb_p, sel.astype(BF16))


FOX_BQ = 1024
FOX_BK = 512


def _fox_kernel(first_ref, q_ref, k_ref, v_ref, a_ref, o_ref, m_ref, acc_ref, *, bq, bk):
    i = pl.program_id(2)
    lane = lax.broadcasted_iota(jnp.int32, (bq, LANES), 1)
    first = lane < HEAD_DIM
    qs = q_ref[0].astype(F32) * (HEAD_DIM ** -0.5 * LOG2E)
    aug0 = jnp.where(lane < 3, 1.0, 0.0)
    aug1 = jnp.where((lane >= 3) & (lane < 6), 1.0, 0.0)
    q01 = jnp.concatenate([jnp.concatenate([jnp.where(first, qs, 0.0), aug0], axis=1),
                           jnp.concatenate([jnp.where(first, 0.0, qs), aug1], axis=1)],
                          axis=0).astype(BF16)
    m_ref[...] = jnp.full_like(m_ref, -jnp.inf)
    acc_ref[...] = jnp.zeros_like(acc_ref)
    lane_k = lax.broadcasted_iota(jnp.int32, (bk, LANES), 1)
    keep_first = jnp.where(lane_k < HEAD_DIM, 1.0, 0.0).astype(BF16)
    keep_second = jnp.where(lane_k < HEAD_DIM, 0.0, 1.0).astype(BF16)

    def both_heads(x, r0):
        if r0 == 0:
            return x[...]
        return jnp.concatenate([x[r0:bq], x[bq + r0:2 * bq]], axis=0)

    def logits(j, r0=0):
        start = pl.multiple_of(j * bk, bk)
        kb = jnp.concatenate([k_ref[0, pl.ds(start, bk), :], a_ref[0, pl.ds(start, bk), :]], axis=1)
        return lax.dot_general(both_heads(q01, r0), kb, (((1,), (1,)), ((), ())),
                               preferred_element_type=F32)

    def consume(j, z, r0=0):
        nr = bq - r0
        start = pl.multiple_of(j * bk, bk)
        vb = v_ref[0, pl.ds(start, bk), :]
        m_prev = both_heads(m_ref, r0)
        m_new = jnp.maximum(m_prev, jnp.max(z, axis=1, keepdims=True))
        alpha = jnp.exp2(m_prev - m_new)
        p = jnp.exp2(z - jnp.concatenate([m_new] * (bk // LANES), axis=1)).astype(BF16)
        pv = jnp.concatenate(
            [jnp.dot(p[:nr], vb * keep_first + keep_second, preferred_element_type=F32),
             jnp.dot(p[nr:], vb * keep_second + keep_first, preferred_element_type=F32)], axis=0)
        acc_new = alpha * both_heads(acc_ref, r0) + pv
        if r0 == 0:
            acc_ref[...] = acc_new
            m_ref[...] = m_new
        else:
            for half, dst in ((slice(0, nr), slice(r0, bq)), (slice(nr, 2 * nr), slice(bq + r0, 2 * bq))):
                acc_ref[dst] = acc_new[half]
                m_ref[dst] = m_new[half]

    per_q = bq // bk
    n_full = i * per_q

    j_first = first_ref[pl.program_id(0), pl.program_id(1), i]
    odd = (n_full - j_first) & 1

    @pl.when(odd == 1)
    def _():
        consume(j_first, logits(j_first))

    def body(step, carry):
        j = j_first + odd + 2 * step
        z_a = logits(j)
        z_b = logits(j + 1)
        consume(j, z_a)
        consume(j + 1, z_b)
        return carry

    lax.fori_loop(0, (n_full - j_first) // 2, body, 0)
    for d in range(per_q):
        r0 = d * bk
        row = lax.broadcasted_iota(jnp.int32, (bq - r0, bk), 0)
        col = lax.broadcasted_iota(jnp.int32, (bq - r0, bk), 1)
        keep = col <= row
        z = logits(n_full + d, r0)
        consume(n_full + d, jnp.where(jnp.concatenate([keep, keep], axis=0), z, -jnp.inf), r0)
    acc = acc_ref[...]
    o = acc / pltpu.roll(acc, HEAD_DIM, 1)
    o_ref[0] = jnp.where(first, o[:bq], o[bq:]).astype(o_ref.dtype)


def _norm_bound_kernel(x_ref, sel_ref, o_ref):
    x = x_ref[...].astype(F32)
    ssq = _dot(x * x, sel_ref[...])
    o_ref[0] = jnp.broadcast_to(jnp.max(ssq, axis=0, keepdims=True), o_ref.shape[1:])


def _fox_first_block(qkv, fbias, bq, bk):
    b, s, _ = qkv.shape
    t = b * s
    nq, nk = s // bq, s // bk
    n_heads2 = 2 * HEADS
    sel = (jnp.arange(2 * WIDTH)[:, None] // HEAD_DIM == jnp.arange(LANES)[None, :]).astype(BF16)
    bounds = pl.pallas_call(
        _norm_bound_kernel,
        out_shape=jax.ShapeDtypeStruct((t // bk, 8, LANES), F32),
        grid=(t // bk,),
        in_specs=[pl.BlockSpec((bk, 2 * WIDTH), lambda i: (i, 0)),
                  pl.BlockSpec((2 * WIDTH, LANES), lambda i: (0, 0))],
        out_specs=pl.BlockSpec((1, 8, LANES), lambda i: (i, 0, 0)),
        compiler_params=_cparams(("arbitrary",)),
        name="fox_norm_bounds",
    )(qkv.reshape(t, -1), sel)
    nrm = jnp.sqrt(bounds[:, 0, :n_heads2]).reshape(b, nk, n_heads2) * 1.01
    qn = nrm[..., :HEADS] * (HEAD_DIM ** -0.5 * LOG2E * 1.01)
    kn = nrm[..., HEADS:]
    per_q = bq // bk
    qn_i = qn.reshape(b, nq, per_q, HEADS).max(axis=2)
    kn_i = kn.reshape(b, nq, per_q, HEADS).max(axis=2)
    kn_pre = lax.cummax(kn, axis=1)
    pairs = HEADS // 2
    def bias_rows(rows):
        pieces = rows.astype(F32).reshape(b, -1, pairs, LANES)[..., :6].reshape(b, -1, pairs, 2, 3)
        return pieces.sum(-1).reshape(b, -1, HEADS)

    nb_end = bias_rows(fbias[:, bk - 1::bk])
    nb_start = bias_rows(fbias[:, ::bq])
    gap = nb_start[:, :, None, :] - nb_end[:, None, :, :]
    need = qn_i[:, :, None, :] * (kn_pre[:, None, :, :] + kn_i[:, :, None, :]) + 152.0
    skip = (gap > need).reshape(b, nq, nk, pairs, 2).all(axis=-1)
    n_full = jnp.arange(nq) * per_q
    skip = skip & (jnp.arange(nk)[None, None, :, None] < n_full[None, :, None, None])
    first = jnp.argmin(skip, axis=2)
    return first.transpose(0, 2, 1).astype(jnp.int32)


def _fox_attention(qkv, fbias, bq=FOX_BQ, bk=FOX_BK):
    b, s, _ = qkv.shape
    pairs = HEADS // 2
    cb = WIDTH // LANES
    first = _fox_first_block(qkv, fbias, bq, bk)
    grid_spec = pltpu.PrefetchScalarGridSpec(
        num_scalar_prefetch=1,
        grid=(b, pairs, s // bq),
        in_specs=[pl.BlockSpec((1, bq, LANES), lambda bi, hp, i, fr: (bi, i, hp)),
                  pl.BlockSpec((1, s, LANES), lambda bi, hp, i, fr: (bi, 0, cb + hp)),
                  pl.BlockSpec((1, s, LANES), lambda bi, hp, i, fr: (bi, 0, 2 * cb + hp)),
                  pl.BlockSpec((1, s, LANES), lambda bi, hp, i, fr: (bi, 0, hp))],
        out_specs=pl.BlockSpec((1, bq, LANES), lambda bi, hp, i, fr: (bi, i, hp)),
        scratch_shapes=[pltpu.VMEM((2 * bq, LANES), F32), pltpu.VMEM((2 * bq, LANES), F32)],
    )
    return pl.pallas_call(
        functools.partial(_fox_kernel, bq=bq, bk=bk),
        out_shape=jax.ShapeDtypeStruct((b, s, WIDTH), BF16),
        grid_spec=grid_spec,
        compiler_params=_cparams(("arbitrary", "arbitrary", "arbitrary")),
        name="fox_attention",
    )(first, qkv, qkv, qkv, fbias)


PAIR = 2 * HEAD_DIM
GROUPS = WIDTH // PAIR


def _group(x, g):
    return x[:, g * PAIR:(g + 1) * PAIR]


def _head_sum(x, bd2):
    return jnp.concatenate([_dot(_group(x, g), bd2) for g in range(GROUPS)], axis=1)


def _head_apply(mats, x, lane_first):
    rows = mats.shape[1]
    outs = []
    for g in range(GROUPS):
        res = _dot(mats[2 * g:2 * g + 2].reshape(2 * rows, rows), _group(x, g))
        outs.append(jnp.where(lane_first, res[:rows], res[rows:]))
    return jnp.concatenate(outs, axis=1)


def _rwkv_kernel(p_ref, mu_ref, w0_ref, a0_ref, kk_ref, ka_ref, rk_ref, lng_ref, lnb_ref,
                 wwa_ref, g2_ref, bd_ref, o_ref, st_ref, prev_ref):
    L = CHUNK

    @pl.when(pl.program_id(1) == 0)
    def _():
        st_ref[...] = jnp.zeros_like(st_ref)
        prev_ref[...] = jnp.zeros_like(prev_ref)

    p = p_ref[0]
    T = p.shape[0]
    rowi = lax.broadcasted_iota(jnp.int32, p.shape, 0)
    prev = jnp.where(rowi == 0, prev_ref[...], pltpu.roll(p, 1, 0))
    prev_ref[...] = p[T - 1:T, :]
    ps = p + (prev - p) * mu_ref[...]
    r = ps[:, 0:WIDTH]
    k = ps[:, WIDTH:2 * WIDTH]
    v = ps[:, 2 * WIDTH:3 * WIDTH]
    wa_in = ps[:, 3 * WIDTH:3 * WIDTH + DECAY_LORA + AAA_LORA]
    gd = ps[:, 3 * WIDTH + DECAY_LORA + AAA_LORA:]
    lane_wa = lax.broadcasted_iota(jnp.int32, wa_in.shape, 1)
    wa_act = jnp.where(lane_wa < DECAY_LORA, jnp.tanh(wa_in), wa_in)
    wa = _dot(wa_act, wwa_ref[...])
    log_w = -_softplus(-(w0_ref[...] + wa[:, :WIDTH])) - 0.5
    lw = -jnp.exp(log_w)
    a = jax.nn.sigmoid(a0_ref[...] + wa[:, WIDTH:])
    out_gate = _dot(jax.nn.sigmoid(gd), g2_ref[...])
    bd = bd_ref[...]
    kk0 = k * kk_ref[...]
    kk = kk0 * lax.rsqrt(jnp.maximum(_head_sum(kk0 * kk0, bd), 1e-24))
    k2 = k * (1.0 + (a - 1.0) * ka_ref[...])
    av = -kk
    bv = kk * a

    n_sub = T // L
    rt_i = lax.broadcasted_iota(jnp.int32, (T, T), 0)
    ct_i = lax.broadcasted_iota(jnp.int32, (T, T), 1)
    tri_tile = (rt_i >= ct_i) & (rt_i // L == ct_i // L)
    cl = _split_dot_left(jnp.where(tri_tile, 1.0, 0.0).astype(BF16), lw)
    cl_end = jnp.concatenate([jnp.broadcast_to(cl[(c + 1) * L - 1:(c + 1) * L, :], (L, WIDTH))
                              for c in range(n_sub)], axis=0)
    at_all = av * jnp.exp(cl - lw)
    rt_all = r * jnp.exp(cl)
    einv = jnp.exp(-cl)
    bt_all = bv * einv
    kt_all = k2 * einv
    edec = jnp.exp(cl_end - cl)
    b_end_all = bv * edec
    k_end_all = k2 * edec

    ri = lax.broadcasted_iota(jnp.int32, (L, L), 0)
    ci = lax.broadcasted_iota(jnp.int32, (L, L), 1)
    tri_incl = ri >= ci
    tri_strict = ri > ci
    eye = jnp.where(ri == ci, 1.0, 0.0)
    lane_first = lax.broadcasted_iota(jnp.int32, (L, PAIR), 1) < HEAD_DIM
    qr = lax.broadcasted_iota(jnp.int32, (PAIR, PAIR), 0) < HEAD_DIM
    qc = lax.broadcasted_iota(jnp.int32, (PAIR, PAIR), 1) < HEAD_DIM
    same_head = qr == qc

    def bmm(x, y):
        return lax.dot_general(x.astype(BF16), y.astype(BF16), (((2,), (1,)), ((0,), (0,))),
                               preferred_element_type=F32)

    def chunk_terms(c):
        rows = slice(c * L, (c + 1) * L)
        at, rt, bt, kt, vc = at_all[rows], rt_all[rows], bt_all[rows], kt_all[rows], v[rows]
        sb_heads, sk_heads = [], []
        for g in range(GROUPS):
            at_g, rt_g = _group(at, g), _group(rt, g)
            lhs = jnp.concatenate([jnp.where(lane_first, at_g, 0.0), jnp.where(lane_first, rt_g, 0.0),
                                   jnp.where(lane_first, 0.0, at_g), jnp.where(lane_first, 0.0, rt_g)],
                                  axis=0).astype(BF16)
            sb_g = _dot_nt(lhs, _group(bt, g))
            sk_g = _dot_nt(lhs, _group(kt, g))
            for hh in range(2):
                sb_heads.append(sb_g[hh * 2 * L:(hh + 1) * 2 * L])
                sk_heads.append(sk_g[hh * 2 * L:(hh + 1) * 2 * L])
        sb = jnp.stack(sb_heads)
        sk = jnp.stack(sk_heads)
        n_ab = jnp.where(tri_strict, sb[:, :L, :], 0.0)
        a_ak = jnp.where(tri_strict, sk[:, :L, :], 0.0)
        a_rb = jnp.where(tri_incl, sb[:, L:, :], 0.0)
        a_rk = jnp.where(tri_incl, sk[:, L:, :], 0.0)
        tinv = eye + n_ab
        pw = bmm(n_ab, n_ab)
        span = 2
        while 2 * span < L:
            both = bmm(jnp.concatenate([tinv, pw], axis=1), pw)
            tinv = tinv + both[:, :L, :]
            pw = both[:, L:, :]
            span *= 2
        tinv = tinv + bmm(tinv, pw)
        av_term = _head_apply(a_ak, vc, lane_first)
        pm = _head_apply(tinv, at, lane_first)
        qm = _head_apply(tinv, av_term, lane_first)
        rkv = _head_apply(a_rk, vc, lane_first)
        return pm, qm, rkv, a_rb

    terms = [chunk_terms(c) for c in range(n_sub)]

    y_chunks = []
    for c in range(n_sub):
        rows = slice(c * L, (c + 1) * L)
        pm, qm, rkv, a_rb = terms[c]
        rt, vc, b_end, k_end = rt_all[rows], v[rows], b_end_all[rows], k_end_all[rows]
        gam_last = jnp.exp(cl[(c + 1) * L - 1:(c + 1) * L, :])
        u_parts, ys_parts = [], []
        for g in range(GROUPS):
            pr = _dot_nt(jnp.concatenate([_group(pm, g), _group(rt, g)], axis=0), st_ref[g])
            u_parts.append(pr[:L] + _group(qm, g))
            ys_parts.append(pr[L:])
        u = jnp.concatenate(u_parts, axis=1)
        y_chunks.append(jnp.concatenate(ys_parts, axis=1) + _head_apply(a_rb, u, lane_first) + rkv)
        for g in range(GROUPS):
            upd = _dot_tn(_group(u, g), _group(b_end, g)) + _dot_tn(_group(vc, g), _group(k_end, g))
            st_ref[g] = st_ref[g] * _group(gam_last, g) + jnp.where(same_head, upd, 0.0)
    y = jnp.concatenate(y_chunks, axis=0)

    inv_n = 1.0 / HEAD_DIM
    mean = _head_sum(y, bd) * inv_n
    dlt = y - mean
    var = _head_sum(dlt * dlt, bd) * inv_n
    yn = dlt * lax.rsqrt(var + LN_X_EPS) * lng_ref[...] + lnb_ref[...]
    bonus = _head_sum(r * k2 * rk_ref[...], bd) * v
    o_ref[0] = ((yn + bonus) * out_gate).astype(o_ref.dtype)


def _split_dot_left(w_bf16, x):
    hi = x.astype(BF16)
    r1 = x - hi.astype(F32)
    mid = r1.astype(BF16)
    lo = (r1 - mid.astype(F32)).astype(BF16)
    return (jnp.dot(w_bf16, hi, preferred_element_type=F32)
            + jnp.dot(w_bf16, mid, preferred_element_type=F32)
            + jnp.dot(w_bf16, lo, preferred_element_type=F32))


def _rwkv_branch(p_rw, mu, w0, w2, a0, a2, g2, k_k, k_a, r_k, ln_g, ln_b):
    b, s, sw = p_rw.shape
    row = lambda t: t.reshape(1, -1).astype(F32)
    wwa = jnp.zeros((DECAY_LORA + AAA_LORA, 2 * WIDTH), F32)
    wwa = wwa.at[:DECAY_LORA, :WIDTH].set(w2).at[DECAY_LORA:, WIDTH:].set(a2).astype(BF16)
    hid = jnp.arange(PAIR) // HEAD_DIM
    bd = (hid[:, None] == hid[None, :]).astype(BF16)
    const = lambda shape: pl.BlockSpec(shape, lambda i, j: (0,) * len(shape))
    return pl.pallas_call(
        _rwkv_kernel,
        out_shape=jax.ShapeDtypeStruct((b, s, WIDTH), BF16),
        grid=(b, s // RWKV_TILE),
        in_specs=[pl.BlockSpec((1, RWKV_TILE, sw), lambda i, j: (i, j, 0)),
                  const((1, sw)), const((1, WIDTH)), const((1, WIDTH)), const((1, WIDTH)),
                  const((1, WIDTH)), const((1, WIDTH)), const((1, WIDTH)), const((1, WIDTH)),
                  const((DECAY_LORA + AAA_LORA, 2 * WIDTH)), const((GATE_LORA, WIDTH)),
                  const((PAIR, PAIR))],
        out_specs=pl.BlockSpec((1, RWKV_TILE, WIDTH), lambda i, j: (i, j, 0)),
        scratch_shapes=[pltpu.VMEM((GROUPS, PAIR, PAIR), F32), pltpu.VMEM((1, sw), F32)],
        compiler_params=_cparams(("arbitrary", "arbitrary")),
        name="rwkv7_scan",
    )(p_rw, row(mu), row(w0), row(a0), row(k_k), row(k_a), row(r_k), row(ln_g), row(ln_b),
      wwa, g2.astype(BF16), bd)


def _out_kernel(of_ref, orw_ref, gate_ref, x_ref, mod_ref, wof_ref, wor_ref, wo_ref, n2g_ref,
                wr_ref, br_ref, x1_ref, h2_ref, route_ref, counts_ref, cnt_ref):
    d = x_ref.shape[-1]
    gate = gate_ref[...].astype(F32)
    merged = (gate[:, :d] * jnp.dot(of_ref[...], wof_ref[...], preferred_element_type=F32)
              + gate[:, d:] * jnp.dot(orw_ref[...], wor_ref[...], preferred_element_type=F32))
    gate1 = mod_ref[0, 2:3, :]
    shift2 = mod_ref[0, 3:4, :]
    scale2 = mod_ref[0, 4:5, :]
    x1 = x_ref[...] + gate1 * jnp.dot(merged.astype(BF16), wo_ref[...], preferred_element_type=F32)
    x1_ref[...] = x1
    ms = jnp.mean(x1 * x1, axis=-1, keepdims=True)
    h2 = x1 * lax.rsqrt(ms + NORM_EPS) * n2g_ref[...] * (1.0 + scale2) + shift2
    h2_ref[...] = _pack_pairs(h2)

    h2_hi = h2.astype(BF16)
    h2_lo = (h2 - h2_hi.astype(F32)).astype(BF16)
    logits = (jnp.dot(h2_hi, wr_ref[0], preferred_element_type=F32)
              + jnp.dot(h2_lo, wr_ref[0], preferred_element_type=F32)
              + jnp.dot(h2_hi, wr_ref[1], preferred_element_type=F32)) + br_ref[...]
    lane = lax.broadcasted_iota(jnp.int32, logits.shape, 1)
    neg = -jnp.inf
    big = jnp.int32(LANES)
    gl = jnp.where(lane < N_GROUPS, logits, neg)
    gmax = jnp.max(gl, axis=1, keepdims=True)
    gidx = jnp.min(jnp.where(gl == gmax, lane, big), axis=1, keepdims=True)
    g_p = 1.0 / jnp.sum(jnp.exp(gl - gmax), axis=1, keepdims=True)
    e_lane = lane - N_GROUPS
    in_grp = (e_lane >= 0) & (e_lane < N_EXPERTS) & ((e_lane // EXPERTS_PER_GROUP) == gidx)
    sel = jnp.where(in_grp, logits, neg)
    m1 = jnp.max(sel, axis=1, keepdims=True)
    i1 = jnp.min(jnp.where(sel == m1, lane, big), axis=1, keepdims=True)
    sel2 = jnp.where(lane == i1, neg, sel)
    m2 = jnp.max(sel2, axis=1, keepdims=True)
    i2 = jnp.min(jnp.where(sel2 == m2, lane, big), axis=1, keepdims=True)
    e21 = jnp.exp(m2 - m1)
    w_first = g_p / (1.0 + e21)
    w_second = g_p * e21 / (1.0 + e21)
    @pl.when(pl.program_id(0) == 0)
    def _():
        cnt_ref[...] = jnp.zeros_like(cnt_ref)

    tm = logits.shape[0]
    oh1 = lane == i1
    oh2 = lane == i2
    both = jnp.where(oh1 | oh2, 1.0, 0.0)
    before = (lax.broadcasted_iota(jnp.int32, (tm, tm), 0)
              > lax.broadcasted_iota(jnp.int32, (tm, tm), 1))
    seen = jnp.dot(jnp.where(before, 1.0, 0.0).astype(BF16), both.astype(BF16),
                   preferred_element_type=F32) + cnt_ref[...]
    rank1 = jnp.sum(jnp.where(oh1, seen, 0.0), axis=1, keepdims=True)
    rank2 = jnp.sum(jnp.where(oh2, seen, 0.0), axis=1, keepdims=True)
    cnt_ref[...] = cnt_ref[...] + jnp.sum(both, axis=0, keepdims=True)
    counts_ref[...] = jnp.broadcast_to(cnt_ref[...], counts_ref.shape)

    route = jnp.where(lane == 0, (i1 - N_GROUPS).astype(F32),
                      jnp.where(lane == 1, (i2 - N_GROUPS).astype(F32),
                                jnp.where(lane == 2, w_first,
                                          jnp.where(lane == 3, w_second,
                                                    jnp.where(lane == 4, rank1,
                                                              jnp.where(lane == 5, rank2, 0.0))))))
    route_ref[...] = route


def _merge_out_router(o_fox, o_rw, gate, x, mod, wof, wor, wo, n2g, wr, br, tm=512):
    b, s, d = x.shape
    t = b * s
    spb = s // tm
    rowspec = lambda w: pl.BlockSpec((tm, w), lambda i: (i, 0))
    const = lambda shape: pl.BlockSpec(shape, lambda i: (0,) * len(shape))
    return pl.pallas_call(
        _out_kernel,
        out_shape=(jax.ShapeDtypeStruct((t, d), F32), jax.ShapeDtypeStruct((t, d // 2), jnp.uint32),
                   jax.ShapeDtypeStruct((t, LANES), F32), jax.ShapeDtypeStruct((8, LANES), F32)),
        grid=(t // tm,),
        in_specs=[rowspec(WIDTH), rowspec(WIDTH), rowspec(2 * d), rowspec(d),
                  pl.BlockSpec((1, 6, d), lambda i: (i // spb, 0, 0)),
                  const((WIDTH, d)), const((WIDTH, d)), const((d, d)), const((1, d)),
                  const((2, d, LANES)), const((1, LANES))],
        out_specs=(rowspec(d), rowspec(d // 2), rowspec(LANES), const((8, LANES))),
        scratch_shapes=[pltpu.VMEM((1, LANES), F32)],
        compiler_params=_cparams(("arbitrary",)),
        name="merge_out_router",
    )(o_fox, o_rw, gate, x.reshape(t, d), mod, wof, wor, wo, n2g.reshape(1, d), wr, br)


def _dispatch_kernel(zstart_ref, dest_ref, h_ref, xs_ref, zero_ref, sem, *, tm):
    @pl.when(pl.program_id(0) == 0)
    def _():
        zero_ref[...] = jnp.zeros_like(zero_ref)
        for e in range(N_EXPERTS):
            zrow = pl.multiple_of(zstart_ref[e], MOE_ROWS)
            pltpu.make_async_copy(zero_ref, xs_ref.at[pl.ds(zrow, MOE_ROWS)], sem).start()
        for e in range(N_EXPERTS):
            pltpu.make_async_copy(zero_ref, xs_ref.at[pl.ds(0, MOE_ROWS)], sem).wait()
        for phase in ("start", "wait"):
            for e in range(N_EXPERTS):
                trow = pl.multiple_of(zstart_ref[N_EXPERTS] + e * MOE_ROWS, MOE_ROWS)

                @pl.when(trow < xs_ref.shape[0])
                def _():
                    tail = pltpu.make_async_copy(zero_ref, xs_ref.at[pl.ds(trow, MOE_ROWS)], sem)
                    tail.start() if phase == "start" else tail.wait()

    def issue(r, carry):
        for kk in range(TOP_K):
            dst = dest_ref[0, 0, TOP_K * r + kk]
            pltpu.make_async_copy(h_ref.at[pl.ds(r, 1)], xs_ref.at[pl.ds(dst, 1)], sem).start(priority=kk)
        return carry

    lax.fori_loop(0, tm, issue, 0, unroll=8)
    for kk in range(TOP_K):
        pltpu.make_async_copy(h_ref, xs_ref.at[pl.ds(0, tm)], sem).wait()


def _moe_dispatch(h2, dest3, zstart, rows, tm=256):
    t, d = h2.shape
    grid_spec = pltpu.PrefetchScalarGridSpec(
        num_scalar_prefetch=1,
        grid=(t // tm,),
        in_specs=[pl.BlockSpec((1, 1, TOP_K * tm), lambda i, zs: (i, 0, 0), memory_space=pltpu.SMEM),
                  pl.BlockSpec((tm, d), lambda i, zs: (i, 0))],
        out_specs=pl.BlockSpec(memory_space=pl.ANY),
        scratch_shapes=[pltpu.VMEM((MOE_ROWS, d), h2.dtype), pltpu.SemaphoreType.DMA(())],
    )
    return pl.pallas_call(
        functools.partial(_dispatch_kernel, tm=tm),
        out_shape=jax.ShapeDtypeStruct((rows, d), h2.dtype),
        grid_spec=grid_spec,
        compiler_params=_cparams(("arbitrary",)),
        name="moe_dispatch",
    )(zstart, dest3, h2)


def _expert_kernel(blk_e_ref, nused_ref, xs_ref, w1_ref, w3_ref, w2_ref, o_ref, w1b_ref, w3b_ref, w2b_ref):
    i = pl.program_id(0)
    live = i * MOE_ROWS < nused_ref[0]
    new_expert = jnp.logical_or(i == 0, blk_e_ref[i] != blk_e_ref[jnp.maximum(i - 1, 0)])

    @pl.when(new_expert)
    def _():
        w1b_ref[...] = w1_ref[0].astype(BF16)
        w3b_ref[...] = w3_ref[0].astype(BF16)
        w2b_ref[...] = w2_ref[0].astype(BF16)

    @pl.when(live)
    def _():
        xb = _unpack_pairs(xs_ref[...]).astype(BF16)
        h1 = jnp.dot(xb, w1b_ref[...], preferred_element_type=F32)
        h3 = jnp.dot(xb, w3b_ref[...], preferred_element_type=F32)
        hh = (h1 * jax.nn.sigmoid(h1)) * h3
        o_ref[...] = _pack_pairs(jnp.dot(hh.astype(BF16), w2b_ref[...], preferred_element_type=F32))

    @pl.when(jnp.logical_not(live))
    def _():
        o_ref[...] = jnp.zeros_like(o_ref)


def _moe_experts(xs, blk_e, nused, w1, w3, w2):
    rows, dp = xs.shape
    _, d, de = w1.shape
    grid_spec = pltpu.PrefetchScalarGridSpec(
        num_scalar_prefetch=2,
        grid=(rows // MOE_ROWS,),
        in_specs=[pl.BlockSpec((MOE_ROWS, dp), lambda i, be, nu: (i, 0)),
                  pl.BlockSpec((1, d, de), lambda i, be, nu: (be[i], 0, 0)),
                  pl.BlockSpec((1, d, de), lambda i, be, nu: (be[i], 0, 0)),
                  pl.BlockSpec((1, de, d), lambda i, be, nu: (be[i], 0, 0))],
        out_specs=pl.BlockSpec((MOE_ROWS, dp), lambda i, be, nu: (i, 0)),
        scratch_shapes=[pltpu.VMEM((d, de), BF16), pltpu.VMEM((d, de), BF16), pltpu.VMEM((de, d), BF16)],
    )
    return pl.pallas_call(
        _expert_kernel,
        out_shape=jax.ShapeDtypeStruct((rows, dp), xs.dtype),
        grid_spec=grid_spec,
        compiler_params=_cparams(("arbitrary",)),
        name="moe_experts",
    )(blk_e, nused, xs, w1, w3, w2)


def _final_kernel(dest_ref, dest_next_ref, route_ref, x1_ref, mod_ref, fg_ref, ys_ref, o_ref, buf_ref, sem,
                  *, tm):
    i = pl.program_id(0)
    half = i & 1

    def gather(idx_ref, into):
        def issue(r, carry):
            for kk in range(TOP_K):
                src = idx_ref[0, 0, TOP_K * r + kk]
                pltpu.make_async_copy(ys_ref.at[pl.ds(src, 1)], buf_ref.at[into, kk, pl.ds(r, 1)],
                                      sem.at[into]).start(priority=kk)
            return carry

        lax.fori_loop(0, tm, issue, 0, unroll=8)

    @pl.when(i == 0)
    def _():
        gather(dest_ref, 0)

    @pl.when(i + 1 < pl.num_programs(0))
    def _():
        gather(dest_next_ref, 1 - half)

    for kk in range(TOP_K):
        pltpu.make_async_copy(ys_ref.at[pl.ds(0, tm)], buf_ref.at[half, kk], sem.at[half]).wait()

    route = route_ref[...]
    y = (route[:, 2:3] * _unpack_pairs(buf_ref[half, 0])
         + route[:, 3:4] * _unpack_pairs(buf_ref[half, 1]))
    gate2 = mod_ref[0, 5:6, :]
    x2 = x1_ref[...] + gate2 * y
    ms = jnp.mean(x2 * x2, axis=-1, keepdims=True)
    o_ref[...] = x2 * lax.rsqrt(ms + NORM_EPS) * fg_ref[...]


def _moe_combine_final(dest3, route, x1, mod, final_g, ys, s, tm=256):
    t, d = x1.shape
    spb = s // tm
    return pl.pallas_call(
        functools.partial(_final_kernel, tm=tm),
        out_shape=jax.ShapeDtypeStruct((t, d), F32),
        grid=(t // tm,),
        in_specs=[pl.BlockSpec((1, 1, TOP_K * tm), lambda i: (i, 0, 0), memory_space=pltpu.SMEM),
                  pl.BlockSpec((1, 1, TOP_K * tm), lambda i: (jnp.minimum(i + 1, t // tm - 1), 0, 0),
                               memory_space=pltpu.SMEM),
                  pl.BlockSpec((tm, LANES), lambda i: (i, 0)),
                  pl.BlockSpec((tm, d), lambda i: (i, 0)),
                  pl.BlockSpec((1, 6, d), lambda i: (i // spb, 0, 0)),
                  pl.BlockSpec((1, d), lambda i: (0, 0)),
                  pl.BlockSpec(memory_space=pl.ANY)],
        out_specs=pl.BlockSpec((tm, d), lambda i: (i, 0)),
        scratch_shapes=[pltpu.VMEM((2, TOP_K, tm, ys.shape[1]), ys.dtype), pltpu.SemaphoreType.DMA((2,))],
        compiler_params=_cparams(("arbitrary",)),
        name="moe_combine_final",
    )(dest3, dest3, route, x1, mod, final_g.reshape(1, d), ys)


def _moe_plan(route, counts, tm):
    t = route.shape[0]
    m = t * TOP_K
    flat_e = route[:, :TOP_K].astype(jnp.int32).reshape(m)
    rank = route[:, 4:4 + TOP_K].astype(jnp.int32).reshape(m)
    counts = counts[0, N_GROUPS:N_GROUPS + N_EXPERTS].astype(jnp.int32)
    padded = (counts + MOE_ROWS - 1) // MOE_ROWS * MOE_ROWS
    pad_end = jnp.cumsum(padded)
    pad_start = pad_end - padded
    experts = jnp.arange(N_EXPERTS, dtype=jnp.int32)
    start_of = jnp.sum(jnp.where(flat_e[:, None] == experts[None, :], pad_start[None, :], 0), axis=1)
    dest = (start_of + rank).astype(jnp.int32)
    n_blocks = m // MOE_ROWS + N_EXPERTS
    blk_start = jnp.arange(n_blocks, dtype=jnp.int32) * MOE_ROWS
    blk_e = jnp.minimum(jnp.sum(pad_end[None, :] <= blk_start[:, None], axis=1), N_EXPERTS - 1).astype(jnp.int32)
    nused = pad_end[-1:].astype(jnp.int32)
    zstart = jnp.concatenate([jnp.maximum(pad_end - MOE_ROWS, 0), pad_end[-1:]]).astype(jnp.int32)
    return dest.reshape(t // tm, 1, TOP_K * tm), blk_e, nused, zstart, n_blocks * MOE_ROWS


def kernel(x, c, ada_w, ada_b, norm1_g, w_in, fox_forget_b, shift_mu, rwkv_w0, rwkv_w2, rwkv_a0, rwkv_a2, rwkv_g2, rwkv_k_k, rwkv_k_a, rwkv_r_k, ln_x_g, ln_x_b, w_out_fox, w_out_rwkv, w_o, norm2_g, router_group_w, router_group_b, router_expert_w, router_expert_b, exp_w1, exp_w3, exp_w2, final_g):
    b, s, d = x.shape
    t = b * s
    assert ada_w.shape[0] == 1, "the final norm is fused into the last layer's combine; one layer is laid out"
    for l in range(1):
        mod = _adaln_mod(c, ada_w[l], ada_b[l])

        h = _norm_mod(x, norm1_g[l], mod, shift_idx=0, scale_idx=1)
        h2d = h.reshape(t, d)
        w = w_in[l]
        o_f = 3 * WIDTH
        o_rw = o_f + HEADS
        o_g = o_rw + SHIFT_WIDTH
        perm = jnp.argsort(fox_forget_b[l])
        by_head = lambda m: m.reshape(d, HEADS, HEAD_DIM)[:, perm]
        wq = by_head(w[:, :WIDTH]).reshape(d, WIDTH)
        wk = by_head(w[:, WIDTH:2 * WIDTH]).reshape(d, WIDTH)
        wv = by_head(w[:, 2 * WIDTH:o_f]).reshape(d, WIDTH)
        qkv = _matmul(h2d, jnp.concatenate([wq, wk, wv], axis=1).astype(BF16), BF16, name="proj_qkv")
        p_rw = _matmul(h2d, w[:, o_rw:o_g].astype(BF16), F32, tn=896, name="proj_rwkv")
        gate = _matmul(h2d, w[:, o_g:].astype(BF16), BF16, act="sigmoid", name="proj_gate")
        fbias = _forget_bias(h, w[:, o_f:o_rw][:, perm], fox_forget_b[l][perm])
        o_fox = _fox_attention(qkv.reshape(b, s, 3 * WIDTH), fbias)
        w_of = w_out_fox[l].reshape(HEADS, HEAD_DIM, d)[perm].reshape(WIDTH, d)
        o_rwkv = _rwkv_branch(p_rw.reshape(b, s, SHIFT_WIDTH), shift_mu[l], rwkv_w0[l], rwkv_w2[l],
                              rwkv_a0[l], rwkv_a2[l], rwkv_g2[l], rwkv_k_k[l], rwkv_k_a[l],
                              rwkv_r_k[l], ln_x_g[l], ln_x_b[l])

        wr = jnp.zeros((d, LANES), F32)
        wr = wr.at[:, :N_GROUPS].set(router_group_w[l]).at[:, N_GROUPS:N_GROUPS + N_EXPERTS].set(router_expert_w[l])
        br = jnp.zeros((1, LANES), F32)
        br = br.at[0, :N_GROUPS].set(router_group_b[l]).at[0, N_GROUPS:N_GROUPS + N_EXPERTS].set(router_expert_b[l])
        wr_hi = wr.astype(BF16)
        wr_lo = (wr - wr_hi.astype(F32)).astype(BF16)
        x1, h2, route, counts = _merge_out_router(
            o_fox.reshape(t, WIDTH), o_rwkv.reshape(t, WIDTH), gate, x, mod,
            w_of.astype(BF16), w_out_rwkv[l].astype(BF16), w_o[l].astype(BF16), norm2_g[l],
            jnp.stack([wr_hi, wr_lo]), br)

        tm_moe = 256
        dest3, blk_e, nused, zstart, rows = _moe_plan(route, counts, tm_moe)
        xs = _moe_dispatch(h2, dest3, zstart, rows, tm=tm_moe)
        ys = _moe_experts(xs, blk_e, nused, exp_w1[l], exp_w3[l], exp_w2[l])
        out = _moe_combine_final(dest3, route, x1, mod, final_g, ys, s, tm=tm_moe)
    return out.reshape(b, s, d)
```

```python
import functools

import jax
import jax.numpy as jnp
from jax import lax
from jax.experimental import pallas as pl
from jax.experimental.pallas import tpu as pltpu
from jax.experimental.pallas import tpu_sc as plsc

F32 = jnp.float32
BF16 = jnp.bfloat16
HIGHEST = lax.Precision.HIGHEST

HEADS = 8
HEAD_DIM = 64
WIDTH = HEADS * HEAD_DIM
DECAY_LORA = 64
AAA_LORA = 64
GATE_LORA = 128
SHIFT_WIDTH = 3 * WIDTH + DECAY_LORA + AAA_LORA + GATE_LORA
LN_X_EPS = 64e-5
NORM_EPS = 1e-6
N_GROUPS = 4
EXPERTS_PER_GROUP = 8
N_EXPERTS = N_GROUPS * EXPERTS_PER_GROUP
TOP_K = 2

LANES = 128
CHUNK = 64
RWKV_TILE = 256
MOE_ROWS = 512
VMEM_LIMIT = 48 * 1024 * 1024


def _cparams(sem):
    return pltpu.CompilerParams(dimension_semantics=sem, vmem_limit_bytes=VMEM_LIMIT)


def _dot(a, b):
    return jnp.dot(a.astype(BF16), b.astype(BF16), preferred_element_type=F32)


def _dot_nt(a, b):
    return lax.dot_general(a.astype(BF16), b.astype(BF16), (((1,), (1,)), ((), ())),
                           preferred_element_type=F32)


def _dot_tn(a, b):
    return lax.dot_general(a.astype(BF16), b.astype(BF16), (((0,), (0,)), ((), ())),
                           preferred_element_type=F32)


def _split_dot(x, w_bf16):
    hi = x.astype(BF16)
    lo = (x - hi.astype(F32)).astype(BF16)
    return (jnp.dot(hi, w_bf16, preferred_element_type=F32)
            + jnp.dot(lo, w_bf16, preferred_element_type=F32))


def _pack_pairs(x):
    n = x.shape[1] // 2
    lo = lax.bitcast_convert_type(x[:, :n].astype(BF16).astype(F32), jnp.uint32) >> 16
    hi = lax.bitcast_convert_type(x[:, n:].astype(BF16).astype(F32), jnp.uint32) & jnp.uint32(0xFFFF0000)
    return lo | hi


def _unpack_pairs(u):
    lo = lax.bitcast_convert_type(u << 16, F32)
    hi = lax.bitcast_convert_type(u & jnp.uint32(0xFFFF0000), F32)
    return jnp.concatenate([lo, hi], axis=1)


def _softplus(x):
    return jnp.maximum(x, 0.0) + jnp.log1p(jnp.exp(-jnp.abs(x)))


def _mod_kernel(c_ref, w_ref, b_ref, o_ref):
    c = c_ref[...]
    sc = c * jax.nn.sigmoid(c)
    o_ref[...] = jnp.dot(sc, w_ref[...], precision=HIGHEST, preferred_element_type=F32) + b_ref[...]


def _adaln_mod(c, ada_w, ada_b):
    b, d = c.shape
    n = ada_w.shape[1]
    rows = 8
    cp = jnp.zeros((rows, d), F32).at[:b].set(c)
    tn = 1024
    out = pl.pallas_call(
        _mod_kernel,
        out_shape=jax.ShapeDtypeStruct((rows, n), F32),
        grid=(n // tn,),
        in_specs=[pl.BlockSpec((rows, d), lambda j: (0, 0)),
                  pl.BlockSpec((d, tn), lambda j: (0, j)),
                  pl.BlockSpec((1, tn), lambda j: (0, j))],
        out_specs=pl.BlockSpec((rows, tn), lambda j: (0, j)),
        compiler_params=_cparams(("arbitrary",)),
        name="adaln_mod",
    )(cp, ada_w, ada_b.reshape(1, n))
    return out[:b].reshape(b, 6, d)


def _norm_mod_kernel(x_ref, g_ref, mod_ref, o_ref, *, shift_idx, scale_idx):
    x = x_ref[0]
    ms = jnp.mean(x * x, axis=-1, keepdims=True)
    y = x * lax.rsqrt(ms + NORM_EPS) * g_ref[...]
    scale = mod_ref[0, scale_idx:scale_idx + 1, :]
    shift = mod_ref[0, shift_idx:shift_idx + 1, :]
    o_ref[0] = (y * (1.0 + scale) + shift).astype(o_ref.dtype)


def _norm_mod(x, g, mod, shift_idx, scale_idx, tm=1024):
    b, s, d = x.shape
    return pl.pallas_call(
        functools.partial(_norm_mod_kernel, shift_idx=shift_idx, scale_idx=scale_idx),
        out_shape=jax.ShapeDtypeStruct((b, s, d), BF16),
        grid=(b, s // tm),
        in_specs=[pl.BlockSpec((1, tm, d), lambda i, j: (i, j, 0)),
                  pl.BlockSpec((1, d), lambda i, j: (0, 0)),
                  pl.BlockSpec((1, 6, d), lambda i, j: (i, 0, 0))],
        out_specs=pl.BlockSpec((1, tm, d), lambda i, j: (i, j, 0)),
        compiler_params=_cparams(("arbitrary", "arbitrary")),
        name="norm1_mod",
    )(x, g.reshape(1, d), mod)


def _mm_kernel(a_ref, w_ref, *rest, act, has_bias):
    o_ref = rest[-1]
    r = jnp.dot(a_ref[...], w_ref[...], preferred_element_type=F32)
    if has_bias:
        r = r + rest[0][...]
    if act == "sigmoid":
        r = 0.5 * jnp.tanh(0.5 * r) + 0.5
    o_ref[...] = r.astype(o_ref.dtype)


def _matmul(a, w, out_dtype, bias=None, act=None, tm=2048, tn=512, name="proj"):
    t, k = a.shape
    n = w.shape[1]
    tn = min(tn, n)
    in_specs = [pl.BlockSpec((tm, k), lambda i, j: (i, 0)),
                pl.BlockSpec((k, tn), lambda i, j: (0, j))]
    args = [a, w]
    if bias is not None:
        in_specs.append(pl.BlockSpec((1, tn), lambda i, j: (0, j)))
        args.append(bias)
    return pl.pallas_call(
        functools.partial(_mm_kernel, act=act, has_bias=bias is not None),
        out_shape=jax.ShapeDtypeStruct((t, n), out_dtype),
        grid=(t // tm, n // tn),
        in_specs=in_specs,
        out_specs=pl.BlockSpec((tm, tn), lambda i, j: (i, j)),
        compiler_params=_cparams(("arbitrary", "arbitrary")),
        name=name,
    )(*args)


LOG2E = 1.4426950408889634


def _split3(x):
    hi = x.astype(BF16)
    r1 = x - hi.astype(F32)
    mid = r1.astype(BF16)
    lo = (r1 - mid.astype(F32)).astype(BF16)
    return hi, mid, lo


def _fcum_kernel(h_ref, wf_ref, fb_ref, sel_ref, o_ref, carry_ref, *, ts):
    @pl.when(pl.program_id(0) == 0)
    def _():
        carry_ref[...] = jnp.zeros_like(carry_ref)

    ri = lax.broadcasted_iota(jnp.int32, (ts, ts), 0)
    ci = lax.broadcasted_iota(jnp.int32, (ts, ts), 1)
    tri = jnp.where(ri >= ci, 1.0, 0.0).astype(BF16)
    for bi in range(h_ref.shape[0]):
        f = jnp.dot(h_ref[bi], wf_ref[...], preferred_element_type=F32) + fb_ref[...]
        lf = -_softplus(-f)
        cum = carry_ref[bi]
        for piece in _split3(lf):
            cum = cum + jnp.dot(tri, piece, preferred_element_type=F32)
        carry_ref[bi] = cum[ts - 1:ts, :]
        out = jnp.zeros((ts, sel_ref.shape[2]), F32)
        for idx, piece in enumerate(_split3(cum * (-LOG2E))):
            out = out + jnp.dot(piece, sel_ref[idx], preferred_element_type=F32)
        o_ref[bi] = out.astype(o_ref.dtype)


def _forget_bias(h, wf, fb, ts=512):
    b, s, d = h.shape
    pairs = HEADS // 2
    wf_p = jnp.zeros((d, LANES), F32).at[:, :HEADS].set(wf).astype(BF16)
    fb_p = jnp.zeros((1, LANES), F32).at[0, :HEADS].set(fb)
    hh = jnp.arange(HEADS)
    sel = jnp.zeros((3, LANES, pairs * LANES), F32)
    for piece in range(3):
        sel = sel.at[piece, hh, (hh // 2) * LANES + (hh % 2) * 3 + piece].set(1.0)
    return pl.pallas_call(
        functools.partial(_fcum_kernel, ts=ts),
        out_shape=jax.ShapeDtypeStruct((b, s, pairs * LANES), BF16),
        grid=(s // ts,),
        in_specs=[pl.BlockSpec((b, ts, d), lambda j: (0, j, 0)),
                  pl.BlockSpec((d, LANES), lambda j: (0, 0)),
                  pl.BlockSpec((1, LANES), lambda j: (0, 0)),
                  pl.BlockSpec((3, LANES, pairs * LANES), lambda j: (0, 0, 0))],
        out_specs=pl.BlockSpec((b, ts, pairs * LANES), lambda j: (0, j, 0)),
        scratch_shapes=[pltpu.VMEM((b, 1, LANES), F32)],
        compiler_params=_cparams(("arbitrary",)),
        name="forget_bias",
    )(h, wf_p, fb_p, sel.astype(BF16))


FOX_BQ = 1024
FOX_BK = 512


def _fox_kernel(first_ref, q_ref, k_ref, v_ref, a_ref, o_ref, m_ref, acc_ref, *, bq, bk):
    i = pl.program_id(2)
    lane = lax.broadcasted_iota(jnp.int32, (bq, LANES), 1)
    first = lane < HEAD_DIM
    qs = q_ref[0].astype(F32) * (HEAD_DIM ** -0.5 * LOG2E)
    aug0 = jnp.where(lane < 3, 1.0, 0.0)
    aug1 = jnp.where((lane >= 3) & (lane < 6), 1.0, 0.0)
    q01 = jnp.concatenate([jnp.concatenate([jnp.where(first, qs, 0.0), aug0], axis=1),
                           jnp.concatenate([jnp.where(first, 0.0, qs), aug1], axis=1)],
                          axis=0).astype(BF16)
    m_ref[...] = jnp.full_like(m_ref, -jnp.inf)
    acc_ref[...] = jnp.zeros_like(acc_ref)
    lane_k = lax.broadcasted_iota(jnp.int32, (bk, LANES), 1)
    keep_first = jnp.where(lane_k < HEAD_DIM, 1.0, 0.0).astype(BF16)
    keep_second = jnp.where(lane_k < HEAD_DIM, 0.0, 1.0).astype(BF16)

    def both_heads(x, r0):
        if r0 == 0:
            return x[...]
        return jnp.concatenate([x[r0:bq], x[bq + r0:2 * bq]], axis=0)

    def logits(j, r0=0):
        start = pl.multiple_of(j * bk, bk)
        kb = jnp.concatenate([k_ref[0, pl.ds(start, bk), :], a_ref[0, pl.ds(start, bk), :]], axis=1)
        return lax.dot_general(both_heads(q01, r0), kb, (((1,), (1,)), ((), ())),
                               preferred_element_type=F32)

    def consume(j, z, r0=0):
        nr = bq - r0
        start = pl.multiple_of(j * bk, bk)
        vb = v_ref[0, pl.ds(start, bk), :]
        m_prev = both_heads(m_ref, r0)
        m_new = jnp.maximum(m_prev, jnp.max(z, axis=1, keepdims=True))
        alpha = jnp.exp2(m_prev - m_new)
        p = jnp.exp2(z - jnp.concatenate([m_new] * (bk // LANES), axis=1)).astype(BF16)
        pv = jnp.concatenate(
            [jnp.dot(p[:nr], vb * keep_first + keep_second, preferred_element_type=F32),
             jnp.dot(p[nr:], vb * keep_second + keep_first, preferred_element_type=F32)], axis=0)
        acc_new = alpha * both_heads(acc_ref, r0) + pv
        if r0 == 0:
            acc_ref[...] = acc_new
            m_ref[...] = m_new
        else:
            for half, dst in ((slice(0, nr), slice(r0, bq)), (slice(nr, 2 * nr), slice(bq + r0, 2 * bq))):
                acc_ref[dst] = acc_new[half]
                m_ref[dst] = m_new[half]

    per_q = bq // bk
    n_full = i * per_q

    j_first = first_ref[pl.program_id(0), pl.program_id(1), i]
    odd = (n_full - j_first) & 1

    @pl.when(odd == 1)
    def _():
        consume(j_first, logits(j_first))

    def body(step, carry):
        j = j_first + odd + 2 * step
        z_a = logits(j)
        z_b = logits(j + 1)
        consume(j, z_a)
        consume(j + 1, z_b)
        return carry

    lax.fori_loop(0, (n_full - j_first) // 2, body, 0)
    for d in range(per_q):
        r0 = d * bk
        row = lax.broadcasted_iota(jnp.int32, (bq - r0, bk), 0)
        col = lax.broadcasted_iota(jnp.int32, (bq - r0, bk), 1)
        keep = col <= row
        z = logits(n_full + d, r0)
        consume(n_full + d, jnp.where(jnp.concatenate([keep, keep], axis=0), z, -jnp.inf), r0)
    acc = acc_ref[...]
    o = acc / pltpu.roll(acc, HEAD_DIM, 1)
    o_ref[0] = jnp.where(first, o[:bq], o[bq:]).astype(o_ref.dtype)


def _norm_bound_kernel(x_ref, sel_ref, o_ref):
    x = x_ref[...].astype(F32)
    ssq = _dot(x * x, sel_ref[...])
    o_ref[0] = jnp.broadcast_to(jnp.max(ssq, axis=0, keepdims=True), o_ref.shape[1:])


def _fox_first_block(qkv, fbias, bq, bk):
    b, s, _ = qkv.shape
    t = b * s
    nq, nk = s // bq, s // bk
    n_heads2 = 2 * HEADS
    sel = (jnp.arange(2 * WIDTH)[:, None] // HEAD_DIM == jnp.arange(LANES)[None, :]).astype(BF16)
    bounds = pl.pallas_call(
        _norm_bound_kernel,
        out_shape=jax.ShapeDtypeStruct((t // bk, 8, LANES), F32),
        grid=(t // bk,),
        in_specs=[pl.BlockSpec((bk, 2 * WIDTH), lambda i: (i, 0)),
                  pl.BlockSpec((2 * WIDTH, LANES), lambda i: (0, 0))],
        out_specs=pl.BlockSpec((1, 8, LANES), lambda i: (i, 0, 0)),
        compiler_params=_cparams(("arbitrary",)),
        name="fox_norm_bounds",
    )(qkv.reshape(t, -1), sel)
    nrm = jnp.sqrt(bounds[:, 0, :n_heads2]).reshape(b, nk, n_heads2) * 1.01
    qn = nrm[..., :HEADS] * (HEAD_DIM ** -0.5 * LOG2E * 1.01)
    kn = nrm[..., HEADS:]
    per_q = bq // bk
    qn_i = qn.reshape(b, nq, per_q, HEADS).max(axis=2)
    kn_i = kn.reshape(b, nq, per_q, HEADS).max(axis=2)
    kn_pre = lax.cummax(kn, axis=1)
    pairs = HEADS // 2
    def bias_rows(rows):
        pieces = rows.astype(F32).reshape(b, -1, pairs, LANES)[..., :6].reshape(b, -1, pairs, 2, 3)
        return pieces.sum(-1).reshape(b, -1, HEADS)

    nb_end = bias_rows(fbias[:, bk - 1::bk])
    nb_start = bias_rows(fbias[:, ::bq])
    gap = nb_start[:, :, None, :] - nb_end[:, None, :, :]
    need = qn_i[:, :, None, :] * (kn_pre[:, None, :, :] + kn_i[:, :, None, :]) + 152.0
    skip = (gap > need).reshape(b, nq, nk, pairs, 2).all(axis=-1)
    n_full = jnp.arange(nq) * per_q
    skip = skip & (jnp.arange(nk)[None, None, :, None] < n_full[None, :, None, None])
    first = jnp.argmin(skip, axis=2)
    return first.transpose(0, 2, 1).astype(jnp.int32)


def _fox_attention(qkv, fbias, bq=FOX_BQ, bk=FOX_BK):
    b, s, _ = qkv.shape
    pairs = HEADS // 2
    cb = WIDTH // LANES
    first = _fox_first_block(qkv, fbias, bq, bk)
    grid_spec = pltpu.PrefetchScalarGridSpec(
        num_scalar_prefetch=1,
        grid=(b, pairs, s // bq),
        in_specs=[pl.BlockSpec((1, bq, LANES), lambda bi, hp, i, fr: (bi, i, hp)),
                  pl.BlockSpec((1, s, LANES), lambda bi, hp, i, fr: (bi, 0, cb + hp)),
                  pl.BlockSpec((1, s, LANES), lambda bi, hp, i, fr: (bi, 0, 2 * cb + hp)),
                  pl.BlockSpec((1, s, LANES), lambda bi, hp, i, fr: (bi, 0, hp))],
        out_specs=pl.BlockSpec((1, bq, LANES), lambda bi, hp, i, fr: (bi, i, hp)),
        scratch_shapes=[pltpu.VMEM((2 * bq, LANES), F32), pltpu.VMEM((2 * bq, LANES), F32)],
    )
    return pl.pallas_call(
        functools.partial(_fox_kernel, bq=bq, bk=bk),
        out_shape=jax.ShapeDtypeStruct((b, s, WIDTH), BF16),
        grid_spec=grid_spec,
        compiler_params=_cparams(("arbitrary", "arbitrary", "arbitrary")),
        name="fox_attention",
    )(first, qkv, qkv, qkv, fbias)


PAIR = 2 * HEAD_DIM
GROUPS = WIDTH // PAIR


def _group(x, g):
    return x[:, g * PAIR:(g + 1) * PAIR]


def _head_sum(x, bd2):
    return jnp.concatenate([_dot(_group(x, g), bd2) for g in range(GROUPS)], axis=1)


def _head_apply(mats, x, lane_first):
    rows = mats.shape[1]
    outs = []
    for g in range(GROUPS):
        res = _dot(mats[2 * g:2 * g + 2].reshape(2 * rows, rows), _group(x, g))
        outs.append(jnp.where(lane_first, res[:rows], res[rows:]))
    return jnp.concatenate(outs, axis=1)


def _rwkv_kernel(p_ref, mu_ref, w0_ref, a0_ref, kk_ref, ka_ref, rk_ref, lng_ref, lnb_ref,
                 wwa_ref, g2_ref, bd_ref, o_ref, st_ref, prev_ref):
    L = CHUNK

    @pl.when(pl.program_id(1) == 0)
    def _():
        st_ref[...] = jnp.zeros_like(st_ref)
        prev_ref[...] = jnp.zeros_like(prev_ref)

    p = p_ref[0]
    T = p.shape[0]
    rowi = lax.broadcasted_iota(jnp.int32, p.shape, 0)
    prev = jnp.where(rowi == 0, prev_ref[...], pltpu.roll(p, 1, 0))
    prev_ref[...] = p[T - 1:T, :]
    ps = p + (prev - p) * mu_ref[...]
    r = ps[:, 0:WIDTH]
    k = ps[:, WIDTH:2 * WIDTH]
    v = ps[:, 2 * WIDTH:3 * WIDTH]
    wa_in = ps[:, 3 * WIDTH:3 * WIDTH + DECAY_LORA + AAA_LORA]
    gd = ps[:, 3 * WIDTH + DECAY_LORA + AAA_LORA:]
    lane_wa = lax.broadcasted_iota(jnp.int32, wa_in.shape, 1)
    wa_act = jnp.where(lane_wa < DECAY_LORA, jnp.tanh(wa_in), wa_in)
    wa = _dot(wa_act, wwa_ref[...])
    log_w = -_softplus(-(w0_ref[...] + wa[:, :WIDTH])) - 0.5
    lw = -jnp.exp(log_w)
    a = jax.nn.sigmoid(a0_ref[...] + wa[:, WIDTH:])
    out_gate = _dot(jax.nn.sigmoid(gd), g2_ref[...])
    bd = bd_ref[...]
    kk0 = k * kk_ref[...]
    kk = kk0 * lax.rsqrt(jnp.maximum(_head_sum(kk0 * kk0, bd), 1e-24))
    k2 = k * (1.0 + (a - 1.0) * ka_ref[...])
    av = -kk
    bv = kk * a

    n_sub = T // L
    rt_i = lax.broadcasted_iota(jnp.int32, (T, T), 0)
    ct_i = lax.broadcasted_iota(jnp.int32, (T, T), 1)
    tri_tile = (rt_i >= ct_i) & (rt_i // L == ct_i // L)
    cl = _split_dot_left(jnp.where(tri_tile, 1.0, 0.0).astype(BF16), lw)
    cl_end = jnp.concatenate([jnp.broadcast_to(cl[(c + 1) * L - 1:(c + 1) * L, :], (L, WIDTH))
                              for c in range(n_sub)], axis=0)
    at_all = av * jnp.exp(cl - lw)
    rt_all = r * jnp.exp(cl)
    einv = jnp.exp(-cl)
    bt_all = bv * einv
    kt_all = k2 * einv
    edec = jnp.exp(cl_end - cl)
    b_end_all = bv * edec
    k_end_all = k2 * edec

    ri = lax.broadcasted_iota(jnp.int32, (L, L), 0)
    ci = lax.broadcasted_iota(jnp.int32, (L, L), 1)
    tri_incl = ri >= ci
    tri_strict = ri > ci
    eye = jnp.where(ri == ci, 1.0, 0.0)
    lane_first = lax.broadcasted_iota(jnp.int32, (L, PAIR), 1) < HEAD_DIM
    qr = lax.broadcasted_iota(jnp.int32, (PAIR, PAIR), 0) < HEAD_DIM
    qc = lax.broadcasted_iota(jnp.int32, (PAIR, PAIR), 1) < HEAD_DIM
    same_head = qr == qc

    def bmm(x, y):
        return lax.dot_general(x.astype(BF16), y.astype(BF16), (((2,), (1,)), ((0,), (0,))),
                               preferred_element_type=F32)

    def chunk_terms(c):
        rows = slice(c * L, (c + 1) * L)
        at, rt, bt, kt, vc = at_all[rows], rt_all[rows], bt_all[rows], kt_all[rows], v[rows]
        sb_heads, sk_heads = [], []
        for g in range(GROUPS):
            at_g, rt_g = _group(at, g), _group(rt, g)
            lhs = jnp.concatenate([jnp.where(lane_first, at_g, 0.0), jnp.where(lane_first, rt_g, 0.0),
                                   jnp.where(lane_first, 0.0, at_g), jnp.where(lane_first, 0.0, rt_g)],
                                  axis=0).astype(BF16)
            sb_g = _dot_nt(lhs, _group(bt, g))
            sk_g = _dot_nt(lhs, _group(kt, g))
            for hh in range(2):
                sb_heads.append(sb_g[hh * 2 * L:(hh + 1) * 2 * L])
                sk_heads.append(sk_g[hh * 2 * L:(hh + 1) * 2 * L])
        sb = jnp.stack(sb_heads)
        sk = jnp.stack(sk_heads)
        n_ab = jnp.where(tri_strict, sb[:, :L, :], 0.0)
        a_ak = jnp.where(tri_strict, sk[:, :L, :], 0.0)
        a_rb = jnp.where(tri_incl, sb[:, L:, :], 0.0)
        a_rk = jnp.where(tri_incl, sk[:, L:, :], 0.0)
        tinv = eye + n_ab
        pw = bmm(n_ab, n_ab)
        span = 2
        while 2 * span < L:
            both = bmm(jnp.concatenate([tinv, pw], axis=1), pw)
            tinv = tinv + both[:, :L, :]
            pw = both[:, L:, :]
            span *= 2
        tinv = tinv + bmm(tinv, pw)
        av_term = _head_apply(a_ak, vc, lane_first)
        pm = _head_apply(tinv, at, lane_first)
        qm = _head_apply(tinv, av_term, lane_first)
        rkv = _head_apply(a_rk, vc, lane_first)
        return pm, qm, rkv, a_rb

    terms = [chunk_terms(c) for c in range(n_sub)]

    y_chunks = []
    for c in range(n_sub):
        rows = slice(c * L, (c + 1) * L)
        pm, qm, rkv, a_rb = terms[c]
        rt, vc, b_end, k_end = rt_all[rows], v[rows], b_end_all[rows], k_end_all[rows]
        gam_last = jnp.exp(cl[(c + 1) * L - 1:(c + 1) * L, :])
        u_parts, ys_parts = [], []
        for g in range(GROUPS):
            pr = _dot_nt(jnp.concatenate([_group(pm, g), _group(rt, g)], axis=0), st_ref[g])
            u_parts.append(pr[:L] + _group(qm, g))
            ys_parts.append(pr[L:])
        u = jnp.concatenate(u_parts, axis=1)
        y_chunks.append(jnp.concatenate(ys_parts, axis=1) + _head_apply(a_rb, u, lane_first) + rkv)
        for g in range(GROUPS):
            upd = _dot_tn(_group(u, g), _group(b_end, g)) + _dot_tn(_group(vc, g), _group(k_end, g))
            st_ref[g] = st_ref[g] * _group(gam_last, g) + jnp.where(same_head, upd, 0.0)
    y = jnp.concatenate(y_chunks, axis=0)

    inv_n = 1.0 / HEAD_DIM
    mean = _head_sum(y, bd) * inv_n
    dlt = y - mean
    var = _head_sum(dlt * dlt, bd) * inv_n
    yn = dlt * lax.rsqrt(var + LN_X_EPS) * lng_ref[...] + lnb_ref[...]
    bonus = _head_sum(r * k2 * rk_ref[...], bd) * v
    o_ref[0] = ((yn + bonus) * out_gate).astype(o_ref.dtype)


def _split_dot_left(w_bf16, x):
    hi = x.astype(BF16)
    r1 = x - hi.astype(F32)
    mid = r1.astype(BF16)
    lo = (r1 - mid.astype(F32)).astype(BF16)
    return (jnp.dot(w_bf16, hi, preferred_element_type=F32)
            + jnp.dot(w_bf16, mid, preferred_element_type=F32)
            + jnp.dot(w_bf16, lo, preferred_element_type=F32))


def _rwkv_branch(p_rw, mu, w0, w2, a0, a2, g2, k_k, k_a, r_k, ln_g, ln_b):
    b, s, sw = p_rw.shape
    row = lambda t: t.reshape(1, -1).astype(F32)
    wwa = jnp.zeros((DECAY_LORA + AAA_LORA, 2 * WIDTH), F32)
    wwa = wwa.at[:DECAY_LORA, :WIDTH].set(w2).at[DECAY_LORA:, WIDTH:].set(a2).astype(BF16)
    hid = jnp.arange(PAIR) // HEAD_DIM
    bd = (hid[:, None] == hid[None, :]).astype(BF16)
    const = lambda shape: pl.BlockSpec(shape, lambda i, j: (0,) * len(shape))
    return pl.pallas_call(
        _rwkv_kernel,
        out_shape=jax.ShapeDtypeStruct((b, s, WIDTH), BF16),
        grid=(b, s // RWKV_TILE),
        in_specs=[pl.BlockSpec((1, RWKV_TILE, sw), lambda i, j: (i, j, 0)),
                  const((1, sw)), const((1, WIDTH)), const((1, WIDTH)), const((1, WIDTH)),
                  const((1, WIDTH)), const((1, WIDTH)), const((1, WIDTH)), const((1, WIDTH)),
                  const((DECAY_LORA + AAA_LORA, 2 * WIDTH)), const((GATE_LORA, WIDTH)),
                  const((PAIR, PAIR))],
        out_specs=pl.BlockSpec((1, RWKV_TILE, WIDTH), lambda i, j: (i, j, 0)),
        scratch_shapes=[pltpu.VMEM((GROUPS, PAIR, PAIR), F32), pltpu.VMEM((1, sw), F32)],
        compiler_params=_cparams(("arbitrary", "arbitrary")),
        name="rwkv7_scan",
    )(p_rw, row(mu), row(w0), row(a0), row(k_k), row(k_a), row(r_k), row(ln_g), row(ln_b),
      wwa, g2.astype(BF16), bd)


def _out_kernel(of_ref, orw_ref, gate_ref, x_ref, mod_ref, wof_ref, wor_ref, wo_ref, n2g_ref,
                wr_ref, br_ref, x1_ref, h2_ref, route_ref, counts_ref, cnt_ref):
    d = x_ref.shape[-1]
    gate = gate_ref[...].astype(F32)
    merged = (gate[:, :d] * jnp.dot(of_ref[...], wof_ref[...], preferred_element_type=F32)
              + gate[:, d:] * jnp.dot(orw_ref[...], wor_ref[...], preferred_element_type=F32))
    gate1 = mod_ref[0, 2:3, :]
    shift2 = mod_ref[0, 3:4, :]
    scale2 = mod_ref[0, 4:5, :]
    x1 = x_ref[...] + gate1 * jnp.dot(merged.astype(BF16), wo_ref[...], preferred_element_type=F32)
    x1_ref[...] = x1
    ms = jnp.mean(x1 * x1, axis=-1, keepdims=True)
    h2 = x1 * lax.rsqrt(ms + NORM_EPS) * n2g_ref[...] * (1.0 + scale2) + shift2
    h2_ref[...] = _pack_pairs(h2)

    h2_hi = h2.astype(BF16)
    h2_lo = (h2 - h2_hi.astype(F32)).astype(BF16)
    logits = (jnp.dot(h2_hi, wr_ref[0], preferred_element_type=F32)
              + jnp.dot(h2_lo, wr_ref[0], preferred_element_type=F32)
              + jnp.dot(h2_hi, wr_ref[1], preferred_element_type=F32)) + br_ref[...]
    lane = lax.broadcasted_iota(jnp.int32, logits.shape, 1)
    neg = -jnp.inf
    big = jnp.int32(LANES)
    gl = jnp.where(lane < N_GROUPS, logits, neg)
    gmax = jnp.max(gl, axis=1, keepdims=True)
    gidx = jnp.min(jnp.where(gl == gmax, lane, big), axis=1, keepdims=True)
    g_p = 1.0 / jnp.sum(jnp.exp(gl - gmax), axis=1, keepdims=True)
    e_lane = lane - N_GROUPS
    in_grp = (e_lane >= 0) & (e_lane < N_EXPERTS) & ((e_lane // EXPERTS_PER_GROUP) == gidx)
    sel = jnp.where(in_grp, logits, neg)
    m1 = jnp.max(sel, axis=1, keepdims=True)
    i1 = jnp.min(jnp.where(sel == m1, lane, big), axis=1, keepdims=True)
    sel2 = jnp.where(lane == i1, neg, sel)
    m2 = jnp.max(sel2, axis=1, keepdims=True)
    i2 = jnp.min(jnp.where(sel2 == m2, lane, big), axis=1, keepdims=True)
    e21 = jnp.exp(m2 - m1)
    w_first = g_p / (1.0 + e21)
    w_second = g_p * e21 / (1.0 + e21)
    @pl.when(pl.program_id(0) == 0)
    def _():
        cnt_ref[...] = jnp.zeros_like(cnt_ref)

    tm = logits.shape[0]
    oh1 = lane == i1
    oh2 = lane == i2
    both = jnp.where(oh1 | oh2, 1.0, 0.0)
    before = (lax.broadcasted_iota(jnp.int32, (tm, tm), 0)
              > lax.broadcasted_iota(jnp.int32, (tm, tm), 1))
    seen = jnp.dot(jnp.where(before, 1.0, 0.0).astype(BF16), both.astype(BF16),
                   preferred_element_type=F32) + cnt_ref[...]
    rank1 = jnp.sum(jnp.where(oh1, seen, 0.0), axis=1, keepdims=True)
    rank2 = jnp.sum(jnp.where(oh2, seen, 0.0), axis=1, keepdims=True)
    cnt_ref[...] = cnt_ref[...] + jnp.sum(both, axis=0, keepdims=True)
    counts_ref[...] = jnp.broadcast_to(cnt_ref[...], counts_ref.shape)

    route = jnp.where(lane == 0, (i1 - N_GROUPS).astype(F32),
                      jnp.where(lane == 1, (i2 - N_GROUPS).astype(F32),
                                jnp.where(lane == 2, w_first,
                                          jnp.where(lane == 3, w_second,
                                                    jnp.where(lane == 4, rank1,
                                                              jnp.where(lane == 5, rank2, 0.0))))))
    route_ref[...] = route


def _merge_out_router(o_fox, o_rw, gate, x, mod, wof, wor, wo, n2g, wr, br, tm=512):
    b, s, d = x.shape
    t = b * s
    spb = s // tm
    rowspec = lambda w: pl.BlockSpec((tm, w), lambda i: (i, 0))
    const = lambda shape: pl.BlockSpec(shape, lambda i: (0,) * len(shape))
    return pl.pallas_call(
        _out_kernel,
        out_shape=(jax.ShapeDtypeStruct((t, d), F32), jax.ShapeDtypeStruct((t, d // 2), jnp.uint32),
                   jax.ShapeDtypeStruct((t, LANES), F32), jax.ShapeDtypeStruct((8, LANES), F32)),
        grid=(t // tm,),
        in_specs=[rowspec(WIDTH), rowspec(WIDTH), rowspec(2 * d), rowspec(d),
                  pl.BlockSpec((1, 6, d), lambda i: (i // spb, 0, 0)),
                  const((WIDTH, d)), const((WIDTH, d)), const((d, d)), const((1, d)),
                  const((2, d, LANES)), const((1, LANES))],
        out_specs=(rowspec(d), rowspec(d // 2), rowspec(LANES), const((8, LANES))),
        scratch_shapes=[pltpu.VMEM((1, LANES), F32)],
        compiler_params=_cparams(("arbitrary",)),
        name="merge_out_router",
    )(o_fox, o_rw, gate, x.reshape(t, d), mod, wof, wor, wo, n2g.reshape(1, d), wr, br)


def _dispatch_kernel(zstart_ref, dest_ref, h_ref, xs_ref, zero_ref, sem, *, tm):
    @pl.when(pl.program_id(0) == 0)
    def _():
        zero_ref[...] = jnp.zeros_like(zero_ref)
        for e in range(N_EXPERTS):
            zrow = pl.multiple_of(zstart_ref[e], MOE_ROWS)
            pltpu.make_async_copy(zero_ref, xs_ref.at[pl.ds(zrow, MOE_ROWS)], sem).start()
        for e in range(N_EXPERTS):
            pltpu.make_async_copy(zero_ref, xs_ref.at[pl.ds(0, MOE_ROWS)], sem).wait()
        for phase in ("start", "wait"):
            for e in range(N_EXPERTS):
                trow = pl.multiple_of(zstart_ref[N_EXPERTS] + e * MOE_ROWS, MOE_ROWS)

                @pl.when(trow < xs_ref.shape[0])
                def _():
                    tail = pltpu.make_async_copy(zero_ref, xs_ref.at[pl.ds(trow, MOE_ROWS)], sem)
                    tail.start() if phase == "start" else tail.wait()

    def issue(r, carry):
        for kk in range(TOP_K):
            dst = dest_ref[0, 0, TOP_K * r + kk]
            pltpu.make_async_copy(h_ref.at[pl.ds(r, 1)], xs_ref.at[pl.ds(dst, 1)], sem).start(priority=kk)
        return carry

    lax.fori_loop(0, tm, issue, 0, unroll=8)
    for kk in range(TOP_K):
        pltpu.make_async_copy(h_ref, xs_ref.at[pl.ds(0, tm)], sem).wait()


def _moe_dispatch(h2, dest3, zstart, rows, tm=256):
    t, d = h2.shape
    grid_spec = pltpu.PrefetchScalarGridSpec(
        num_scalar_prefetch=1,
        grid=(t // tm,),
        in_specs=[pl.BlockSpec((1, 1, TOP_K * tm), lambda i, zs: (i, 0, 0), memory_space=pltpu.SMEM),
                  pl.BlockSpec((tm, d), lambda i, zs: (i, 0))],
        out_specs=pl.BlockSpec(memory_space=pl.ANY),
        scratch_shapes=[pltpu.VMEM((MOE_ROWS, d), h2.dtype), pltpu.SemaphoreType.DMA(())],
    )
    return pl.pallas_call(
        functools.partial(_dispatch_kernel, tm=tm),
        out_shape=jax.ShapeDtypeStruct((rows, d), h2.dtype),
        grid_spec=grid_spec,
        compiler_params=_cparams(("arbitrary",)),
        name="moe_dispatch",
    )(zstart, dest3, h2)


def _expert_kernel(blk_e_ref, nused_ref, xs_ref, w1_ref, w3_ref, w2_ref, o_ref, w1b_ref, w3b_ref, w2b_ref):
    i = pl.program_id(0)
    live = i * MOE_ROWS < nused_ref[0]
    new_expert = jnp.logical_or(i == 0, blk_e_ref[i] != blk_e_ref[jnp.maximum(i - 1, 0)])

    @pl.when(new_expert)
    def _():
        w1b_ref[...] = w1_ref[0].astype(BF16)
        w3b_ref[...] = w3_ref[0].astype(BF16)
        w2b_ref[...] = w2_ref[0].astype(BF16)

    @pl.when(live)
    def _():
        xb = _unpack_pairs(xs_ref[...]).astype(BF16)
        h1 = jnp.dot(xb, w1b_ref[...], preferred_element_type=F32)
        h3 = jnp.dot(xb, w3b_ref[...], preferred_element_type=F32)
        hh = (h1 * jax.nn.sigmoid(h1)) * h3
        o_ref[...] = _pack_pairs(jnp.dot(hh.astype(BF16), w2b_ref[...], preferred_element_type=F32))

    @pl.when(jnp.logical_not(live))
    def _():
        o_ref[...] = jnp.zeros_like(o_ref)


def _moe_experts(xs, blk_e, nused, w1, w3, w2):
    rows, dp = xs.shape
    _, d, de = w1.shape
    grid_spec = pltpu.PrefetchScalarGridSpec(
        num_scalar_prefetch=2,
        grid=(rows // MOE_ROWS,),
        in_specs=[pl.BlockSpec((MOE_ROWS, dp), lambda i, be, nu: (i, 0)),
                  pl.BlockSpec((1, d, de), lambda i, be, nu: (be[i], 0, 0)),
                  pl.BlockSpec((1, d, de), lambda i, be, nu: (be[i], 0, 0)),
                  pl.BlockSpec((1, de, d), lambda i, be, nu: (be[i], 0, 0))],
        out_specs=pl.BlockSpec((MOE_ROWS, dp), lambda i, be, nu: (i, 0)),
        scratch_shapes=[pltpu.VMEM((d, de), BF16), pltpu.VMEM((d, de), BF16), pltpu.VMEM((de, d), BF16)],
    )
    return pl.pallas_call(
        _expert_kernel,
        out_shape=jax.ShapeDtypeStruct((rows, dp), xs.dtype),
        grid_spec=grid_spec,
        compiler_params=_cparams(("arbitrary",)),
        name="moe_experts",
    )(blk_e, nused, xs, w1, w3, w2)


SC_CORES = 2
SC_SUBCORES = 16
SC_CHUNK = 128


def _sc_gather_rows(table, idx):
    n_idx = idx.shape[0]
    width = table.shape[1]
    per_worker = n_idx // (SC_CORES * SC_SUBCORES)
    n_chunks = per_worker // SC_CHUNK
    assert n_chunks * SC_CHUNK * SC_CORES * SC_SUBCORES == n_idx
    mesh = plsc.VectorSubcoreMesh(core_axis_name="c", subcore_axis_name="s")

    @functools.partial(
        pl.kernel, mesh=mesh,
        out_type=jax.ShapeDtypeStruct((n_idx, width), table.dtype),
        scratch_types=[pltpu.VMEM((SC_CHUNK,), jnp.int32), pltpu.VMEM((SC_CHUNK, width), table.dtype),
                       pltpu.SemaphoreType.DMA])
    def gather(table_hbm, idx_hbm, out_hbm, idx_v, rows_v, sem):
        worker = lax.axis_index("s") * SC_CORES + lax.axis_index("c")

        @pl.loop(0, n_chunks)
        def _(ci):
            off = pl.multiple_of(worker * per_worker + ci * SC_CHUNK, SC_CHUNK)
            pltpu.sync_copy(idx_hbm.at[pl.ds(off, SC_CHUNK)], idx_v)
            pltpu.async_copy(table_hbm.at[idx_v], rows_v, sem).wait()
            pltpu.sync_copy(rows_v, out_hbm.at[pl.ds(off, SC_CHUNK)])

    return gather(table, idx)


def _final_kernel(route_ref, x1_ref, mod_ref, fg_ref, y0_ref, y1_ref, o_ref):
    route = route_ref[...]
    y = route[:, 2:3] * _unpack_pairs(y0_ref[...]) + route[:, 3:4] * _unpack_pairs(y1_ref[...])
    gate2 = mod_ref[0, 5:6, :]
    x2 = x1_ref[...] + gate2 * y
    ms = jnp.mean(x2 * x2, axis=-1, keepdims=True)
    o_ref[...] = x2 * lax.rsqrt(ms + NORM_EPS) * fg_ref[...]


def _moe_combine_final(dest_slots, route, x1, mod, final_g, ys, s, tm=512):
    t, d = x1.shape
    spb = s // tm
    picked = _sc_gather_rows(ys, dest_slots)
    tiles = t // tm
    return pl.pallas_call(
        _final_kernel,
        out_shape=jax.ShapeDtypeStruct((t, d), F32),
        grid=(tiles,),
        in_specs=[pl.BlockSpec((tm, LANES), lambda i: (i, 0)),
                  pl.BlockSpec((tm, d), lambda i: (i, 0)),
                  pl.BlockSpec((1, 6, d), lambda i: (i // spb, 0, 0)),
                  pl.BlockSpec((1, d), lambda i: (0, 0)),
                  pl.BlockSpec((tm, picked.shape[1]), lambda i: (i, 0)),
                  pl.BlockSpec((tm, picked.shape[1]), lambda i: (tiles + i, 0))],
        out_specs=pl.BlockSpec((tm, d), lambda i: (i, 0)),
        compiler_params=_cparams(("arbitrary",)),
        name="moe_combine_final",
    )(route, x1, mod, final_g.reshape(1, d), picked, picked)


def _moe_plan(route, counts, tm):
    t = route.shape[0]
    m = t * TOP_K
    flat_e = route[:, :TOP_K].astype(jnp.int32).reshape(m)
    rank = route[:, 4:4 + TOP_K].astype(jnp.int32).reshape(m)
    counts = counts[0, N_GROUPS:N_GROUPS + N_EXPERTS].astype(jnp.int32)
    padded = (counts + MOE_ROWS - 1) // MOE_ROWS * MOE_ROWS
    pad_end = jnp.cumsum(padded)
    pad_start = pad_end - padded
    experts = jnp.arange(N_EXPERTS, dtype=jnp.int32)
    start_of = jnp.sum(jnp.where(flat_e[:, None] == experts[None, :], pad_start[None, :], 0), axis=1)
    dest = (start_of + rank).astype(jnp.int32)
    n_blocks = m // MOE_ROWS + N_EXPERTS
    blk_start = jnp.arange(n_blocks, dtype=jnp.int32) * MOE_ROWS
    blk_e = jnp.minimum(jnp.sum(pad_end[None, :] <= blk_start[:, None], axis=1), N_EXPERTS - 1).astype(jnp.int32)
    nused = pad_end[-1:].astype(jnp.int32)
    zstart = jnp.concatenate([jnp.maximum(pad_end - MOE_ROWS, 0), pad_end[-1:]]).astype(jnp.int32)
    dest_slots = dest.reshape(t, TOP_K).T.reshape(m)
    return dest.reshape(t // tm, 1, TOP_K * tm), dest_slots, blk_e, nused, zstart, n_blocks * MOE_ROWS


def kernel(x, c, ada_w, ada_b, norm1_g, w_in, fox_forget_b, shift_mu, rwkv_w0, rwkv_w2, rwkv_a0, rwkv_a2, rwkv_g2, rwkv_k_k, rwkv_k_a, rwkv_r_k, ln_x_g, ln_x_b, w_out_fox, w_out_rwkv, w_o, norm2_g, router_group_w, router_group_b, router_expert_w, router_expert_b, exp_w1, exp_w3, exp_w2, final_g):
    b, s, d = x.shape
    t = b * s
    assert ada_w.shape[0] == 1, "the final norm is fused into the last layer's combine; one layer is laid out"
    for l in range(1):
        mod = _adaln_mod(c, ada_w[l], ada_b[l])

        h = _norm_mod(x, norm1_g[l], mod, shift_idx=0, scale_idx=1)
        h2d = h.reshape(t, d)
        w = w_in[l]
        o_f = 3 * WIDTH
        o_rw = o_f + HEADS
        o_g = o_rw + SHIFT_WIDTH
        perm = jnp.argsort(fox_forget_b[l])
        by_head = lambda m: m.reshape(d, HEADS, HEAD_DIM)[:, perm]
        wq = by_head(w[:, :WIDTH]).reshape(d, WIDTH)
        wk = by_head(w[:, WIDTH:2 * WIDTH]).reshape(d, WIDTH)
        wv = by_head(w[:, 2 * WIDTH:o_f]).reshape(d, WIDTH)
        qkv = _matmul(h2d, jnp.concatenate([wq, wk, wv], axis=1).astype(BF16), BF16, name="proj_qkv")
        p_rw = _matmul(h2d, w[:, o_rw:o_g].astype(BF16), F32, tn=896, name="proj_rwkv")
        gate = _matmul(h2d, w[:, o_g:].astype(BF16), BF16, act="sigmoid", name="proj_gate")
        fbias = _forget_bias(h, w[:, o_f:o_rw][:, perm], fox_forget_b[l][perm])
        o_fox = _fox_attention(qkv.reshape(b, s, 3 * WIDTH), fbias)
        w_of = w_out_fox[l].reshape(HEADS, HEAD_DIM, d)[perm].reshape(WIDTH, d)
        o_rwkv = _rwkv_branch(p_rw.reshape(b, s, SHIFT_WIDTH), shift_mu[l], rwkv_w0[l], rwkv_w2[l],
                              rwkv_a0[l], rwkv_a2[l], rwkv_g2[l], rwkv_k_k[l], rwkv_k_a[l],
                              rwkv_r_k[l], ln_x_g[l], ln_x_b[l])

        wr = jnp.zeros((d, LANES), F32)
        wr = wr.at[:, :N_GROUPS].set(router_group_w[l]).at[:, N_GROUPS:N_GROUPS + N_EXPERTS].set(router_expert_w[l])
        br = jnp.zeros((1, LANES), F32)
        br = br.at[0, :N_GROUPS].set(router_group_b[l]).at[0, N_GROUPS:N_GROUPS + N_EXPERTS].set(router_expert_b[l])
        wr_hi = wr.astype(BF16)
        wr_lo = (wr - wr_hi.astype(F32)).astype(BF16)
        x1, h2, route, counts = _merge_out_router(
            o_fox.reshape(t, WIDTH), o_rwkv.reshape(t, WIDTH), gate, x, mod,
            w_of.astype(BF16), w_out_rwkv[l].astype(BF16), w_o[l].astype(BF16), norm2_g[l],
            jnp.stack([wr_hi, wr_lo]), br)

        tm_moe = 256
        dest3, dest_slots, blk_e, nused, zstart, rows = _moe_plan(route, counts, tm_moe)
        xs = _moe_dispatch(h2, dest3, zstart, rows, tm=tm_moe)
        ys = _moe_experts(xs, blk_e, nused, exp_w1[l], exp_w3[l], exp_w2[l])
        out = _moe_combine_final(dest_slots, route, x1, mod, final_g, ys, s)
    return out.reshape(b, s, d)
```

```python
import functools

import jax
import jax.numpy as jnp
from jax import lax
from jax.experimental import pallas as pl
from jax.experimental.pallas import tpu as pltpu
from jax.experimental.pallas import tpu_sc as plsc

F32 = jnp.float32
BF16 = jnp.bfloat16
HIGHEST = lax.Precision.HIGHEST

HEADS = 8
HEAD_DIM = 64
WIDTH = HEADS * HEAD_DIM
DECAY_LORA = 64
AAA_LORA = 64
GATE_LORA = 128
SHIFT_WIDTH = 3 * WIDTH + DECAY_LORA + AAA_LORA + GATE_LORA
LN_X_EPS = 64e-5
NORM_EPS = 1e-6
N_GROUPS = 4
EXPERTS_PER_GROUP = 8
N_EXPERTS = N_GROUPS * EXPERTS_PER_GROUP
TOP_K = 2

LANES = 128
CHUNK = 64
RWKV_TILE = 256
MOE_ROWS = 512
VMEM_LIMIT = 48 * 1024 * 1024


def _cparams(sem):
    return pltpu.CompilerParams(dimension_semantics=sem, vmem_limit_bytes=VMEM_LIMIT)


def _dot(a, b):
    return jnp.dot(a.astype(BF16), b.astype(BF16), preferred_element_type=F32)


def _dot_nt(a, b):
    return lax.dot_general(a.astype(BF16), b.astype(BF16), (((1,), (1,)), ((), ())),
                           preferred_element_type=F32)


def _dot_tn(a, b):
    return lax.dot_general(a.astype(BF16), b.astype(BF16), (((0,), (0,)), ((), ())),
                           preferred_element_type=F32)


def _split_dot(x, w_bf16):
    hi = x.astype(BF16)
    lo = (x - hi.astype(F32)).astype(BF16)
    return (jnp.dot(hi, w_bf16, preferred_element_type=F32)
            + jnp.dot(lo, w_bf16, preferred_element_type=F32))


def _pack_pairs(x):
    n = x.shape[1] // 2
    lo = lax.bitcast_convert_type(x[:, :n].astype(BF16).astype(F32), jnp.uint32) >> 16
    hi = lax.bitcast_convert_type(x[:, n:].astype(BF16).astype(F32), jnp.uint32) & jnp.uint32(0xFFFF0000)
    return lo | hi


def _unpack_pairs(u):
    lo = lax.bitcast_convert_type(u << 16, F32)
    hi = lax.bitcast_convert_type(u & jnp.uint32(0xFFFF0000), F32)
    return jnp.concatenate([lo, hi], axis=1)


def _softplus(x):
    return jnp.maximum(x, 0.0) + jnp.log1p(jnp.exp(-jnp.abs(x)))


def _mod_kernel(c_ref, w_ref, b_ref, o_ref):
    c = c_ref[...]
    sc = c * jax.nn.sigmoid(c)
    o_ref[...] = jnp.dot(sc, w_ref[...], precision=HIGHEST, preferred_element_type=F32) + b_ref[...]


def _adaln_mod(c, ada_w, ada_b):
    b, d = c.shape
    n = ada_w.shape[1]
    rows = 8
    cp = jnp.zeros((rows, d), F32).at[:b].set(c)
    tn = 1024
    out = pl.pallas_call(
        _mod_kernel,
        out_shape=jax.ShapeDtypeStruct((rows, n), F32),
        grid=(n // tn,),
        in_specs=[pl.BlockSpec((rows, d), lambda j: (0, 0)),
                  pl.BlockSpec((d, tn), lambda j: (0, j)),
                  pl.BlockSpec((1, tn), lambda j: (0, j))],
        out_specs=pl.BlockSpec((rows, tn), lambda j: (0, j)),
        compiler_params=_cparams(("arbitrary",)),
        name="adaln_mod",
    )(cp, ada_w, ada_b.reshape(1, n))
    return out[:b].reshape(b, 6, d)


def _norm_mod_kernel(x_ref, g_ref, mod_ref, o_ref, *, shift_idx, scale_idx):
    x = x_ref[0]
    ms = jnp.mean(x * x, axis=-1, keepdims=True)
    y = x * lax.rsqrt(ms + NORM_EPS) * g_ref[...]
    scale = mod_ref[0, scale_idx:scale_idx + 1, :]
    shift = mod_ref[0, shift_idx:shift_idx + 1, :]
    o_ref[0] = (y * (1.0 + scale) + shift).astype(o_ref.dtype)


def _norm_mod(x, g, mod, shift_idx, scale_idx, tm=1024):
    b, s, d = x.shape
    return pl.pallas_call(
        functools.partial(_norm_mod_kernel, shift_idx=shift_idx, scale_idx=scale_idx),
        out_shape=jax.ShapeDtypeStruct((b, s, d), BF16),
        grid=(b, s // tm),
        in_specs=[pl.BlockSpec((1, tm, d), lambda i, j: (i, j, 0)),
                  pl.BlockSpec((1, d), lambda i, j: (0, 0)),
                  pl.BlockSpec((1, 6, d), lambda i, j: (i, 0, 0))],
        out_specs=pl.BlockSpec((1, tm, d), lambda i, j: (i, j, 0)),
        compiler_params=_cparams(("arbitrary", "arbitrary")),
        name="norm1_mod",
    )(x, g.reshape(1, d), mod)


def _mm_kernel(a_ref, w_ref, *rest, act, has_bias):
    o_ref = rest[-1]
    r = jnp.dot(a_ref[...], w_ref[...], preferred_element_type=F32)
    if has_bias:
        r = r + rest[0][...]
    if act == "sigmoid":
        r = 0.5 * jnp.tanh(0.5 * r) + 0.5
    o_ref[...] = r.astype(o_ref.dtype)


def _matmul(a, w, out_dtype, bias=None, act=None, tm=2048, tn=512, name="proj"):
    t, k = a.shape
    n = w.shape[1]
    tn = min(tn, n)
    in_specs = [pl.BlockSpec((tm, k), lambda i, j: (i, 0)),
                pl.BlockSpec((k, tn), lambda i, j: (0, j))]
    args = [a, w]
    if bias is not None:
        in_specs.append(pl.BlockSpec((1, tn), lambda i, j: (0, j)))
        args.append(bias)
    return pl.pallas_call(
        functools.partial(_mm_kernel, act=act, has_bias=bias is not None),
        out_shape=jax.ShapeDtypeStruct((t, n), out_dtype),
        grid=(t // tm, n // tn),
        in_specs=in_specs,
        out_specs=pl.BlockSpec((tm, tn), lambda i, j: (i, j)),
        compiler_params=_cparams(("arbitrary", "arbitrary")),
        name=name,
    )(*args)


LOG2E = 1.4426950408889634


def _split3(x):
    hi = x.astype(BF16)
    r1 = x - hi.astype(F32)
    mid = r1.astype(BF16)
    lo = (r1 - mid.astype(F32)).astype(BF16)
    return hi, mid, lo


def _fcum_kernel(h_ref, wf_ref, fb_ref, sel_ref, o_ref, carry_ref, *, ts):
    @pl.when(pl.program_id(0) == 0)
    def _():
        carry_ref[...] = jnp.zeros_like(carry_ref)

    ri = lax.broadcasted_iota(jnp.int32, (ts, ts), 0)
    ci = lax.broadcasted_iota(jnp.int32, (ts, ts), 1)
    tri = jnp.where(ri >= ci, 1.0, 0.0).astype(BF16)
    for bi in range(h_ref.shape[0]):
        f = jnp.dot(h_ref[bi], wf_ref[...], preferred_element_type=F32) + fb_ref[...]
        lf = -_softplus(-f)
        cum = carry_ref[bi]
        for piece in _split3(lf):
            cum = cum + jnp.dot(tri, piece, preferred_element_type=F32)
        carry_ref[bi] = cum[ts - 1:ts, :]
        out = jnp.zeros((ts, sel_ref.shape[2]), F32)
        for idx, piece in enumerate(_split3(cum * (-LOG2E))):
            out = out + jnp.dot(piece, sel_ref[idx], preferred_element_type=F32)
        o_ref[bi] = out.astype(o_ref.dtype)


def _forget_bias(h, wf, fb, ts=512):
    b, s, d = h.shape
    pairs = HEADS // 2
    wf_p = jnp.zeros((d, LANES), F32).at[:, :HEADS].set(wf).astype(BF16)
    fb_p = jnp.zeros((1, LANES), F32).at[0, :HEADS].set(fb)
    hh = jnp.arange(HEADS)
    sel = jnp.zeros((3, LANES, pairs * LANES), F32)
    for piece in range(3):
        sel = sel.at[piece, hh, (hh // 2) * LANES + (hh % 2) * 3 + piece].set(1.0)
    return pl.pallas_call(
        functools.partial(_fcum_kernel, ts=ts),
        out_shape=jax.ShapeDtypeStruct((b, s, pairs * LANES), BF16),
        grid=(s // ts,),
        in_specs=[pl.BlockSpec((b, ts, d), lambda j: (0, j, 0)),
                  pl.BlockSpec((d, LANES), lambda j: (0, 0)),
                  pl.BlockSpec((1, LANES), lambda j: (0, 0)),
                  pl.BlockSpec((3, LANES, pairs * LANES), lambda j: (0, 0, 0))],
        out_specs=pl.BlockSpec((b, ts, pairs * LANES), lambda j: (0, j, 0)),
        scratch_shapes=[pltpu.VMEM((b, 1, LANES), F32)],
        compiler_params=_cparams(("arbitrary",)),
        name="forget_bias",
    )(h, wf_p, fb_p, sel.astype(BF16))


FOX_BQ = 1024
FOX_BK = 512


def _fox_kernel(first_ref, q_ref, k_ref, v_ref, a_ref, o_ref, m_ref, acc_ref, *, bq, bk):
    i = pl.program_id(2)
    lane = lax.broadcasted_iota(jnp.int32, (bq, LANES), 1)
    first = lane < HEAD_DIM
    qs = q_ref[0].astype(F32) * (HEAD_DIM ** -0.5 * LOG2E)
    aug0 = jnp.where(lane < 3, 1.0, 0.0)
    aug1 = jnp.where((lane >= 3) & (lane < 6), 1.0, 0.0)
    q01 = jnp.concatenate([jnp.concatenate([jnp.where(first, qs, 0.0), aug0], axis=1),
                           jnp.concatenate([jnp.where(first, 0.0, qs), aug1], axis=1)],
                          axis=0).astype(BF16)
    m_ref[...] = jnp.full_like(m_ref, -jnp.inf)
    acc_ref[...] = jnp.zeros_like(acc_ref)
    lane_k = lax.broadcasted_iota(jnp.int32, (bk, LANES), 1)
    keep_first = jnp.where(lane_k < HEAD_DIM, 1.0, 0.0).astype(BF16)
    keep_second = jnp.where(lane_k < HEAD_DIM, 0.0, 1.0).astype(BF16)

    def both_heads(x, r0):
        if r0 == 0:
            return x[...]
        return jnp.concatenate([x[r0:bq], x[bq + r0:2 * bq]], axis=0)

    def logits(j, r0=0):
        start = pl.multiple_of(j * bk, bk)
        kb = jnp.concatenate([k_ref[0, pl.ds(start, bk), :], a_ref[0, pl.ds(start, bk), :]], axis=1)
        return lax.dot_general(both_heads(q01, r0), kb, (((1,), (1,)), ((), ())),
                               preferred_element_type=F32)

    def consume(j, z, r0=0):
        nr = bq - r0
        start = pl.multiple_of(j * bk, bk)
        vb = v_ref[0, pl.ds(start, bk), :]
        m_prev = both_heads(m_ref, r0)
        m_new = jnp.maximum(m_prev, jnp.max(z, axis=1, keepdims=True))
        alpha = jnp.exp2(m_prev - m_new)
        p = jnp.exp2(z - jnp.concatenate([m_new] * (bk // LANES), axis=1)).astype(BF16)
        pv = jnp.concatenate(
            [jnp.dot(p[:nr], vb * keep_first + keep_second, preferred_element_type=F32),
             jnp.dot(p[nr:], vb * keep_second + keep_first, preferred_element_type=F32)], axis=0)
        acc_new = alpha * both_heads(acc_ref, r0) + pv
        if r0 == 0:
            acc_ref[...] = acc_new
            m_ref[...] = m_new
        else:
            for half, dst in ((slice(0, nr), slice(r0, bq)), (slice(nr, 2 * nr), slice(bq + r0, 2 * bq))):
                acc_ref[dst] = acc_new[half]
                m_ref[dst] = m_new[half]

    per_q = bq // bk
    n_full = i * per_q

    j_first = first_ref[pl.program_id(0), pl.program_id(1), i]
    odd = (n_full - j_first) & 1

    @pl.when(odd == 1)
    def _():
        consume(j_first, logits(j_first))

    def body(step, carry):
        j = j_first + odd + 2 * step
        z_a = logits(j)
        z_b = logits(j + 1)
        consume(j, z_a)
        consume(j + 1, z_b)
        return carry

    lax.fori_loop(0, (n_full - j_first) // 2, body, 0)
    for d in range(per_q):
        r0 = d * bk
        row = lax.broadcasted_iota(jnp.int32, (bq - r0, bk), 0)
        col = lax.broadcasted_iota(jnp.int32, (bq - r0, bk), 1)
        keep = col <= row
        z = logits(n_full + d, r0)
        consume(n_full + d, jnp.where(jnp.concatenate([keep, keep], axis=0), z, -jnp.inf), r0)
    acc = acc_ref[...]
    o = acc / pltpu.roll(acc, HEAD_DIM, 1)
    o_ref[0] = jnp.where(first, o[:bq], o[bq:]).astype(o_ref.dtype)


def _norm_bound_kernel(x_ref, sel_ref, o_ref):
    x = x_ref[...].astype(F32)
    ssq = _dot(x * x, sel_ref[...])
    o_ref[0] = jnp.broadcast_to(jnp.max(ssq, axis=0, keepdims=True), o_ref.shape[1:])


def _fox_first_block(qkv, fbias, bq, bk):
    b, s, _ = qkv.shape
    t = b * s
    nq, nk = s // bq, s // bk
    n_heads2 = 2 * HEADS
    sel = (jnp.arange(2 * WIDTH)[:, None] // HEAD_DIM == jnp.arange(LANES)[None, :]).astype(BF16)
    bounds = pl.pallas_call(
        _norm_bound_kernel,
        out_shape=jax.ShapeDtypeStruct((t // bk, 8, LANES), F32),
        grid=(t // bk,),
        in_specs=[pl.BlockSpec((bk, 2 * WIDTH), lambda i: (i, 0)),
                  pl.BlockSpec((2 * WIDTH, LANES), lambda i: (0, 0))],
        out_specs=pl.BlockSpec((1, 8, LANES), lambda i: (i, 0, 0)),
        compiler_params=_cparams(("arbitrary",)),
        name="fox_norm_bounds",
    )(qkv.reshape(t, -1), sel)
    nrm = jnp.sqrt(bounds[:, 0, :n_heads2]).reshape(b, nk, n_heads2) * 1.01
    qn = nrm[..., :HEADS] * (HEAD_DIM ** -0.5 * LOG2E * 1.01)
    kn = nrm[..., HEADS:]
    per_q = bq // bk
    qn_i = qn.reshape(b, nq, per_q, HEADS).max(axis=2)
    kn_i = kn.reshape(b, nq, per_q, HEADS).max(axis=2)
    kn_pre = lax.cummax(kn, axis=1)
    pairs = HEADS // 2
    def bias_rows(rows):
        pieces = rows.astype(F32).reshape(b, -1, pairs, LANES)[..., :6].reshape(b, -1, pairs, 2, 3)
        return pieces.sum(-1).reshape(b, -1, HEADS)

    nb_end = bias_rows(fbias[:, bk - 1::bk])
    nb_start = bias_rows(fbias[:, ::bq])
    gap = nb_start[:, :, None, :] - nb_end[:, None, :, :]
    need = qn_i[:, :, None, :] * (kn_pre[:, None, :, :] + kn_i[:, :, None, :]) + 152.0
    skip = (gap > need).reshape(b, nq, nk, pairs, 2).all(axis=-1)
    n_full = jnp.arange(nq) * per_q
    skip = skip & (jnp.arange(nk)[None, None, :, None] < n_full[None, :, None, None])
    first = jnp.argmin(skip, axis=2)
    return first.transpose(0, 2, 1).astype(jnp.int32)


def _fox_attention(qkv, fbias, bq=FOX_BQ, bk=FOX_BK):
    b, s, _ = qkv.shape
    pairs = HEADS // 2
    cb = WIDTH // LANES
    first = _fox_first_block(qkv, fbias, bq, bk)
    grid_spec = pltpu.PrefetchScalarGridSpec(
        num_scalar_prefetch=1,
        grid=(b, pairs, s // bq),
        in_specs=[pl.BlockSpec((1, bq, LANES), lambda bi, hp, i, fr: (bi, i, hp)),
                  pl.BlockSpec((1, s, LANES), lambda bi, hp, i, fr: (bi, 0, cb + hp)),
                  pl.BlockSpec((1, s, LANES), lambda bi, hp, i, fr: (bi, 0, 2 * cb + hp)),
                  pl.BlockSpec((1, s, LANES), lambda bi, hp, i, fr: (bi, 0, hp))],
        out_specs=pl.BlockSpec((1, bq, LANES), lambda bi, hp, i, fr: (bi, i, hp)),
        scratch_shapes=[pltpu.VMEM((2 * bq, LANES), F32), pltpu.VMEM((2 * bq, LANES), F32)],
    )
    return pl.pallas_call(
        functools.partial(_fox_kernel, bq=bq, bk=bk),
        out_shape=jax.ShapeDtypeStruct((b, s, WIDTH), BF16),
        grid_spec=grid_spec,
        compiler_params=_cparams(("arbitrary", "arbitrary", "arbitrary")),
        name="fox_attention",
    )(first, qkv, qkv, qkv, fbias)


PAIR = 2 * HEAD_DIM
GROUPS = WIDTH // PAIR


def _group(x, g):
    return x[:, g * PAIR:(g + 1) * PAIR]


def _head_sum(x, bd2):
    return jnp.concatenate([_dot(_group(x, g), bd2) for g in range(GROUPS)], axis=1)


def _head_apply(mats, x, lane_first):
    rows = mats.shape[1]
    outs = []
    for g in range(GROUPS):
        res = _dot(mats[2 * g:2 * g + 2].reshape(2 * rows, rows), _group(x, g))
        outs.append(jnp.where(lane_first, res[:rows], res[rows:]))
    return jnp.concatenate(outs, axis=1)


def _rwkv_kernel(p_ref, mu_ref, w0_ref, a0_ref, kk_ref, ka_ref, rk_ref, lng_ref, lnb_ref,
                 wwa_ref, g2_ref, bd_ref, o_ref, st_ref, prev_ref):
    L = CHUNK

    @pl.when(pl.program_id(1) == 0)
    def _():
        st_ref[...] = jnp.zeros_like(st_ref)
        prev_ref[...] = jnp.zeros_like(prev_ref)

    p = p_ref[0]
    T = p.shape[0]
    rowi = lax.broadcasted_iota(jnp.int32, p.shape, 0)
    prev = jnp.where(rowi == 0, prev_ref[...], pltpu.roll(p, 1, 0))
    prev_ref[...] = p[T - 1:T, :]
    ps = p + (prev - p) * mu_ref[...]
    r = ps[:, 0:WIDTH]
    k = ps[:, WIDTH:2 * WIDTH]
    v = ps[:, 2 * WIDTH:3 * WIDTH]
    wa_in = ps[:, 3 * WIDTH:3 * WIDTH + DECAY_LORA + AAA_LORA]
    gd = ps[:, 3 * WIDTH + DECAY_LORA + AAA_LORA:]
    lane_wa = lax.broadcasted_iota(jnp.int32, wa_in.shape, 1)
    wa_act = jnp.where(lane_wa < DECAY_LORA, jnp.tanh(wa_in), wa_in)
    wa = _dot(wa_act, wwa_ref[...])
    log_w = -_softplus(-(w0_ref[...] + wa[:, :WIDTH])) - 0.5
    lw = -jnp.exp(log_w)
    a = jax.nn.sigmoid(a0_ref[...] + wa[:, WIDTH:])
    out_gate = _dot(jax.nn.sigmoid(gd), g2_ref[...])
    bd = bd_ref[...]
    kk0 = k * kk_ref[...]
    kk = kk0 * lax.rsqrt(jnp.maximum(_head_sum(kk0 * kk0, bd), 1e-24))
    k2 = k * (1.0 + (a - 1.0) * ka_ref[...])
    av = -kk
    bv = kk * a

    n_sub = T // L
    rt_i = lax.broadcasted_iota(jnp.int32, (T, T), 0)
    ct_i = lax.broadcasted_iota(jnp.int32, (T, T), 1)
    tri_tile = (rt_i >= ct_i) & (rt_i // L == ct_i // L)
    cl = _split_dot_left(jnp.where(tri_tile, 1.0, 0.0).astype(BF16), lw)
    cl_end = jnp.concatenate([jnp.broadcast_to(cl[(c + 1) * L - 1:(c + 1) * L, :], (L, WIDTH))
                              for c in range(n_sub)], axis=0)
    at_all = av * jnp.exp(cl - lw)
    rt_all = r * jnp.exp(cl)
    einv = jnp.exp(-cl)
    bt_all = bv * einv
    kt_all = k2 * einv
    edec = jnp.exp(cl_end - cl)
    b_end_all = bv * edec
    k_end_all = k2 * edec

    ri = lax.broadcasted_iota(jnp.int32, (L, L), 0)
    ci = lax.broadcasted_iota(jnp.int32, (L, L), 1)
    tri_incl = ri >= ci
    tri_strict = ri > ci
    eye = jnp.where(ri == ci, 1.0, 0.0)
    lane_first = lax.broadcasted_iota(jnp.int32, (L, PAIR), 1) < HEAD_DIM
    qr = lax.broadcasted_iota(jnp.int32, (PAIR, PAIR), 0) < HEAD_DIM
    qc = lax.broadcasted_iota(jnp.int32, (PAIR, PAIR), 1) < HEAD_DIM
    same_head = qr == qc

    def bmm(x, y):
        return lax.dot_general(x.astype(BF16), y.astype(BF16), (((2,), (1,)), ((0,), (0,))),
                               preferred_element_type=F32)

    def chunk_terms(c):
        rows = slice(c * L, (c + 1) * L)
        at, rt, bt, kt, vc = at_all[rows], rt_all[rows], bt_all[rows], kt_all[rows], v[rows]
        sb_heads, sk_heads = [], []
        for g in range(GROUPS):
            at_g, rt_g = _group(at, g), _group(rt, g)
            lhs = jnp.concatenate([jnp.where(lane_first, at_g, 0.0), jnp.where(lane_first, rt_g, 0.0),
                                   jnp.where(lane_first, 0.0, at_g), jnp.where(lane_first, 0.0, rt_g)],
                                  axis=0).astype(BF16)
            sb_g = _dot_nt(lhs, _group(bt, g))
            sk_g = _dot_nt(lhs, _group(kt, g))
            for hh in range(2):
                sb_heads.append(sb_g[hh * 2 * L:(hh + 1) * 2 * L])
                sk_heads.append(sk_g[hh * 2 * L:(hh + 1) * 2 * L])
        sb = jnp.stack(sb_heads)
        sk = jnp.stack(sk_heads)
        n_ab = jnp.where(tri_strict, sb[:, :L, :], 0.0)
        a_ak = jnp.where(tri_strict, sk[:, :L, :], 0.0)
        a_rb = jnp.where(tri_incl, sb[:, L:, :], 0.0)
        a_rk = jnp.where(tri_incl, sk[:, L:, :], 0.0)
        tinv = eye + n_ab
        pw = bmm(n_ab, n_ab)
        span = 2
        while 2 * span < L:
            both = bmm(jnp.concatenate([tinv, pw], axis=1), pw)
            tinv = tinv + both[:, :L, :]
            pw = both[:, L:, :]
            span *= 2
        tinv = tinv + bmm(tinv, pw)
        av_term = _head_apply(a_ak, vc, lane_first)
        pm = _head_apply(tinv, at, lane_first)
        qm = _head_apply(tinv, av_term, lane_first)
        rkv = _head_apply(a_rk, vc, lane_first)
        return pm, qm, rkv, a_rb

    terms = [chunk_terms(c) for c in range(n_sub)]

    y_chunks = []
    for c in range(n_sub):
        rows = slice(c * L, (c + 1) * L)
        pm, qm, rkv, a_rb = terms[c]
        rt, vc, b_end, k_end = rt_all[rows], v[rows], b_end_all[rows], k_end_all[rows]
        gam_last = jnp.exp(cl[(c + 1) * L - 1:(c + 1) * L, :])
        u_parts, ys_parts = [], []
        for g in range(GROUPS):
            pr = _dot_nt(jnp.concatenate([_group(pm, g), _group(rt, g)], axis=0), st_ref[g])
            u_parts.append(pr[:L] + _group(qm, g))
            ys_parts.append(pr[L:])
        u = jnp.concatenate(u_parts, axis=1)
        y_chunks.append(jnp.concatenate(ys_parts, axis=1) + _head_apply(a_rb, u, lane_first) + rkv)
        for g in range(GROUPS):
            upd = _dot_tn(_group(u, g), _group(b_end, g)) + _dot_tn(_group(vc, g), _group(k_end, g))
            st_ref[g] = st_ref[g] * _group(gam_last, g) + jnp.where(same_head, upd, 0.0)
    y = jnp.concatenate(y_chunks, axis=0)

    inv_n = 1.0 / HEAD_DIM
    mean = _head_sum(y, bd) * inv_n
    dlt = y - mean
    var = _head_sum(dlt * dlt, bd) * inv_n
    yn = dlt * lax.rsqrt(var + LN_X_EPS) * lng_ref[...] + lnb_ref[...]
    bonus = _head_sum(r * k2 * rk_ref[...], bd) * v
    o_ref[0] = ((yn + bonus) * out_gate).astype(o_ref.dtype)


def _split_dot_left(w_bf16, x):
    hi = x.astype(BF16)
    r1 = x - hi.astype(F32)
    mid = r1.astype(BF16)
    lo = (r1 - mid.astype(F32)).astype(BF16)
    return (jnp.dot(w_bf16, hi, preferred_element_type=F32)
            + jnp.dot(w_bf16, mid, preferred_element_type=F32)
            + jnp.dot(w_bf16, lo, preferred_element_type=F32))


def _rwkv_branch(p_rw, mu, w0, w2, a0, a2, g2, k_k, k_a, r_k, ln_g, ln_b):
    b, s, sw = p_rw.shape
    row = lambda t: t.reshape(1, -1).astype(F32)
    wwa = jnp.zeros((DECAY_LORA + AAA_LORA, 2 * WIDTH), F32)
    wwa = wwa.at[:DECAY_LORA, :WIDTH].set(w2).at[DECAY_LORA:, WIDTH:].set(a2).astype(BF16)
    hid = jnp.arange(PAIR) // HEAD_DIM
    bd = (hid[:, None] == hid[None, :]).astype(BF16)
    const = lambda shape: pl.BlockSpec(shape, lambda i, j: (0,) * len(shape))
    return pl.pallas_call(
        _rwkv_kernel,
        out_shape=jax.ShapeDtypeStruct((b, s, WIDTH), BF16),
        grid=(b, s // RWKV_TILE),
        in_specs=[pl.BlockSpec((1, RWKV_TILE, sw), lambda i, j: (i, j, 0)),
                  const((1, sw)), const((1, WIDTH)), const((1, WIDTH)), const((1, WIDTH)),
                  const((1, WIDTH)), const((1, WIDTH)), const((1, WIDTH)), const((1, WIDTH)),
                  const((DECAY_LORA + AAA_LORA, 2 * WIDTH)), const((GATE_LORA, WIDTH)),
                  const((PAIR, PAIR))],
        out_specs=pl.BlockSpec((1, RWKV_TILE, WIDTH), lambda i, j: (i, j, 0)),
        scratch_shapes=[pltpu.VMEM((GROUPS, PAIR, PAIR), F32), pltpu.VMEM((1, sw), F32)],
        compiler_params=_cparams(("arbitrary", "arbitrary")),
        name="rwkv7_scan",
    )(p_rw, row(mu), row(w0), row(a0), row(k_k), row(k_a), row(r_k), row(ln_g), row(ln_b),
      wwa, g2.astype(BF16), bd)


def _out_kernel(of_ref, orw_ref, gate_ref, x_ref, mod_ref, wof_ref, wor_ref, wo_ref, n2g_ref,
                wr_ref, br_ref, x1_ref, h2_ref, route_ref, counts_ref, cnt_ref):
    d = x_ref.shape[-1]
    gate = gate_ref[...].astype(F32)
    merged = (gate[:, :d] * jnp.dot(of_ref[...], wof_ref[...], preferred_element_type=F32)
              + gate[:, d:] * jnp.dot(orw_ref[...], wor_ref[...], preferred_element_type=F32))
    gate1 = mod_ref[0, 2:3, :]
    shift2 = mod_ref[0, 3:4, :]
    scale2 = mod_ref[0, 4:5, :]
    x1 = x_ref[...] + gate1 * jnp.dot(merged.astype(BF16), wo_ref[...], preferred_element_type=F32)
    x1_ref[...] = x1
    ms = jnp.mean(x1 * x1, axis=-1, keepdims=True)
    h2 = x1 * lax.rsqrt(ms + NORM_EPS) * n2g_ref[...] * (1.0 + scale2) + shift2
    h2_ref[...] = _pack_pairs(h2)

    h2_hi = h2.astype(BF16)
    h2_lo = (h2 - h2_hi.astype(F32)).astype(BF16)
    logits = (jnp.dot(h2_hi, wr_ref[0], preferred_element_type=F32)
              + jnp.dot(h2_lo, wr_ref[0], preferred_element_type=F32)
              + jnp.dot(h2_hi, wr_ref[1], preferred_element_type=F32)) + br_ref[...]
    lane = lax.broadcasted_iota(jnp.int32, logits.shape, 1)
    neg = -jnp.inf
    big = jnp.int32(LANES)
    gl = jnp.where(lane < N_GROUPS, logits, neg)
    gmax = jnp.max(gl, axis=1, keepdims=True)
    gidx = jnp.min(jnp.where(gl == gmax, lane, big), axis=1, keepdims=True)
    g_p = 1.0 / jnp.sum(jnp.exp(gl - gmax), axis=1, keepdims=True)
    e_lane = lane - N_GROUPS
    in_grp = (e_lane >= 0) & (e_lane < N_EXPERTS) & ((e_lane // EXPERTS_PER_GROUP) == gidx)
    sel = jnp.where(in_grp, logits, neg)
    m1 = jnp.max(sel, axis=1, keepdims=True)
    i1 = jnp.min(jnp.where(sel == m1, lane, big), axis=1, keepdims=True)
    sel2 = jnp.where(lane == i1, neg, sel)
    m2 = jnp.max(sel2, axis=1, keepdims=True)
    i2 = jnp.min(jnp.where(sel2 == m2, lane, big), axis=1, keepdims=True)
    e21 = jnp.exp(m2 - m1)
    w_first = g_p / (1.0 + e21)
    w_second = g_p * e21 / (1.0 + e21)
    @pl.when(pl.program_id(0) == 0)
    def _():
        cnt_ref[...] = jnp.zeros_like(cnt_ref)

    tm = logits.shape[0]
    oh1 = lane == i1
    oh2 = lane == i2
    both = jnp.where(oh1 | oh2, 1.0, 0.0)
    before = (lax.broadcasted_iota(jnp.int32, (tm, tm), 0)
              > lax.broadcasted_iota(jnp.int32, (tm, tm), 1))
    seen = jnp.dot(jnp.where(before, 1.0, 0.0).astype(BF16), both.astype(BF16),
                   preferred_element_type=F32) + cnt_ref[...]
    rank1 = jnp.sum(jnp.where(oh1, seen, 0.0), axis=1, keepdims=True)
    rank2 = jnp.sum(jnp.where(oh2, seen, 0.0), axis=1, keepdims=True)
    cnt_ref[...] = cnt_ref[...] + jnp.sum(both, axis=0, keepdims=True)
    counts_ref[...] = jnp.broadcast_to(cnt_ref[...], counts_ref.shape)

    route = jnp.where(lane == 0, (i1 - N_GROUPS).astype(F32),
                      jnp.where(lane == 1, (i2 - N_GROUPS).astype(F32),
                                jnp.where(lane == 2, w_first,
                                          jnp.where(lane == 3, w_second,
                                                    jnp.where(lane == 4, rank1,
                                                              jnp.where(lane == 5, rank2, 0.0))))))
    route_ref[...] = route


def _merge_out_router(o_fox, o_rw, gate, x, mod, wof, wor, wo, n2g, wr, br, tm=512):
    b, s, d = x.shape
    t = b * s
    spb = s // tm
    rowspec = lambda w: pl.BlockSpec((tm, w), lambda i: (i, 0))
    const = lambda shape: pl.BlockSpec(shape, lambda i: (0,) * len(shape))
    return pl.pallas_call(
        _out_kernel,
        out_shape=(jax.ShapeDtypeStruct((t, d), F32), jax.ShapeDtypeStruct((t, d // 2), jnp.uint32),
                   jax.ShapeDtypeStruct((t, LANES), F32), jax.ShapeDtypeStruct((8, LANES), F32)),
        grid=(t // tm,),
        in_specs=[rowspec(WIDTH), rowspec(WIDTH), rowspec(2 * d), rowspec(d),
                  pl.BlockSpec((1, 6, d), lambda i: (i // spb, 0, 0)),
                  const((WIDTH, d)), const((WIDTH, d)), const((d, d)), const((1, d)),
                  const((2, d, LANES)), const((1, LANES))],
        out_specs=(rowspec(d), rowspec(d // 2), rowspec(LANES), const((8, LANES))),
        scratch_shapes=[pltpu.VMEM((1, LANES), F32)],
        compiler_params=_cparams(("arbitrary",)),
        name="merge_out_router",
    )(o_fox, o_rw, gate, x.reshape(t, d), mod, wof, wor, wo, n2g.reshape(1, d), wr, br)


SC_CORES = 2
SC_SUBCORES = 16
SC_CHUNK = 128


def _sc_scatter_rows(src, idx, n_rows):
    t, width = src.shape
    per_worker = t // (SC_CORES * SC_SUBCORES)
    n_chunks = per_worker // SC_CHUNK
    assert n_chunks * SC_CHUNK * SC_CORES * SC_SUBCORES == t and idx.shape[0] == TOP_K * t
    mesh = plsc.VectorSubcoreMesh(core_axis_name="c", subcore_axis_name="s")

    @functools.partial(
        pl.kernel, mesh=mesh,
        out_type=jax.ShapeDtypeStruct((n_rows, width), src.dtype),
        scratch_types=[pltpu.VMEM((SC_CHUNK,), jnp.int32) for _ in range(TOP_K)]
        + [pltpu.VMEM((SC_CHUNK, width), src.dtype), pltpu.SemaphoreType.DMA])
    def scatter(src_hbm, idx_hbm, out_hbm, *scratch):
        idx_v, rows_v, sem = scratch[:TOP_K], scratch[TOP_K], scratch[TOP_K + 1]
        worker = lax.axis_index("s") * SC_CORES + lax.axis_index("c")

        @pl.loop(0, n_chunks)
        def _(ci):
            off = pl.multiple_of(worker * per_worker + ci * SC_CHUNK, SC_CHUNK)
            pltpu.sync_copy(src_hbm.at[pl.ds(off, SC_CHUNK)], rows_v)
            for kk in range(TOP_K):
                pltpu.sync_copy(idx_hbm.at[pl.ds(kk * t + off, SC_CHUNK)], idx_v[kk])
            for kk in range(TOP_K):
                pltpu.async_copy(rows_v, out_hbm.at[idx_v[kk]], sem).wait()

    return scatter(src, idx)


def _expert_kernel(blk_e_ref, nused_ref, valid_ref, xs_ref, w1_ref, w3_ref, w2_ref, o_ref,
                   w1b_ref, w3b_ref, w2b_ref):
    i = pl.program_id(0)
    live = i * MOE_ROWS < nused_ref[0]
    new_expert = jnp.logical_or(i == 0, blk_e_ref[i] != blk_e_ref[jnp.maximum(i - 1, 0)])

    @pl.when(new_expert)
    def _():
        w1b_ref[...] = w1_ref[0].astype(BF16)
        w3b_ref[...] = w3_ref[0].astype(BF16)
        w2b_ref[...] = w2_ref[0].astype(BF16)

    @pl.when(live)
    def _():
        row = lax.broadcasted_iota(jnp.int32, xs_ref.shape, 0)
        xs = jnp.where(row < valid_ref[i], xs_ref[...], jnp.uint32(0))
        xb = _unpack_pairs(xs).astype(BF16)
        h1 = jnp.dot(xb, w1b_ref[...], preferred_element_type=F32)
        h3 = jnp.dot(xb, w3b_ref[...], preferred_element_type=F32)
        hh = (h1 * jax.nn.sigmoid(h1)) * h3
        o_ref[...] = _pack_pairs(jnp.dot(hh.astype(BF16), w2b_ref[...], preferred_element_type=F32))

    @pl.when(jnp.logical_not(live))
    def _():
        o_ref[...] = jnp.zeros_like(o_ref)


def _moe_experts(xs, blk_e, nused, blk_valid, w1, w3, w2):
    rows, dp = xs.shape
    _, d, de = w1.shape
    grid_spec = pltpu.PrefetchScalarGridSpec(
        num_scalar_prefetch=3,
        grid=(rows // MOE_ROWS,),
        in_specs=[pl.BlockSpec((MOE_ROWS, dp), lambda i, be, nu, va: (i, 0)),
                  pl.BlockSpec((1, d, de), lambda i, be, nu, va: (be[i], 0, 0)),
                  pl.BlockSpec((1, d, de), lambda i, be, nu, va: (be[i], 0, 0)),
                  pl.BlockSpec((1, de, d), lambda i, be, nu, va: (be[i], 0, 0))],
        out_specs=pl.BlockSpec((MOE_ROWS, dp), lambda i, be, nu, va: (i, 0)),
        scratch_shapes=[pltpu.VMEM((d, de), BF16), pltpu.VMEM((d, de), BF16), pltpu.VMEM((de, d), BF16)],
    )
    return pl.pallas_call(
        _expert_kernel,
        out_shape=jax.ShapeDtypeStruct((rows, dp), xs.dtype),
        grid_spec=grid_spec,
        compiler_params=_cparams(("arbitrary",)),
        name="moe_experts",
    )(blk_e, nused, blk_valid, xs, w1, w3, w2)


def _sc_gather_rows(table, idx):
    n_idx = idx.shape[0]
    width = table.shape[1]
    per_worker = n_idx // (SC_CORES * SC_SUBCORES)
    n_chunks = per_worker // SC_CHUNK
    assert n_chunks * SC_CHUNK * SC_CORES * SC_SUBCORES == n_idx
    mesh = plsc.VectorSubcoreMesh(core_axis_name="c", subcore_axis_name="s")

    @functools.partial(
        pl.kernel, mesh=mesh,
        out_type=jax.ShapeDtypeStruct((n_idx, width), table.dtype),
        scratch_types=[pltpu.VMEM((SC_CHUNK,), jnp.int32), pltpu.VMEM((SC_CHUNK, width), table.dtype),
                       pltpu.SemaphoreType.DMA])
    def gather(table_hbm, idx_hbm, out_hbm, idx_v, rows_v, sem):
        worker = lax.axis_index("s") * SC_CORES + lax.axis_index("c")

        @pl.loop(0, n_chunks)
        def _(ci):
            off = pl.multiple_of(worker * per_worker + ci * SC_CHUNK, SC_CHUNK)
            pltpu.sync_copy(idx_hbm.at[pl.ds(off, SC_CHUNK)], idx_v)
            pltpu.async_copy(table_hbm.at[idx_v], rows_v, sem).wait()
            pltpu.sync_copy(rows_v, out_hbm.at[pl.ds(off, SC_CHUNK)])

    return gather(table, idx)


def _final_kernel(route_ref, x1_ref, mod_ref, fg_ref, y0_ref, y1_ref, o_ref):
    route = route_ref[...]
    y = route[:, 2:3] * _unpack_pairs(y0_ref[...]) + route[:, 3:4] * _unpack_pairs(y1_ref[...])
    gate2 = mod_ref[0, 5:6, :]
    x2 = x1_ref[...] + gate2 * y
    ms = jnp.mean(x2 * x2, axis=-1, keepdims=True)
    o_ref[...] = x2 * lax.rsqrt(ms + NORM_EPS) * fg_ref[...]


def _moe_combine_final(dest_slots, route, x1, mod, final_g, ys, s, tm=512):
    t, d = x1.shape
    spb = s // tm
    picked = _sc_gather_rows(ys, dest_slots)
    tiles = t // tm
    return pl.pallas_call(
        _final_kernel,
        out_shape=jax.ShapeDtypeStruct((t, d), F32),
        grid=(tiles,),
        in_specs=[pl.BlockSpec((tm, LANES), lambda i: (i, 0)),
                  pl.BlockSpec((tm, d), lambda i: (i, 0)),
                  pl.BlockSpec((1, 6, d), lambda i: (i // spb, 0, 0)),
                  pl.BlockSpec((1, d), lambda i: (0, 0)),
                  pl.BlockSpec((tm, picked.shape[1]), lambda i: (i, 0)),
                  pl.BlockSpec((tm, picked.shape[1]), lambda i: (tiles + i, 0))],
        out_specs=pl.BlockSpec((tm, d), lambda i: (i, 0)),
        compiler_params=_cparams(("arbitrary",)),
        name="moe_combine_final",
    )(route, x1, mod, final_g.reshape(1, d), picked, picked)


def _moe_plan(route, counts):
    t = route.shape[0]
    m = t * TOP_K
    flat_e = route[:, :TOP_K].astype(jnp.int32).T.reshape(m)
    rank = route[:, 4:4 + TOP_K].astype(jnp.int32).T.reshape(m)
    counts = counts[0, N_GROUPS:N_GROUPS + N_EXPERTS].astype(jnp.int32)
    padded = (counts + MOE_ROWS - 1) // MOE_ROWS * MOE_ROWS
    pad_end = jnp.cumsum(padded)
    pad_start = pad_end - padded
    experts = jnp.arange(N_EXPERTS, dtype=jnp.int32)
    start_of = jnp.sum(jnp.where(flat_e[:, None] == experts[None, :], pad_start[None, :], 0), axis=1)
    dest = (start_of + rank).astype(jnp.int32)
    n_blocks = m // MOE_ROWS + N_EXPERTS
    blk_start = jnp.arange(n_blocks, dtype=jnp.int32) * MOE_ROWS
    blk_e = jnp.minimum(jnp.sum(pad_end[None, :] <= blk_start[:, None], axis=1), N_EXPERTS - 1).astype(jnp.int32)
    nused = pad_end[-1:].astype(jnp.int32)
    blk_valid = jnp.clip(counts[blk_e] - (blk_start - pad_start[blk_e]), 0, MOE_ROWS).astype(jnp.int32)
    return dest, blk_e, nused, blk_valid, n_blocks * MOE_ROWS


def kernel(x, c, ada_w, ada_b, norm1_g, w_in, fox_forget_b, shift_mu, rwkv_w0, rwkv_w2, rwkv_a0, rwkv_a2, rwkv_g2, rwkv_k_k, rwkv_k_a, rwkv_r_k, ln_x_g, ln_x_b, w_out_fox, w_out_rwkv, w_o, norm2_g, router_group_w, router_group_b, router_expert_w, router_expert_b, exp_w1, exp_w3, exp_w2, final_g):
    b, s, d = x.shape
    t = b * s
    assert ada_w.shape[0] == 1, "the final norm is fused into the last layer's combine; one layer is laid out"
    for l in range(1):
        mod = _adaln_mod(c, ada_w[l], ada_b[l])

        h = _norm_mod(x, norm1_g[l], mod, shift_idx=0, scale_idx=1)
        h2d = h.reshape(t, d)
        w = w_in[l]
        o_f = 3 * WIDTH
        o_rw = o_f + HEADS
        o_g = o_rw + SHIFT_WIDTH
        perm = jnp.argsort(fox_forget_b[l])
        by_head = lambda m: m.reshape(d, HEADS, HEAD_DIM)[:, perm]
        wq = by_head(w[:, :WIDTH]).reshape(d, WIDTH)
        wk = by_head(w[:, WIDTH:2 * WIDTH]).reshape(d, WIDTH)
        wv = by_head(w[:, 2 * WIDTH:o_f]).reshape(d, WIDTH)
        qkv = _matmul(h2d, jnp.concatenate([wq, wk, wv], axis=1).astype(BF16), BF16, name="proj_qkv")
        p_rw = _matmul(h2d, w[:, o_rw:o_g].astype(BF16), F32, tn=896, name="proj_rwkv")
        gate = _matmul(h2d, w[:, o_g:].astype(BF16), BF16, act="sigmoid", name="proj_gate")
        fbias = _forget_bias(h, w[:, o_f:o_rw][:, perm], fox_forget_b[l][perm])
        o_fox = _fox_attention(qkv.reshape(b, s, 3 * WIDTH), fbias)
        w_of = w_out_fox[l].reshape(HEADS, HEAD_DIM, d)[perm].reshape(WIDTH, d)
        o_rwkv = _rwkv_branch(p_rw.reshape(b, s, SHIFT_WIDTH), shift_mu[l], rwkv_w0[l], rwkv_w2[l],
                              rwkv_a0[l], rwkv_a2[l], rwkv_g2[l], rwkv_k_k[l], rwkv_k_a[l],
                              rwkv_r_k[l], ln_x_g[l], ln_x_b[l])

        wr = jnp.zeros((d, LANES), F32)
        wr = wr.at[:, :N_GROUPS].set(router_group_w[l]).at[:, N_GROUPS:N_GROUPS + N_EXPERTS].set(router_expert_w[l])
        br = jnp.zeros((1, LANES), F32)
        br = br.at[0, :N_GROUPS].set(router_group_b[l]).at[0, N_GROUPS:N_GROUPS + N_EXPERTS].set(router_expert_b[l])
        wr_hi = wr.astype(BF16)
        wr_lo = (wr - wr_hi.astype(F32)).astype(BF16)
        x1, h2, route, counts = _merge_out_router(
            o_fox.reshape(t, WIDTH), o_rwkv.reshape(t, WIDTH), gate, x, mod,
            w_of.astype(BF16), w_out_rwkv[l].astype(BF16), w_o[l].astype(BF16), norm2_g[l],
            jnp.stack([wr_hi, wr_lo]), br)

        dest, blk_e, nused, blk_valid, rows = _moe_plan(route, counts)
        xs = _sc_scatter_rows(h2, dest, rows)
        ys = _moe_experts(xs, blk_e, nused, blk_valid, exp_w1[l], exp_w3[l], exp_w2[l])
        out = _moe_combine_final(dest, route, x1, mod, final_g, ys, s)
    return out.reshape(b, s, d)
```

```python
import functools

import jax
import jax.numpy as jnp
from jax import lax
from jax.experimental import pallas as pl
from jax.experimental.pallas import tpu as pltpu
from jax.experimental.pallas import tpu_sc as plsc

F32 = jnp.float32
BF16 = jnp.bfloat16
HIGHEST = lax.Precision.HIGHEST

HEADS = 8
HEAD_DIM = 64
WIDTH = HEADS * HEAD_DIM
DECAY_LORA = 64
AAA_LORA = 64
GATE_LORA = 128
SHIFT_WIDTH = 3 * WIDTH + DECAY_LORA + AAA_LORA + GATE_LORA
LN_X_EPS = 64e-5
NORM_EPS = 1e-6
N_GROUPS = 4
EXPERTS_PER_GROUP = 8
N_EXPERTS = N_GROUPS * EXPERTS_PER_GROUP
TOP_K = 2

LANES = 128
CHUNK = 64
RWKV_TILE = 256
MOE_ROWS = 512
VMEM_LIMIT = 48 * 1024 * 1024


def _cparams(sem):
    return pltpu.CompilerParams(dimension_semantics=sem, vmem_limit_bytes=VMEM_LIMIT)


def _dot(a, b):
    return jnp.dot(a.astype(BF16), b.astype(BF16), preferred_element_type=F32)


def _dot_nt(a, b):
    return lax.dot_general(a.astype(BF16), b.astype(BF16), (((1,), (1,)), ((), ())),
                           preferred_element_type=F32)


def _dot_tn(a, b):
    return lax.dot_general(a.astype(BF16), b.astype(BF16), (((0,), (0,)), ((), ())),
                           preferred_element_type=F32)


def _split_dot(x, w_bf16):
    hi = x.astype(BF16)
    lo = (x - hi.astype(F32)).astype(BF16)
    return (jnp.dot(hi, w_bf16, preferred_element_type=F32)
            + jnp.dot(lo, w_bf16, preferred_element_type=F32))


def _pack_pairs(x):
    n = x.shape[1] // 2
    lo = lax.bitcast_convert_type(x[:, :n].astype(BF16).astype(F32), jnp.uint32) >> 16
    hi = lax.bitcast_convert_type(x[:, n:].astype(BF16).astype(F32), jnp.uint32) & jnp.uint32(0xFFFF0000)
    return lo | hi


def _unpack_pairs(u):
    lo = lax.bitcast_convert_type(u << 16, F32)
    hi = lax.bitcast_convert_type(u & jnp.uint32(0xFFFF0000), F32)
    return jnp.concatenate([lo, hi], axis=1)


def _softplus(x):
    return jnp.maximum(x, 0.0) + jnp.log1p(jnp.exp(-jnp.abs(x)))


def _mod_kernel(c_ref, w_ref, b_ref, o_ref):
    c = c_ref[...]
    sc = c * jax.nn.sigmoid(c)
    o_ref[...] = jnp.dot(sc, w_ref[...], precision=HIGHEST, preferred_element_type=F32) + b_ref[...]


def _adaln_mod(c, ada_w, ada_b):
    b, d = c.shape
    n = ada_w.shape[1]
    rows = 8
    cp = jnp.zeros((rows, d), F32).at[:b].set(c)
    tn = 1024
    out = pl.pallas_call(
        _mod_kernel,
        out_shape=jax.ShapeDtypeStruct((rows, n), F32),
        grid=(n // tn,),
        in_specs=[pl.BlockSpec((rows, d), lambda j: (0, 0)),
                  pl.BlockSpec((d, tn), lambda j: (0, j)),
                  pl.BlockSpec((1, tn), lambda j: (0, j))],
        out_specs=pl.BlockSpec((rows, tn), lambda j: (0, j)),
        compiler_params=_cparams(("arbitrary",)),
        name="adaln_mod",
    )(cp, ada_w, ada_b.reshape(1, n))
    return out[:b].reshape(b, 6, d)


def _norm_mod_kernel(x_ref, g_ref, mod_ref, o_ref, *, shift_idx, scale_idx):
    x = x_ref[0]
    ms = jnp.mean(x * x, axis=-1, keepdims=True)
    y = x * lax.rsqrt(ms + NORM_EPS) * g_ref[...]
    scale = mod_ref[0, scale_idx:scale_idx + 1, :]
    shift = mod_ref[0, shift_idx:shift_idx + 1, :]
    o_ref[0] = (y * (1.0 + scale) + shift).astype(o_ref.dtype)


def _norm_mod(x, g, mod, shift_idx, scale_idx, tm=1024):
    b, s, d = x.shape
    return pl.pallas_call(
        functools.partial(_norm_mod_kernel, shift_idx=shift_idx, scale_idx=scale_idx),
        out_shape=jax.ShapeDtypeStruct((b, s, d), BF16),
        grid=(b, s // tm),
        in_specs=[pl.BlockSpec((1, tm, d), lambda i, j: (i, j, 0)),
                  pl.BlockSpec((1, d), lambda i, j: (0, 0)),
                  pl.BlockSpec((1, 6, d), lambda i, j: (i, 0, 0))],
        out_specs=pl.BlockSpec((1, tm, d), lambda i, j: (i, j, 0)),
        compiler_params=_cparams(("arbitrary", "arbitrary")),
        name="norm1_mod",
    )(x, g.reshape(1, d), mod)


def _mm_kernel(a_ref, w_ref, *rest, act, has_bias):
    o_ref = rest[-1]
    r = jnp.dot(a_ref[...], w_ref[...], preferred_element_type=F32)
    if has_bias:
        r = r + rest[0][...]
    if act == "sigmoid":
        r = 0.5 * jnp.tanh(0.5 * r) + 0.5
    o_ref[...] = r.astype(o_ref.dtype)


def _matmul(a, w, out_dtype, bias=None, act=None, tm=2048, tn=512, name="proj"):
    t, k = a.shape
    n = w.shape[1]
    tn = min(tn, n)
    in_specs = [pl.BlockSpec((tm, k), lambda i, j: (i, 0)),
                pl.BlockSpec((k, tn), lambda i, j: (0, j))]
    args = [a, w]
    if bias is not None:
        in_specs.append(pl.BlockSpec((1, tn), lambda i, j: (0, j)))
        args.append(bias)
    return pl.pallas_call(
        functools.partial(_mm_kernel, act=act, has_bias=bias is not None),
        out_shape=jax.ShapeDtypeStruct((t, n), out_dtype),
        grid=(t // tm, n // tn),
        in_specs=in_specs,
        out_specs=pl.BlockSpec((tm, tn), lambda i, j: (i, j)),
        compiler_params=_cparams(("arbitrary", "arbitrary")),
        name=name,
    )(*args)


LOG2E = 1.4426950408889634


def _split3(x):
    hi = x.astype(BF16)
    r1 = x - hi.astype(F32)
    mid = r1.astype(BF16)
    lo = (r1 - mid.astype(F32)).astype(BF16)
    return hi, mid, lo


def _fcum_kernel(h_ref, wf_ref, fb_ref, sel_ref, o_ref, carry_ref, *, ts):
    @pl.when(pl.program_id(0) == 0)
    def _():
        carry_ref[...] = jnp.zeros_like(carry_ref)

    ri = lax.broadcasted_iota(jnp.int32, (ts, ts), 0)
    ci = lax.broadcasted_iota(jnp.int32, (ts, ts), 1)
    tri = jnp.where(ri >= ci, 1.0, 0.0).astype(BF16)
    for bi in range(h_ref.shape[0]):
        f = jnp.dot(h_ref[bi], wf_ref[...], preferred_element_type=F32) + fb_ref[...]
        lf = -_softplus(-f)
        cum = carry_ref[bi]
        for piece in _split3(lf):
            cum = cum + jnp.dot(tri, piece, preferred_element_type=F32)
        carry_ref[bi] = cum[ts - 1:ts, :]
        out = jnp.zeros((ts, sel_ref.shape[2]), F32)
        for idx, piece in enumerate(_split3(cum * (-LOG2E))):
            out = out + jnp.dot(piece, sel_ref[idx], preferred_element_type=F32)
        o_ref[bi] = out.astype(o_ref.dtype)


def _forget_bias(h, wf, fb, ts=512):
    b, s, d = h.shape
    pairs = HEADS // 2
    wf_p = jnp.zeros((d, LANES), F32).at[:, :HEADS].set(wf).astype(BF16)
    fb_p = jnp.zeros((1, LANES), F32).at[0, :HEADS].set(fb)
    hh = jnp.arange(HEADS)
    sel = jnp.zeros((3, LANES, pairs * LANES), F32)
    for piece in range(3):
        sel = sel.at[piece, hh, (hh // 2) * LANES + (hh % 2) * 3 + piece].set(1.0)
    return pl.pallas_call(
        functools.partial(_fcum_kernel, ts=ts),
        out_shape=jax.ShapeDtypeStruct((b, s, pairs * LANES), BF16),
        grid=(s // ts,),
        in_specs=[pl.BlockSpec((b, ts, d), lambda j: (0, j, 0)),
                  pl.BlockSpec((d, LANES), lambda j: (0, 0)),
                  pl.BlockSpec((1, LANES), lambda j: (0, 0)),
                  pl.BlockSpec((3, LANES, pairs * LANES), lambda j: (0, 0, 0))],
        out_specs=pl.BlockSpec((b, ts, pairs * LANES), lambda j: (0, j, 0)),
        scratch_shapes=[pltpu.VMEM((b, 1, LANES), F32)],
        compiler_params=_cparams(("arbitrary",)),
        name="forget_bias",
    )(h, wf_p, fb_p, sel.astype(BF16))


FOX_BQ = 1024
FOX_BK = 512


def _fox_kernel(first_ref, q_ref, k_ref, v_ref, a_ref, o_ref, m_ref, acc_ref, *, bq, bk):
    i = pl.program_id(2)
    lane = lax.broadcasted_iota(jnp.int32, (bq, LANES), 1)
    first = lane < HEAD_DIM
    qs = q_ref[0].astype(F32) * (HEAD_DIM ** -0.5 * LOG2E)
    aug0 = jnp.where(lane < 3, 1.0, 0.0)
    aug1 = jnp.where((lane >= 3) & (lane < 6), 1.0, 0.0)
    q01 = jnp.concatenate([jnp.concatenate([jnp.where(first, qs, 0.0), aug0], axis=1),
                           jnp.concatenate([jnp.where(first, 0.0, qs), aug1], axis=1)],
                          axis=0).astype(BF16)
    m_ref[...] = jnp.full_like(m_ref, -jnp.inf)
    acc_ref[...] = jnp.zeros_like(acc_ref)
    lane_k = lax.broadcasted_iota(jnp.int32, (bk, LANES), 1)
    keep_first = jnp.where(lane_k < HEAD_DIM, 1.0, 0.0).astype(BF16)
    keep_second = jnp.where(lane_k < HEAD_DIM, 0.0, 1.0).astype(BF16)

    def both_heads(x, r0):
        if r0 == 0:
            return x[...]
        return jnp.concatenate([x[r0:bq], x[bq + r0:2 * bq]], axis=0)

    def logits(j, r0=0):
        start = pl.multiple_of(j * bk, bk)
        kb = jnp.concatenate([k_ref[0, pl.ds(start, bk), :], a_ref[0, pl.ds(start, bk), :]], axis=1)
        return lax.dot_general(both_heads(q01, r0), kb, (((1,), (1,)), ((), ())),
                               preferred_element_type=F32)

    def consume(j, z, r0=0):
        nr = bq - r0
        start = pl.multiple_of(j * bk, bk)
        vb = v_ref[0, pl.ds(start, bk), :]
        m_prev = both_heads(m_ref, r0)
        m_new = jnp.maximum(m_prev, jnp.max(z, axis=1, keepdims=True))
        alpha = jnp.exp2(m_prev - m_new)
        p = jnp.exp2(z - jnp.concatenate([m_new] * (bk // LANES), axis=1)).astype(BF16)
        pv = jnp.concatenate(
            [jnp.dot(p[:nr], vb * keep_first + keep_second, preferred_element_type=F32),
             jnp.dot(p[nr:], vb * keep_second + keep_first, preferred_element_type=F32)], axis=0)
        acc_new = alpha * both_heads(acc_ref, r0) + pv
        if r0 == 0:
            acc_ref[...] = acc_new
            m_ref[...] = m_new
        else:
            for half, dst in ((slice(0, nr), slice(r0, bq)), (slice(nr, 2 * nr), slice(bq + r0, 2 * bq))):
                acc_ref[dst] = acc_new[half]
                m_ref[dst] = m_new[half]

    per_q = bq // bk
    n_full = i * per_q

    j_first = first_ref[pl.program_id(0), pl.program_id(1), i]
    odd = (n_full - j_first) & 1

    @pl.when(odd == 1)
    def _():
        consume(j_first, logits(j_first))

    def body(step, carry):
        j = j_first + odd + 2 * step
        z_a = logits(j)
        z_b = logits(j + 1)
        consume(j, z_a)
        consume(j + 1, z_b)
        return carry

    lax.fori_loop(0, (n_full - j_first) // 2, body, 0)
    for d in range(per_q):
        r0 = d * bk
        row = lax.broadcasted_iota(jnp.int32, (bq - r0, bk), 0)
        col = lax.broadcasted_iota(jnp.int32, (bq - r0, bk), 1)
        keep = col <= row
        z = logits(n_full + d, r0)
        consume(n_full + d, jnp.where(jnp.concatenate([keep, keep], axis=0), z, -jnp.inf), r0)
    acc = acc_ref[...]
    o = acc / pltpu.roll(acc, HEAD_DIM, 1)
    o_ref[0] = jnp.where(first, o[:bq], o[bq:]).astype(o_ref.dtype)


def _norm_bound_kernel(x_ref, sel_ref, o_ref):
    x = x_ref[...].astype(F32)
    ssq = _dot(x * x, sel_ref[...])
    dots = _dot(x[:, :WIDTH] * x[:, WIDTH:], sel_ref[0:WIDTH, :])
    sub = lax.broadcasted_iota(jnp.int32, o_ref.shape[1:], 0)
    o_ref[0] = jnp.where(sub == 0, jnp.max(ssq, axis=0, keepdims=True), jnp.min(dots, axis=0, keepdims=True))


FOX_ZERO_LOG2 = 136.0


def _fox_first_block(qkv, fbias, bq, bk):
    b, s, _ = qkv.shape
    t = b * s
    nq, nk = s // bq, s // bk
    n_heads2 = 2 * HEADS
    sel = (jnp.arange(2 * WIDTH)[:, None] // HEAD_DIM == jnp.arange(LANES)[None, :]).astype(BF16)
    bounds = pl.pallas_call(
        _norm_bound_kernel,
        out_shape=jax.ShapeDtypeStruct((t // bk, 8, LANES), F32),
        grid=(t // bk,),
        in_specs=[pl.BlockSpec((bk, 2 * WIDTH), lambda i: (i, 0)),
                  pl.BlockSpec((2 * WIDTH, LANES), lambda i: (0, 0))],
        out_specs=pl.BlockSpec((1, 8, LANES), lambda i: (i, 0, 0)),
        compiler_params=_cparams(("arbitrary",)),
        name="fox_norm_bounds",
    )(qkv.reshape(t, -1), sel)
    nrm = jnp.sqrt(bounds[:, 0, :n_heads2]).reshape(b, nk, n_heads2) * 1.01
    q_scale = HEAD_DIM ** -0.5 * LOG2E
    qn = nrm[..., :HEADS] * (q_scale * 1.01)
    kn = nrm[..., HEADS:]
    per_q = bq // bk
    qn_i = qn.reshape(b, nq, per_q, HEADS).max(axis=2)
    kn_i = kn.reshape(b, nq, per_q, HEADS).max(axis=2)
    kn_pre = lax.cummax(kn, axis=1)
    diag = bounds[:, 1, :HEADS].reshape(b, nq, per_q, HEADS).min(axis=2) * q_scale
    diag_low = diag - 0.02 * qn_i * kn_i
    pairs = HEADS // 2
    def bias_rows(rows):
        pieces = rows.astype(F32).reshape(b, -1, pairs, LANES)[..., :6].reshape(b, -1, pairs, 2, 3)
        return pieces.sum(-1).reshape(b, -1, HEADS)

    nb_end = bias_rows(fbias[:, bk - 1::bk])
    nb_start = bias_rows(fbias[:, ::bq])
    gap = nb_start[:, :, None, :] - nb_end[:, None, :, :]
    need = qn_i[:, :, None, :] * kn_pre[:, None, :, :] - diag_low[:, :, None, :] + FOX_ZERO_LOG2
    skip = (gap > need).reshape(b, nq, nk, pairs, 2).all(axis=-1)
    n_full = jnp.arange(nq) * per_q
    skip = skip & (jnp.arange(nk)[None, None, :, None] < n_full[None, :, None, None])
    first = jnp.argmin(skip, axis=2)
    return first.transpose(0, 2, 1).astype(jnp.int32)


def _fox_attention(qkv, fbias, bq=FOX_BQ, bk=FOX_BK):
    b, s, _ = qkv.shape
    pairs = HEADS // 2
    cb = WIDTH // LANES
    first = _fox_first_block(qkv, fbias, bq, bk)
    grid_spec = pltpu.PrefetchScalarGridSpec(
        num_scalar_prefetch=1,
        grid=(b, pairs, s // bq),
        in_specs=[pl.BlockSpec((1, bq, LANES), lambda bi, hp, i, fr: (bi, i, hp)),
                  pl.BlockSpec((1, s, LANES), lambda bi, hp, i, fr: (bi, 0, cb + hp)),
                  pl.BlockSpec((1, s, LANES), lambda bi, hp, i, fr: (bi, 0, 2 * cb + hp)),
                  pl.BlockSpec((1, s, LANES), lambda bi, hp, i, fr: (bi, 0, hp))],
        out_specs=pl.BlockSpec((1, bq, LANES), lambda bi, hp, i, fr: (bi, i, hp)),
        scratch_shapes=[pltpu.VMEM((2 * bq, LANES), F32), pltpu.VMEM((2 * bq, LANES), F32)],
    )
    return pl.pallas_call(
        functools.partial(_fox_kernel, bq=bq, bk=bk),
        out_shape=jax.ShapeDtypeStruct((b, s, WIDTH), BF16),
        grid_spec=grid_spec,
        compiler_params=_cparams(("arbitrary", "arbitrary", "arbitrary")),
        name="fox_attention",
    )(first, qkv, qkv, qkv, fbias)


PAIR = 2 * HEAD_DIM
GROUPS = WIDTH // PAIR


def _group(x, g):
    return x[:, g * PAIR:(g + 1) * PAIR]


def _head_sum(x, bd2):
    return jnp.concatenate([_dot(_group(x, g), bd2) for g in range(GROUPS)], axis=1)


def _head_apply(mats, x, lane_first):
    rows = mats.shape[1]
    outs = []
    for g in range(GROUPS):
        res = _dot(mats[2 * g:2 * g + 2].reshape(2 * rows, rows), _group(x, g))
        outs.append(jnp.where(lane_first, res[:rows], res[rows:]))
    return jnp.concatenate(outs, axis=1)


def _rwkv_kernel(p_ref, mu_ref, w0_ref, a0_ref, kk_ref, ka_ref, rk_ref, lng_ref, lnb_ref,
                 wwa_ref, g2_ref, bd_ref, o_ref, st_ref, prev_ref):
    L = CHUNK

    @pl.when(pl.program_id(1) == 0)
    def _():
        st_ref[...] = jnp.zeros_like(st_ref)
        prev_ref[...] = jnp.zeros_like(prev_ref)

    p = p_ref[0]
    T = p.shape[0]
    rowi = lax.broadcasted_iota(jnp.int32, p.shape, 0)
    prev = jnp.where(rowi == 0, prev_ref[...], pltpu.roll(p, 1, 0))
    prev_ref[...] = p[T - 1:T, :]
    ps = p + (prev - p) * mu_ref[...]
    r = ps[:, 0:WIDTH]
    k = ps[:, WIDTH:2 * WIDTH]
    v = ps[:, 2 * WIDTH:3 * WIDTH]
    wa_in = ps[:, 3 * WIDTH:3 * WIDTH + DECAY_LORA + AAA_LORA]
    gd = ps[:, 3 * WIDTH + DECAY_LORA + AAA_LORA:]
    lane_wa = lax.broadcasted_iota(jnp.int32, wa_in.shape, 1)
    wa_act = jnp.where(lane_wa < DECAY_LORA, jnp.tanh(wa_in), wa_in)
    wa = _dot(wa_act, wwa_ref[...])
    log_w = -_softplus(-(w0_ref[...] + wa[:, :WIDTH])) - 0.5
    lw = -jnp.exp(log_w)
    a = jax.nn.sigmoid(a0_ref[...] + wa[:, WIDTH:])
    out_gate = _dot(jax.nn.sigmoid(gd), g2_ref[...])
    bd = bd_ref[...]
    kk0 = k * kk_ref[...]
    kk = kk0 * lax.rsqrt(jnp.maximum(_head_sum(kk0 * kk0, bd), 1e-24))
    k2 = k * (1.0 + (a - 1.0) * ka_ref[...])
    av = -kk
    bv = kk * a

    n_sub = T // L
    rt_i = lax.broadcasted_iota(jnp.int32, (T, T), 0)
    ct_i = lax.broadcasted_iota(jnp.int32, (T, T), 1)
    tri_tile = (rt_i >= ct_i) & (rt_i // L == ct_i // L)
    cl = _split_dot_left(jnp.where(tri_tile, 1.0, 0.0).astype(BF16), lw)
    cl_end = jnp.concatenate([jnp.broadcast_to(cl[(c + 1) * L - 1:(c + 1) * L, :], (L, WIDTH))
                              for c in range(n_sub)], axis=0)
    at_all = av * jnp.exp(cl - lw)
    rt_all = r * jnp.exp(cl)
    einv = jnp.exp(-cl)
    bt_all = bv * einv
    kt_all = k2 * einv
    edec = jnp.exp(cl_end - cl)
    b_end_all = bv * edec
    k_end_all = k2 * edec

    ri = lax.broadcasted_iota(jnp.int32, (L, L), 0)
    ci = lax.broadcasted_iota(jnp.int32, (L, L), 1)
    tri_incl = ri >= ci
    tri_strict = ri > ci
    eye = jnp.where(ri == ci, 1.0, 0.0)
    lane_first = lax.broadcasted_iota(jnp.int32, (L, PAIR), 1) < HEAD_DIM
    qr = lax.broadcasted_iota(jnp.int32, (PAIR, PAIR), 0) < HEAD_DIM
    qc = lax.broadcasted_iota(jnp.int32, (PAIR, PAIR), 1) < HEAD_DIM
    same_head = qr == qc

    def bmm(x, y):
        return lax.dot_general(x.astype(BF16), y.astype(BF16), (((2,), (1,)), ((0,), (0,))),
                               preferred_element_type=F32)

    def chunk_terms(c):
        rows = slice(c * L, (c + 1) * L)
        at, rt, bt, kt, vc = at_all[rows], rt_all[rows], bt_all[rows], kt_all[rows], v[rows]
        sb_heads, sk_heads = [], []
        for g in range(GROUPS):
            at_g, rt_g = _group(at, g), _group(rt, g)
            lhs = jnp.concatenate([jnp.where(lane_first, at_g, 0.0), jnp.where(lane_first, rt_g, 0.0),
                                   jnp.where(lane_first, 0.0, at_g), jnp.where(lane_first, 0.0, rt_g)],
                                  axis=0).astype(BF16)
            sb_g = _dot_nt(lhs, _group(bt, g))
            sk_g = _dot_nt(lhs, _group(kt, g))
            for hh in range(2):
                sb_heads.append(sb_g[hh * 2 * L:(hh + 1) * 2 * L])
                sk_heads.append(sk_g[hh * 2 * L:(hh + 1) * 2 * L])
        sb = jnp.stack(sb_heads)
        sk = jnp.stack(sk_heads)
        n_ab = jnp.where(tri_strict, sb[:, :L, :], 0.0)
        a_ak = jnp.where(tri_strict, sk[:, :L, :], 0.0)
        a_rb = jnp.where(tri_incl, sb[:, L:, :], 0.0)
        a_rk = jnp.where(tri_incl, sk[:, L:, :], 0.0)
        tinv = eye + n_ab
        pw = bmm(n_ab, n_ab)
        span = 2
        while 2 * span < L:
            both = bmm(jnp.concatenate([tinv, pw], axis=1), pw)
            tinv = tinv + both[:, :L, :]
            pw = both[:, L:, :]
            span *= 2
        tinv = tinv + bmm(tinv, pw)
        av_term = _head_apply(a_ak, vc, lane_first)
        pm = _head_apply(tinv, at, lane_first)
        qm = _head_apply(tinv, av_term, lane_first)
        rkv = _head_apply(a_rk, vc, lane_first)
        return pm, qm, rkv, a_rb

    terms = [chunk_terms(c) for c in range(n_sub)]

    y_chunks = []
    for c in range(n_sub):
        rows = slice(c * L, (c + 1) * L)
        pm, qm, rkv, a_rb = terms[c]
        rt, vc, b_end, k_end = rt_all[rows], v[rows], b_end_all[rows], k_end_all[rows]
        gam_last = jnp.exp(cl[(c + 1) * L - 1:(c + 1) * L, :])
        u_parts, ys_parts = [], []
        for g in range(GROUPS):
            pr = _dot_nt(jnp.concatenate([_group(pm, g), _group(rt, g)], axis=0), st_ref[g])
            u_parts.append(pr[:L] + _group(qm, g))
            ys_parts.append(pr[L:])
        u = jnp.concatenate(u_parts, axis=1)
        y_chunks.append(jnp.concatenate(ys_parts, axis=1) + _head_apply(a_rb, u, lane_first) + rkv)
        for g in range(GROUPS):
            upd = _dot_tn(_group(u, g), _group(b_end, g)) + _dot_tn(_group(vc, g), _group(k_end, g))
            st_ref[g] = st_ref[g] * _group(gam_last, g) + jnp.where(same_head, upd, 0.0)
    y = jnp.concatenate(y_chunks, axis=0)

    inv_n = 1.0 / HEAD_DIM
    mean = _head_sum(y, bd) * inv_n
    dlt = y - mean
    var = _head_sum(dlt * dlt, bd) * inv_n
    yn = dlt * lax.rsqrt(var + LN_X_EPS) * lng_ref[...] + lnb_ref[...]
    bonus = _head_sum(r * k2 * rk_ref[...], bd) * v
    o_ref[0] = ((yn + bonus) * out_gate).astype(o_ref.dtype)


def _split_dot_left(w_bf16, x):
    hi = x.astype(BF16)
    r1 = x - hi.astype(F32)
    mid = r1.astype(BF16)
    lo = (r1 - mid.astype(F32)).astype(BF16)
    return (jnp.dot(w_bf16, hi, preferred_element_type=F32)
            + jnp.dot(w_bf16, mid, preferred_element_type=F32)
            + jnp.dot(w_bf16, lo, preferred_element_type=F32))


def _rwkv_branch(p_rw, mu, w0, w2, a0, a2, g2, k_k, k_a, r_k, ln_g, ln_b):
    b, s, sw = p_rw.shape
    row = lambda t: t.reshape(1, -1).astype(F32)
    wwa = jnp.zeros((DECAY_LORA + AAA_LORA, 2 * WIDTH), F32)
    wwa = wwa.at[:DECAY_LORA, :WIDTH].set(w2).at[DECAY_LORA:, WIDTH:].set(a2).astype(BF16)
    hid = jnp.arange(PAIR) // HEAD_DIM
    bd = (hid[:, None] == hid[None, :]).astype(BF16)
    const = lambda shape: pl.BlockSpec(shape, lambda i, j: (0,) * len(shape))
    return pl.pallas_call(
        _rwkv_kernel,
        out_shape=jax.ShapeDtypeStruct((b, s, WIDTH), BF16),
        grid=(b, s // RWKV_TILE),
        in_specs=[pl.BlockSpec((1, RWKV_TILE, sw), lambda i, j: (i, j, 0)),
                  const((1, sw)), const((1, WIDTH)), const((1, WIDTH)), const((1, WIDTH)),
                  const((1, WIDTH)), const((1, WIDTH)), const((1, WIDTH)), const((1, WIDTH)),
                  const((DECAY_LORA + AAA_LORA, 2 * WIDTH)), const((GATE_LORA, WIDTH)),
                  const((PAIR, PAIR))],
        out_specs=pl.BlockSpec((1, RWKV_TILE, WIDTH), lambda i, j: (i, j, 0)),
        scratch_shapes=[pltpu.VMEM((GROUPS, PAIR, PAIR), F32), pltpu.VMEM((1, sw), F32)],
        compiler_params=_cparams(("arbitrary", "arbitrary")),
        name="rwkv7_scan",
    )(p_rw, row(mu), row(w0), row(a0), row(k_k), row(k_a), row(r_k), row(ln_g), row(ln_b),
      wwa, g2.astype(BF16), bd)


def _out_kernel(of_ref, orw_ref, gate_ref, x_ref, mod_ref, wof_ref, wor_ref, wo_ref, n2g_ref,
                wr_ref, br_ref, x1_ref, h2_ref, route_ref, counts_ref, cnt_ref):
    d = x_ref.shape[-1]
    gate = gate_ref[...].astype(F32)
    merged = (gate[:, :d] * jnp.dot(of_ref[...], wof_ref[...], preferred_element_type=F32)
              + gate[:, d:] * jnp.dot(orw_ref[...], wor_ref[...], preferred_element_type=F32))
    gate1 = mod_ref[0, 2:3, :]
    shift2 = mod_ref[0, 3:4, :]
    scale2 = mod_ref[0, 4:5, :]
    x1 = x_ref[...] + gate1 * jnp.dot(merged.astype(BF16), wo_ref[...], preferred_element_type=F32)
    x1_ref[...] = x1
    ms = jnp.mean(x1 * x1, axis=-1, keepdims=True)
    h2 = x1 * lax.rsqrt(ms + NORM_EPS) * n2g_ref[...] * (1.0 + scale2) + shift2
    h2_ref[...] = _pack_pairs(h2)

    h2_hi = h2.astype(BF16)
    h2_lo = (h2 - h2_hi.astype(F32)).astype(BF16)
    logits = (jnp.dot(h2_hi, wr_ref[0], preferred_element_type=F32)
              + jnp.dot(h2_lo, wr_ref[0], preferred_element_type=F32)
              + jnp.dot(h2_hi, wr_ref[1], preferred_element_type=F32)) + br_ref[...]
    lane = lax.broadcasted_iota(jnp.int32, logits.shape, 1)
    neg = -jnp.inf
    big = jnp.int32(LANES)
    gl = jnp.where(lane < N_GROUPS, logits, neg)
    gmax = jnp.max(gl, axis=1, keepdims=True)
    gidx = jnp.min(jnp.where(gl == gmax, lane, big), axis=1, keepdims=True)
    g_p = 1.0 / jnp.sum(jnp.exp(gl - gmax), axis=1, keepdims=True)
    e_lane = lane - N_GROUPS
    in_grp = (e_lane >= 0) & (e_lane < N_EXPERTS) & ((e_lane // EXPERTS_PER_GROUP) == gidx)
    sel = jnp.where(in_grp, logits, neg)
    m1 = jnp.max(sel, axis=1, keepdims=True)
    i1 = jnp.min(jnp.where(sel == m1, lane, big), axis=1, keepdims=True)
    sel2 = jnp.where(lane == i1, neg, sel)
    m2 = jnp.max(sel2, axis=1, keepdims=True)
    i2 = jnp.min(jnp.where(sel2 == m2, lane, big), axis=1, keepdims=True)
    e21 = jnp.exp(m2 - m1)
    w_first = g_p / (1.0 + e21)
    w_second = g_p * e21 / (1.0 + e21)
    @pl.when(pl.program_id(0) == 0)
    def _():
        cnt_ref[...] = jnp.zeros_like(cnt_ref)

    tm = logits.shape[0]
    oh1 = lane == i1
    oh2 = lane == i2
    both = jnp.where(oh1 | oh2, 1.0, 0.0)
    before = (lax.broadcasted_iota(jnp.int32, (tm, tm), 0)
              > lax.broadcasted_iota(jnp.int32, (tm, tm), 1))
    seen = jnp.dot(jnp.where(before, 1.0, 0.0).astype(BF16), both.astype(BF16),
                   preferred_element_type=F32) + cnt_ref[...]
    rank1 = jnp.sum(jnp.where(oh1, seen, 0.0), axis=1, keepdims=True)
    rank2 = jnp.sum(jnp.where(oh2, seen, 0.0), axis=1, keepdims=True)
    cnt_ref[...] = cnt_ref[...] + jnp.sum(both, axis=0, keepdims=True)
    counts_ref[...] = jnp.broadcast_to(cnt_ref[...], counts_ref.shape)

    route = jnp.where(lane == 0, (i1 - N_GROUPS).astype(F32),
                      jnp.where(lane == 1, (i2 - N_GROUPS).astype(F32),
                                jnp.where(lane == 2, w_first,
                                          jnp.where(lane == 3, w_second,
                                                    jnp.where(lane == 4, rank1,
                                                              jnp.where(lane == 5, rank2, 0.0))))))
    route_ref[...] = route


def _merge_out_router(o_fox, o_rw, gate, x, mod, wof, wor, wo, n2g, wr, br, tm=512):
    b, s, d = x.shape
    t = b * s
    spb = s // tm
    rowspec = lambda w: pl.BlockSpec((tm, w), lambda i: (i, 0))
    const = lambda shape: pl.BlockSpec(shape, lambda i: (0,) * len(shape))
    return pl.pallas_call(
        _out_kernel,
        out_shape=(jax.ShapeDtypeStruct((t, d), F32), jax.ShapeDtypeStruct((t, d // 2), jnp.uint32),
                   jax.ShapeDtypeStruct((t, LANES), F32), jax.ShapeDtypeStruct((8, LANES), F32)),
        grid=(t // tm,),
        in_specs=[rowspec(WIDTH), rowspec(WIDTH), rowspec(2 * d), rowspec(d),
                  pl.BlockSpec((1, 6, d), lambda i: (i // spb, 0, 0)),
                  const((WIDTH, d)), const((WIDTH, d)), const((d, d)), const((1, d)),
                  const((2, d, LANES)), const((1, LANES))],
        out_specs=(rowspec(d), rowspec(d // 2), rowspec(LANES), const((8, LANES))),
        scratch_shapes=[pltpu.VMEM((1, LANES), F32)],
        compiler_params=_cparams(("arbitrary",)),
        name="merge_out_router",
    )(o_fox, o_rw, gate, x.reshape(t, d), mod, wof, wor, wo, n2g.reshape(1, d), wr, br)


SC_CORES = 2
SC_SUBCORES = 16
SC_CHUNK = 128


def _sc_scatter_rows(src, idx, n_rows):
    t, width = src.shape
    per_worker = t // (SC_CORES * SC_SUBCORES)
    n_chunks = per_worker // SC_CHUNK
    assert n_chunks * SC_CHUNK * SC_CORES * SC_SUBCORES == t and idx.shape[0] == TOP_K * t
    mesh = plsc.VectorSubcoreMesh(core_axis_name="c", subcore_axis_name="s")

    @functools.partial(
        pl.kernel, mesh=mesh,
        out_type=jax.ShapeDtypeStruct((n_rows, width), src.dtype),
        scratch_types=[pltpu.VMEM((SC_CHUNK,), jnp.int32) for _ in range(TOP_K)]
        + [pltpu.VMEM((SC_CHUNK, width), src.dtype), pltpu.SemaphoreType.DMA])
    def scatter(src_hbm, idx_hbm, out_hbm, *scratch):
        idx_v, rows_v, sem = scratch[:TOP_K], scratch[TOP_K], scratch[TOP_K + 1]
        worker = lax.axis_index("s") * SC_CORES + lax.axis_index("c")

        @pl.loop(0, n_chunks)
        def _(ci):
            off = pl.multiple_of(worker * per_worker + ci * SC_CHUNK, SC_CHUNK)
            pltpu.sync_copy(src_hbm.at[pl.ds(off, SC_CHUNK)], rows_v)
            for kk in range(TOP_K):
                pltpu.sync_copy(idx_hbm.at[pl.ds(kk * t + off, SC_CHUNK)], idx_v[kk])
            for kk in range(TOP_K):
                pltpu.async_copy(rows_v, out_hbm.at[idx_v[kk]], sem).wait()

    return scatter(src, idx)


def _expert_kernel(blk_e_ref, nused_ref, valid_ref, xs_ref, w1_ref, w3_ref, w2_ref, o_ref,
                   w1b_ref, w3b_ref, w2b_ref):
    i = pl.program_id(0)
    live = i * MOE_ROWS < nused_ref[0]
    new_expert = jnp.logical_or(i == 0, blk_e_ref[i] != blk_e_ref[jnp.maximum(i - 1, 0)])

    @pl.when(new_expert)
    def _():
        w1b_ref[...] = w1_ref[0].astype(BF16)
        w3b_ref[...] = w3_ref[0].astype(BF16)
        w2b_ref[...] = w2_ref[0].astype(BF16)

    @pl.when(live)
    def _():
        row = lax.broadcasted_iota(jnp.int32, xs_ref.shape, 0)
        xs = jnp.where(row < valid_ref[i], xs_ref[...], jnp.uint32(0))
        xb = _unpack_pairs(xs).astype(BF16)
        h1 = jnp.dot(xb, w1b_ref[...], preferred_element_type=F32)
        h3 = jnp.dot(xb, w3b_ref[...], preferred_element_type=F32)
        hh = (h1 * jax.nn.sigmoid(h1)) * h3
        o_ref[...] = _pack_pairs(jnp.dot(hh.astype(BF16), w2b_ref[...], preferred_element_type=F32))

    @pl.when(jnp.logical_not(live))
    def _():
        o_ref[...] = jnp.zeros_like(o_ref)


def _moe_experts(xs, blk_e, nused, blk_valid, w1, w3, w2):
    rows, dp = xs.shape
    _, d, de = w1.shape
    grid_spec = pltpu.PrefetchScalarGridSpec(
        num_scalar_prefetch=3,
        grid=(rows // MOE_ROWS,),
        in_specs=[pl.BlockSpec((MOE_ROWS, dp), lambda i, be, nu, va: (i, 0)),
                  pl.BlockSpec((1, d, de), lambda i, be, nu, va: (be[i], 0, 0)),
                  pl.BlockSpec((1, d, de), lambda i, be, nu, va: (be[i], 0, 0)),
                  pl.BlockSpec((1, de, d), lambda i, be, nu, va: (be[i], 0, 0))],
        out_specs=pl.BlockSpec((MOE_ROWS, dp), lambda i, be, nu, va: (i, 0)),
        scratch_shapes=[pltpu.VMEM((d, de), BF16), pltpu.VMEM((d, de), BF16), pltpu.VMEM((de, d), BF16)],
    )
    return pl.pallas_call(
        _expert_kernel,
        out_shape=jax.ShapeDtypeStruct((rows, dp), xs.dtype),
        grid_spec=grid_spec,
        compiler_params=_cparams(("arbitrary",)),
        name="moe_experts",
    )(blk_e, nused, blk_valid, xs, w1, w3, w2)


def _sc_gather_rows(table, idx):
    n_idx = idx.shape[0]
    width = table.shape[1]
    per_worker = n_idx // (SC_CORES * SC_SUBCORES)
    n_chunks = per_worker // SC_CHUNK
    assert n_chunks * SC_CHUNK * SC_CORES * SC_SUBCORES == n_idx
    mesh = plsc.VectorSubcoreMesh(core_axis_name="c", subcore_axis_name="s")

    @functools.partial(
        pl.kernel, mesh=mesh,
        out_type=jax.ShapeDtypeStruct((n_idx, width), table.dtype),
        scratch_types=[pltpu.VMEM((SC_CHUNK,), jnp.int32), pltpu.VMEM((SC_CHUNK, width), table.dtype),
                       pltpu.SemaphoreType.DMA])
    def gather(table_hbm, idx_hbm, out_hbm, idx_v, rows_v, sem):
        worker = lax.axis_index("s") * SC_CORES + lax.axis_index("c")

        @pl.loop(0, n_chunks)
        def _(ci):
            off = pl.multiple_of(worker * per_worker + ci * SC_CHUNK, SC_CHUNK)
            pltpu.sync_copy(idx_hbm.at[pl.ds(off, SC_CHUNK)], idx_v)
            pltpu.async_copy(table_hbm.at[idx_v], rows_v, sem).wait()
            pltpu.sync_copy(rows_v, out_hbm.at[pl.ds(off, SC_CHUNK)])

    return gather(table, idx)


def _final_kernel(route_ref, x1_ref, mod_ref, fg_ref, y0_ref, y1_ref, o_ref):
    route = route_ref[...]
    y = route[:, 2:3] * _unpack_pairs(y0_ref[...]) + route[:, 3:4] * _unpack_pairs(y1_ref[...])
    gate2 = mod_ref[0, 5:6, :]
    x2 = x1_ref[...] + gate2 * y
    ms = jnp.mean(x2 * x2, axis=-1, keepdims=True)
    o_ref[...] = x2 * lax.rsqrt(ms + NORM_EPS) * fg_ref[...]


def _moe_combine_final(dest_slots, route, x1, mod, final_g, ys, s, tm=512):
    t, d = x1.shape
    spb = s // tm
    picked = _sc_gather_rows(ys, dest_slots)
    tiles = t // tm
    return pl.pallas_call(
        _final_kernel,
        out_shape=jax.ShapeDtypeStruct((t, d), F32),
        grid=(tiles,),
        in_specs=[pl.BlockSpec((tm, LANES), lambda i: (i, 0)),
                  pl.BlockSpec((tm, d), lambda i: (i, 0)),
                  pl.BlockSpec((1, 6, d), lambda i: (i // spb, 0, 0)),
                  pl.BlockSpec((1, d), lambda i: (0, 0)),
                  pl.BlockSpec((tm, picked.shape[1]), lambda i: (i, 0)),
                  pl.BlockSpec((tm, picked.shape[1]), lambda i: (tiles + i, 0))],
        out_specs=pl.BlockSpec((tm, d), lambda i: (i, 0)),
        compiler_params=_cparams(("arbitrary",)),
        name="moe_combine_final",
    )(route, x1, mod, final_g.reshape(1, d), picked, picked)


def _moe_plan(route, counts):
    t = route.shape[0]
    m = t * TOP_K
    flat_e = route[:, :TOP_K].astype(jnp.int32).T.reshape(m)
    rank = route[:, 4:4 + TOP_K].astype(jnp.int32).T.reshape(m)
    counts = counts[0, N_GROUPS:N_GROUPS + N_EXPERTS].astype(jnp.int32)
    padded = (counts + MOE_ROWS - 1) // MOE_ROWS * MOE_ROWS
    pad_end = jnp.cumsum(padded)
    pad_start = pad_end - padded
    experts = jnp.arange(N_EXPERTS, dtype=jnp.int32)
    start_of = jnp.sum(jnp.where(flat_e[:, None] == experts[None, :], pad_start[None, :], 0), axis=1)
    dest = (start_of + rank).astype(jnp.int32)
    n_blocks = m // MOE_ROWS + N_EXPERTS
    blk_start = jnp.arange(n_blocks, dtype=jnp.int32) * MOE_ROWS
    blk_e = jnp.minimum(jnp.sum(pad_end[None, :] <= blk_start[:, None], axis=1), N_EXPERTS - 1).astype(jnp.int32)
    nused = pad_end[-1:].astype(jnp.int32)
    blk_valid = jnp.clip(counts[blk_e] - (blk_start - pad_start[blk_e]), 0, MOE_ROWS).astype(jnp.int32)
    return dest, blk_e, nused, blk_valid, n_blocks * MOE_ROWS


def kernel(x, c, ada_w, ada_b, norm1_g, w_in, fox_forget_b, shift_mu, rwkv_w0, rwkv_w2, rwkv_a0, rwkv_a2, rwkv_g2, rwkv_k_k, rwkv_k_a, rwkv_r_k, ln_x_g, ln_x_b, w_out_fox, w_out_rwkv, w_o, norm2_g, router_group_w, router_group_b, router_expert_w, router_expert_b, exp_w1, exp_w3, exp_w2, final_g):
    b, s, d = x.shape
    t = b * s
    assert ada_w.shape[0] == 1, "the final norm is fused into the last layer's combine; one layer is laid out"
    for l in range(1):
        mod = _adaln_mod(c, ada_w[l], ada_b[l])

        h = _norm_mod(x, norm1_g[l], mod, shift_idx=0, scale_idx=1)
        h2d = h.reshape(t, d)
        w = w_in[l]
        o_f = 3 * WIDTH
        o_rw = o_f + HEADS
        o_g = o_rw + SHIFT_WIDTH
        perm = jnp.argsort(fox_forget_b[l])
        by_head = lambda m: m.reshape(d, HEADS, HEAD_DIM)[:, perm]
        wq = by_head(w[:, :WIDTH]).reshape(d, WIDTH)
        wk = by_head(w[:, WIDTH:2 * WIDTH]).reshape(d, WIDTH)
        wv = by_head(w[:, 2 * WIDTH:o_f]).reshape(d, WIDTH)
        qkv = _matmul(h2d, jnp.concatenate([wq, wk, wv], axis=1).astype(BF16), BF16, name="proj_qkv")
        p_rw = _matmul(h2d, w[:, o_rw:o_g].astype(BF16), F32, tn=896, name="proj_rwkv")
        gate = _matmul(h2d, w[:, o_g:].astype(BF16), BF16, act="sigmoid", name="proj_gate")
        fbias = _forget_bias(h, w[:, o_f:o_rw][:, perm], fox_forget_b[l][perm])
        o_fox = _fox_attention(qkv.reshape(b, s, 3 * WIDTH), fbias)
        w_of = w_out_fox[l].reshape(HEADS, HEAD_DIM, d)[perm].reshape(WIDTH, d)
        o_rwkv = _rwkv_branch(p_rw.reshape(b, s, SHIFT_WIDTH), shift_mu[l], rwkv_w0[l], rwkv_w2[l],
                              rwkv_a0[l], rwkv_a2[l], rwkv_g2[l], rwkv_k_k[l], rwkv_k_a[l],
                              rwkv_r_k[l], ln_x_g[l], ln_x_b[l])

        wr = jnp.zeros((d, LANES), F32)
        wr = wr.at[:, :N_GROUPS].set(router_group_w[l]).at[:, N_GROUPS:N_GROUPS + N_EXPERTS].set(router_expert_w[l])
        br = jnp.zeros((1, LANES), F32)
        br = br.at[0, :N_GROUPS].set(router_group_b[l]).at[0, N_GROUPS:N_GROUPS + N_EXPERTS].set(router_expert_b[l])
        wr_hi = wr.astype(BF16)
        wr_lo = (wr - wr_hi.astype(F32)).astype(BF16)
        x1, h2, route, counts = _merge_out_router(
            o_fox.reshape(t, WIDTH), o_rwkv.reshape(t, WIDTH), gate, x, mod,
            w_of.astype(BF16), w_out_rwkv[l].astype(BF16), w_o[l].astype(BF16), norm2_g[l],
            jnp.stack([wr_hi, wr_lo]), br)

        dest, blk_e, nused, blk_valid, rows = _moe_plan(route, counts)
        xs = _sc_scatter_rows(h2, dest, rows)
        ys = _moe_experts(xs, blk_e, nused, blk_valid, exp_w1[l], exp_w3[l], exp_w2[l])
        out = _moe_combine_final(dest, route, x1, mod, final_g, ys, s)
    return out.reshape(b, s, d)
```

```python
import functools

import jax
import jax.numpy as jnp
from jax import lax
from jax.experimental import pallas as pl
from jax.experimental.pallas import tpu as pltpu
from jax.experimental.pallas import tpu_sc as plsc

F32 = jnp.float32
BF16 = jnp.bfloat16
HIGHEST = lax.Precision.HIGHEST

HEADS = 8
HEAD_DIM = 64
WIDTH = HEADS * HEAD_DIM
DECAY_LORA = 64
AAA_LORA = 64
GATE_LORA = 128
SHIFT_WIDTH = 3 * WIDTH + DECAY_LORA + AAA_LORA + GATE_LORA
LN_X_EPS = 64e-5
NORM_EPS = 1e-6
N_GROUPS = 4
EXPERTS_PER_GROUP = 8
N_EXPERTS = N_GROUPS * EXPERTS_PER_GROUP
TOP_K = 2

LANES = 128
CHUNK = 64
RWKV_TILE = 256
MOE_ROWS = 512
VMEM_LIMIT = 48 * 1024 * 1024


def _cparams(sem):
    return pltpu.CompilerParams(dimension_semantics=sem, vmem_limit_bytes=VMEM_LIMIT)


def _dot(a, b):
    return jnp.dot(a.astype(BF16), b.astype(BF16), preferred_element_type=F32)


def _dot_nt(a, b):
    return lax.dot_general(a.astype(BF16), b.astype(BF16), (((1,), (1,)), ((), ())),
                           preferred_element_type=F32)


def _dot_tn(a, b):
    return lax.dot_general(a.astype(BF16), b.astype(BF16), (((0,), (0,)), ((), ())),
                           preferred_element_type=F32)


def _pack_pairs(x):
    n = x.shape[1] // 2
    lo = lax.bitcast_convert_type(x[:, :n].astype(BF16).astype(F32), jnp.uint32) >> 16
    hi = lax.bitcast_convert_type(x[:, n:].astype(BF16).astype(F32), jnp.uint32) & jnp.uint32(0xFFFF0000)
    return lo | hi


def _unpack_pairs(u):
    lo = lax.bitcast_convert_type(u << 16, F32)
    hi = lax.bitcast_convert_type(u & jnp.uint32(0xFFFF0000), F32)
    return jnp.concatenate([lo, hi], axis=1)


def _softplus(x):
    return jnp.maximum(x, 0.0) + jnp.log1p(jnp.exp(-jnp.abs(x)))


def _mod_kernel(c_ref, w_ref, b_ref, o_ref):
    c = c_ref[...]
    sc = c * jax.nn.sigmoid(c)
    o_ref[...] = jnp.dot(sc, w_ref[...], precision=HIGHEST, preferred_element_type=F32) + b_ref[...]


def _adaln_mod(c, ada_w, ada_b):
    b, d = c.shape
    n = ada_w.shape[1]
    rows = 8
    cp = jnp.zeros((rows, d), F32).at[:b].set(c)
    tn = 1024
    out = pl.pallas_call(
        _mod_kernel,
        out_shape=jax.ShapeDtypeStruct((rows, n), F32),
        grid=(n // tn,),
        in_specs=[pl.BlockSpec((rows, d), lambda j: (0, 0)),
                  pl.BlockSpec((d, tn), lambda j: (0, j)),
                  pl.BlockSpec((1, tn), lambda j: (0, j))],
        out_specs=pl.BlockSpec((rows, tn), lambda j: (0, j)),
        compiler_params=_cparams(("arbitrary",)),
        name="adaln_mod",
    )(cp, ada_w, ada_b.reshape(1, n))
    return out[:b].reshape(b, 6, d)


def _norm_mod_kernel(x_ref, g_ref, mod_ref, o_ref, *, shift_idx, scale_idx):
    x = x_ref[0]
    ms = jnp.mean(x * x, axis=-1, keepdims=True)
    y = x * lax.rsqrt(ms + NORM_EPS) * g_ref[...]
    scale = mod_ref[0, scale_idx:scale_idx + 1, :]
    shift = mod_ref[0, shift_idx:shift_idx + 1, :]
    o_ref[0] = (y * (1.0 + scale) + shift).astype(o_ref.dtype)


def _norm_mod(x, g, mod, shift_idx, scale_idx, tm=1024):
    b, s, d = x.shape
    return pl.pallas_call(
        functools.partial(_norm_mod_kernel, shift_idx=shift_idx, scale_idx=scale_idx),
        out_shape=jax.ShapeDtypeStruct((b, s, d), BF16),
        grid=(b, s // tm),
        in_specs=[pl.BlockSpec((1, tm, d), lambda i, j: (i, j, 0)),
                  pl.BlockSpec((1, d), lambda i, j: (0, 0)),
                  pl.BlockSpec((1, 6, d), lambda i, j: (i, 0, 0))],
        out_specs=pl.BlockSpec((1, tm, d), lambda i, j: (i, j, 0)),
        compiler_params=_cparams(("arbitrary", "arbitrary")),
        name="norm1_mod",
    )(x, g.reshape(1, d), mod)


def _mm_kernel(a_ref, w_ref, *rest, act, has_bias):
    o_ref = rest[-1]
    r = jnp.dot(a_ref[...], w_ref[...], preferred_element_type=F32)
    if has_bias:
        r = r + rest[0][...]
    if act == "sigmoid":
        r = 0.5 * jnp.tanh(0.5 * r) + 0.5
    o_ref[...] = r.astype(o_ref.dtype)


def _matmul(a, w, out_dtype, bias=None, act=None, tm=2048, tn=512, name="proj"):
    t, k = a.shape
    n = w.shape[1]
    tn = min(tn, n)
    in_specs = [pl.BlockSpec((tm, k), lambda i, j: (i, 0)),
                pl.BlockSpec((k, tn), lambda i, j: (0, j))]
    args = [a, w]
    if bias is not None:
        in_specs.append(pl.BlockSpec((1, tn), lambda i, j: (0, j)))
        args.append(bias)
    return pl.pallas_call(
        functools.partial(_mm_kernel, act=act, has_bias=bias is not None),
        out_shape=jax.ShapeDtypeStruct((t, n), out_dtype),
        grid=(t // tm, n // tn),
        in_specs=in_specs,
        out_specs=pl.BlockSpec((tm, tn), lambda i, j: (i, j)),
        compiler_params=_cparams(("arbitrary", "arbitrary")),
        name=name,
    )(*args)


LOG2E = 1.4426950408889634


def _split3(x):
    hi = x.astype(BF16)
    r1 = x - hi.astype(F32)
    mid = r1.astype(BF16)
    lo = (r1 - mid.astype(F32)).astype(BF16)
    return hi, mid, lo


def _fcum_kernel(h_ref, wf_ref, fb_ref, sel_ref, o_ref, carry_ref, *, ts):
    @pl.when(pl.program_id(0) == 0)
    def _():
        carry_ref[...] = jnp.zeros_like(carry_ref)

    ri = lax.broadcasted_iota(jnp.int32, (ts, ts), 0)
    ci = lax.broadcasted_iota(jnp.int32, (ts, ts), 1)
    tri = jnp.where(ri >= ci, 1.0, 0.0).astype(BF16)
    for bi in range(h_ref.shape[0]):
        f = jnp.dot(h_ref[bi], wf_ref[...], preferred_element_type=F32) + fb_ref[...]
        lf = -_softplus(-f)
        cum = carry_ref[bi]
        for piece in _split3(lf):
            cum = cum + jnp.dot(tri, piece, preferred_element_type=F32)
        carry_ref[bi] = cum[ts - 1:ts, :]
        out = jnp.zeros((ts, sel_ref.shape[2]), F32)
        for idx, piece in enumerate(_split3(cum * (-LOG2E))):
            out = out + jnp.dot(piece, sel_ref[idx], preferred_element_type=F32)
        o_ref[bi] = out.astype(o_ref.dtype)


def _forget_bias(h, wf, fb, ts=512):
    b, s, d = h.shape
    pairs = HEADS // 2
    wf_p = jnp.zeros((d, LANES), F32).at[:, :HEADS].set(wf).astype(BF16)
    fb_p = jnp.zeros((1, LANES), F32).at[0, :HEADS].set(fb)
    hh = jnp.arange(HEADS)
    sel = jnp.zeros((3, LANES, pairs * LANES), F32)
    for piece in range(3):
        sel = sel.at[piece, hh, (hh // 2) * LANES + (hh % 2) * 3 + piece].set(1.0)
    return pl.pallas_call(
        functools.partial(_fcum_kernel, ts=ts),
        out_shape=jax.ShapeDtypeStruct((b, s, pairs * LANES), BF16),
        grid=(s // ts,),
        in_specs=[pl.BlockSpec((b, ts, d), lambda j: (0, j, 0)),
                  pl.BlockSpec((d, LANES), lambda j: (0, 0)),
                  pl.BlockSpec((1, LANES), lambda j: (0, 0)),
                  pl.BlockSpec((3, LANES, pairs * LANES), lambda j: (0, 0, 0))],
        out_specs=pl.BlockSpec((b, ts, pairs * LANES), lambda j: (0, j, 0)),
        scratch_shapes=[pltpu.VMEM((b, 1, LANES), F32)],
        compiler_params=_cparams(("arbitrary",)),
        name="forget_bias",
    )(h, wf_p, fb_p, sel.astype(BF16))


FOX_BQ = 1024
FOX_BK = 512
FOX_UNROLL = 2


def _fox_kernel(first_ref, q_ref, k_ref, v_ref, a_ref, o_ref, m_ref, acc_ref, *, bq, bk):
    i = pl.program_id(2)
    lane = lax.broadcasted_iota(jnp.int32, (bq, LANES), 1)
    first = lane < HEAD_DIM
    qs = q_ref[0].astype(F32) * (HEAD_DIM ** -0.5 * LOG2E)
    aug0 = jnp.where(lane < 3, 1.0, 0.0)
    aug1 = jnp.where((lane >= 3) & (lane < 6), 1.0, 0.0)
    q01 = jnp.concatenate([jnp.concatenate([jnp.where(first, qs, 0.0), aug0], axis=1),
                           jnp.concatenate([jnp.where(first, 0.0, qs), aug1], axis=1)],
                          axis=0).astype(BF16)
    m_ref[...] = jnp.full_like(m_ref, -jnp.inf)
    acc_ref[...] = jnp.zeros_like(acc_ref)
    lane_k = lax.broadcasted_iota(jnp.int32, (bk, LANES), 1)
    keep_first = jnp.where(lane_k < HEAD_DIM, 1.0, 0.0).astype(BF16)
    keep_second = jnp.where(lane_k < HEAD_DIM, 0.0, 1.0).astype(BF16)

    def both_heads(x, r0):
        if r0 == 0:
            return x[...]
        return jnp.concatenate([x[r0:bq], x[bq + r0:2 * bq]], axis=0)

    def logits(j, r0=0):
        start = pl.multiple_of(j * bk, bk)
        kb = jnp.concatenate([k_ref[0, pl.ds(start, bk), :], a_ref[0, pl.ds(start, bk), :]], axis=1)
        return lax.dot_general(both_heads(q01, r0), kb, (((1,), (1,)), ((), ())),
                               preferred_element_type=F32)

    def consume(j, z, r0=0):
        nr = bq - r0
        start = pl.multiple_of(j * bk, bk)
        vb = v_ref[0, pl.ds(start, bk), :]
        m_prev = both_heads(m_ref, r0)
        m_new = jnp.maximum(m_prev, jnp.max(z, axis=1, keepdims=True))
        alpha = jnp.exp2(m_prev - m_new)
        p = jnp.exp2(z - jnp.concatenate([m_new] * (bk // LANES), axis=1)).astype(BF16)
        pv = jnp.concatenate(
            [jnp.dot(p[:nr], vb * keep_first + keep_second, preferred_element_type=F32),
             jnp.dot(p[nr:], vb * keep_second + keep_first, preferred_element_type=F32)], axis=0)
        acc_new = alpha * both_heads(acc_ref, r0) + pv
        if r0 == 0:
            acc_ref[...] = acc_new
            m_ref[...] = m_new
        else:
            for half, dst in ((slice(0, nr), slice(r0, bq)), (slice(nr, 2 * nr), slice(bq + r0, 2 * bq))):
                acc_ref[dst] = acc_new[half]
                m_ref[dst] = m_new[half]

    per_q = bq // bk
    n_full = i * per_q

    j_first = first_ref[pl.program_id(0), pl.program_id(1), i]
    left_over = (n_full - j_first) % FOX_UNROLL

    def single(step, carry):
        consume(j_first + step, logits(j_first + step))
        return carry

    lax.fori_loop(0, left_over, single, 0)

    def body(step, carry):
        j = j_first + left_over + FOX_UNROLL * step
        zs = [logits(j + u) for u in range(FOX_UNROLL)]
        for u in range(FOX_UNROLL):
            consume(j + u, zs[u])
        return carry

    lax.fori_loop(0, (n_full - j_first) // FOX_UNROLL, body, 0)
    for d in range(per_q):
        r0 = d * bk
        row = lax.broadcasted_iota(jnp.int32, (bq - r0, bk), 0)
        col = lax.broadcasted_iota(jnp.int32, (bq - r0, bk), 1)
        keep = col <= row
        z = logits(n_full + d, r0)
        consume(n_full + d, jnp.where(jnp.concatenate([keep, keep], axis=0), z, -jnp.inf), r0)
    acc = acc_ref[...]
    o = acc / pltpu.roll(acc, HEAD_DIM, 1)
    o_ref[0] = jnp.where(first, o[:bq], o[bq:]).astype(o_ref.dtype)


def _norm_bound_kernel(x_ref, sel_ref, o_ref):
    x = x_ref[...].astype(F32)
    ssq = _dot(x * x, sel_ref[...])
    dots = _dot(x[:, :WIDTH] * x[:, WIDTH:], sel_ref[0:WIDTH, :])
    sub = lax.broadcasted_iota(jnp.int32, o_ref.shape[1:], 0)
    o_ref[0] = jnp.where(sub == 0, jnp.max(ssq, axis=0, keepdims=True), jnp.min(dots, axis=0, keepdims=True))


FOX_ZERO_LOG2 = 136.0


def _fox_first_block(qkv, fbias, bq, bk):
    b, s, _ = qkv.shape
    t = b * s
    nq, nk = s // bq, s // bk
    n_heads2 = 2 * HEADS
    sel = (jnp.arange(2 * WIDTH)[:, None] // HEAD_DIM == jnp.arange(LANES)[None, :]).astype(BF16)
    bounds = pl.pallas_call(
        _norm_bound_kernel,
        out_shape=jax.ShapeDtypeStruct((t // bk, 8, LANES), F32),
        grid=(t // bk,),
        in_specs=[pl.BlockSpec((bk, 2 * WIDTH), lambda i: (i, 0)),
                  pl.BlockSpec((2 * WIDTH, LANES), lambda i: (0, 0))],
        out_specs=pl.BlockSpec((1, 8, LANES), lambda i: (i, 0, 0)),
        compiler_params=_cparams(("arbitrary",)),
        name="fox_norm_bounds",
    )(qkv.reshape(t, -1), sel)
    nrm = jnp.sqrt(bounds[:, 0, :n_heads2]).reshape(b, nk, n_heads2) * 1.01
    q_scale = HEAD_DIM ** -0.5 * LOG2E
    qn = nrm[..., :HEADS] * (q_scale * 1.01)
    kn = nrm[..., HEADS:]
    per_q = bq // bk
    qn_i = qn.reshape(b, nq, per_q, HEADS).max(axis=2)
    kn_i = kn.reshape(b, nq, per_q, HEADS).max(axis=2)
    kn_pre = lax.cummax(kn, axis=1)
    diag = bounds[:, 1, :HEADS].reshape(b, nq, per_q, HEADS).min(axis=2) * q_scale
    diag_low = diag - 0.02 * qn_i * kn_i
    pairs = HEADS // 2
    def bias_rows(rows):
        pieces = rows.astype(F32).reshape(b, -1, pairs, LANES)[..., :6].reshape(b, -1, pairs, 2, 3)
        return pieces.sum(-1).reshape(b, -1, HEADS)

    nb_end = bias_rows(fbias[:, bk - 1::bk])
    nb_start = bias_rows(fbias[:, ::bq])
    gap = nb_start[:, :, None, :] - nb_end[:, None, :, :]
    need = qn_i[:, :, None, :] * kn_pre[:, None, :, :] - diag_low[:, :, None, :] + FOX_ZERO_LOG2
    skip = (gap > need).reshape(b, nq, nk, pairs, 2).all(axis=-1)
    n_full = jnp.arange(nq) * per_q
    skip = skip & (jnp.arange(nk)[None, None, :, None] < n_full[None, :, None, None])
    first = jnp.argmin(skip, axis=2)
    return first.transpose(0, 2, 1).astype(jnp.int32)


def _fox_attention(qkv, fbias, bq=FOX_BQ, bk=FOX_BK):
    b, s, _ = qkv.shape
    pairs = HEADS // 2
    cb = WIDTH // LANES
    first = _fox_first_block(qkv, fbias, bq, bk)
    grid_spec = pltpu.PrefetchScalarGridSpec(
        num_scalar_prefetch=1,
        grid=(b, pairs, s // bq),
        in_specs=[pl.BlockSpec((1, bq, LANES), lambda bi, hp, i, fr: (bi, i, hp)),
                  pl.BlockSpec((1, s, LANES), lambda bi, hp, i, fr: (bi, 0, cb + hp)),
                  pl.BlockSpec((1, s, LANES), lambda bi, hp, i, fr: (bi, 0, 2 * cb + hp)),
                  pl.BlockSpec((1, s, LANES), lambda bi, hp, i, fr: (bi, 0, hp))],
        out_specs=pl.BlockSpec((1, bq, LANES), lambda bi, hp, i, fr: (bi, i, hp)),
        scratch_shapes=[pltpu.VMEM((2 * bq, LANES), F32), pltpu.VMEM((2 * bq, LANES), F32)],
    )
    return pl.pallas_call(
        functools.partial(_fox_kernel, bq=bq, bk=bk),
        out_shape=jax.ShapeDtypeStruct((b, s, WIDTH), BF16),
        grid_spec=grid_spec,
        compiler_params=_cparams(("arbitrary", "arbitrary", "arbitrary")),
        name="fox_attention",
    )(first, qkv, qkv, qkv, fbias)


PAIR = 2 * HEAD_DIM
GROUPS = WIDTH // PAIR


def _group(x, g):
    return x[:, g * PAIR:(g + 1) * PAIR]


def _head_sum(x, bd2):
    return jnp.concatenate([_dot(_group(x, g), bd2) for g in range(GROUPS)], axis=1)


def _head_apply(mats, x, lane_first):
    rows = mats.shape[1]
    outs = []
    for g in range(GROUPS):
        res = _dot(mats[2 * g:2 * g + 2].reshape(2 * rows, rows), _group(x, g))
        outs.append(jnp.where(lane_first, res[:rows], res[rows:]))
    return jnp.concatenate(outs, axis=1)


def _rwkv_kernel(p_ref, mu_ref, w0_ref, a0_ref, kk_ref, ka_ref, rk_ref, lng_ref, lnb_ref,
                 wwa_ref, g2_ref, bd_ref, o_ref, st_ref, prev_ref):
    L = CHUNK

    @pl.when(pl.program_id(1) == 0)
    def _():
        st_ref[...] = jnp.zeros_like(st_ref)
        prev_ref[...] = jnp.zeros_like(prev_ref)

    p = p_ref[0]
    T = p.shape[0]
    rowi = lax.broadcasted_iota(jnp.int32, p.shape, 0)
    prev = jnp.where(rowi == 0, prev_ref[...], pltpu.roll(p, 1, 0))
    prev_ref[...] = p[T - 1:T, :]
    ps = p + (prev - p) * mu_ref[...]
    r = ps[:, 0:WIDTH]
    k = ps[:, WIDTH:2 * WIDTH]
    v = ps[:, 2 * WIDTH:3 * WIDTH]
    wa_in = ps[:, 3 * WIDTH:3 * WIDTH + DECAY_LORA + AAA_LORA]
    gd = ps[:, 3 * WIDTH + DECAY_LORA + AAA_LORA:]
    lane_wa = lax.broadcasted_iota(jnp.int32, wa_in.shape, 1)
    wa_act = jnp.where(lane_wa < DECAY_LORA, jnp.tanh(wa_in), wa_in)
    wa = _dot(wa_act, wwa_ref[...])
    log_w = -_softplus(-(w0_ref[...] + wa[:, :WIDTH])) - 0.5
    lw = -jnp.exp(log_w)
    a = jax.nn.sigmoid(a0_ref[...] + wa[:, WIDTH:])
    out_gate = _dot(jax.nn.sigmoid(gd), g2_ref[...])
    bd = bd_ref[...]
    kk0 = k * kk_ref[...]
    kk = kk0 * lax.rsqrt(jnp.maximum(_head_sum(kk0 * kk0, bd), 1e-24))
    k2 = k * (1.0 + (a - 1.0) * ka_ref[...])
    av = -kk
    bv = kk * a

    n_sub = T // L
    rt_i = lax.broadcasted_iota(jnp.int32, (T, T), 0)
    ct_i = lax.broadcasted_iota(jnp.int32, (T, T), 1)
    tri_tile = (rt_i >= ct_i) & (rt_i // L == ct_i // L)
    cl = _split_dot_left(jnp.where(tri_tile, 1.0, 0.0).astype(BF16), lw)
    cl_end = jnp.concatenate([jnp.broadcast_to(cl[(c + 1) * L - 1:(c + 1) * L, :], (L, WIDTH))
                              for c in range(n_sub)], axis=0)
    at_all = av * jnp.exp(cl - lw)
    rt_all = r * jnp.exp(cl)
    einv = jnp.exp(-cl)
    bt_all = bv * einv
    kt_all = k2 * einv
    edec = jnp.exp(cl_end - cl)
    b_end_all = bv * edec
    k_end_all = k2 * edec

    ri = lax.broadcasted_iota(jnp.int32, (L, L), 0)
    ci = lax.broadcasted_iota(jnp.int32, (L, L), 1)
    tri_incl = ri >= ci
    tri_strict = ri > ci
    eye = jnp.where(ri == ci, 1.0, 0.0)
    lane_first = lax.broadcasted_iota(jnp.int32, (L, PAIR), 1) < HEAD_DIM
    qr = lax.broadcasted_iota(jnp.int32, (PAIR, PAIR), 0) < HEAD_DIM
    qc = lax.broadcasted_iota(jnp.int32, (PAIR, PAIR), 1) < HEAD_DIM
    same_head = qr == qc

    def bmm(x, y):
        return lax.dot_general(x.astype(BF16), y.astype(BF16), (((2,), (1,)), ((0,), (0,))),
                               preferred_element_type=F32)

    def chunk_terms(c):
        rows = slice(c * L, (c + 1) * L)
        at, rt, bt, kt, vc = at_all[rows], rt_all[rows], bt_all[rows], kt_all[rows], v[rows]
        sb_heads, sk_heads = [], []
        for g in range(GROUPS):
            at_g, rt_g = _group(at, g), _group(rt, g)
            lhs = jnp.concatenate([jnp.where(lane_first, at_g, 0.0), jnp.where(lane_first, rt_g, 0.0),
                                   jnp.where(lane_first, 0.0, at_g), jnp.where(lane_first, 0.0, rt_g)],
                                  axis=0).astype(BF16)
            sb_g = _dot_nt(lhs, _group(bt, g))
            sk_g = _dot_nt(lhs, _group(kt, g))
            for hh in range(2):
                sb_heads.append(sb_g[hh * 2 * L:(hh + 1) * 2 * L])
                sk_heads.append(sk_g[hh * 2 * L:(hh + 1) * 2 * L])
        sb = jnp.stack(sb_heads)
        sk = jnp.stack(sk_heads)
        n_ab = jnp.where(tri_strict, sb[:, :L, :], 0.0)
        a_ak = jnp.where(tri_strict, sk[:, :L, :], 0.0)
        a_rb = jnp.where(tri_incl, sb[:, L:, :], 0.0)
        a_rk = jnp.where(tri_incl, sk[:, L:, :], 0.0)
        tinv = eye + n_ab
        pw = bmm(n_ab, n_ab)
        span = 2
        while 2 * span < L:
            both = bmm(jnp.concatenate([tinv, pw], axis=1), pw)
            tinv = tinv + both[:, :L, :]
            pw = both[:, L:, :]
            span *= 2
        tinv = tinv + bmm(tinv, pw)
        av_term = _head_apply(a_ak, vc, lane_first)
        pm = _head_apply(tinv, at, lane_first)
        qm = _head_apply(tinv, av_term, lane_first)
        rkv = _head_apply(a_rk, vc, lane_first)
        return pm, qm, rkv, a_rb

    terms = [chunk_terms(c) for c in range(n_sub)]

    y_chunks = []
    for c in range(n_sub):
        rows = slice(c * L, (c + 1) * L)
        pm, qm, rkv, a_rb = terms[c]
        rt, vc, b_end, k_end = rt_all[rows], v[rows], b_end_all[rows], k_end_all[rows]
        gam_last = jnp.exp(cl[(c + 1) * L - 1:(c + 1) * L, :])
        u_parts, ys_parts = [], []
        for g in range(GROUPS):
            pr = _dot_nt(jnp.concatenate([_group(pm, g), _group(rt, g)], axis=0), st_ref[g])
            u_parts.append(pr[:L] + _group(qm, g))
            ys_parts.append(pr[L:])
        u = jnp.concatenate(u_parts, axis=1)
        y_chunks.append(jnp.concatenate(ys_parts, axis=1) + _head_apply(a_rb, u, lane_first) + rkv)
        for g in range(GROUPS):
            upd = _dot_tn(_group(u, g), _group(b_end, g)) + _dot_tn(_group(vc, g), _group(k_end, g))
            st_ref[g] = st_ref[g] * _group(gam_last, g) + jnp.where(same_head, upd, 0.0)
    y = jnp.concatenate(y_chunks, axis=0)

    inv_n = 1.0 / HEAD_DIM
    mean = _head_sum(y, bd) * inv_n
    dlt = y - mean
    var = _head_sum(dlt * dlt, bd) * inv_n
    yn = dlt * lax.rsqrt(var + LN_X_EPS) * lng_ref[...] + lnb_ref[...]
    bonus = _head_sum(r * k2 * rk_ref[...], bd) * v
    o_ref[0] = ((yn + bonus) * out_gate).astype(o_ref.dtype)


def _split_dot_left(w_bf16, x):
    hi, mid, lo = _split3(x)
    return (jnp.dot(w_bf16, hi, preferred_element_type=F32)
            + jnp.dot(w_bf16, mid, preferred_element_type=F32)
            + jnp.dot(w_bf16, lo, preferred_element_type=F32))


def _rwkv_branch(p_rw, mu, w0, w2, a0, a2, g2, k_k, k_a, r_k, ln_g, ln_b):
    b, s, sw = p_rw.shape
    row = lambda t: t.reshape(1, -1).astype(F32)
    wwa = jnp.zeros((DECAY_LORA + AAA_LORA, 2 * WIDTH), F32)
    wwa = wwa.at[:DECAY_LORA, :WIDTH].set(w2).at[DECAY_LORA:, WIDTH:].set(a2).astype(BF16)
    hid = jnp.arange(PAIR) // HEAD_DIM
    bd = (hid[:, None] == hid[None, :]).astype(BF16)
    const = lambda shape: pl.BlockSpec(shape, lambda i, j: (0,) * len(shape))
    return pl.pallas_call(
        _rwkv_kernel,
        out_shape=jax.ShapeDtypeStruct((b, s, WIDTH), BF16),
        grid=(b, s // RWKV_TILE),
        in_specs=[pl.BlockSpec((1, RWKV_TILE, sw), lambda i, j: (i, j, 0)),
                  const((1, sw)), const((1, WIDTH)), const((1, WIDTH)), const((1, WIDTH)),
                  const((1, WIDTH)), const((1, WIDTH)), const((1, WIDTH)), const((1, WIDTH)),
                  const((DECAY_LORA + AAA_LORA, 2 * WIDTH)), const((GATE_LORA, WIDTH)),
                  const((PAIR, PAIR))],
        out_specs=pl.BlockSpec((1, RWKV_TILE, WIDTH), lambda i, j: (i, j, 0)),
        scratch_shapes=[pltpu.VMEM((GROUPS, PAIR, PAIR), F32), pltpu.VMEM((1, sw), F32)],
        compiler_params=_cparams(("arbitrary", "arbitrary")),
        name="rwkv7_scan",
    )(p_rw, row(mu), row(w0), row(a0), row(k_k), row(k_a), row(r_k), row(ln_g), row(ln_b),
      wwa, g2.astype(BF16), bd)


def _out_kernel(of_ref, orw_ref, gate_ref, x_ref, mod_ref, wof_ref, wor_ref, wo_ref, n2g_ref,
                wr_ref, br_ref, x1_ref, h2_ref, route_ref, counts_ref, cnt_ref):
    d = x_ref.shape[-1]
    gate = gate_ref[...].astype(F32)
    merged = (gate[:, :d] * jnp.dot(of_ref[...], wof_ref[...], preferred_element_type=F32)
              + gate[:, d:] * jnp.dot(orw_ref[...], wor_ref[...], preferred_element_type=F32))
    gate1 = mod_ref[0, 2:3, :]
    shift2 = mod_ref[0, 3:4, :]
    scale2 = mod_ref[0, 4:5, :]
    x1 = x_ref[...] + gate1 * jnp.dot(merged.astype(BF16), wo_ref[...], preferred_element_type=F32)
    x1_ref[...] = x1
    ms = jnp.mean(x1 * x1, axis=-1, keepdims=True)
    h2 = x1 * lax.rsqrt(ms + NORM_EPS) * n2g_ref[...] * (1.0 + scale2) + shift2
    h2_ref[...] = _pack_pairs(h2)

    h2_hi = h2.astype(BF16)
    h2_lo = (h2 - h2_hi.astype(F32)).astype(BF16)
    logits = (jnp.dot(h2_hi, wr_ref[0], preferred_element_type=F32)
              + jnp.dot(h2_lo, wr_ref[0], preferred_element_type=F32)
              + jnp.dot(h2_hi, wr_ref[1], preferred_element_type=F32)) + br_ref[...]
    lane = lax.broadcasted_iota(jnp.int32, logits.shape, 1)
    neg = -jnp.inf
    big = jnp.int32(LANES)
    gl = jnp.where(lane < N_GROUPS, logits, neg)
    gmax = jnp.max(gl, axis=1, keepdims=True)
    gidx = jnp.min(jnp.where(gl == gmax, lane, big), axis=1, keepdims=True)
    g_p = 1.0 / jnp.sum(jnp.exp(gl - gmax), axis=1, keepdims=True)
    e_lane = lane - N_GROUPS
    in_grp = (e_lane >= 0) & (e_lane < N_EXPERTS) & ((e_lane // EXPERTS_PER_GROUP) == gidx)
    sel = jnp.where(in_grp, logits, neg)
    m1 = jnp.max(sel, axis=1, keepdims=True)
    i1 = jnp.min(jnp.where(sel == m1, lane, big), axis=1, keepdims=True)
    sel2 = jnp.where(lane == i1, neg, sel)
    m2 = jnp.max(sel2, axis=1, keepdims=True)
    i2 = jnp.min(jnp.where(sel2 == m2, lane, big), axis=1, keepdims=True)
    e21 = jnp.exp(m2 - m1)
    w_first = g_p / (1.0 + e21)
    w_second = g_p * e21 / (1.0 + e21)
    @pl.when(pl.program_id(0) == 0)
    def _():
        cnt_ref[...] = jnp.zeros_like(cnt_ref)

    tm = logits.shape[0]
    oh1 = lane == i1
    oh2 = lane == i2
    both = jnp.where(oh1 | oh2, 1.0, 0.0)
    before = (lax.broadcasted_iota(jnp.int32, (tm, tm), 0)
              > lax.broadcasted_iota(jnp.int32, (tm, tm), 1))
    seen = jnp.dot(jnp.where(before, 1.0, 0.0).astype(BF16), both.astype(BF16),
                   preferred_element_type=F32) + cnt_ref[...]
    rank1 = jnp.sum(jnp.where(oh1, seen, 0.0), axis=1, keepdims=True)
    rank2 = jnp.sum(jnp.where(oh2, seen, 0.0), axis=1, keepdims=True)
    cnt_ref[...] = cnt_ref[...] + jnp.sum(both, axis=0, keepdims=True)
    counts_ref[...] = jnp.broadcast_to(cnt_ref[...], counts_ref.shape)

    route = jnp.where(lane == 0, (i1 - N_GROUPS).astype(F32),
                      jnp.where(lane == 1, (i2 - N_GROUPS).astype(F32),
                                jnp.where(lane == 2, w_first,
                                          jnp.where(lane == 3, w_second,
                                                    jnp.where(lane == 4, rank1,
                                                              jnp.where(lane == 5, rank2, 0.0))))))
    route_ref[...] = route


def _merge_out_router(o_fox, o_rw, gate, x, mod, wof, wor, wo, n2g, wr, br, tm=512):
    b, s, d = x.shape
    t = b * s
    spb = s // tm
    rowspec = lambda w: pl.BlockSpec((tm, w), lambda i: (i, 0))
    const = lambda shape: pl.BlockSpec(shape, lambda i: (0,) * len(shape))
    return pl.pallas_call(
        _out_kernel,
        out_shape=(jax.ShapeDtypeStruct((t, d), F32), jax.ShapeDtypeStruct((t, d // 2), jnp.uint32),
                   jax.ShapeDtypeStruct((t, LANES), F32), jax.ShapeDtypeStruct((8, LANES), F32)),
        grid=(t // tm,),
        in_specs=[rowspec(WIDTH), rowspec(WIDTH), rowspec(2 * d), rowspec(d),
                  pl.BlockSpec((1, 6, d), lambda i: (i // spb, 0, 0)),
                  const((WIDTH, d)), const((WIDTH, d)), const((d, d)), const((1, d)),
                  const((2, d, LANES)), const((1, LANES))],
        out_specs=(rowspec(d), rowspec(d // 2), rowspec(LANES), const((8, LANES))),
        scratch_shapes=[pltpu.VMEM((1, LANES), F32)],
        compiler_params=_cparams(("arbitrary",)),
        name="merge_out_router",
    )(o_fox, o_rw, gate, x.reshape(t, d), mod, wof, wor, wo, n2g.reshape(1, d), wr, br)


SC_CORES = 2
SC_SUBCORES = 16
SC_CHUNK = 128


def _sc_scatter_rows(src, idx, n_rows):
    t, width = src.shape
    per_worker = t // (SC_CORES * SC_SUBCORES)
    n_chunks = per_worker // SC_CHUNK
    assert n_chunks * SC_CHUNK * SC_CORES * SC_SUBCORES == t and idx.shape[0] == TOP_K * t
    mesh = plsc.VectorSubcoreMesh(core_axis_name="c", subcore_axis_name="s")

    @functools.partial(
        pl.kernel, mesh=mesh,
        out_type=jax.ShapeDtypeStruct((n_rows, width), src.dtype),
        scratch_types=[pltpu.VMEM((SC_CHUNK,), jnp.int32) for _ in range(TOP_K)]
        + [pltpu.VMEM((SC_CHUNK, width), src.dtype), pltpu.SemaphoreType.DMA])
    def scatter(src_hbm, idx_hbm, out_hbm, *scratch):
        idx_v, rows_v, sem = scratch[:TOP_K], scratch[TOP_K], scratch[TOP_K + 1]
        worker = lax.axis_index("s") * SC_CORES + lax.axis_index("c")

        @pl.loop(0, n_chunks)
        def _(ci):
            off = pl.multiple_of(worker * per_worker + ci * SC_CHUNK, SC_CHUNK)
            pltpu.sync_copy(src_hbm.at[pl.ds(off, SC_CHUNK)], rows_v)
            for kk in range(TOP_K):
                pltpu.sync_copy(idx_hbm.at[pl.ds(kk * t + off, SC_CHUNK)], idx_v[kk])
            for kk in range(TOP_K):
                pltpu.async_copy(rows_v, out_hbm.at[idx_v[kk]], sem).wait()

    return scatter(src, idx)


def _expert_kernel(blk_e_ref, nused_ref, valid_ref, xs_ref, w1_ref, w3_ref, w2_ref, o_ref,
                   w1b_ref, w3b_ref, w2b_ref):
    i = pl.program_id(0)
    live = i * MOE_ROWS < nused_ref[0]
    new_expert = jnp.logical_or(i == 0, blk_e_ref[i] != blk_e_ref[jnp.maximum(i - 1, 0)])

    @pl.when(new_expert)
    def _():
        w1b_ref[...] = w1_ref[0].astype(BF16)
        w3b_ref[...] = w3_ref[0].astype(BF16)
        w2b_ref[...] = w2_ref[0].astype(BF16)

    @pl.when(live)
    def _():
        row = lax.broadcasted_iota(jnp.int32, xs_ref.shape, 0)
        xs = jnp.where(row < valid_ref[i], xs_ref[...], jnp.uint32(0))
        xb = _unpack_pairs(xs).astype(BF16)
        h1 = jnp.dot(xb, w1b_ref[...], preferred_element_type=F32)
        h3 = jnp.dot(xb, w3b_ref[...], preferred_element_type=F32)
        hh = (h1 * jax.nn.sigmoid(h1)) * h3
        o_ref[...] = _pack_pairs(jnp.dot(hh.astype(BF16), w2b_ref[...], preferred_element_type=F32))

    @pl.when(jnp.logical_not(live))
    def _():
        o_ref[...] = jnp.zeros_like(o_ref)


def _moe_experts(xs, blk_e, nused, blk_valid, w1, w3, w2):
    rows, dp = xs.shape
    _, d, de = w1.shape
    grid_spec = pltpu.PrefetchScalarGridSpec(
        num_scalar_prefetch=3,
        grid=(rows // MOE_ROWS,),
        in_specs=[pl.BlockSpec((MOE_ROWS, dp), lambda i, be, nu, va: (i, 0)),
                  pl.BlockSpec((1, d, de), lambda i, be, nu, va: (be[i], 0, 0)),
                  pl.BlockSpec((1, d, de), lambda i, be, nu, va: (be[i], 0, 0)),
                  pl.BlockSpec((1, de, d), lambda i, be, nu, va: (be[i], 0, 0))],
        out_specs=pl.BlockSpec((MOE_ROWS, dp), lambda i, be, nu, va: (i, 0)),
        scratch_shapes=[pltpu.VMEM((d, de), BF16), pltpu.VMEM((d, de), BF16), pltpu.VMEM((de, d), BF16)],
    )
    return pl.pallas_call(
        _expert_kernel,
        out_shape=jax.ShapeDtypeStruct((rows, dp), xs.dtype),
        grid_spec=grid_spec,
        compiler_params=_cparams(("arbitrary",)),
        name="moe_experts",
    )(blk_e, nused, blk_valid, xs, w1, w3, w2)


def _sc_gather_rows(table, idx):
    n_idx = idx.shape[0]
    width = table.shape[1]
    per_worker = n_idx // (SC_CORES * SC_SUBCORES)
    n_chunks = per_worker // SC_CHUNK
    assert n_chunks * SC_CHUNK * SC_CORES * SC_SUBCORES == n_idx
    mesh = plsc.VectorSubcoreMesh(core_axis_name="c", subcore_axis_name="s")

    @functools.partial(
        pl.kernel, mesh=mesh,
        out_type=jax.ShapeDtypeStruct((n_idx, width), table.dtype),
        scratch_types=[pltpu.VMEM((SC_CHUNK,), jnp.int32), pltpu.VMEM((SC_CHUNK, width), table.dtype),
                       pltpu.SemaphoreType.DMA])
    def gather(table_hbm, idx_hbm, out_hbm, idx_v, rows_v, sem):
        worker = lax.axis_index("s") * SC_CORES + lax.axis_index("c")

        @pl.loop(0, n_chunks)
        def _(ci):
            off = pl.multiple_of(worker * per_worker + ci * SC_CHUNK, SC_CHUNK)
            pltpu.sync_copy(idx_hbm.at[pl.ds(off, SC_CHUNK)], idx_v)
            pltpu.async_copy(table_hbm.at[idx_v], rows_v, sem).wait()
            pltpu.sync_copy(rows_v, out_hbm.at[pl.ds(off, SC_CHUNK)])

    return gather(table, idx)


def _final_kernel(route_ref, x1_ref, mod_ref, fg_ref, y0_ref, y1_ref, o_ref):
    route = route_ref[...]
    y = route[:, 2:3] * _unpack_pairs(y0_ref[...]) + route[:, 3:4] * _unpack_pairs(y1_ref[...])
    gate2 = mod_ref[0, 5:6, :]
    x2 = x1_ref[...] + gate2 * y
    ms = jnp.mean(x2 * x2, axis=-1, keepdims=True)
    o_ref[...] = x2 * lax.rsqrt(ms + NORM_EPS) * fg_ref[...]


def _moe_combine_final(dest_slots, route, x1, mod, final_g, ys, s, tm=512):
    t, d = x1.shape
    spb = s // tm
    picked = _sc_gather_rows(ys, dest_slots)
    tiles = t // tm
    return pl.pallas_call(
        _final_kernel,
        out_shape=jax.ShapeDtypeStruct((t, d), F32),
        grid=(tiles,),
        in_specs=[pl.BlockSpec((tm, LANES), lambda i: (i, 0)),
                  pl.BlockSpec((tm, d), lambda i: (i, 0)),
                  pl.BlockSpec((1, 6, d), lambda i: (i // spb, 0, 0)),
                  pl.BlockSpec((1, d), lambda i: (0, 0)),
                  pl.BlockSpec((tm, picked.shape[1]), lambda i: (i, 0)),
                  pl.BlockSpec((tm, picked.shape[1]), lambda i: (tiles + i, 0))],
        out_specs=pl.BlockSpec((tm, d), lambda i: (i, 0)),
        compiler_params=_cparams(("arbitrary",)),
        name="moe_combine_final",
    )(route, x1, mod, final_g.reshape(1, d), picked, picked)


def _moe_plan(route, counts):
    t = route.shape[0]
    m = t * TOP_K
    flat_e = route[:, :TOP_K].astype(jnp.int32).T.reshape(m)
    rank = route[:, 4:4 + TOP_K].astype(jnp.int32).T.reshape(m)
    counts = counts[0, N_GROUPS:N_GROUPS + N_EXPERTS].astype(jnp.int32)
    padded = (counts + MOE_ROWS - 1) // MOE_ROWS * MOE_ROWS
    pad_end = jnp.cumsum(padded)
    pad_start = pad_end - padded
    experts = jnp.arange(N_EXPERTS, dtype=jnp.int32)
    start_of = jnp.sum(jnp.where(flat_e[:, None] == experts[None, :], pad_start[None, :], 0), axis=1)
    dest = (start_of + rank).astype(jnp.int32)
    n_blocks = m // MOE_ROWS + N_EXPERTS
    blk_start = jnp.arange(n_blocks, dtype=jnp.int32) * MOE_ROWS
    blk_e = jnp.minimum(jnp.sum(pad_end[None, :] <= blk_start[:, None], axis=1), N_EXPERTS - 1).astype(jnp.int32)
    nused = pad_end[-1:].astype(jnp.int32)
    blk_valid = jnp.clip(counts[blk_e] - (blk_start - pad_start[blk_e]), 0, MOE_ROWS).astype(jnp.int32)
    return dest, blk_e, nused, blk_valid, n_blocks * MOE_ROWS


def kernel(x, c, ada_w, ada_b, norm1_g, w_in, fox_forget_b, shift_mu, rwkv_w0, rwkv_w2, rwkv_a0, rwkv_a2, rwkv_g2, rwkv_k_k, rwkv_k_a, rwkv_r_k, ln_x_g, ln_x_b, w_out_fox, w_out_rwkv, w_o, norm2_g, router_group_w, router_group_b, router_expert_w, router_expert_b, exp_w1, exp_w3, exp_w2, final_g):
    b, s, d = x.shape
    t = b * s
    assert ada_w.shape[0] == 1, "the final norm is fused into the last layer's combine; one layer is laid out"
    for l in range(1):
        mod = _adaln_mod(c, ada_w[l], ada_b[l])

        h = _norm_mod(x, norm1_g[l], mod, shift_idx=0, scale_idx=1)
        h2d = h.reshape(t, d)
        w = w_in[l]
        o_f = 3 * WIDTH
        o_rw = o_f + HEADS
        o_g = o_rw + SHIFT_WIDTH
        perm = jnp.argsort(fox_forget_b[l])
        by_head = lambda m: m.reshape(d, HEADS, HEAD_DIM)[:, perm]
        wq = by_head(w[:, :WIDTH]).reshape(d, WIDTH)
        wk = by_head(w[:, WIDTH:2 * WIDTH]).reshape(d, WIDTH)
        wv = by_head(w[:, 2 * WIDTH:o_f]).reshape(d, WIDTH)
        qkv = _matmul(h2d, jnp.concatenate([wq, wk, wv], axis=1).astype(BF16), BF16, name="proj_qkv")
        p_rw = _matmul(h2d, w[:, o_rw:o_g].astype(BF16), F32, tn=896, name="proj_rwkv")
        gate = _matmul(h2d, w[:, o_g:].astype(BF16), BF16, act="sigmoid", name="proj_gate")
        fbias = _forget_bias(h, w[:, o_f:o_rw][:, perm], fox_forget_b[l][perm])
        o_fox = _fox_attention(qkv.reshape(b, s, 3 * WIDTH), fbias)
        w_of = w_out_fox[l].reshape(HEADS, HEAD_DIM, d)[perm].reshape(WIDTH, d)
        o_rwkv = _rwkv_branch(p_rw.reshape(b, s, SHIFT_WIDTH), shift_mu[l], rwkv_w0[l], rwkv_w2[l],
                              rwkv_a0[l], rwkv_a2[l], rwkv_g2[l], rwkv_k_k[l], rwkv_k_a[l],
                              rwkv_r_k[l], ln_x_g[l], ln_x_b[l])

        wr = jnp.zeros((d, LANES), F32)
        wr = wr.at[:, :N_GROUPS].set(router_group_w[l]).at[:, N_GROUPS:N_GROUPS + N_EXPERTS].set(router_expert_w[l])
        br = jnp.zeros((1, LANES), F32)
        br = br.at[0, :N_GROUPS].set(router_group_b[l]).at[0, N_GROUPS:N_GROUPS + N_EXPERTS].set(router_expert_b[l])
        wr_hi = wr.astype(BF16)
        wr_lo = (wr - wr_hi.astype(F32)).astype(BF16)
        x1, h2, route, counts = _merge_out_router(
            o_fox.reshape(t, WIDTH), o_rwkv.reshape(t, WIDTH), gate, x, mod,
            w_of.astype(BF16), w_out_rwkv[l].astype(BF16), w_o[l].astype(BF16), norm2_g[l],
            jnp.stack([wr_hi, wr_lo]), br)

        dest, blk_e, nused, blk_valid, rows = _moe_plan(route, counts)
        xs = _sc_scatter_rows(h2, dest, rows)
        ys = _moe_experts(xs, blk_e, nused, blk_valid, exp_w1[l], exp_w3[l], exp_w2[l])
        out = _moe_combine_final(dest, route, x1, mod, final_g, ys, s)
    return out.reshape(b, s, d)
```

```python
import functools

import jax
import jax.numpy as jnp
from jax import lax
from jax.experimental import pallas as pl
from jax.experimental.pallas import tpu as pltpu
from jax.experimental.pallas import tpu_sc as plsc

F32 = jnp.float32
BF16 = jnp.bfloat16
HIGHEST = lax.Precision.HIGHEST

HEADS = 8
HEAD_DIM = 64
WIDTH = HEADS * HEAD_DIM
DECAY_LORA = 64
AAA_LORA = 64
GATE_LORA = 128
SHIFT_WIDTH = 3 * WIDTH + DECAY_LORA + AAA_LORA + GATE_LORA
LN_X_EPS = 64e-5
NORM_EPS = 1e-6
N_GROUPS = 4
EXPERTS_PER_GROUP = 8
N_EXPERTS = N_GROUPS * EXPERTS_PER_GROUP
TOP_K = 2

LANES = 128
CHUNK = 64
RWKV_TILE = 256
MOE_ROWS = 512
VMEM_LIMIT = 48 * 1024 * 1024


def _cparams(sem):
    return pltpu.CompilerParams(dimension_semantics=sem, vmem_limit_bytes=VMEM_LIMIT)


def _dot(a, b):
    return jnp.dot(a.astype(BF16), b.astype(BF16), preferred_element_type=F32)


def _dot_nt(a, b):
    return lax.dot_general(a.astype(BF16), b.astype(BF16), (((1,), (1,)), ((), ())),
                           preferred_element_type=F32)


def _dot_tn(a, b):
    return lax.dot_general(a.astype(BF16), b.astype(BF16), (((0,), (0,)), ((), ())),
                           preferred_element_type=F32)


def _pack_pairs(x):
    n = x.shape[1] // 2
    lo = lax.bitcast_convert_type(x[:, :n].astype(BF16).astype(F32), jnp.uint32) >> 16
    hi = lax.bitcast_convert_type(x[:, n:].astype(BF16).astype(F32), jnp.uint32) & jnp.uint32(0xFFFF0000)
    return lo | hi


def _unpack_pairs(u):
    lo = lax.bitcast_convert_type(u << 16, F32)
    hi = lax.bitcast_convert_type(u & jnp.uint32(0xFFFF0000), F32)
    return jnp.concatenate([lo, hi], axis=1)


def _softplus(x):
    return jnp.maximum(x, 0.0) + jnp.log1p(jnp.exp(-jnp.abs(x)))


def _mod_kernel(c_ref, w_ref, b_ref, o_ref):
    c = c_ref[...]
    sc = c * jax.nn.sigmoid(c)
    o_ref[...] = jnp.dot(sc, w_ref[...], precision=HIGHEST, preferred_element_type=F32) + b_ref[...]


def _adaln_mod(c, ada_w, ada_b):
    b, d = c.shape
    n = ada_w.shape[1]
    rows = 8
    cp = jnp.zeros((rows, d), F32).at[:b].set(c)
    tn = 1024
    out = pl.pallas_call(
        _mod_kernel,
        out_shape=jax.ShapeDtypeStruct((rows, n), F32),
        grid=(n // tn,),
        in_specs=[pl.BlockSpec((rows, d), lambda j: (0, 0)),
                  pl.BlockSpec((d, tn), lambda j: (0, j)),
                  pl.BlockSpec((1, tn), lambda j: (0, j))],
        out_specs=pl.BlockSpec((rows, tn), lambda j: (0, j)),
        compiler_params=_cparams(("arbitrary",)),
        name="adaln_mod",
    )(cp, ada_w, ada_b.reshape(1, n))
    return out[:b].reshape(b, 6, d)


def _norm_mod_kernel(x_ref, g_ref, mod_ref, o_ref, *, shift_idx, scale_idx):
    x = x_ref[0]
    ms = jnp.mean(x * x, axis=-1, keepdims=True)
    y = x * lax.rsqrt(ms + NORM_EPS) * g_ref[...]
    scale = mod_ref[0, scale_idx:scale_idx + 1, :]
    shift = mod_ref[0, shift_idx:shift_idx + 1, :]
    o_ref[0] = (y * (1.0 + scale) + shift).astype(o_ref.dtype)


def _norm_mod(x, g, mod, shift_idx, scale_idx, tm=1024):
    b, s, d = x.shape
    return pl.pallas_call(
        functools.partial(_norm_mod_kernel, shift_idx=shift_idx, scale_idx=scale_idx),
        out_shape=jax.ShapeDtypeStruct((b, s, d), BF16),
        grid=(b, s // tm),
        in_specs=[pl.BlockSpec((1, tm, d), lambda i, j: (i, j, 0)),
                  pl.BlockSpec((1, d), lambda i, j: (0, 0)),
                  pl.BlockSpec((1, 6, d), lambda i, j: (i, 0, 0))],
        out_specs=pl.BlockSpec((1, tm, d), lambda i, j: (i, j, 0)),
        compiler_params=_cparams(("arbitrary", "arbitrary")),
        name="norm1_mod",
    )(x, g.reshape(1, d), mod)


def _mm_kernel(a_ref, w_ref, o_ref):
    o_ref[...] = jnp.dot(a_ref[...], w_ref[...], preferred_element_type=F32).astype(o_ref.dtype)


def _matmul(a, w, out_dtype, tm=2048, tn=512, name="proj"):
    t, k = a.shape
    n = w.shape[1]
    tn = min(tn, n)
    return pl.pallas_call(
        _mm_kernel,
        out_shape=jax.ShapeDtypeStruct((t, n), out_dtype),
        grid=(t // tm, n // tn),
        in_specs=[pl.BlockSpec((tm, k), lambda i, j: (i, 0)),
                  pl.BlockSpec((k, tn), lambda i, j: (0, j))],
        out_specs=pl.BlockSpec((tm, tn), lambda i, j: (i, j)),
        compiler_params=_cparams(("arbitrary", "arbitrary")),
        name=name,
    )(a, w)


LOG2E = 1.4426950408889634


def _split3(x):
    hi = x.astype(BF16)
    r1 = x - hi.astype(F32)
    mid = r1.astype(BF16)
    lo = (r1 - mid.astype(F32)).astype(BF16)
    return hi, mid, lo


def _fcum_kernel(h_ref, wf_ref, fb_ref, sel_ref, o_ref, carry_ref, *, ts):
    @pl.when(pl.program_id(0) == 0)
    def _():
        carry_ref[...] = jnp.zeros_like(carry_ref)

    ri = lax.broadcasted_iota(jnp.int32, (ts, ts), 0)
    ci = lax.broadcasted_iota(jnp.int32, (ts, ts), 1)
    tri = jnp.where(ri >= ci, 1.0, 0.0).astype(BF16)
    for bi in range(h_ref.shape[0]):
        f = jnp.dot(h_ref[bi], wf_ref[...], preferred_element_type=F32) + fb_ref[...]
        lf = -_softplus(-f)
        cum = carry_ref[bi]
        for piece in _split3(lf):
            cum = cum + jnp.dot(tri, piece, preferred_element_type=F32)
        carry_ref[bi] = cum[ts - 1:ts, :]
        out = jnp.zeros((ts, sel_ref.shape[2]), F32)
        for idx, piece in enumerate(_split3(cum * (-LOG2E))):
            out = out + jnp.dot(piece, sel_ref[idx], preferred_element_type=F32)
        o_ref[bi] = out.astype(o_ref.dtype)


def _forget_bias(h, wf, fb, ts=512):
    b, s, d = h.shape
    pairs = HEADS // 2
    wf_p = jnp.zeros((d, LANES), F32).at[:, :HEADS].set(wf).astype(BF16)
    fb_p = jnp.zeros((1, LANES), F32).at[0, :HEADS].set(fb)
    hh = jnp.arange(HEADS)
    sel = jnp.zeros((3, LANES, pairs * LANES), F32)
    for piece in range(3):
        sel = sel.at[piece, hh, (hh // 2) * LANES + (hh % 2) * 3 + piece].set(1.0)
    return pl.pallas_call(
        functools.partial(_fcum_kernel, ts=ts),
        out_shape=jax.ShapeDtypeStruct((b, s, pairs * LANES), BF16),
        grid=(s // ts,),
        in_specs=[pl.BlockSpec((b, ts, d), lambda j: (0, j, 0)),
                  pl.BlockSpec((d, LANES), lambda j: (0, 0)),
                  pl.BlockSpec((1, LANES), lambda j: (0, 0)),
                  pl.BlockSpec((3, LANES, pairs * LANES), lambda j: (0, 0, 0))],
        out_specs=pl.BlockSpec((b, ts, pairs * LANES), lambda j: (0, j, 0)),
        scratch_shapes=[pltpu.VMEM((b, 1, LANES), F32)],
        compiler_params=_cparams(("arbitrary",)),
        name="forget_bias",
    )(h, wf_p, fb_p, sel.astype(BF16))


FOX_BQ = 1024
FOX_BK = 512


def _fox_kernel(first_ref, q_ref, k_ref, v_ref, a_ref, o_ref, m_ref, acc_ref, *, bq, bk):
    i = pl.program_id(2)
    lane = lax.broadcasted_iota(jnp.int32, (bq, LANES), 1)
    first = lane < HEAD_DIM
    qs = q_ref[0].astype(F32) * (HEAD_DIM ** -0.5 * LOG2E)
    aug0 = jnp.where(lane < 3, 1.0, 0.0)
    aug1 = jnp.where((lane >= 3) & (lane < 6), 1.0, 0.0)
    q01 = jnp.concatenate([jnp.concatenate([jnp.where(first, qs, 0.0), aug0], axis=1),
                           jnp.concatenate([jnp.where(first, 0.0, qs), aug1], axis=1)],
                          axis=0).astype(BF16)
    m_ref[...] = jnp.full_like(m_ref, -jnp.inf)
    acc_ref[...] = jnp.zeros_like(acc_ref)
    lane_k = lax.broadcasted_iota(jnp.int32, (bk, LANES), 1)
    keep_first = jnp.where(lane_k < HEAD_DIM, 1.0, 0.0).astype(BF16)
    keep_second = jnp.where(lane_k < HEAD_DIM, 0.0, 1.0).astype(BF16)

    def both_heads(x, r0):
        if r0 == 0:
            return x[...]
        return jnp.concatenate([x[r0:bq], x[bq + r0:2 * bq]], axis=0)

    def logits(j, r0=0):
        start = pl.multiple_of(j * bk, bk)
        kb = jnp.concatenate([k_ref[0, pl.ds(start, bk), :], a_ref[0, pl.ds(start, bk), :]], axis=1)
        return lax.dot_general(both_heads(q01, r0), kb, (((1,), (1,)), ((), ())),
                               preferred_element_type=F32)

    def consume(j, z, r0=0):
        nr = bq - r0
        start = pl.multiple_of(j * bk, bk)
        vb = v_ref[0, pl.ds(start, bk), :]
        m_prev = both_heads(m_ref, r0)
        m_new = jnp.maximum(m_prev, jnp.max(z, axis=1, keepdims=True))
        alpha = jnp.exp2(m_prev - m_new)
        p = jnp.exp2(z - jnp.concatenate([m_new] * (bk // LANES), axis=1)).astype(BF16)
        pv = jnp.concatenate(
            [jnp.dot(p[:nr], vb * keep_first + keep_second, preferred_element_type=F32),
             jnp.dot(p[nr:], vb * keep_second + keep_first, preferred_element_type=F32)], axis=0)
        acc_new = alpha * both_heads(acc_ref, r0) + pv
        if r0 == 0:
            acc_ref[...] = acc_new
            m_ref[...] = m_new
        else:
            for half, dst in ((slice(0, nr), slice(r0, bq)), (slice(nr, 2 * nr), slice(bq + r0, 2 * bq))):
                acc_ref[dst] = acc_new[half]
                m_ref[dst] = m_new[half]

    per_q = bq // bk
    n_full = i * per_q

    j_first = first_ref[pl.program_id(0), pl.program_id(1), i]
    odd = (n_full - j_first) & 1

    @pl.when(odd == 1)
    def _():
        consume(j_first, logits(j_first))

    def body(step, carry):
        j = j_first + odd + 2 * step
        z_a = logits(j)
        z_b = logits(j + 1)
        consume(j, z_a)
        consume(j + 1, z_b)
        return carry

    lax.fori_loop(0, (n_full - j_first) // 2, body, 0)
    for d in range(per_q):
        r0 = d * bk
        row = lax.broadcasted_iota(jnp.int32, (bq - r0, bk), 0)
        col = lax.broadcasted_iota(jnp.int32, (bq - r0, bk), 1)
        keep = col <= row
        z = logits(n_full + d, r0)
        consume(n_full + d, jnp.where(jnp.concatenate([keep, keep], axis=0), z, -jnp.inf), r0)
    acc = acc_ref[...]
    o = acc / pltpu.roll(acc, HEAD_DIM, 1)
    o_ref[0] = jnp.where(first, o[:bq], o[bq:]).astype(o_ref.dtype)


def _norm_bound_kernel(x_ref, sel_ref, o_ref):
    x = x_ref[...].astype(F32)
    ssq = _dot(x * x, sel_ref[...])
    dots = _dot(x[:, :WIDTH] * x[:, WIDTH:], sel_ref[0:WIDTH, :])
    sub = lax.broadcasted_iota(jnp.int32, o_ref.shape[1:], 0)
    o_ref[0] = jnp.where(sub == 0, jnp.max(ssq, axis=0, keepdims=True), jnp.min(dots, axis=0, keepdims=True))


FOX_ZERO_LOG2 = 136.0


def _fox_first_block(qkv, fbias, bq, bk):
    b, s, _ = qkv.shape
    t = b * s
    nq, nk = s // bq, s // bk
    n_heads2 = 2 * HEADS
    sel = (jnp.arange(2 * WIDTH)[:, None] // HEAD_DIM == jnp.arange(LANES)[None, :]).astype(BF16)
    bounds = pl.pallas_call(
        _norm_bound_kernel,
        out_shape=jax.ShapeDtypeStruct((t // bk, 8, LANES), F32),
        grid=(t // bk,),
        in_specs=[pl.BlockSpec((bk, 2 * WIDTH), lambda i: (i, 0)),
                  pl.BlockSpec((2 * WIDTH, LANES), lambda i: (0, 0))],
        out_specs=pl.BlockSpec((1, 8, LANES), lambda i: (i, 0, 0)),
        compiler_params=_cparams(("arbitrary",)),
        name="fox_norm_bounds",
    )(qkv.reshape(t, -1), sel)
    nrm = jnp.sqrt(bounds[:, 0, :n_heads2]).reshape(b, nk, n_heads2) * 1.01
    q_scale = HEAD_DIM ** -0.5 * LOG2E
    qn = nrm[..., :HEADS] * (q_scale * 1.01)
    kn = nrm[..., HEADS:]
    per_q = bq // bk
    qn_i = qn.reshape(b, nq, per_q, HEADS).max(axis=2)
    kn_i = kn.reshape(b, nq, per_q, HEADS).max(axis=2)
    kn_pre = lax.cummax(kn, axis=1)
    diag = bounds[:, 1, :HEADS].reshape(b, nq, per_q, HEADS).min(axis=2) * q_scale
    diag_low = diag - 0.02 * qn_i * kn_i
    pairs = HEADS // 2
    def bias_rows(rows):
        pieces = rows.astype(F32).reshape(b, -1, pairs, LANES)[..., :6].reshape(b, -1, pairs, 2, 3)
        return pieces.sum(-1).reshape(b, -1, HEADS)

    nb_end = bias_rows(fbias[:, bk - 1::bk])
    nb_start = bias_rows(fbias[:, ::bq])
    gap = nb_start[:, :, None, :] - nb_end[:, None, :, :]
    need = qn_i[:, :, None, :] * kn_pre[:, None, :, :] - diag_low[:, :, None, :] + FOX_ZERO_LOG2
    skip = (gap > need).reshape(b, nq, nk, pairs, 2).all(axis=-1)
    n_full = jnp.arange(nq) * per_q
    skip = skip & (jnp.arange(nk)[None, None, :, None] < n_full[None, :, None, None])
    first = jnp.argmin(skip, axis=2)
    return first.transpose(0, 2, 1).astype(jnp.int32)


def _fox_attention(qkv, fbias, bq=FOX_BQ, bk=FOX_BK):
    b, s, _ = qkv.shape
    pairs = HEADS // 2
    cb = WIDTH // LANES
    first = _fox_first_block(qkv, fbias, bq, bk)
    grid_spec = pltpu.PrefetchScalarGridSpec(
        num_scalar_prefetch=1,
        grid=(b, pairs, s // bq),
        in_specs=[pl.BlockSpec((1, bq, LANES), lambda bi, hp, i, fr: (bi, i, hp)),
                  pl.BlockSpec((1, s, LANES), lambda bi, hp, i, fr: (bi, 0, cb + hp)),
                  pl.BlockSpec((1, s, LANES), lambda bi, hp, i, fr: (bi, 0, 2 * cb + hp)),
                  pl.BlockSpec((1, s, LANES), lambda bi, hp, i, fr: (bi, 0, hp))],
        out_specs=pl.BlockSpec((1, bq, LANES), lambda bi, hp, i, fr: (bi, i, hp)),
        scratch_shapes=[pltpu.VMEM((2 * bq, LANES), F32), pltpu.VMEM((2 * bq, LANES), F32)],
    )
    return pl.pallas_call(
        functools.partial(_fox_kernel, bq=bq, bk=bk),
        out_shape=jax.ShapeDtypeStruct((b, s, WIDTH), BF16),
        grid_spec=grid_spec,
        compiler_params=_cparams(("arbitrary", "arbitrary", "arbitrary")),
        name="fox_attention",
    )(first, qkv, qkv, qkv, fbias)


PAIR = 2 * HEAD_DIM
GROUPS = WIDTH // PAIR


def _group(x, g):
    return x[:, g * PAIR:(g + 1) * PAIR]


def _head_sum(x, bd2):
    return jnp.concatenate([_dot(_group(x, g), bd2) for g in range(GROUPS)], axis=1)


def _head_apply(mats, x, lane_first):
    rows = mats.shape[1]
    outs = []
    for g in range(GROUPS):
        res = _dot(mats[2 * g:2 * g + 2].reshape(2 * rows, rows), _group(x, g))
        outs.append(jnp.where(lane_first, res[:rows], res[rows:]))
    return jnp.concatenate(outs, axis=1)


def _rwkv_kernel(p_ref, mu_ref, w0_ref, a0_ref, kk_ref, ka_ref, rk_ref, lng_ref, lnb_ref,
                 wwa_ref, g2_ref, bd_ref, o_ref, st_ref, prev_ref):
    L = CHUNK

    @pl.when(pl.program_id(1) == 0)
    def _():
        st_ref[...] = jnp.zeros_like(st_ref)
        prev_ref[...] = jnp.zeros_like(prev_ref)

    p = p_ref[0]
    T = p.shape[0]
    rowi = lax.broadcasted_iota(jnp.int32, p.shape, 0)
    prev = jnp.where(rowi == 0, prev_ref[...], pltpu.roll(p, 1, 0))
    prev_ref[...] = p[T - 1:T, :]
    ps = p + (prev - p) * mu_ref[...]
    r = ps[:, 0:WIDTH]
    k = ps[:, WIDTH:2 * WIDTH]
    v = ps[:, 2 * WIDTH:3 * WIDTH]
    wa_in = ps[:, 3 * WIDTH:3 * WIDTH + DECAY_LORA + AAA_LORA]
    gd = ps[:, 3 * WIDTH + DECAY_LORA + AAA_LORA:]
    lane_wa = lax.broadcasted_iota(jnp.int32, wa_in.shape, 1)
    wa_act = jnp.where(lane_wa < DECAY_LORA, jnp.tanh(wa_in), wa_in)
    wa = _dot(wa_act, wwa_ref[...])
    log_w = -_softplus(-(w0_ref[...] + wa[:, :WIDTH])) - 0.5
    lw = -jnp.exp(log_w)
    a = jax.nn.sigmoid(a0_ref[...] + wa[:, WIDTH:])
    out_gate = _dot(jax.nn.sigmoid(gd), g2_ref[...])
    bd = bd_ref[...]
    kk0 = k * kk_ref[...]
    kk = kk0 * lax.rsqrt(jnp.maximum(_head_sum(kk0 * kk0, bd), 1e-24))
    k2 = k * (1.0 + (a - 1.0) * ka_ref[...])
    av = -kk
    bv = kk * a

    n_sub = T // L
    rt_i = lax.broadcasted_iota(jnp.int32, (T, T), 0)
    ct_i = lax.broadcasted_iota(jnp.int32, (T, T), 1)
    tri_tile = (rt_i >= ct_i) & (rt_i // L == ct_i // L)
    cl = _split_dot_left(jnp.where(tri_tile, 1.0, 0.0).astype(BF16), lw)
    cl_end = jnp.concatenate([jnp.broadcast_to(cl[(c + 1) * L - 1:(c + 1) * L, :], (L, WIDTH))
                              for c in range(n_sub)], axis=0)
    at_all = av * jnp.exp(cl - lw)
    rt_all = r * jnp.exp(cl)
    einv = jnp.exp(-cl)
    bt_all = bv * einv
    kt_all = k2 * einv
    edec = jnp.exp(cl_end - cl)
    b_end_all = bv * edec
    k_end_all = k2 * edec

    ri = lax.broadcasted_iota(jnp.int32, (L, L), 0)
    ci = lax.broadcasted_iota(jnp.int32, (L, L), 1)
    tri_incl = ri >= ci
    tri_strict = ri > ci
    eye = jnp.where(ri == ci, 1.0, 0.0)
    lane_first = lax.broadcasted_iota(jnp.int32, (L, PAIR), 1) < HEAD_DIM
    qr = lax.broadcasted_iota(jnp.int32, (PAIR, PAIR), 0) < HEAD_DIM
    qc = lax.broadcasted_iota(jnp.int32, (PAIR, PAIR), 1) < HEAD_DIM
    same_head = qr == qc

    def bmm(x, y):
        return lax.dot_general(x.astype(BF16), y.astype(BF16), (((2,), (1,)), ((0,), (0,))),
                               preferred_element_type=F32)

    def chunk_terms(c):
        rows = slice(c * L, (c + 1) * L)
        at, rt, bt, kt, vc = at_all[rows], rt_all[rows], bt_all[rows], kt_all[rows], v[rows]
        sb_heads, sk_heads = [], []
        for g in range(GROUPS):
            at_g, rt_g = _group(at, g), _group(rt, g)
            lhs = jnp.concatenate([jnp.where(lane_first, at_g, 0.0), jnp.where(lane_first, rt_g, 0.0),
                                   jnp.where(lane_first, 0.0, at_g), jnp.where(lane_first, 0.0, rt_g)],
                                  axis=0).astype(BF16)
            sb_g = _dot_nt(lhs, _group(bt, g))
            sk_g = _dot_nt(lhs, _group(kt, g))
            for hh in range(2):
                sb_heads.append(sb_g[hh * 2 * L:(hh + 1) * 2 * L])
                sk_heads.append(sk_g[hh * 2 * L:(hh + 1) * 2 * L])
        sb = jnp.stack(sb_heads)
        sk = jnp.stack(sk_heads)
        n_ab = jnp.where(tri_strict, sb[:, :L, :], 0.0)
        a_ak = jnp.where(tri_strict, sk[:, :L, :], 0.0)
        a_rb = jnp.where(tri_incl, sb[:, L:, :], 0.0)
        a_rk = jnp.where(tri_incl, sk[:, L:, :], 0.0)
        tinv = eye + n_ab
        pw = bmm(n_ab, n_ab)
        span = 2
        while 2 * span < L:
            both = bmm(jnp.concatenate([tinv, pw], axis=1), pw)
            tinv = tinv + both[:, :L, :]
            pw = both[:, L:, :]
            span *= 2
        tinv = tinv + bmm(tinv, pw)
        av_term = _head_apply(a_ak, vc, lane_first)
        pm = _head_apply(tinv, at, lane_first)
        qm = _head_apply(tinv, av_term, lane_first)
        rkv = _head_apply(a_rk, vc, lane_first)
        return pm, qm, rkv, a_rb

    terms = [chunk_terms(c) for c in range(n_sub)]

    y_chunks = []
    for c in range(n_sub):
        rows = slice(c * L, (c + 1) * L)
        pm, qm, rkv, a_rb = terms[c]
        rt, vc, b_end, k_end = rt_all[rows], v[rows], b_end_all[rows], k_end_all[rows]
        gam_last = jnp.exp(cl[(c + 1) * L - 1:(c + 1) * L, :])
        u_parts, ys_parts = [], []
        for g in range(GROUPS):
            pr = _dot_nt(jnp.concatenate([_group(pm, g), _group(rt, g)], axis=0), st_ref[g])
            u_parts.append(pr[:L] + _group(qm, g))
            ys_parts.append(pr[L:])
        u = jnp.concatenate(u_parts, axis=1)
        y_chunks.append(jnp.concatenate(ys_parts, axis=1) + _head_apply(a_rb, u, lane_first) + rkv)
        for g in range(GROUPS):
            upd = _dot_tn(_group(u, g), _group(b_end, g)) + _dot_tn(_group(vc, g), _group(k_end, g))
            st_ref[g] = st_ref[g] * _group(gam_last, g) + jnp.where(same_head, upd, 0.0)
    y = jnp.concatenate(y_chunks, axis=0)

    inv_n = 1.0 / HEAD_DIM
    mean = _head_sum(y, bd) * inv_n
    dlt = y - mean
    var = _head_sum(dlt * dlt, bd) * inv_n
    yn = dlt * lax.rsqrt(var + LN_X_EPS) * lng_ref[...] + lnb_ref[...]
    bonus = _head_sum(r * k2 * rk_ref[...], bd) * v
    o_ref[0] = ((yn + bonus) * out_gate).astype(o_ref.dtype)


def _split_dot_left(w_bf16, x):
    hi, mid, lo = _split3(x)
    return (jnp.dot(w_bf16, hi, preferred_element_type=F32)
            + jnp.dot(w_bf16, mid, preferred_element_type=F32)
            + jnp.dot(w_bf16, lo, preferred_element_type=F32))


def _rwkv_branch(p_rw, mu, w0, w2, a0, a2, g2, k_k, k_a, r_k, ln_g, ln_b):
    b, s, sw = p_rw.shape
    row = lambda t: t.reshape(1, -1).astype(F32)
    wwa = jnp.zeros((DECAY_LORA + AAA_LORA, 2 * WIDTH), F32)
    wwa = wwa.at[:DECAY_LORA, :WIDTH].set(w2).at[DECAY_LORA:, WIDTH:].set(a2).astype(BF16)
    hid = jnp.arange(PAIR) // HEAD_DIM
    bd = (hid[:, None] == hid[None, :]).astype(BF16)
    const = lambda shape: pl.BlockSpec(shape, lambda i, j: (0,) * len(shape))
    return pl.pallas_call(
        _rwkv_kernel,
        out_shape=jax.ShapeDtypeStruct((b, s, WIDTH), BF16),
        grid=(b, s // RWKV_TILE),
        in_specs=[pl.BlockSpec((1, RWKV_TILE, sw), lambda i, j: (i, j, 0)),
                  const((1, sw)), const((1, WIDTH)), const((1, WIDTH)), const((1, WIDTH)),
                  const((1, WIDTH)), const((1, WIDTH)), const((1, WIDTH)), const((1, WIDTH)),
                  const((DECAY_LORA + AAA_LORA, 2 * WIDTH)), const((GATE_LORA, WIDTH)),
                  const((PAIR, PAIR))],
        out_specs=pl.BlockSpec((1, RWKV_TILE, WIDTH), lambda i, j: (i, j, 0)),
        scratch_shapes=[pltpu.VMEM((GROUPS, PAIR, PAIR), F32), pltpu.VMEM((1, sw), F32)],
        compiler_params=_cparams(("arbitrary", "arbitrary")),
        name="rwkv7_scan",
    )(p_rw, row(mu), row(w0), row(a0), row(k_k), row(k_a), row(r_k), row(ln_g), row(ln_b),
      wwa, g2.astype(BF16), bd)


def _out_kernel(of_ref, orw_ref, gate_ref, x_ref, mod_ref, wof_ref, wor_ref, wo_ref, n2g_ref,
                wr_ref, br_ref, x1_ref, h2_ref, route_ref, counts_ref, cnt_ref):
    d = x_ref.shape[-1]
    gate = 0.5 * jnp.tanh(0.5 * gate_ref[...].astype(F32)) + 0.5
    merged = (gate[:, :d] * jnp.dot(of_ref[...], wof_ref[...], preferred_element_type=F32)
              + gate[:, d:] * jnp.dot(orw_ref[...], wor_ref[...], preferred_element_type=F32))
    gate1 = mod_ref[0, 2:3, :]
    shift2 = mod_ref[0, 3:4, :]
    scale2 = mod_ref[0, 4:5, :]
    x1 = x_ref[...] + gate1 * jnp.dot(merged.astype(BF16), wo_ref[...], preferred_element_type=F32)
    x1_ref[...] = x1
    ms = jnp.mean(x1 * x1, axis=-1, keepdims=True)
    h2 = x1 * lax.rsqrt(ms + NORM_EPS) * n2g_ref[...] * (1.0 + scale2) + shift2
    h2_ref[...] = _pack_pairs(h2)

    h2_hi = h2.astype(BF16)
    h2_lo = (h2 - h2_hi.astype(F32)).astype(BF16)
    logits = (jnp.dot(h2_hi, wr_ref[0], preferred_element_type=F32)
              + jnp.dot(h2_lo, wr_ref[0], preferred_element_type=F32)
              + jnp.dot(h2_hi, wr_ref[1], preferred_element_type=F32)) + br_ref[...]
    lane = lax.broadcasted_iota(jnp.int32, logits.shape, 1)
    neg = -jnp.inf
    big = jnp.int32(LANES)
    gl = jnp.where(lane < N_GROUPS, logits, neg)
    gmax = jnp.max(gl, axis=1, keepdims=True)
    gidx = jnp.min(jnp.where(gl == gmax, lane, big), axis=1, keepdims=True)
    g_p = 1.0 / jnp.sum(jnp.exp(gl - gmax), axis=1, keepdims=True)
    e_lane = lane - N_GROUPS
    in_grp = (e_lane >= 0) & (e_lane < N_EXPERTS) & ((e_lane // EXPERTS_PER_GROUP) == gidx)
    sel = jnp.where(in_grp, logits, neg)
    m1 = jnp.max(sel, axis=1, keepdims=True)
    i1 = jnp.min(jnp.where(sel == m1, lane, big), axis=1, keepdims=True)
    sel2 = jnp.where(lane == i1, neg, sel)
    m2 = jnp.max(sel2, axis=1, keepdims=True)
    i2 = jnp.min(jnp.where(sel2 == m2, lane, big), axis=1, keepdims=True)
    e21 = jnp.exp(m2 - m1)
    w_first = g_p / (1.0 + e21)
    w_second = g_p * e21 / (1.0 + e21)
    @pl.when(pl.program_id(0) == 0)
    def _():
        cnt_ref[...] = jnp.zeros_like(cnt_ref)

    tm = logits.shape[0]
    oh1 = lane == i1
    oh2 = lane == i2
    both = jnp.where(oh1 | oh2, 1.0, 0.0)
    before = (lax.broadcasted_iota(jnp.int32, (tm, tm), 0)
              > lax.broadcasted_iota(jnp.int32, (tm, tm), 1))
    seen = jnp.dot(jnp.where(before, 1.0, 0.0).astype(BF16), both.astype(BF16),
                   preferred_element_type=F32) + cnt_ref[...]
    rank1 = jnp.sum(jnp.where(oh1, seen, 0.0), axis=1, keepdims=True)
    rank2 = jnp.sum(jnp.where(oh2, seen, 0.0), axis=1, keepdims=True)
    cnt_ref[...] = cnt_ref[...] + jnp.sum(both, axis=0, keepdims=True)
    counts_ref[...] = jnp.broadcast_to(cnt_ref[...], counts_ref.shape)

    route = jnp.where(lane == 0, (i1 - N_GROUPS).astype(F32),
                      jnp.where(lane == 1, (i2 - N_GROUPS).astype(F32),
                                jnp.where(lane == 2, w_first,
                                          jnp.where(lane == 3, w_second,
                                                    jnp.where(lane == 4, rank1,
                                                              jnp.where(lane == 5, rank2, 0.0))))))
    route_ref[...] = route


def _merge_out_router(o_fox, o_rw, gate, x, mod, wof, wor, wo, n2g, wr, br, tm=512):
    b, s, d = x.shape
    t = b * s
    spb = s // tm
    rowspec = lambda w: pl.BlockSpec((tm, w), lambda i: (i, 0))
    const = lambda shape: pl.BlockSpec(shape, lambda i: (0,) * len(shape))
    return pl.pallas_call(
        _out_kernel,
        out_shape=(jax.ShapeDtypeStruct((t, d), F32), jax.ShapeDtypeStruct((t, d // 2), jnp.uint32),
                   jax.ShapeDtypeStruct((t, LANES), F32), jax.ShapeDtypeStruct((8, LANES), F32)),
        grid=(t // tm,),
        in_specs=[rowspec(WIDTH), rowspec(WIDTH), rowspec(2 * d), rowspec(d),
                  pl.BlockSpec((1, 6, d), lambda i: (i // spb, 0, 0)),
                  const((WIDTH, d)), const((WIDTH, d)), const((d, d)), const((1, d)),
                  const((2, d, LANES)), const((1, LANES))],
        out_specs=(rowspec(d), rowspec(d // 2), rowspec(LANES), const((8, LANES))),
        scratch_shapes=[pltpu.VMEM((1, LANES), F32)],
        compiler_params=_cparams(("arbitrary",)),
        name="merge_out_router",
    )(o_fox, o_rw, gate, x.reshape(t, d), mod, wof, wor, wo, n2g.reshape(1, d), wr, br)


SC_CORES = 2
SC_SUBCORES = 16
SC_CHUNK = 128


def _sc_scatter_rows(src, idx, n_rows):
    t, width = src.shape
    per_worker = t // (SC_CORES * SC_SUBCORES)
    n_chunks = per_worker // SC_CHUNK
    assert n_chunks * SC_CHUNK * SC_CORES * SC_SUBCORES == t and idx.shape[0] == TOP_K * t
    mesh = plsc.VectorSubcoreMesh(core_axis_name="c", subcore_axis_name="s")

    @functools.partial(
        pl.kernel, mesh=mesh,
        out_type=jax.ShapeDtypeStruct((n_rows, width), src.dtype),
        scratch_types=[pltpu.VMEM((SC_CHUNK,), jnp.int32) for _ in range(TOP_K)]
        + [pltpu.VMEM((SC_CHUNK, width), src.dtype), pltpu.SemaphoreType.DMA])
    def scatter(src_hbm, idx_hbm, out_hbm, *scratch):
        idx_v, rows_v, sem = scratch[:TOP_K], scratch[TOP_K], scratch[TOP_K + 1]
        worker = lax.axis_index("s") * SC_CORES + lax.axis_index("c")

        @pl.loop(0, n_chunks)
        def _(ci):
            off = pl.multiple_of(worker * per_worker + ci * SC_CHUNK, SC_CHUNK)
            pltpu.sync_copy(src_hbm.at[pl.ds(off, SC_CHUNK)], rows_v)
            for kk in range(TOP_K):
                pltpu.sync_copy(idx_hbm.at[pl.ds(kk * t + off, SC_CHUNK)], idx_v[kk])
            for kk in range(TOP_K):
                pltpu.async_copy(rows_v, out_hbm.at[idx_v[kk]], sem).wait()

    return scatter(src, idx)


def _expert_kernel(blk_e_ref, nused_ref, valid_ref, xs_ref, w1_ref, w3_ref, w2_ref, o_ref,
                   w1b_ref, w3b_ref, w2b_ref):
    i = pl.program_id(0)
    live = i * MOE_ROWS < nused_ref[0]
    new_expert = jnp.logical_or(i == 0, blk_e_ref[i] != blk_e_ref[jnp.maximum(i - 1, 0)])

    @pl.when(new_expert)
    def _():
        w1b_ref[...] = w1_ref[0].astype(BF16)
        w3b_ref[...] = w3_ref[0].astype(BF16)
        w2b_ref[...] = w2_ref[0].astype(BF16)

    @pl.when(live)
    def _():
        row = lax.broadcasted_iota(jnp.int32, xs_ref.shape, 0)
        xs = jnp.where(row < valid_ref[i], xs_ref[...], jnp.uint32(0))
        xb = _unpack_pairs(xs).astype(BF16)
        h1 = jnp.dot(xb, w1b_ref[...], preferred_element_type=F32)
        h3 = jnp.dot(xb, w3b_ref[...], preferred_element_type=F32)
        hh = (h1 * jax.nn.sigmoid(h1)) * h3
        o_ref[...] = _pack_pairs(jnp.dot(hh.astype(BF16), w2b_ref[...], preferred_element_type=F32))

    @pl.when(jnp.logical_not(live))
    def _():
        o_ref[...] = jnp.zeros_like(o_ref)


def _moe_experts(xs, blk_e, nused, blk_valid, w1, w3, w2):
    rows, dp = xs.shape
    _, d, de = w1.shape
    grid_spec = pltpu.PrefetchScalarGridSpec(
        num_scalar_prefetch=3,
        grid=(rows // MOE_ROWS,),
        in_specs=[pl.BlockSpec((MOE_ROWS, dp), lambda i, be, nu, va: (i, 0)),
                  pl.BlockSpec((1, d, de), lambda i, be, nu, va: (be[i], 0, 0)),
                  pl.BlockSpec((1, d, de), lambda i, be, nu, va: (be[i], 0, 0)),
                  pl.BlockSpec((1, de, d), lambda i, be, nu, va: (be[i], 0, 0))],
        out_specs=pl.BlockSpec((MOE_ROWS, dp), lambda i, be, nu, va: (i, 0)),
        scratch_shapes=[pltpu.VMEM((d, de), BF16), pltpu.VMEM((d, de), BF16), pltpu.VMEM((de, d), BF16)],
    )
    return pl.pallas_call(
        _expert_kernel,
        out_shape=jax.ShapeDtypeStruct((rows, dp), xs.dtype),
        grid_spec=grid_spec,
        compiler_params=_cparams(("arbitrary",)),
        name="moe_experts",
    )(blk_e, nused, blk_valid, xs, w1, w3, w2)


def _sc_gather_rows(table, idx):
    n_idx = idx.shape[0]
    width = table.shape[1]
    per_worker = n_idx // (SC_CORES * SC_SUBCORES)
    n_chunks = per_worker // SC_CHUNK
    assert n_chunks * SC_CHUNK * SC_CORES * SC_SUBCORES == n_idx
    mesh = plsc.VectorSubcoreMesh(core_axis_name="c", subcore_axis_name="s")

    @functools.partial(
        pl.kernel, mesh=mesh,
        out_type=jax.ShapeDtypeStruct((n_idx, width), table.dtype),
        scratch_types=[pltpu.VMEM((SC_CHUNK,), jnp.int32), pltpu.VMEM((SC_CHUNK, width), table.dtype),
                       pltpu.SemaphoreType.DMA])
    def gather(table_hbm, idx_hbm, out_hbm, idx_v, rows_v, sem):
        worker = lax.axis_index("s") * SC_CORES + lax.axis_index("c")

        @pl.loop(0, n_chunks)
        def _(ci):
            off = pl.multiple_of(worker * per_worker + ci * SC_CHUNK, SC_CHUNK)
            pltpu.sync_copy(idx_hbm.at[pl.ds(off, SC_CHUNK)], idx_v)
            pltpu.async_copy(table_hbm.at[idx_v], rows_v, sem).wait()
            pltpu.sync_copy(rows_v, out_hbm.at[pl.ds(off, SC_CHUNK)])

    return gather(table, idx)


def _final_kernel(route_ref, x1_ref, mod_ref, fg_ref, y0_ref, y1_ref, o_ref):
    route = route_ref[...]
    y = route[:, 2:3] * _unpack_pairs(y0_ref[...]) + route[:, 3:4] * _unpack_pairs(y1_ref[...])
    gate2 = mod_ref[0, 5:6, :]
    x2 = x1_ref[...] + gate2 * y
    ms = jnp.mean(x2 * x2, axis=-1, keepdims=True)
    o_ref[...] = x2 * lax.rsqrt(ms + NORM_EPS) * fg_ref[...]


def _moe_combine_final(dest_slots, route, x1, mod, final_g, ys, s, tm=512):
    t, d = x1.shape
    spb = s // tm
    picked = _sc_gather_rows(ys, dest_slots)
    tiles = t // tm
    return pl.pallas_call(
        _final_kernel,
        out_shape=jax.ShapeDtypeStruct((t, d), F32),
        grid=(tiles,),
        in_specs=[pl.BlockSpec((tm, LANES), lambda i: (i, 0)),
                  pl.BlockSpec((tm, d), lambda i: (i, 0)),
                  pl.BlockSpec((1, 6, d), lambda i: (i // spb, 0, 0)),
                  pl.BlockSpec((1, d), lambda i: (0, 0)),
                  pl.BlockSpec((tm, picked.shape[1]), lambda i: (i, 0)),
                  pl.BlockSpec((tm, picked.shape[1]), lambda i: (tiles + i, 0))],
        out_specs=pl.BlockSpec((tm, d), lambda i: (i, 0)),
        compiler_params=_cparams(("arbitrary",)),
        name="moe_combine_final",
    )(route, x1, mod, final_g.reshape(1, d), picked, picked)


def _moe_plan(route, counts):
    t = route.shape[0]
    m = t * TOP_K
    flat_e = route[:, :TOP_K].astype(jnp.int32).T.reshape(m)
    rank = route[:, 4:4 + TOP_K].astype(jnp.int32).T.reshape(m)
    counts = counts[0, N_GROUPS:N_GROUPS + N_EXPERTS].astype(jnp.int32)
    padded = (counts + MOE_ROWS - 1) // MOE_ROWS * MOE_ROWS
    pad_end = jnp.cumsum(padded)
    pad_start = pad_end - padded
    experts = jnp.arange(N_EXPERTS, dtype=jnp.int32)
    start_of = jnp.sum(jnp.where(flat_e[:, None] == experts[None, :], pad_start[None, :], 0), axis=1)
    dest = (start_of + rank).astype(jnp.int32)
    n_blocks = m // MOE_ROWS + N_EXPERTS
    blk_start = jnp.arange(n_blocks, dtype=jnp.int32) * MOE_ROWS
    blk_e = jnp.minimum(jnp.sum(pad_end[None, :] <= blk_start[:, None], axis=1), N_EXPERTS - 1).astype(jnp.int32)
    nused = pad_end[-1:].astype(jnp.int32)
    blk_valid = jnp.clip(counts[blk_e] - (blk_start - pad_start[blk_e]), 0, MOE_ROWS).astype(jnp.int32)
    return dest, blk_e, nused, blk_valid, n_blocks * MOE_ROWS


def kernel(x, c, ada_w, ada_b, norm1_g, w_in, fox_forget_b, shift_mu, rwkv_w0, rwkv_w2, rwkv_a0, rwkv_a2, rwkv_g2, rwkv_k_k, rwkv_k_a, rwkv_r_k, ln_x_g, ln_x_b, w_out_fox, w_out_rwkv, w_o, norm2_g, router_group_w, router_group_b, router_expert_w, router_expert_b, exp_w1, exp_w3, exp_w2, final_g):
    b, s, d = x.shape
    t = b * s
    assert ada_w.shape[0] == 1, "the final norm is fused into the last layer's combine; one layer is laid out"
    for l in range(1):
        mod = _adaln_mod(c, ada_w[l], ada_b[l])

        h = _norm_mod(x, norm1_g[l], mod, shift_idx=0, scale_idx=1)
        h2d = h.reshape(t, d)
        w = w_in[l]
        o_f = 3 * WIDTH
        o_rw = o_f + HEADS
        o_g = o_rw + SHIFT_WIDTH
        perm = jnp.argsort(fox_forget_b[l])
        by_head = lambda m: m.reshape(d, HEADS, HEAD_DIM)[:, perm]
        wq = by_head(w[:, :WIDTH]).reshape(d, WIDTH)
        wk = by_head(w[:, WIDTH:2 * WIDTH]).reshape(d, WIDTH)
        wv = by_head(w[:, 2 * WIDTH:o_f]).reshape(d, WIDTH)
        qkv = _matmul(h2d, jnp.concatenate([wq, wk, wv], axis=1).astype(BF16), BF16, name="proj_qkv")
        p_rw = _matmul(h2d, w[:, o_rw:o_g].astype(BF16), F32, tn=896, name="proj_rwkv")
        gate = _matmul(h2d, w[:, o_g:].astype(BF16), BF16, name="proj_gate")
        fbias = _forget_bias(h, w[:, o_f:o_rw][:, perm], fox_forget_b[l][perm])
        o_fox = _fox_attention(qkv.reshape(b, s, 3 * WIDTH), fbias)
        w_of = w_out_fox[l].reshape(HEADS, HEAD_DIM, d)[perm].reshape(WIDTH, d)
        o_rwkv = _rwkv_branch(p_rw.reshape(b, s, SHIFT_WIDTH), shift_mu[l], rwkv_w0[l], rwkv_w2[l],
                              rwkv_a0[l], rwkv_a2[l], rwkv_g2[l], rwkv_k_k[l], rwkv_k_a[l],
                              rwkv_r_k[l], ln_x_g[l], ln_x_b[l])

        wr = jnp.zeros((d, LANES), F32)
        wr = wr.at[:, :N_GROUPS].set(router_group_w[l]).at[:, N_GROUPS:N_GROUPS + N_EXPERTS].set(router_expert_w[l])
        br = jnp.zeros((1, LANES), F32)
        br = br.at[0, :N_GROUPS].set(router_group_b[l]).at[0, N_GROUPS:N_GROUPS + N_EXPERTS].set(router_expert_b[l])
        wr_hi = wr.astype(BF16)
        wr_lo = (wr - wr_hi.astype(F32)).astype(BF16)
        x1, h2, route, counts = _merge_out_router(
            o_fox.reshape(t, WIDTH), o_rwkv.reshape(t, WIDTH), gate, x, mod,
            w_of.astype(BF16), w_out_rwkv[l].astype(BF16), w_o[l].astype(BF16), norm2_g[l],
            jnp.stack([wr_hi, wr_lo]), br)

        dest, blk_e, nused, blk_valid, rows = _moe_plan(route, counts)
        xs = _sc_scatter_rows(h2, dest, rows)
        ys = _moe_experts(xs, blk_e, nused, blk_valid, exp_w1[l], exp_w3[l], exp_w2[l])
        out = _moe_combine_final(dest, route, x1, mod, final_g, ys, s)
    return out.reshape(b, s, d)
```

```python
import functools

import jax
import jax.numpy as jnp
from jax import lax
from jax.experimental import pallas as pl
from jax.experimental.pallas import tpu as pltpu
from jax.experimental.pallas import tpu_sc as plsc

F32 = jnp.float32
BF16 = jnp.bfloat16
HIGHEST = lax.Precision.HIGHEST

HEADS = 8
HEAD_DIM = 64
WIDTH = HEADS * HEAD_DIM
DECAY_LORA = 64
AAA_LORA = 64
GATE_LORA = 128
SHIFT_WIDTH = 3 * WIDTH + DECAY_LORA + AAA_LORA + GATE_LORA
LN_X_EPS = 64e-5
NORM_EPS = 1e-6
N_GROUPS = 4
EXPERTS_PER_GROUP = 8
N_EXPERTS = N_GROUPS * EXPERTS_PER_GROUP
TOP_K = 2

LANES = 128
FOX_BQ = 1024
FOX_BK = 512
CHUNK = 64
RWKV_TILE = 256
MOE_ROWS = 512
VMEM_LIMIT = 48 * 1024 * 1024


def _cparams(sem):
    return pltpu.CompilerParams(dimension_semantics=sem, vmem_limit_bytes=VMEM_LIMIT)


def _dot(a, b):
    return jnp.dot(a.astype(BF16), b.astype(BF16), preferred_element_type=F32)


def _dot_nt(a, b):
    return lax.dot_general(a.astype(BF16), b.astype(BF16), (((1,), (1,)), ((), ())),
                           preferred_element_type=F32)


def _dot_tn(a, b):
    return lax.dot_general(a.astype(BF16), b.astype(BF16), (((0,), (0,)), ((), ())),
                           preferred_element_type=F32)


def _pack_pairs(x):
    n = x.shape[1] // 2
    lo = lax.bitcast_convert_type(x[:, :n].astype(BF16).astype(F32), jnp.uint32) >> 16
    hi = lax.bitcast_convert_type(x[:, n:].astype(BF16).astype(F32), jnp.uint32) & jnp.uint32(0xFFFF0000)
    return lo | hi


def _unpack_pairs(u):
    lo = lax.bitcast_convert_type(u << 16, F32)
    hi = lax.bitcast_convert_type(u & jnp.uint32(0xFFFF0000), F32)
    return jnp.concatenate([lo, hi], axis=1)


def _softplus(x):
    return jnp.maximum(x, 0.0) + jnp.log1p(jnp.exp(-jnp.abs(x)))


def _mod_kernel(c_ref, w_ref, b_ref, o_ref):
    c = c_ref[...]
    sc = c * jax.nn.sigmoid(c)
    o_ref[...] = jnp.dot(sc, w_ref[...], precision=HIGHEST, preferred_element_type=F32) + b_ref[...]


def _adaln_mod(c, ada_w, ada_b):
    b, d = c.shape
    n = ada_w.shape[1]
    rows = 8
    cp = jnp.zeros((rows, d), F32).at[:b].set(c)
    tn = 1024
    out = pl.pallas_call(
        _mod_kernel,
        out_shape=jax.ShapeDtypeStruct((rows, n), F32),
        grid=(n // tn,),
        in_specs=[pl.BlockSpec((rows, d), lambda j: (0, 0)),
                  pl.BlockSpec((d, tn), lambda j: (0, j)),
                  pl.BlockSpec((1, tn), lambda j: (0, j))],
        out_specs=pl.BlockSpec((rows, tn), lambda j: (0, j)),
        compiler_params=_cparams(("arbitrary",)),
        name="adaln_mod",
    )(cp, ada_w, ada_b.reshape(1, n))
    return out[:b].reshape(b, 6, d)


def _norm_mod_kernel(x_ref, g_ref, mod_ref, o_ref, *, shift_idx, scale_idx):
    x = x_ref[0]
    ms = jnp.mean(x * x, axis=-1, keepdims=True)
    y = x * lax.rsqrt(ms + NORM_EPS) * g_ref[...]
    scale = mod_ref[0, scale_idx:scale_idx + 1, :]
    shift = mod_ref[0, shift_idx:shift_idx + 1, :]
    o_ref[0] = (y * (1.0 + scale) + shift).astype(o_ref.dtype)


def _norm_mod(x, g, mod, shift_idx, scale_idx, tm=1024):
    b, s, d = x.shape
    return pl.pallas_call(
        functools.partial(_norm_mod_kernel, shift_idx=shift_idx, scale_idx=scale_idx),
        out_shape=jax.ShapeDtypeStruct((b, s, d), BF16),
        grid=(b, s // tm),
        in_specs=[pl.BlockSpec((1, tm, d), lambda i, j: (i, j, 0)),
                  pl.BlockSpec((1, d), lambda i, j: (0, 0)),
                  pl.BlockSpec((1, 6, d), lambda i, j: (i, 0, 0))],
        out_specs=pl.BlockSpec((1, tm, d), lambda i, j: (i, j, 0)),
        compiler_params=_cparams(("arbitrary", "arbitrary")),
        name="norm1_mod",
    )(x, g.reshape(1, d), mod)


def _mm_kernel(a_ref, w_ref, o_ref):
    o_ref[...] = jnp.dot(a_ref[...], w_ref[...], preferred_element_type=F32).astype(o_ref.dtype)


def _matmul(a, w, out_dtype, tm=2048, tn=512, name="proj"):
    t, k = a.shape
    n = w.shape[1]
    tn = min(tn, n)
    return pl.pallas_call(
        _mm_kernel,
        out_shape=jax.ShapeDtypeStruct((t, n), out_dtype),
        grid=(t // tm, n // tn),
        in_specs=[pl.BlockSpec((tm, k), lambda i, j: (i, 0)),
                  pl.BlockSpec((k, tn), lambda i, j: (0, j))],
        out_specs=pl.BlockSpec((tm, tn), lambda i, j: (i, j)),
        compiler_params=_cparams(("arbitrary", "arbitrary")),
        name=name,
    )(a, w)


LOG2E = 1.4426950408889634


def _split3(x):
    hi = x.astype(BF16)
    r1 = x - hi.astype(F32)
    mid = r1.astype(BF16)
    lo = (r1 - mid.astype(F32)).astype(BF16)
    return hi, mid, lo


def _fcum_kernel(h_ref, wf_ref, fb_ref, sel_ref, qk_ref, hsel_ref, o_ref, bounds_ref, carry_ref, *, ts):
    @pl.when(pl.program_id(0) == 0)
    def _():
        carry_ref[...] = jnp.zeros_like(carry_ref)

    ri = lax.broadcasted_iota(jnp.int32, (ts, ts), 0)
    ci = lax.broadcasted_iota(jnp.int32, (ts, ts), 1)
    tri = jnp.where(ri >= ci, 1.0, 0.0).astype(BF16)
    for bi in range(h_ref.shape[0]):
        f = jnp.dot(h_ref[bi], wf_ref[...], preferred_element_type=F32) + fb_ref[...]
        lf = -_softplus(-f)
        cum = carry_ref[bi]
        for piece in _split3(lf):
            cum = cum + jnp.dot(tri, piece, preferred_element_type=F32)
        carry_ref[bi] = cum[ts - 1:ts, :]
        out = jnp.zeros((ts, sel_ref.shape[2]), F32)
        for idx, piece in enumerate(_split3(cum * (-LOG2E))):
            out = out + jnp.dot(piece, sel_ref[idx], preferred_element_type=F32)
        o_ref[bi] = out.astype(o_ref.dtype)

        x = qk_ref[bi].astype(F32)
        ssq = _dot(x * x, hsel_ref[...])
        dots = _dot(x[:, :WIDTH] * x[:, WIDTH:], hsel_ref[0:WIDTH, :])
        sub = lax.broadcasted_iota(jnp.int32, bounds_ref.shape[2:], 0)
        bounds_ref[bi, 0] = jnp.where(sub == 0, jnp.max(ssq, axis=0, keepdims=True),
                                      jnp.min(dots, axis=0, keepdims=True))


def _forget_bias(h, wf, fb, qkv, ts=FOX_BK):
    b, s, d = h.shape
    pairs = HEADS // 2
    hsel = (jnp.arange(2 * WIDTH)[:, None] // HEAD_DIM == jnp.arange(LANES)[None, :]).astype(BF16)
    wf_p = jnp.zeros((d, LANES), F32).at[:, :HEADS].set(wf).astype(BF16)
    fb_p = jnp.zeros((1, LANES), F32).at[0, :HEADS].set(fb)
    hh = jnp.arange(HEADS)
    sel = jnp.zeros((3, LANES, pairs * LANES), F32)
    for piece in range(3):
        sel = sel.at[piece, hh, (hh // 2) * LANES + (hh % 2) * 3 + piece].set(1.0)
    return pl.pallas_call(
        functools.partial(_fcum_kernel, ts=ts),
        out_shape=(jax.ShapeDtypeStruct((b, s, pairs * LANES), BF16),
                   jax.ShapeDtypeStruct((b, s // ts, 8, LANES), F32)),
        grid=(s // ts,),
        in_specs=[pl.BlockSpec((b, ts, d), lambda j: (0, j, 0)),
                  pl.BlockSpec((d, LANES), lambda j: (0, 0)),
                  pl.BlockSpec((1, LANES), lambda j: (0, 0)),
                  pl.BlockSpec((3, LANES, pairs * LANES), lambda j: (0, 0, 0)),
                  pl.BlockSpec((b, ts, 2 * WIDTH), lambda j: (0, j, 0)),
                  pl.BlockSpec((2 * WIDTH, LANES), lambda j: (0, 0))],
        out_specs=(pl.BlockSpec((b, ts, pairs * LANES), lambda j: (0, j, 0)),
                   pl.BlockSpec((b, 1, 8, LANES), lambda j: (0, j, 0, 0))),
        scratch_shapes=[pltpu.VMEM((b, 1, LANES), F32)],
        compiler_params=_cparams(("arbitrary",)),
        name="forget_bias",
    )(h, wf_p, fb_p, sel.astype(BF16), qkv, hsel)


def _fox_kernel(first_ref, q_ref, k_ref, v_ref, a_ref, o_ref, m_ref, acc_ref, *, bq, bk):
    i = pl.program_id(2)
    lane = lax.broadcasted_iota(jnp.int32, (bq, LANES), 1)
    first = lane < HEAD_DIM
    qs = q_ref[0].astype(F32) * (HEAD_DIM ** -0.5 * LOG2E)
    aug0 = jnp.where(lane < 3, 1.0, 0.0)
    aug1 = jnp.where((lane >= 3) & (lane < 6), 1.0, 0.0)
    q01 = jnp.concatenate([jnp.concatenate([jnp.where(first, qs, 0.0), aug0], axis=1),
                           jnp.concatenate([jnp.where(first, 0.0, qs), aug1], axis=1)],
                          axis=0).astype(BF16)
    m_ref[...] = jnp.full_like(m_ref, -jnp.inf)
    acc_ref[...] = jnp.zeros_like(acc_ref)
    lane_k = lax.broadcasted_iota(jnp.int32, (bk, LANES), 1)
    keep_first = jnp.where(lane_k < HEAD_DIM, 1.0, 0.0).astype(BF16)
    keep_second = jnp.where(lane_k < HEAD_DIM, 0.0, 1.0).astype(BF16)

    def both_heads(x, r0):
        if r0 == 0:
            return x[...]
        return jnp.concatenate([x[r0:bq], x[bq + r0:2 * bq]], axis=0)

    def logits(j, r0=0):
        start = pl.multiple_of(j * bk, bk)
        kb = jnp.concatenate([k_ref[0, pl.ds(start, bk), :], a_ref[0, pl.ds(start, bk), :]], axis=1)
        return lax.dot_general(both_heads(q01, r0), kb, (((1,), (1,)), ((), ())),
                               preferred_element_type=F32)

    def consume(j, z, r0=0):
        nr = bq - r0
        start = pl.multiple_of(j * bk, bk)
        vb = v_ref[0, pl.ds(start, bk), :]
        m_prev = both_heads(m_ref, r0)
        m_new = jnp.maximum(m_prev, jnp.max(z, axis=1, keepdims=True))
        alpha = jnp.exp2(m_prev - m_new)
        p = jnp.exp2(z - jnp.concatenate([m_new] * (bk // LANES), axis=1)).astype(BF16)
        pv = jnp.concatenate(
            [jnp.dot(p[:nr], vb * keep_first + keep_second, preferred_element_type=F32),
             jnp.dot(p[nr:], vb * keep_second + keep_first, preferred_element_type=F32)], axis=0)
        acc_new = alpha * both_heads(acc_ref, r0) + pv
        if r0 == 0:
            acc_ref[...] = acc_new
            m_ref[...] = m_new
        else:
            for half, dst in ((slice(0, nr), slice(r0, bq)), (slice(nr, 2 * nr), slice(bq + r0, 2 * bq))):
                acc_ref[dst] = acc_new[half]
                m_ref[dst] = m_new[half]

    per_q = bq // bk
    n_full = i * per_q

    j_first = first_ref[pl.program_id(0), pl.program_id(1), i]
    odd = (n_full - j_first) & 1

    @pl.when(odd == 1)
    def _():
        consume(j_first, logits(j_first))

    def body(step, carry):
        j = j_first + odd + 2 * step
        z_a = logits(j)
        z_b = logits(j + 1)
        consume(j, z_a)
        consume(j + 1, z_b)
        return carry

    lax.fori_loop(0, (n_full - j_first) // 2, body, 0)
    for d in range(per_q):
        r0 = d * bk
        row = lax.broadcasted_iota(jnp.int32, (bq - r0, bk), 0)
        col = lax.broadcasted_iota(jnp.int32, (bq - r0, bk), 1)
        keep = col <= row
        z = logits(n_full + d, r0)
        consume(n_full + d, jnp.where(jnp.concatenate([keep, keep], axis=0), z, -jnp.inf), r0)
    acc = acc_ref[...]
    o = acc / pltpu.roll(acc, HEAD_DIM, 1)
    o_ref[0] = jnp.where(first, o[:bq], o[bq:]).astype(o_ref.dtype)


FOX_ZERO_LOG2 = 136.0


def _fox_first_block(fbias, bounds, bq, bk):
    b, s, _ = fbias.shape
    nq, nk = s // bq, s // bk
    n_heads2 = 2 * HEADS
    nrm = jnp.sqrt(bounds[:, :, 0, :n_heads2]) * 1.01
    q_scale = HEAD_DIM ** -0.5 * LOG2E
    qn = nrm[..., :HEADS] * (q_scale * 1.01)
    kn = nrm[..., HEADS:]
    per_q = bq // bk
    qn_i = qn.reshape(b, nq, per_q, HEADS).max(axis=2)
    kn_i = kn.reshape(b, nq, per_q, HEADS).max(axis=2)
    kn_pre = lax.cummax(kn, axis=1)
    diag = bounds[:, :, 1, :HEADS].reshape(b, nq, per_q, HEADS).min(axis=2) * q_scale
    diag_low = diag - 0.02 * qn_i * kn_i
    pairs = HEADS // 2
    def bias_rows(rows):
        pieces = rows.astype(F32).reshape(b, -1, pairs, LANES)[..., :6].reshape(b, -1, pairs, 2, 3)
        return pieces.sum(-1).reshape(b, -1, HEADS)

    nb_end = bias_rows(fbias[:, bk - 1::bk])
    nb_start = bias_rows(fbias[:, ::bq])
    gap = nb_start[:, :, None, :] - nb_end[:, None, :, :]
    need = qn_i[:, :, None, :] * kn_pre[:, None, :, :] - diag_low[:, :, None, :] + FOX_ZERO_LOG2
    skip = (gap > need).reshape(b, nq, nk, pairs, 2).all(axis=-1)
    n_full = jnp.arange(nq) * per_q
    skip = skip & (jnp.arange(nk)[None, None, :, None] < n_full[None, :, None, None])
    first = jnp.argmin(skip, axis=2)
    return first.transpose(0, 2, 1).astype(jnp.int32)


def _fox_attention(qkv, fbias, bounds, bq=FOX_BQ, bk=FOX_BK):
    b, s, _ = qkv.shape
    pairs = HEADS // 2
    cb = WIDTH // LANES
    first = _fox_first_block(fbias, bounds, bq, bk)
    grid_spec = pltpu.PrefetchScalarGridSpec(
        num_scalar_prefetch=1,
        grid=(b, pairs, s // bq),
        in_specs=[pl.BlockSpec((1, bq, LANES), lambda bi, hp, i, fr: (bi, i, hp)),
                  pl.BlockSpec((1, s, LANES), lambda bi, hp, i, fr: (bi, 0, cb + hp)),
                  pl.BlockSpec((1, s, LANES), lambda bi, hp, i, fr: (bi, 0, 2 * cb + hp)),
                  pl.BlockSpec((1, s, LANES), lambda bi, hp, i, fr: (bi, 0, hp))],
        out_specs=pl.BlockSpec((1, bq, LANES), lambda bi, hp, i, fr: (bi, i, hp)),
        scratch_shapes=[pltpu.VMEM((2 * bq, LANES), F32), pltpu.VMEM((2 * bq, LANES), F32)],
    )
    return pl.pallas_call(
        functools.partial(_fox_kernel, bq=bq, bk=bk),
        out_shape=jax.ShapeDtypeStruct((b, s, WIDTH), BF16),
        grid_spec=grid_spec,
        compiler_params=_cparams(("arbitrary", "arbitrary", "arbitrary")),
        name="fox_attention",
    )(first, qkv, qkv, qkv, fbias)


PAIR = 2 * HEAD_DIM
GROUPS = WIDTH // PAIR


def _group(x, g):
    return x[:, g * PAIR:(g + 1) * PAIR]


def _head_sum(x, bd2):
    return jnp.concatenate([_dot(_group(x, g), bd2) for g in range(GROUPS)], axis=1)


def _head_apply(mats, x, lane_first):
    rows = mats.shape[1]
    outs = []
    for g in range(GROUPS):
        res = _dot(mats[2 * g:2 * g + 2].reshape(2 * rows, rows), _group(x, g))
        outs.append(jnp.where(lane_first, res[:rows], res[rows:]))
    return jnp.concatenate(outs, axis=1)


def _rwkv_kernel(p_ref, mu_ref, w0_ref, a0_ref, kk_ref, ka_ref, rk_ref, lng_ref, lnb_ref,
                 wwa_ref, g2_ref, bd_ref, o_ref, st_ref, prev_ref):
    L = CHUNK

    @pl.when(pl.program_id(1) == 0)
    def _():
        st_ref[...] = jnp.zeros_like(st_ref)
        prev_ref[...] = jnp.zeros_like(prev_ref)

    p = p_ref[0]
    T = p.shape[0]
    rowi = lax.broadcasted_iota(jnp.int32, p.shape, 0)
    prev = jnp.where(rowi == 0, prev_ref[...], pltpu.roll(p, 1, 0))
    prev_ref[...] = p[T - 1:T, :]
    ps = p + (prev - p) * mu_ref[...]
    r = ps[:, 0:WIDTH]
    k = ps[:, WIDTH:2 * WIDTH]
    v = ps[:, 2 * WIDTH:3 * WIDTH]
    wa_in = ps[:, 3 * WIDTH:3 * WIDTH + DECAY_LORA + AAA_LORA]
    gd = ps[:, 3 * WIDTH + DECAY_LORA + AAA_LORA:]
    lane_wa = lax.broadcasted_iota(jnp.int32, wa_in.shape, 1)
    wa_act = jnp.where(lane_wa < DECAY_LORA, jnp.tanh(wa_in), wa_in)
    wa = _dot(wa_act, wwa_ref[...])
    log_w = -_softplus(-(w0_ref[...] + wa[:, :WIDTH])) - 0.5
    lw = -jnp.exp(log_w)
    a = jax.nn.sigmoid(a0_ref[...] + wa[:, WIDTH:])
    out_gate = _dot(jax.nn.sigmoid(gd), g2_ref[...])
    bd = bd_ref[...]
    kk0 = k * kk_ref[...]
    kk = kk0 * lax.rsqrt(jnp.maximum(_head_sum(kk0 * kk0, bd), 1e-24))
    k2 = k * (1.0 + (a - 1.0) * ka_ref[...])
    av = -kk
    bv = kk * a

    n_sub = T // L
    rt_i = lax.broadcasted_iota(jnp.int32, (T, T), 0)
    ct_i = lax.broadcasted_iota(jnp.int32, (T, T), 1)
    tri_tile = (rt_i >= ct_i) & (rt_i // L == ct_i // L)
    cl = _split_dot_left(jnp.where(tri_tile, 1.0, 0.0).astype(BF16), lw)
    cl_end = jnp.concatenate([jnp.broadcast_to(cl[(c + 1) * L - 1:(c + 1) * L, :], (L, WIDTH))
                              for c in range(n_sub)], axis=0)
    at_all = av * jnp.exp(cl - lw)
    rt_all = r * jnp.exp(cl)
    einv = jnp.exp(-cl)
    bt_all = bv * einv
    kt_all = k2 * einv
    edec = jnp.exp(cl_end - cl)
    b_end_all = bv * edec
    k_end_all = k2 * edec

    ri = lax.broadcasted_iota(jnp.int32, (L, L), 0)
    ci = lax.broadcasted_iota(jnp.int32, (L, L), 1)
    tri_incl = ri >= ci
    tri_strict = ri > ci
    eye = jnp.where(ri == ci, 1.0, 0.0)
    lane_first = lax.broadcasted_iota(jnp.int32, (L, PAIR), 1) < HEAD_DIM
    qr = lax.broadcasted_iota(jnp.int32, (PAIR, PAIR), 0) < HEAD_DIM
    qc = lax.broadcasted_iota(jnp.int32, (PAIR, PAIR), 1) < HEAD_DIM
    same_head = qr == qc

    def bmm(x, y):
        return lax.dot_general(x.astype(BF16), y.astype(BF16), (((2,), (1,)), ((0,), (0,))),
                               preferred_element_type=F32)

    def chunk_terms(c):
        rows = slice(c * L, (c + 1) * L)
        at, rt, bt, kt, vc = at_all[rows], rt_all[rows], bt_all[rows], kt_all[rows], v[rows]
        sb_heads, sk_heads = [], []
        for g in range(GROUPS):
            at_g, rt_g = _group(at, g), _group(rt, g)
            lhs = jnp.concatenate([jnp.where(lane_first, at_g, 0.0), jnp.where(lane_first, rt_g, 0.0),
                                   jnp.where(lane_first, 0.0, at_g), jnp.where(lane_first, 0.0, rt_g)],
                                  axis=0).astype(BF16)
            sb_g = _dot_nt(lhs, _group(bt, g))
            sk_g = _dot_nt(lhs, _group(kt, g))
            for hh in range(2):
                sb_heads.append(sb_g[hh * 2 * L:(hh + 1) * 2 * L])
                sk_heads.append(sk_g[hh * 2 * L:(hh + 1) * 2 * L])
        sb = jnp.stack(sb_heads)
        sk = jnp.stack(sk_heads)
        n_ab = jnp.where(tri_strict, sb[:, :L, :], 0.0)
        a_ak = jnp.where(tri_strict, sk[:, :L, :], 0.0)
        a_rb = jnp.where(tri_incl, sb[:, L:, :], 0.0)
        a_rk = jnp.where(tri_incl, sk[:, L:, :], 0.0)
        tinv = eye + n_ab
        pw = bmm(n_ab, n_ab)
        span = 2
        while 2 * span < L:
            both = bmm(jnp.concatenate([tinv, pw], axis=1), pw)
            tinv = tinv + both[:, :L, :]
            pw = both[:, L:, :]
            span *= 2
        tinv = tinv + bmm(tinv, pw)
        av_term = _head_apply(a_ak, vc, lane_first)
        pm = _head_apply(tinv, at, lane_first)
        qm = _head_apply(tinv, av_term, lane_first)
        rkv = _head_apply(a_rk, vc, lane_first)
        return pm, qm, rkv, a_rb

    terms = [chunk_terms(c) for c in range(n_sub)]

    y_chunks = []
    for c in range(n_sub):
        rows = slice(c * L, (c + 1) * L)
        pm, qm, rkv, a_rb = terms[c]
        rt, vc, b_end, k_end = rt_all[rows], v[rows], b_end_all[rows], k_end_all[rows]
        gam_last = jnp.exp(cl[(c + 1) * L - 1:(c + 1) * L, :])
        u_parts, ys_parts = [], []
        for g in range(GROUPS):
            pr = _dot_nt(jnp.concatenate([_group(pm, g), _group(rt, g)], axis=0), st_ref[g])
            u_parts.append(pr[:L] + _group(qm, g))
            ys_parts.append(pr[L:])
        u = jnp.concatenate(u_parts, axis=1)
        y_chunks.append(jnp.concatenate(ys_parts, axis=1) + _head_apply(a_rb, u, lane_first) + rkv)
        for g in range(GROUPS):
            upd = _dot_tn(_group(u, g), _group(b_end, g)) + _dot_tn(_group(vc, g), _group(k_end, g))
            st_ref[g] = st_ref[g] * _group(gam_last, g) + jnp.where(same_head, upd, 0.0)
    y = jnp.concatenate(y_chunks, axis=0)

    inv_n = 1.0 / HEAD_DIM
    mean = _head_sum(y, bd) * inv_n
    dlt = y - mean
    var = _head_sum(dlt * dlt, bd) * inv_n
    yn = dlt * lax.rsqrt(var + LN_X_EPS) * lng_ref[...] + lnb_ref[...]
    bonus = _head_sum(r * k2 * rk_ref[...], bd) * v
    o_ref[0] = ((yn + bonus) * out_gate).astype(o_ref.dtype)


def _split_dot_left(w_bf16, x):
    hi, mid, lo = _split3(x)
    return (jnp.dot(w_bf16, hi, preferred_element_type=F32)
            + jnp.dot(w_bf16, mid, preferred_element_type=F32)
            + jnp.dot(w_bf16, lo, preferred_element_type=F32))


def _rwkv_branch(p_rw, mu, w0, w2, a0, a2, g2, k_k, k_a, r_k, ln_g, ln_b):
    b, s, sw = p_rw.shape
    row = lambda t: t.reshape(1, -1).astype(F32)
    wwa = jnp.zeros((DECAY_LORA + AAA_LORA, 2 * WIDTH), F32)
    wwa = wwa.at[:DECAY_LORA, :WIDTH].set(w2).at[DECAY_LORA:, WIDTH:].set(a2).astype(BF16)
    hid = jnp.arange(PAIR) // HEAD_DIM
    bd = (hid[:, None] == hid[None, :]).astype(BF16)
    const = lambda shape: pl.BlockSpec(shape, lambda i, j: (0,) * len(shape))
    return pl.pallas_call(
        _rwkv_kernel,
        out_shape=jax.ShapeDtypeStruct((b, s, WIDTH), BF16),
        grid=(b, s // RWKV_TILE),
        in_specs=[pl.BlockSpec((1, RWKV_TILE, sw), lambda i, j: (i, j, 0)),
                  const((1, sw)), const((1, WIDTH)), const((1, WIDTH)), const((1, WIDTH)),
                  const((1, WIDTH)), const((1, WIDTH)), const((1, WIDTH)), const((1, WIDTH)),
                  const((DECAY_LORA + AAA_LORA, 2 * WIDTH)), const((GATE_LORA, WIDTH)),
                  const((PAIR, PAIR))],
        out_specs=pl.BlockSpec((1, RWKV_TILE, WIDTH), lambda i, j: (i, j, 0)),
        scratch_shapes=[pltpu.VMEM((GROUPS, PAIR, PAIR), F32), pltpu.VMEM((1, sw), F32)],
        compiler_params=_cparams(("arbitrary", "arbitrary")),
        name="rwkv7_scan",
    )(p_rw, row(mu), row(w0), row(a0), row(k_k), row(k_a), row(r_k), row(ln_g), row(ln_b),
      wwa, g2.astype(BF16), bd)


def _out_kernel(of_ref, orw_ref, gate_ref, x_ref, mod_ref, wof_ref, wor_ref, wo_ref, n2g_ref,
                wr_ref, br_ref, x1_ref, h2_ref, route_ref, counts_ref, cnt_ref):
    d = x_ref.shape[-1]
    gate = 0.5 * jnp.tanh(0.5 * gate_ref[...].astype(F32)) + 0.5
    merged = (gate[:, :d] * jnp.dot(of_ref[...], wof_ref[...], preferred_element_type=F32)
              + gate[:, d:] * jnp.dot(orw_ref[...], wor_ref[...], preferred_element_type=F32))
    gate1 = mod_ref[0, 2:3, :]
    shift2 = mod_ref[0, 3:4, :]
    scale2 = mod_ref[0, 4:5, :]
    x1 = x_ref[...] + gate1 * jnp.dot(merged.astype(BF16), wo_ref[...], preferred_element_type=F32)
    x1_ref[...] = x1
    ms = jnp.mean(x1 * x1, axis=-1, keepdims=True)
    h2 = x1 * lax.rsqrt(ms + NORM_EPS) * n2g_ref[...] * (1.0 + scale2) + shift2
    h2_ref[...] = _pack_pairs(h2)

    h2_hi = h2.astype(BF16)
    h2_lo = (h2 - h2_hi.astype(F32)).astype(BF16)
    logits = (jnp.dot(h2_hi, wr_ref[0], preferred_element_type=F32)
              + jnp.dot(h2_lo, wr_ref[0], preferred_element_type=F32)
              + jnp.dot(h2_hi, wr_ref[1], preferred_element_type=F32)) + br_ref[...]
    lane = lax.broadcasted_iota(jnp.int32, logits.shape, 1)
    neg = -jnp.inf
    big = jnp.int32(LANES)
    gl = jnp.where(lane < N_GROUPS, logits, neg)
    gmax = jnp.max(gl, axis=1, keepdims=True)
    gidx = jnp.min(jnp.where(gl == gmax, lane, big), axis=1, keepdims=True)
    g_p = 1.0 / jnp.sum(jnp.exp(gl - gmax), axis=1, keepdims=True)
    e_lane = lane - N_GROUPS
    in_grp = (e_lane >= 0) & (e_lane < N_EXPERTS) & ((e_lane // EXPERTS_PER_GROUP) == gidx)
    sel = jnp.where(in_grp, logits, neg)
    m1 = jnp.max(sel, axis=1, keepdims=True)
    i1 = jnp.min(jnp.where(sel == m1, lane, big), axis=1, keepdims=True)
    sel2 = jnp.where(lane == i1, neg, sel)
    m2 = jnp.max(sel2, axis=1, keepdims=True)
    i2 = jnp.min(jnp.where(sel2 == m2, lane, big), axis=1, keepdims=True)
    e21 = jnp.exp(m2 - m1)
    w_first = g_p / (1.0 + e21)
    w_second = g_p * e21 / (1.0 + e21)
    @pl.when(pl.program_id(0) == 0)
    def _():
        cnt_ref[...] = jnp.zeros_like(cnt_ref)

    tm = logits.shape[0]
    oh1 = lane == i1
    oh2 = lane == i2
    both = jnp.where(oh1 | oh2, 1.0, 0.0)
    before = (lax.broadcasted_iota(jnp.int32, (tm, tm), 0)
              > lax.broadcasted_iota(jnp.int32, (tm, tm), 1))
    seen = jnp.dot(jnp.where(before, 1.0, 0.0).astype(BF16), both.astype(BF16),
                   preferred_element_type=F32) + cnt_ref[...]
    rank1 = jnp.sum(jnp.where(oh1, seen, 0.0), axis=1, keepdims=True)
    rank2 = jnp.sum(jnp.where(oh2, seen, 0.0), axis=1, keepdims=True)
    cnt_ref[...] = cnt_ref[...] + jnp.sum(both, axis=0, keepdims=True)
    counts_ref[...] = jnp.broadcast_to(cnt_ref[...], counts_ref.shape)

    route = jnp.where(lane == 0, (i1 - N_GROUPS).astype(F32),
                      jnp.where(lane == 1, (i2 - N_GROUPS).astype(F32),
                                jnp.where(lane == 2, w_first,
                                          jnp.where(lane == 3, w_second,
                                                    jnp.where(lane == 4, rank1,
                                                              jnp.where(lane == 5, rank2, 0.0))))))
    route_ref[...] = route


def _merge_out_router(o_fox, o_rw, gate, x, mod, wof, wor, wo, n2g, wr, br, tm=512):
    b, s, d = x.shape
    t = b * s
    spb = s // tm
    rowspec = lambda w: pl.BlockSpec((tm, w), lambda i: (i, 0))
    const = lambda shape: pl.BlockSpec(shape, lambda i: (0,) * len(shape))
    return pl.pallas_call(
        _out_kernel,
        out_shape=(jax.ShapeDtypeStruct((t, d), F32), jax.ShapeDtypeStruct((t, d // 2), jnp.uint32),
                   jax.ShapeDtypeStruct((t, LANES), F32), jax.ShapeDtypeStruct((8, LANES), F32)),
        grid=(t // tm,),
        in_specs=[rowspec(WIDTH), rowspec(WIDTH), rowspec(2 * d), rowspec(d),
                  pl.BlockSpec((1, 6, d), lambda i: (i // spb, 0, 0)),
                  const((WIDTH, d)), const((WIDTH, d)), const((d, d)), const((1, d)),
                  const((2, d, LANES)), const((1, LANES))],
        out_specs=(rowspec(d), rowspec(d // 2), rowspec(LANES), const((8, LANES))),
        scratch_shapes=[pltpu.VMEM((1, LANES), F32)],
        compiler_params=_cparams(("arbitrary",)),
        name="merge_out_router",
    )(o_fox, o_rw, gate, x.reshape(t, d), mod, wof, wor, wo, n2g.reshape(1, d), wr, br)


SC_CORES = 2
SC_SUBCORES = 16
SC_CHUNK = 128


def _sc_scatter_rows(src, idx, n_rows):
    t, width = src.shape
    per_worker = t // (SC_CORES * SC_SUBCORES)
    n_chunks = per_worker // SC_CHUNK
    assert n_chunks * SC_CHUNK * SC_CORES * SC_SUBCORES == t and idx.shape[0] == TOP_K * t
    mesh = plsc.VectorSubcoreMesh(core_axis_name="c", subcore_axis_name="s")

    @functools.partial(
        pl.kernel, mesh=mesh,
        out_type=jax.ShapeDtypeStruct((n_rows, width), src.dtype),
        scratch_types=[pltpu.VMEM((SC_CHUNK,), jnp.int32) for _ in range(TOP_K)]
        + [pltpu.VMEM((SC_CHUNK, width), src.dtype), pltpu.SemaphoreType.DMA])
    def scatter(src_hbm, idx_hbm, out_hbm, *scratch):
        idx_v, rows_v, sem = scratch[:TOP_K], scratch[TOP_K], scratch[TOP_K + 1]
        worker = lax.axis_index("s") * SC_CORES + lax.axis_index("c")

        @pl.loop(0, n_chunks)
        def _(ci):
            off = pl.multiple_of(worker * per_worker + ci * SC_CHUNK, SC_CHUNK)
            pltpu.sync_copy(src_hbm.at[pl.ds(off, SC_CHUNK)], rows_v)
            for kk in range(TOP_K):
                pltpu.sync_copy(idx_hbm.at[pl.ds(kk * t + off, SC_CHUNK)], idx_v[kk])
            for kk in range(TOP_K):
                pltpu.async_copy(rows_v, out_hbm.at[idx_v[kk]], sem).wait()

    return scatter(src, idx)


def _expert_kernel(blk_e_ref, nused_ref, valid_ref, xs_ref, w1_ref, w3_ref, w2_ref, o_ref,
                   w1b_ref, w3b_ref, w2b_ref):
    i = pl.program_id(0)
    live = i * MOE_ROWS < nused_ref[0]
    new_expert = jnp.logical_or(i == 0, blk_e_ref[i] != blk_e_ref[jnp.maximum(i - 1, 0)])

    @pl.when(new_expert)
    def _():
        w1b_ref[...] = w1_ref[0].astype(BF16)
        w3b_ref[...] = w3_ref[0].astype(BF16)
        w2b_ref[...] = w2_ref[0].astype(BF16)

    @pl.when(live)
    def _():
        row = lax.broadcasted_iota(jnp.int32, xs_ref.shape, 0)
        xs = jnp.where(row < valid_ref[i], xs_ref[...], jnp.uint32(0))
        xb = _unpack_pairs(xs).astype(BF16)
        h1 = jnp.dot(xb, w1b_ref[...], preferred_element_type=F32)
        h3 = jnp.dot(xb, w3b_ref[...], preferred_element_type=F32)
        hh = (h1 * jax.nn.sigmoid(h1)) * h3
        o_ref[...] = _pack_pairs(jnp.dot(hh.astype(BF16), w2b_ref[...], preferred_element_type=F32))

    @pl.when(jnp.logical_not(live))
    def _():
        o_ref[...] = jnp.zeros_like(o_ref)


def _moe_experts(xs, blk_e, nused, blk_valid, w1, w3, w2):
    rows, dp = xs.shape
    _, d, de = w1.shape
    grid_spec = pltpu.PrefetchScalarGridSpec(
        num_scalar_prefetch=3,
        grid=(rows // MOE_ROWS,),
        in_specs=[pl.BlockSpec((MOE_ROWS, dp), lambda i, be, nu, va: (i, 0)),
                  pl.BlockSpec((1, d, de), lambda i, be, nu, va: (be[i], 0, 0)),
                  pl.BlockSpec((1, d, de), lambda i, be, nu, va: (be[i], 0, 0)),
                  pl.BlockSpec((1, de, d), lambda i, be, nu, va: (be[i], 0, 0))],
        out_specs=pl.BlockSpec((MOE_ROWS, dp), lambda i, be, nu, va: (i, 0)),
        scratch_shapes=[pltpu.VMEM((d, de), BF16), pltpu.VMEM((d, de), BF16), pltpu.VMEM((de, d), BF16)],
    )
    return pl.pallas_call(
        _expert_kernel,
        out_shape=jax.ShapeDtypeStruct((rows, dp), xs.dtype),
        grid_spec=grid_spec,
        compiler_params=_cparams(("arbitrary",)),
        name="moe_experts",
    )(blk_e, nused, blk_valid, xs, w1, w3, w2)


def _sc_gather_rows(table, idx):
    n_idx = idx.shape[0]
    width = table.shape[1]
    per_worker = n_idx // (SC_CORES * SC_SUBCORES)
    n_chunks = per_worker // SC_CHUNK
    assert n_chunks * SC_CHUNK * SC_CORES * SC_SUBCORES == n_idx
    mesh = plsc.VectorSubcoreMesh(core_axis_name="c", subcore_axis_name="s")

    @functools.partial(
        pl.kernel, mesh=mesh,
        out_type=jax.ShapeDtypeStruct((n_idx, width), table.dtype),
        scratch_types=[pltpu.VMEM((SC_CHUNK,), jnp.int32), pltpu.VMEM((SC_CHUNK, width), table.dtype),
                       pltpu.SemaphoreType.DMA])
    def gather(table_hbm, idx_hbm, out_hbm, idx_v, rows_v, sem):
        worker = lax.axis_index("s") * SC_CORES + lax.axis_index("c")

        @pl.loop(0, n_chunks)
        def _(ci):
            off = pl.multiple_of(worker * per_worker + ci * SC_CHUNK, SC_CHUNK)
            pltpu.sync_copy(idx_hbm.at[pl.ds(off, SC_CHUNK)], idx_v)
            pltpu.async_copy(table_hbm.at[idx_v], rows_v, sem).wait()
            pltpu.sync_copy(rows_v, out_hbm.at[pl.ds(off, SC_CHUNK)])

    return gather(table, idx)


def _final_kernel(route_ref, x1_ref, mod_ref, fg_ref, y0_ref, y1_ref, o_ref):
    route = route_ref[...]
    y = route[:, 2:3] * _unpack_pairs(y0_ref[...]) + route[:, 3:4] * _unpack_pairs(y1_ref[...])
    gate2 = mod_ref[0, 5:6, :]
    x2 = x1_ref[...] + gate2 * y
    ms = jnp.mean(x2 * x2, axis=-1, keepdims=True)
    o_ref[...] = x2 * lax.rsqrt(ms + NORM_EPS) * fg_ref[...]


def _moe_combine_final(dest_slots, route, x1, mod, final_g, ys, s, tm=512):
    t, d = x1.shape
    spb = s // tm
    picked = _sc_gather_rows(ys, dest_slots)
    tiles = t // tm
    return pl.pallas_call(
        _final_kernel,
        out_shape=jax.ShapeDtypeStruct((t, d), F32),
        grid=(tiles,),
        in_specs=[pl.BlockSpec((tm, LANES), lambda i: (i, 0)),
                  pl.BlockSpec((tm, d), lambda i: (i, 0)),
                  pl.BlockSpec((1, 6, d), lambda i: (i // spb, 0, 0)),
                  pl.BlockSpec((1, d), lambda i: (0, 0)),
                  pl.BlockSpec((tm, picked.shape[1]), lambda i: (i, 0)),
                  pl.BlockSpec((tm, picked.shape[1]), lambda i: (tiles + i, 0))],
        out_specs=pl.BlockSpec((tm, d), lambda i: (i, 0)),
        compiler_params=_cparams(("arbitrary",)),
        name="moe_combine_final",
    )(route, x1, mod, final_g.reshape(1, d), picked, picked)


def _moe_plan(route, counts):
    t = route.shape[0]
    m = t * TOP_K
    flat_e = route[:, :TOP_K].astype(jnp.int32).T.reshape(m)
    rank = route[:, 4:4 + TOP_K].astype(jnp.int32).T.reshape(m)
    counts = counts[0, N_GROUPS:N_GROUPS + N_EXPERTS].astype(jnp.int32)
    padded = (counts + MOE_ROWS - 1) // MOE_ROWS * MOE_ROWS
    pad_end = jnp.cumsum(padded)
    pad_start = pad_end - padded
    experts = jnp.arange(N_EXPERTS, dtype=jnp.int32)
    start_of = jnp.sum(jnp.where(flat_e[:, None] == experts[None, :], pad_start[None, :], 0), axis=1)
    dest = (start_of + rank).astype(jnp.int32)
    n_blocks = m // MOE_ROWS + N_EXPERTS
    blk_start = jnp.arange(n_blocks, dtype=jnp.int32) * MOE_ROWS
    blk_e = jnp.minimum(jnp.sum(pad_end[None, :] <= blk_start[:, None], axis=1), N_EXPERTS - 1).astype(jnp.int32)
    nused = pad_end[-1:].astype(jnp.int32)
    blk_valid = jnp.clip(counts[blk_e] - (blk_start - pad_start[blk_e]), 0, MOE_ROWS).astype(jnp.int32)
    return dest, blk_e, nused, blk_valid, n_blocks * MOE_ROWS


def kernel(x, c, ada_w, ada_b, norm1_g, w_in, fox_forget_b, shift_mu, rwkv_w0, rwkv_w2, rwkv_a0, rwkv_a2, rwkv_g2, rwkv_k_k, rwkv_k_a, rwkv_r_k, ln_x_g, ln_x_b, w_out_fox, w_out_rwkv, w_o, norm2_g, router_group_w, router_group_b, router_expert_w, router_expert_b, exp_w1, exp_w3, exp_w2, final_g):
    b, s, d = x.shape
    t = b * s
    assert ada_w.shape[0] == 1, "the final norm is fused into the last layer's combine; one layer is laid out"
    for l in range(1):
        mod = _adaln_mod(c, ada_w[l], ada_b[l])

        h = _norm_mod(x, norm1_g[l], mod, shift_idx=0, scale_idx=1)
        h2d = h.reshape(t, d)
        w = w_in[l]
        o_f = 3 * WIDTH
        o_rw = o_f + HEADS
        o_g = o_rw + SHIFT_WIDTH
        perm = jnp.argsort(fox_forget_b[l])
        by_head = lambda m: m.reshape(d, HEADS, HEAD_DIM)[:, perm]
        wq = by_head(w[:, :WIDTH]).reshape(d, WIDTH)
        wk = by_head(w[:, WIDTH:2 * WIDTH]).reshape(d, WIDTH)
        wv = by_head(w[:, 2 * WIDTH:o_f]).reshape(d, WIDTH)
        qkv = _matmul(h2d, jnp.concatenate([wq, wk, wv], axis=1).astype(BF16), BF16, name="proj_qkv")
        p_rw = _matmul(h2d, w[:, o_rw:o_g].astype(BF16), F32, tn=896, name="proj_rwkv")
        gate = _matmul(h2d, w[:, o_g:].astype(BF16), BF16, name="proj_gate")
        qkv = qkv.reshape(b, s, 3 * WIDTH)
        fbias, bounds = _forget_bias(h, w[:, o_f:o_rw][:, perm], fox_forget_b[l][perm], qkv)
        o_fox = _fox_attention(qkv, fbias, bounds)
        w_of = w_out_fox[l].reshape(HEADS, HEAD_DIM, d)[perm].reshape(WIDTH, d)
        o_rwkv = _rwkv_branch(p_rw.reshape(b, s, SHIFT_WIDTH), shift_mu[l], rwkv_w0[l], rwkv_w2[l],
                              rwkv_a0[l], rwkv_a2[l], rwkv_g2[l], rwkv_k_k[l], rwkv_k_a[l],
                              rwkv_r_k[l], ln_x_g[l], ln_x_b[l])

        wr = jnp.zeros((d, LANES), F32)
        wr = wr.at[:, :N_GROUPS].set(router_group_w[l]).at[:, N_GROUPS:N_GROUPS + N_EXPERTS].set(router_expert_w[l])
        br = jnp.zeros((1, LANES), F32)
        br = br.at[0, :N_GROUPS].set(router_group_b[l]).at[0, N_GROUPS:N_GROUPS + N_EXPERTS].set(router_expert_b[l])
        wr_hi = wr.astype(BF16)
        wr_lo = (wr - wr_hi.astype(F32)).astype(BF16)
        x1, h2, route, counts = _merge_out_router(
            o_fox.reshape(t, WIDTH), o_rwkv.reshape(t, WIDTH), gate, x, mod,
            w_of.astype(BF16), w_out_rwkv[l].astype(BF16), w_o[l].astype(BF16), norm2_g[l],
            jnp.stack([wr_hi, wr_lo]), br)

        dest, blk_e, nused, blk_valid, rows = _moe_plan(route, counts)
        xs = _sc_scatter_rows(h2, dest, rows)
        ys = _moe_experts(xs, blk_e, nused, blk_valid, exp_w1[l], exp_w3[l], exp_w2[l])
        out = _moe_combine_final(dest, route, x1, mod, final_g, ys, s)
    return out.reshape(b, s, d)
```

```python
import functools

import jax
import jax.numpy as jnp
from jax import lax
from jax.experimental import pallas as pl
from jax.experimental.pallas import tpu as pltpu
from jax.experimental.pallas import tpu_sc as plsc

F32 = jnp.float32
BF16 = jnp.bfloat16
HIGHEST = lax.Precision.HIGHEST

HEADS = 8
HEAD_DIM = 64
WIDTH = HEADS * HEAD_DIM
DECAY_LORA = 64
AAA_LORA = 64
GATE_LORA = 128
SHIFT_WIDTH = 3 * WIDTH + DECAY_LORA + AAA_LORA + GATE_LORA
LN_X_EPS = 64e-5
NORM_EPS = 1e-6
N_GROUPS = 4
EXPERTS_PER_GROUP = 8
N_EXPERTS = N_GROUPS * EXPERTS_PER_GROUP
TOP_K = 2

LANES = 128
FOX_BQ = 1024
FOX_BK = 512
CHUNK = 64
RWKV_TILE = 256
MOE_ROWS = 1024
VMEM_LIMIT = 48 * 1024 * 1024


def _cparams(sem):
    return pltpu.CompilerParams(dimension_semantics=sem, vmem_limit_bytes=VMEM_LIMIT)


def _dot(a, b):
    return jnp.dot(a.astype(BF16), b.astype(BF16), preferred_element_type=F32)


def _dot_nt(a, b):
    return lax.dot_general(a.astype(BF16), b.astype(BF16), (((1,), (1,)), ((), ())),
                           preferred_element_type=F32)


def _dot_tn(a, b):
    return lax.dot_general(a.astype(BF16), b.astype(BF16), (((0,), (0,)), ((), ())),
                           preferred_element_type=F32)


def _pack_pairs(x):
    n = x.shape[1] // 2
    lo = lax.bitcast_convert_type(x[:, :n].astype(BF16).astype(F32), jnp.uint32) >> 16
    hi = lax.bitcast_convert_type(x[:, n:].astype(BF16).astype(F32), jnp.uint32) & jnp.uint32(0xFFFF0000)
    return lo | hi


def _unpack_pairs(u):
    lo = lax.bitcast_convert_type(u << 16, F32)
    hi = lax.bitcast_convert_type(u & jnp.uint32(0xFFFF0000), F32)
    return jnp.concatenate([lo, hi], axis=1)


def _softplus(x):
    return jnp.maximum(x, 0.0) + jnp.log1p(jnp.exp(-jnp.abs(x)))


def _mod_kernel(c_ref, w_ref, b_ref, o_ref):
    c = c_ref[...]
    sc = c * jax.nn.sigmoid(c)
    o_ref[...] = jnp.dot(sc, w_ref[...], precision=HIGHEST, preferred_element_type=F32) + b_ref[...]


def _adaln_mod(c, ada_w, ada_b):
    b, d = c.shape
    n = ada_w.shape[1]
    rows = 8
    cp = jnp.zeros((rows, d), F32).at[:b].set(c)
    tn = 1024
    out = pl.pallas_call(
        _mod_kernel,
        out_shape=jax.ShapeDtypeStruct((rows, n), F32),
        grid=(n // tn,),
        in_specs=[pl.BlockSpec((rows, d), lambda j: (0, 0)),
                  pl.BlockSpec((d, tn), lambda j: (0, j)),
                  pl.BlockSpec((1, tn), lambda j: (0, j))],
        out_specs=pl.BlockSpec((rows, tn), lambda j: (0, j)),
        compiler_params=_cparams(("arbitrary",)),
        name="adaln_mod",
    )(cp, ada_w, ada_b.reshape(1, n))
    return out[:b].reshape(b, 6, d)


def _norm_mod_kernel(x_ref, g_ref, mod_ref, o_ref, *, shift_idx, scale_idx):
    x = x_ref[0]
    ms = jnp.mean(x * x, axis=-1, keepdims=True)
    y = x * lax.rsqrt(ms + NORM_EPS) * g_ref[...]
    scale = mod_ref[0, scale_idx:scale_idx + 1, :]
    shift = mod_ref[0, shift_idx:shift_idx + 1, :]
    o_ref[0] = (y * (1.0 + scale) + shift).astype(o_ref.dtype)


def _norm_mod(x, g, mod, shift_idx, scale_idx, tm=1024):
    b, s, d = x.shape
    return pl.pallas_call(
        functools.partial(_norm_mod_kernel, shift_idx=shift_idx, scale_idx=scale_idx),
        out_shape=jax.ShapeDtypeStruct((b, s, d), BF16),
        grid=(b, s // tm),
        in_specs=[pl.BlockSpec((1, tm, d), lambda i, j: (i, j, 0)),
                  pl.BlockSpec((1, d), lambda i, j: (0, 0)),
                  pl.BlockSpec((1, 6, d), lambda i, j: (i, 0, 0))],
        out_specs=pl.BlockSpec((1, tm, d), lambda i, j: (i, j, 0)),
        compiler_params=_cparams(("arbitrary", "arbitrary")),
        name="norm1_mod",
    )(x, g.reshape(1, d), mod)


def _mm_kernel(a_ref, w_ref, o_ref):
    o_ref[...] = jnp.dot(a_ref[...], w_ref[...], preferred_element_type=F32).astype(o_ref.dtype)


def _matmul(a, w, out_dtype, tm=2048, tn=512, name="proj"):
    t, k = a.shape
    n = w.shape[1]
    tn = min(tn, n)
    return pl.pallas_call(
        _mm_kernel,
        out_shape=jax.ShapeDtypeStruct((t, n), out_dtype),
        grid=(t // tm, n // tn),
        in_specs=[pl.BlockSpec((tm, k), lambda i, j: (i, 0)),
                  pl.BlockSpec((k, tn), lambda i, j: (0, j))],
        out_specs=pl.BlockSpec((tm, tn), lambda i, j: (i, j)),
        compiler_params=_cparams(("arbitrary", "arbitrary")),
        name=name,
    )(a, w)


LOG2E = 1.4426950408889634


def _split3(x):
    hi = x.astype(BF16)
    r1 = x - hi.astype(F32)
    mid = r1.astype(BF16)
    lo = (r1 - mid.astype(F32)).astype(BF16)
    return hi, mid, lo


def _fcum_kernel(h_ref, wf_ref, fb_ref, sel_ref, qk_ref, hsel_ref, o_ref, bounds_ref, carry_ref, *, ts):
    @pl.when(pl.program_id(0) == 0)
    def _():
        carry_ref[...] = jnp.zeros_like(carry_ref)

    ri = lax.broadcasted_iota(jnp.int32, (ts, ts), 0)
    ci = lax.broadcasted_iota(jnp.int32, (ts, ts), 1)
    tri = jnp.where(ri >= ci, 1.0, 0.0).astype(BF16)
    for bi in range(h_ref.shape[0]):
        f = jnp.dot(h_ref[bi], wf_ref[...], preferred_element_type=F32) + fb_ref[...]
        lf = -_softplus(-f)
        cum = carry_ref[bi]
        for piece in _split3(lf):
            cum = cum + jnp.dot(tri, piece, preferred_element_type=F32)
        carry_ref[bi] = cum[ts - 1:ts, :]
        out = jnp.zeros((ts, sel_ref.shape[2]), F32)
        for idx, piece in enumerate(_split3(cum * (-LOG2E))):
            out = out + jnp.dot(piece, sel_ref[idx], preferred_element_type=F32)
        o_ref[bi] = out.astype(o_ref.dtype)

        x = qk_ref[bi].astype(F32)
        ssq = _dot(x * x, hsel_ref[...])
        dots = _dot(x[:, :WIDTH] * x[:, WIDTH:], hsel_ref[0:WIDTH, :])
        sub = lax.broadcasted_iota(jnp.int32, bounds_ref.shape[2:], 0)
        bounds_ref[bi, 0] = jnp.where(sub == 0, jnp.max(ssq, axis=0, keepdims=True),
                                      jnp.min(dots, axis=0, keepdims=True))


def _forget_bias(h, wf, fb, qkv, ts=FOX_BK):
    b, s, d = h.shape
    pairs = HEADS // 2
    hsel = (jnp.arange(2 * WIDTH)[:, None] // HEAD_DIM == jnp.arange(LANES)[None, :]).astype(BF16)
    wf_p = jnp.zeros((d, LANES), F32).at[:, :HEADS].set(wf).astype(BF16)
    fb_p = jnp.zeros((1, LANES), F32).at[0, :HEADS].set(fb)
    hh = jnp.arange(HEADS)
    sel = jnp.zeros((3, LANES, pairs * LANES), F32)
    for piece in range(3):
        sel = sel.at[piece, hh, (hh // 2) * LANES + (hh % 2) * 3 + piece].set(1.0)
    return pl.pallas_call(
        functools.partial(_fcum_kernel, ts=ts),
        out_shape=(jax.ShapeDtypeStruct((b, s, pairs * LANES), BF16),
                   jax.ShapeDtypeStruct((b, s // ts, 8, LANES), F32)),
        grid=(s // ts,),
        in_specs=[pl.BlockSpec((b, ts, d), lambda j: (0, j, 0)),
                  pl.BlockSpec((d, LANES), lambda j: (0, 0)),
                  pl.BlockSpec((1, LANES), lambda j: (0, 0)),
                  pl.BlockSpec((3, LANES, pairs * LANES), lambda j: (0, 0, 0)),
                  pl.BlockSpec((b, ts, 2 * WIDTH), lambda j: (0, j, 0)),
                  pl.BlockSpec((2 * WIDTH, LANES), lambda j: (0, 0))],
        out_specs=(pl.BlockSpec((b, ts, pairs * LANES), lambda j: (0, j, 0)),
                   pl.BlockSpec((b, 1, 8, LANES), lambda j: (0, j, 0, 0))),
        scratch_shapes=[pltpu.VMEM((b, 1, LANES), F32)],
        compiler_params=_cparams(("arbitrary",)),
        name="forget_bias",
    )(h, wf_p, fb_p, sel.astype(BF16), qkv, hsel)


def _fox_kernel(first_ref, q_ref, k_ref, v_ref, a_ref, o_ref, m_ref, acc_ref, *, bq, bk):
    i = pl.program_id(2)
    lane = lax.broadcasted_iota(jnp.int32, (bq, LANES), 1)
    first = lane < HEAD_DIM
    qs = q_ref[0].astype(F32) * (HEAD_DIM ** -0.5 * LOG2E)
    aug0 = jnp.where(lane < 3, 1.0, 0.0)
    aug1 = jnp.where((lane >= 3) & (lane < 6), 1.0, 0.0)
    q01 = jnp.concatenate([jnp.concatenate([jnp.where(first, qs, 0.0), aug0], axis=1),
                           jnp.concatenate([jnp.where(first, 0.0, qs), aug1], axis=1)],
                          axis=0).astype(BF16)
    m_ref[...] = jnp.full_like(m_ref, -jnp.inf)
    acc_ref[...] = jnp.zeros_like(acc_ref)
    lane_k = lax.broadcasted_iota(jnp.int32, (bk, LANES), 1)
    keep_first = jnp.where(lane_k < HEAD_DIM, 1.0, 0.0).astype(BF16)
    keep_second = jnp.where(lane_k < HEAD_DIM, 0.0, 1.0).astype(BF16)

    def both_heads(x, r0):
        if r0 == 0:
            return x[...]
        return jnp.concatenate([x[r0:bq], x[bq + r0:2 * bq]], axis=0)

    def logits(j, r0=0):
        start = pl.multiple_of(j * bk, bk)
        kb = jnp.concatenate([k_ref[0, pl.ds(start, bk), :], a_ref[0, pl.ds(start, bk), :]], axis=1)
        return lax.dot_general(both_heads(q01, r0), kb, (((1,), (1,)), ((), ())),
                               preferred_element_type=F32)

    def consume(j, z, r0=0):
        nr = bq - r0
        start = pl.multiple_of(j * bk, bk)
        vb = v_ref[0, pl.ds(start, bk), :]
        m_prev = both_heads(m_ref, r0)
        m_new = jnp.maximum(m_prev, jnp.max(z, axis=1, keepdims=True))
        alpha = jnp.exp2(m_prev - m_new)
        p = jnp.exp2(z - jnp.concatenate([m_new] * (bk // LANES), axis=1)).astype(BF16)
        pv = jnp.concatenate(
            [jnp.dot(p[:nr], vb * keep_first + keep_second, preferred_element_type=F32),
             jnp.dot(p[nr:], vb * keep_second + keep_first, preferred_element_type=F32)], axis=0)
        acc_new = alpha * both_heads(acc_ref, r0) + pv
        if r0 == 0:
            acc_ref[...] = acc_new
            m_ref[...] = m_new
        else:
            for half, dst in ((slice(0, nr), slice(r0, bq)), (slice(nr, 2 * nr), slice(bq + r0, 2 * bq))):
                acc_ref[dst] = acc_new[half]
                m_ref[dst] = m_new[half]

    per_q = bq // bk
    n_full = i * per_q

    j_first = first_ref[pl.program_id(0), pl.program_id(1), i]
    odd = (n_full - j_first) & 1

    @pl.when(odd == 1)
    def _():
        consume(j_first, logits(j_first))

    def body(step, carry):
        j = j_first + odd + 2 * step
        z_a = logits(j)
        z_b = logits(j + 1)
        consume(j, z_a)
        consume(j + 1, z_b)
        return carry

    lax.fori_loop(0, (n_full - j_first) // 2, body, 0)
    for d in range(per_q):
        r0 = d * bk
        row = lax.broadcasted_iota(jnp.int32, (bq - r0, bk), 0)
        col = lax.broadcasted_iota(jnp.int32, (bq - r0, bk), 1)
        keep = col <= row
        z = logits(n_full + d, r0)
        consume(n_full + d, jnp.where(jnp.concatenate([keep, keep], axis=0), z, -jnp.inf), r0)
    acc = acc_ref[...]
    o = acc / pltpu.roll(acc, HEAD_DIM, 1)
    o_ref[0] = jnp.where(first, o[:bq], o[bq:]).astype(o_ref.dtype)


FOX_ZERO_LOG2 = 136.0


def _fox_first_block(fbias, bounds, bq, bk):
    b, s, _ = fbias.shape
    nq, nk = s // bq, s // bk
    n_heads2 = 2 * HEADS
    nrm = jnp.sqrt(bounds[:, :, 0, :n_heads2]) * 1.01
    q_scale = HEAD_DIM ** -0.5 * LOG2E
    qn = nrm[..., :HEADS] * (q_scale * 1.01)
    kn = nrm[..., HEADS:]
    per_q = bq // bk
    qn_i = qn.reshape(b, nq, per_q, HEADS).max(axis=2)
    kn_i = kn.reshape(b, nq, per_q, HEADS).max(axis=2)
    kn_pre = lax.cummax(kn, axis=1)
    diag = bounds[:, :, 1, :HEADS].reshape(b, nq, per_q, HEADS).min(axis=2) * q_scale
    diag_low = diag - 0.02 * qn_i * kn_i
    pairs = HEADS // 2
    def bias_rows(rows):
        pieces = rows.astype(F32).reshape(b, -1, pairs, LANES)[..., :6].reshape(b, -1, pairs, 2, 3)
        return pieces.sum(-1).reshape(b, -1, HEADS)

    nb_end = bias_rows(fbias[:, bk - 1::bk])
    nb_start = bias_rows(fbias[:, ::bq])
    gap = nb_start[:, :, None, :] - nb_end[:, None, :, :]
    need = qn_i[:, :, None, :] * kn_pre[:, None, :, :] - diag_low[:, :, None, :] + FOX_ZERO_LOG2
    skip = (gap > need).reshape(b, nq, nk, pairs, 2).all(axis=-1)
    n_full = jnp.arange(nq) * per_q
    skip = skip & (jnp.arange(nk)[None, None, :, None] < n_full[None, :, None, None])
    first = jnp.argmin(skip, axis=2)
    return first.transpose(0, 2, 1).astype(jnp.int32)


def _fox_attention(qkv, fbias, bounds, bq=FOX_BQ, bk=FOX_BK):
    b, s, _ = qkv.shape
    pairs = HEADS // 2
    cb = WIDTH // LANES
    first = _fox_first_block(fbias, bounds, bq, bk)
    grid_spec = pltpu.PrefetchScalarGridSpec(
        num_scalar_prefetch=1,
        grid=(b, pairs, s // bq),
        in_specs=[pl.BlockSpec((1, bq, LANES), lambda bi, hp, i, fr: (bi, i, hp)),
                  pl.BlockSpec((1, s, LANES), lambda bi, hp, i, fr: (bi, 0, cb + hp)),
                  pl.BlockSpec((1, s, LANES), lambda bi, hp, i, fr: (bi, 0, 2 * cb + hp)),
                  pl.BlockSpec((1, s, LANES), lambda bi, hp, i, fr: (bi, 0, hp))],
        out_specs=pl.BlockSpec((1, bq, LANES), lambda bi, hp, i, fr: (bi, i, hp)),
        scratch_shapes=[pltpu.VMEM((2 * bq, LANES), F32), pltpu.VMEM((2 * bq, LANES), F32)],
    )
    return pl.pallas_call(
        functools.partial(_fox_kernel, bq=bq, bk=bk),
        out_shape=jax.ShapeDtypeStruct((b, s, WIDTH), BF16),
        grid_spec=grid_spec,
        compiler_params=_cparams(("arbitrary", "arbitrary", "arbitrary")),
        name="fox_attention",
    )(first, qkv, qkv, qkv, fbias)


PAIR = 2 * HEAD_DIM
GROUPS = WIDTH // PAIR


def _group(x, g):
    return x[:, g * PAIR:(g + 1) * PAIR]


def _head_sum(x, bd2):
    return jnp.concatenate([_dot(_group(x, g), bd2) for g in range(GROUPS)], axis=1)


def _head_apply(mats, x, lane_first):
    rows = mats.shape[1]
    outs = []
    for g in range(GROUPS):
        res = _dot(mats[2 * g:2 * g + 2].reshape(2 * rows, rows), _group(x, g))
        outs.append(jnp.where(lane_first, res[:rows], res[rows:]))
    return jnp.concatenate(outs, axis=1)


def _rwkv_kernel(p_ref, mu_ref, w0_ref, a0_ref, kk_ref, ka_ref, rk_ref, lng_ref, lnb_ref,
                 wwa_ref, g2_ref, bd_ref, o_ref, st_ref, prev_ref):
    L = CHUNK

    @pl.when(pl.program_id(1) == 0)
    def _():
        st_ref[...] = jnp.zeros_like(st_ref)
        prev_ref[...] = jnp.zeros_like(prev_ref)

    p = p_ref[0]
    T = p.shape[0]
    rowi = lax.broadcasted_iota(jnp.int32, p.shape, 0)
    prev = jnp.where(rowi == 0, prev_ref[...], pltpu.roll(p, 1, 0))
    prev_ref[...] = p[T - 1:T, :]
    ps = p + (prev - p) * mu_ref[...]
    r = ps[:, 0:WIDTH]
    k = ps[:, WIDTH:2 * WIDTH]
    v = ps[:, 2 * WIDTH:3 * WIDTH]
    wa_in = ps[:, 3 * WIDTH:3 * WIDTH + DECAY_LORA + AAA_LORA]
    gd = ps[:, 3 * WIDTH + DECAY_LORA + AAA_LORA:]
    lane_wa = lax.broadcasted_iota(jnp.int32, wa_in.shape, 1)
    wa_act = jnp.where(lane_wa < DECAY_LORA, jnp.tanh(wa_in), wa_in)
    wa = _dot(wa_act, wwa_ref[...])
    log_w = -_softplus(-(w0_ref[...] + wa[:, :WIDTH])) - 0.5
    lw = -jnp.exp(log_w)
    a = jax.nn.sigmoid(a0_ref[...] + wa[:, WIDTH:])
    out_gate = _dot(jax.nn.sigmoid(gd), g2_ref[...])
    bd = bd_ref[...]
    kk0 = k * kk_ref[...]
    kk = kk0 * lax.rsqrt(jnp.maximum(_head_sum(kk0 * kk0, bd), 1e-24))
    k2 = k * (1.0 + (a - 1.0) * ka_ref[...])
    av = -kk
    bv = kk * a

    n_sub = T // L
    rt_i = lax.broadcasted_iota(jnp.int32, (T, T), 0)
    ct_i = lax.broadcasted_iota(jnp.int32, (T, T), 1)
    tri_tile = (rt_i >= ct_i) & (rt_i // L == ct_i // L)
    cl = _split_dot_left(jnp.where(tri_tile, 1.0, 0.0).astype(BF16), lw)
    cl_end = jnp.concatenate([jnp.broadcast_to(cl[(c + 1) * L - 1:(c + 1) * L, :], (L, WIDTH))
                              for c in range(n_sub)], axis=0)
    at_all = av * jnp.exp(cl - lw)
    rt_all = r * jnp.exp(cl)
    einv = jnp.exp(-cl)
    bt_all = bv * einv
    kt_all = k2 * einv
    edec = jnp.exp(cl_end - cl)
    b_end_all = bv * edec
    k_end_all = k2 * edec

    ri = lax.broadcasted_iota(jnp.int32, (L, L), 0)
    ci = lax.broadcasted_iota(jnp.int32, (L, L), 1)
    tri_incl = ri >= ci
    tri_strict = ri > ci
    eye = jnp.where(ri == ci, 1.0, 0.0)
    lane_first = lax.broadcasted_iota(jnp.int32, (L, PAIR), 1) < HEAD_DIM
    qr = lax.broadcasted_iota(jnp.int32, (PAIR, PAIR), 0) < HEAD_DIM
    qc = lax.broadcasted_iota(jnp.int32, (PAIR, PAIR), 1) < HEAD_DIM
    same_head = qr == qc

    def bmm(x, y):
        return lax.dot_general(x.astype(BF16), y.astype(BF16), (((2,), (1,)), ((0,), (0,))),
                               preferred_element_type=F32)

    def chunk_terms(c):
        rows = slice(c * L, (c + 1) * L)
        at, rt, bt, kt, vc = at_all[rows], rt_all[rows], bt_all[rows], kt_all[rows], v[rows]
        sb_heads, sk_heads = [], []
        for g in range(GROUPS):
            at_g, rt_g = _group(at, g), _group(rt, g)
            lhs = jnp.concatenate([jnp.where(lane_first, at_g, 0.0), jnp.where(lane_first, rt_g, 0.0),
                                   jnp.where(lane_first, 0.0, at_g), jnp.where(lane_first, 0.0, rt_g)],
                                  axis=0).astype(BF16)
            sb_g = _dot_nt(lhs, _group(bt, g))
            sk_g = _dot_nt(lhs, _group(kt, g))
            for hh in range(2):
                sb_heads.append(sb_g[hh * 2 * L:(hh + 1) * 2 * L])
                sk_heads.append(sk_g[hh * 2 * L:(hh + 1) * 2 * L])
        sb = jnp.stack(sb_heads)
        sk = jnp.stack(sk_heads)
        n_ab = jnp.where(tri_strict, sb[:, :L, :], 0.0)
        a_ak = jnp.where(tri_strict, sk[:, :L, :], 0.0)
        a_rb = jnp.where(tri_incl, sb[:, L:, :], 0.0)
        a_rk = jnp.where(tri_incl, sk[:, L:, :], 0.0)
        tinv = eye + n_ab
        pw = bmm(n_ab, n_ab)
        span = 2
        while 2 * span < L:
            both = bmm(jnp.concatenate([tinv, pw], axis=1), pw)
            tinv = tinv + both[:, :L, :]
            pw = both[:, L:, :]
            span *= 2
        tinv = tinv + bmm(tinv, pw)
        av_term = _head_apply(a_ak, vc, lane_first)
        pm = _head_apply(tinv, at, lane_first)
        qm = _head_apply(tinv, av_term, lane_first)
        rkv = _head_apply(a_rk, vc, lane_first)
        return pm, qm, rkv, a_rb

    terms = [chunk_terms(c) for c in range(n_sub)]

    y_chunks = []
    for c in range(n_sub):
        rows = slice(c * L, (c + 1) * L)
        pm, qm, rkv, a_rb = terms[c]
        rt, vc, b_end, k_end = rt_all[rows], v[rows], b_end_all[rows], k_end_all[rows]
        gam_last = jnp.exp(cl[(c + 1) * L - 1:(c + 1) * L, :])
        u_parts, ys_parts = [], []
        for g in range(GROUPS):
            pr = _dot_nt(jnp.concatenate([_group(pm, g), _group(rt, g)], axis=0), st_ref[g])
            u_parts.append(pr[:L] + _group(qm, g))
            ys_parts.append(pr[L:])
        u = jnp.concatenate(u_parts, axis=1)
        y_chunks.append(jnp.concatenate(ys_parts, axis=1) + _head_apply(a_rb, u, lane_first) + rkv)
        for g in range(GROUPS):
            upd = _dot_tn(_group(u, g), _group(b_end, g)) + _dot_tn(_group(vc, g), _group(k_end, g))
            st_ref[g] = st_ref[g] * _group(gam_last, g) + jnp.where(same_head, upd, 0.0)
    y = jnp.concatenate(y_chunks, axis=0)

    inv_n = 1.0 / HEAD_DIM
    mean = _head_sum(y, bd) * inv_n
    dlt = y - mean
    var = _head_sum(dlt * dlt, bd) * inv_n
    yn = dlt * lax.rsqrt(var + LN_X_EPS) * lng_ref[...] + lnb_ref[...]
    bonus = _head_sum(r * k2 * rk_ref[...], bd) * v
    o_ref[0] = ((yn + bonus) * out_gate).astype(o_ref.dtype)


def _split_dot_left(w_bf16, x):
    hi, mid, lo = _split3(x)
    return (jnp.dot(w_bf16, hi, preferred_element_type=F32)
            + jnp.dot(w_bf16, mid, preferred_element_type=F32)
            + jnp.dot(w_bf16, lo, preferred_element_type=F32))


def _rwkv_branch(p_rw, mu, w0, w2, a0, a2, g2, k_k, k_a, r_k, ln_g, ln_b):
    b, s, sw = p_rw.shape
    row = lambda t: t.reshape(1, -1).astype(F32)
    wwa = jnp.zeros((DECAY_LORA + AAA_LORA, 2 * WIDTH), F32)
    wwa = wwa.at[:DECAY_LORA, :WIDTH].set(w2).at[DECAY_LORA:, WIDTH:].set(a2).astype(BF16)
    hid = jnp.arange(PAIR) // HEAD_DIM
    bd = (hid[:, None] == hid[None, :]).astype(BF16)
    const = lambda shape: pl.BlockSpec(shape, lambda i, j: (0,) * len(shape))
    return pl.pallas_call(
        _rwkv_kernel,
        out_shape=jax.ShapeDtypeStruct((b, s, WIDTH), BF16),
        grid=(b, s // RWKV_TILE),
        in_specs=[pl.BlockSpec((1, RWKV_TILE, sw), lambda i, j: (i, j, 0)),
                  const((1, sw)), const((1, WIDTH)), const((1, WIDTH)), const((1, WIDTH)),
                  const((1, WIDTH)), const((1, WIDTH)), const((1, WIDTH)), const((1, WIDTH)),
                  const((DECAY_LORA + AAA_LORA, 2 * WIDTH)), const((GATE_LORA, WIDTH)),
                  const((PAIR, PAIR))],
        out_specs=pl.BlockSpec((1, RWKV_TILE, WIDTH), lambda i, j: (i, j, 0)),
        scratch_shapes=[pltpu.VMEM((GROUPS, PAIR, PAIR), F32), pltpu.VMEM((1, sw), F32)],
        compiler_params=_cparams(("arbitrary", "arbitrary")),
        name="rwkv7_scan",
    )(p_rw, row(mu), row(w0), row(a0), row(k_k), row(k_a), row(r_k), row(ln_g), row(ln_b),
      wwa, g2.astype(BF16), bd)


def _out_kernel(of_ref, orw_ref, gate_ref, x_ref, mod_ref, wof_ref, wor_ref, wo_ref, n2g_ref,
                wr_ref, br_ref, x1_ref, h2_ref, route_ref, counts_ref, cnt_ref):
    d = x_ref.shape[-1]
    gate = 0.5 * jnp.tanh(0.5 * gate_ref[...].astype(F32)) + 0.5
    merged = (gate[:, :d] * jnp.dot(of_ref[...], wof_ref[...], preferred_element_type=F32)
              + gate[:, d:] * jnp.dot(orw_ref[...], wor_ref[...], preferred_element_type=F32))
    gate1 = mod_ref[0, 2:3, :]
    shift2 = mod_ref[0, 3:4, :]
    scale2 = mod_ref[0, 4:5, :]
    x1 = x_ref[...] + gate1 * jnp.dot(merged.astype(BF16), wo_ref[...], preferred_element_type=F32)
    x1_ref[...] = x1
    ms = jnp.mean(x1 * x1, axis=-1, keepdims=True)
    h2 = x1 * lax.rsqrt(ms + NORM_EPS) * n2g_ref[...] * (1.0 + scale2) + shift2
    h2_ref[...] = _pack_pairs(h2)

    h2_hi = h2.astype(BF16)
    h2_lo = (h2 - h2_hi.astype(F32)).astype(BF16)
    logits = (jnp.dot(h2_hi, wr_ref[0], preferred_element_type=F32)
              + jnp.dot(h2_lo, wr_ref[0], preferred_element_type=F32)
              + jnp.dot(h2_hi, wr_ref[1], preferred_element_type=F32)) + br_ref[...]
    lane = lax.broadcasted_iota(jnp.int32, logits.shape, 1)
    neg = -jnp.inf
    big = jnp.int32(LANES)
    gl = jnp.where(lane < N_GROUPS, logits, neg)
    gmax = jnp.max(gl, axis=1, keepdims=True)
    gidx = jnp.min(jnp.where(gl == gmax, lane, big), axis=1, keepdims=True)
    g_p = 1.0 / jnp.sum(jnp.exp(gl - gmax), axis=1, keepdims=True)
    e_lane = lane - N_GROUPS
    in_grp = (e_lane >= 0) & (e_lane < N_EXPERTS) & ((e_lane // EXPERTS_PER_GROUP) == gidx)
    sel = jnp.where(in_grp, logits, neg)
    m1 = jnp.max(sel, axis=1, keepdims=True)
    i1 = jnp.min(jnp.where(sel == m1, lane, big), axis=1, keepdims=True)
    sel2 = jnp.where(lane == i1, neg, sel)
    m2 = jnp.max(sel2, axis=1, keepdims=True)
    i2 = jnp.min(jnp.where(sel2 == m2, lane, big), axis=1, keepdims=True)
    e21 = jnp.exp(m2 - m1)
    w_first = g_p / (1.0 + e21)
    w_second = g_p * e21 / (1.0 + e21)
    @pl.when(pl.program_id(0) == 0)
    def _():
        cnt_ref[...] = jnp.zeros_like(cnt_ref)

    tm = logits.shape[0]
    oh1 = lane == i1
    oh2 = lane == i2
    both = jnp.where(oh1 | oh2, 1.0, 0.0)
    before = (lax.broadcasted_iota(jnp.int32, (tm, tm), 0)
              > lax.broadcasted_iota(jnp.int32, (tm, tm), 1))
    seen = jnp.dot(jnp.where(before, 1.0, 0.0).astype(BF16), both.astype(BF16),
                   preferred_element_type=F32) + cnt_ref[...]
    rank1 = jnp.sum(jnp.where(oh1, seen, 0.0), axis=1, keepdims=True)
    rank2 = jnp.sum(jnp.where(oh2, seen, 0.0), axis=1, keepdims=True)
    cnt_ref[...] = cnt_ref[...] + jnp.sum(both, axis=0, keepdims=True)
    counts_ref[...] = jnp.broadcast_to(cnt_ref[...], counts_ref.shape)

    route = jnp.where(lane == 0, (i1 - N_GROUPS).astype(F32),
                      jnp.where(lane == 1, (i2 - N_GROUPS).astype(F32),
                                jnp.where(lane == 2, w_first,
                                          jnp.where(lane == 3, w_second,
                                                    jnp.where(lane == 4, rank1,
                                                              jnp.where(lane == 5, rank2, 0.0))))))
    route_ref[...] = route


def _merge_out_router(o_fox, o_rw, gate, x, mod, wof, wor, wo, n2g, wr, br, tm=512):
    b, s, d = x.shape
    t = b * s
    spb = s // tm
    rowspec = lambda w: pl.BlockSpec((tm, w), lambda i: (i, 0))
    const = lambda shape: pl.BlockSpec(shape, lambda i: (0,) * len(shape))
    return pl.pallas_call(
        _out_kernel,
        out_shape=(jax.ShapeDtypeStruct((t, d), F32), jax.ShapeDtypeStruct((t, d // 2), jnp.uint32),
                   jax.ShapeDtypeStruct((t, LANES), F32), jax.ShapeDtypeStruct((8, LANES), F32)),
        grid=(t // tm,),
        in_specs=[rowspec(WIDTH), rowspec(WIDTH), rowspec(2 * d), rowspec(d),
                  pl.BlockSpec((1, 6, d), lambda i: (i // spb, 0, 0)),
                  const((WIDTH, d)), const((WIDTH, d)), const((d, d)), const((1, d)),
                  const((2, d, LANES)), const((1, LANES))],
        out_specs=(rowspec(d), rowspec(d // 2), rowspec(LANES), const((8, LANES))),
        scratch_shapes=[pltpu.VMEM((1, LANES), F32)],
        compiler_params=_cparams(("arbitrary",)),
        name="merge_out_router",
    )(o_fox, o_rw, gate, x.reshape(t, d), mod, wof, wor, wo, n2g.reshape(1, d), wr, br)


SC_CORES = 2
SC_SUBCORES = 16
SC_CHUNK = 128


def _sc_scatter_rows(src, idx, n_rows):
    t, width = src.shape
    per_worker = t // (SC_CORES * SC_SUBCORES)
    n_chunks = per_worker // SC_CHUNK
    assert n_chunks * SC_CHUNK * SC_CORES * SC_SUBCORES == t and idx.shape[0] == TOP_K * t
    mesh = plsc.VectorSubcoreMesh(core_axis_name="c", subcore_axis_name="s")

    @functools.partial(
        pl.kernel, mesh=mesh,
        out_type=jax.ShapeDtypeStruct((n_rows, width), src.dtype),
        scratch_types=[pltpu.VMEM((SC_CHUNK,), jnp.int32) for _ in range(TOP_K)]
        + [pltpu.VMEM((SC_CHUNK, width), src.dtype), pltpu.SemaphoreType.DMA])
    def scatter(src_hbm, idx_hbm, out_hbm, *scratch):
        idx_v, rows_v, sem = scratch[:TOP_K], scratch[TOP_K], scratch[TOP_K + 1]
        worker = lax.axis_index("s") * SC_CORES + lax.axis_index("c")

        @pl.loop(0, n_chunks)
        def _(ci):
            off = pl.multiple_of(worker * per_worker + ci * SC_CHUNK, SC_CHUNK)
            pltpu.sync_copy(src_hbm.at[pl.ds(off, SC_CHUNK)], rows_v)
            for kk in range(TOP_K):
                pltpu.sync_copy(idx_hbm.at[pl.ds(kk * t + off, SC_CHUNK)], idx_v[kk])
            for kk in range(TOP_K):
                pltpu.async_copy(rows_v, out_hbm.at[idx_v[kk]], sem).wait()

    return scatter(src, idx)


def _expert_kernel(blk_e_ref, nused_ref, valid_ref, xs_ref, w1_ref, w3_ref, w2_ref, o_ref,
                   w1b_ref, w3b_ref, w2b_ref):
    i = pl.program_id(0)
    live = i * MOE_ROWS < nused_ref[0]
    new_expert = jnp.logical_or(i == 0, blk_e_ref[i] != blk_e_ref[jnp.maximum(i - 1, 0)])

    @pl.when(new_expert)
    def _():
        w1b_ref[...] = w1_ref[0].astype(BF16)
        w3b_ref[...] = w3_ref[0].astype(BF16)
        w2b_ref[...] = w2_ref[0].astype(BF16)

    @pl.when(live)
    def _():
        row = lax.broadcasted_iota(jnp.int32, xs_ref.shape, 0)
        xs = jnp.where(row < valid_ref[i], xs_ref[...], jnp.uint32(0))
        xb = _unpack_pairs(xs).astype(BF16)
        h1 = jnp.dot(xb, w1b_ref[...], preferred_element_type=F32)
        h3 = jnp.dot(xb, w3b_ref[...], preferred_element_type=F32)
        hh = (h1 * jax.nn.sigmoid(h1)) * h3
        o_ref[...] = _pack_pairs(jnp.dot(hh.astype(BF16), w2b_ref[...], preferred_element_type=F32))

    @pl.when(jnp.logical_not(live))
    def _():
        o_ref[...] = jnp.zeros_like(o_ref)


def _moe_experts(xs, blk_e, nused, blk_valid, w1, w3, w2):
    rows, dp = xs.shape
    _, d, de = w1.shape
    grid_spec = pltpu.PrefetchScalarGridSpec(
        num_scalar_prefetch=3,
        grid=(rows // MOE_ROWS,),
        in_specs=[pl.BlockSpec((MOE_ROWS, dp), lambda i, be, nu, va: (i, 0)),
                  pl.BlockSpec((1, d, de), lambda i, be, nu, va: (be[i], 0, 0)),
                  pl.BlockSpec((1, d, de), lambda i, be, nu, va: (be[i], 0, 0)),
                  pl.BlockSpec((1, de, d), lambda i, be, nu, va: (be[i], 0, 0))],
        out_specs=pl.BlockSpec((MOE_ROWS, dp), lambda i, be, nu, va: (i, 0)),
        scratch_shapes=[pltpu.VMEM((d, de), BF16), pltpu.VMEM((d, de), BF16), pltpu.VMEM((de, d), BF16)],
    )
    return pl.pallas_call(
        _expert_kernel,
        out_shape=jax.ShapeDtypeStruct((rows, dp), xs.dtype),
        grid_spec=grid_spec,
        compiler_params=_cparams(("arbitrary",)),
        name="moe_experts",
    )(blk_e, nused, blk_valid, xs, w1, w3, w2)


def _sc_gather_rows(table, idx):
    n_idx = idx.shape[0]
    width = table.shape[1]
    per_worker = n_idx // (SC_CORES * SC_SUBCORES)
    n_chunks = per_worker // SC_CHUNK
    assert n_chunks * SC_CHUNK * SC_CORES * SC_SUBCORES == n_idx
    mesh = plsc.VectorSubcoreMesh(core_axis_name="c", subcore_axis_name="s")

    @functools.partial(
        pl.kernel, mesh=mesh,
        out_type=jax.ShapeDtypeStruct((n_idx, width), table.dtype),
        scratch_types=[pltpu.VMEM((SC_CHUNK,), jnp.int32), pltpu.VMEM((SC_CHUNK, width), table.dtype),
                       pltpu.SemaphoreType.DMA])
    def gather(table_hbm, idx_hbm, out_hbm, idx_v, rows_v, sem):
        worker = lax.axis_index("s") * SC_CORES + lax.axis_index("c")

        @pl.loop(0, n_chunks)
        def _(ci):
            off = pl.multiple_of(worker * per_worker + ci * SC_CHUNK, SC_CHUNK)
            pltpu.sync_copy(idx_hbm.at[pl.ds(off, SC_CHUNK)], idx_v)
            pltpu.async_copy(table_hbm.at[idx_v], rows_v, sem).wait()
            pltpu.sync_copy(rows_v, out_hbm.at[pl.ds(off, SC_CHUNK)])

    return gather(table, idx)


def _final_kernel(route_ref, x1_ref, mod_ref, fg_ref, y0_ref, y1_ref, o_ref):
    route = route_ref[...]
    y = route[:, 2:3] * _unpack_pairs(y0_ref[...]) + route[:, 3:4] * _unpack_pairs(y1_ref[...])
    gate2 = mod_ref[0, 5:6, :]
    x2 = x1_ref[...] + gate2 * y
    ms = jnp.mean(x2 * x2, axis=-1, keepdims=True)
    o_ref[...] = x2 * lax.rsqrt(ms + NORM_EPS) * fg_ref[...]


def _moe_combine_final(dest_slots, route, x1, mod, final_g, ys, s, tm=512):
    t, d = x1.shape
    spb = s // tm
    picked = _sc_gather_rows(ys, dest_slots)
    tiles = t // tm
    return pl.pallas_call(
        _final_kernel,
        out_shape=jax.ShapeDtypeStruct((t, d), F32),
        grid=(tiles,),
        in_specs=[pl.BlockSpec((tm, LANES), lambda i: (i, 0)),
                  pl.BlockSpec((tm, d), lambda i: (i, 0)),
                  pl.BlockSpec((1, 6, d), lambda i: (i // spb, 0, 0)),
                  pl.BlockSpec((1, d), lambda i: (0, 0)),
                  pl.BlockSpec((tm, picked.shape[1]), lambda i: (i, 0)),
                  pl.BlockSpec((tm, picked.shape[1]), lambda i: (tiles + i, 0))],
        out_specs=pl.BlockSpec((tm, d), lambda i: (i, 0)),
        compiler_params=_cparams(("arbitrary",)),
        name="moe_combine_final",
    )(route, x1, mod, final_g.reshape(1, d), picked, picked)


def _moe_plan(route, counts):
    t = route.shape[0]
    m = t * TOP_K
    flat_e = route[:, :TOP_K].astype(jnp.int32).T.reshape(m)
    rank = route[:, 4:4 + TOP_K].astype(jnp.int32).T.reshape(m)
    counts = counts[0, N_GROUPS:N_GROUPS + N_EXPERTS].astype(jnp.int32)
    padded = (counts + MOE_ROWS - 1) // MOE_ROWS * MOE_ROWS
    pad_end = jnp.cumsum(padded)
    pad_start = pad_end - padded
    experts = jnp.arange(N_EXPERTS, dtype=jnp.int32)
    start_of = jnp.sum(jnp.where(flat_e[:, None] == experts[None, :], pad_start[None, :], 0), axis=1)
    dest = (start_of + rank).astype(jnp.int32)
    n_blocks = m // MOE_ROWS + N_EXPERTS
    blk_start = jnp.arange(n_blocks, dtype=jnp.int32) * MOE_ROWS
    blk_e = jnp.minimum(jnp.sum(pad_end[None, :] <= blk_start[:, None], axis=1), N_EXPERTS - 1).astype(jnp.int32)
    nused = pad_end[-1:].astype(jnp.int32)
    blk_valid = jnp.clip(counts[blk_e] - (blk_start - pad_start[blk_e]), 0, MOE_ROWS).astype(jnp.int32)
    return dest, blk_e, nused, blk_valid, n_blocks * MOE_ROWS


def kernel(x, c, ada_w, ada_b, norm1_g, w_in, fox_forget_b, shift_mu, rwkv_w0, rwkv_w2, rwkv_a0, rwkv_a2, rwkv_g2, rwkv_k_k, rwkv_k_a, rwkv_r_k, ln_x_g, ln_x_b, w_out_fox, w_out_rwkv, w_o, norm2_g, router_group_w, router_group_b, router_expert_w, router_expert_b, exp_w1, exp_w3, exp_w2, final_g):
    b, s, d = x.shape
    t = b * s
    assert ada_w.shape[0] == 1, "the final norm is fused into the last layer's combine; one layer is laid out"
    for l in range(1):
        mod = _adaln_mod(c, ada_w[l], ada_b[l])

        h = _norm_mod(x, norm1_g[l], mod, shift_idx=0, scale_idx=1)
        h2d = h.reshape(t, d)
        w = w_in[l]
        o_f = 3 * WIDTH
        o_rw = o_f + HEADS
        o_g = o_rw + SHIFT_WIDTH
        perm = jnp.argsort(fox_forget_b[l])
        by_head = lambda m: m.reshape(d, HEADS, HEAD_DIM)[:, perm]
        wq = by_head(w[:, :WIDTH]).reshape(d, WIDTH)
        wk = by_head(w[:, WIDTH:2 * WIDTH]).reshape(d, WIDTH)
        wv = by_head(w[:, 2 * WIDTH:o_f]).reshape(d, WIDTH)
        qkv = _matmul(h2d, jnp.concatenate([wq, wk, wv], axis=1).astype(BF16), BF16, name="proj_qkv")
        p_rw = _matmul(h2d, w[:, o_rw:o_g].astype(BF16), F32, tn=896, name="proj_rwkv")
        gate = _matmul(h2d, w[:, o_g:].astype(BF16), BF16, name="proj_gate")
        qkv = qkv.reshape(b, s, 3 * WIDTH)
        fbias, bounds = _forget_bias(h, w[:, o_f:o_rw][:, perm], fox_forget_b[l][perm], qkv)
        o_fox = _fox_attention(qkv, fbias, bounds)
        w_of = w_out_fox[l].reshape(HEADS, HEAD_DIM, d)[perm].reshape(WIDTH, d)
        o_rwkv = _rwkv_branch(p_rw.reshape(b, s, SHIFT_WIDTH), shift_mu[l], rwkv_w0[l], rwkv_w2[l],
                              rwkv_a0[l], rwkv_a2[l], rwkv_g2[l], rwkv_k_k[l], rwkv_k_a[l],
                              rwkv_r_k[l], ln_x_g[l], ln_x_b[l])

        wr = jnp.zeros((d, LANES), F32)
        wr = wr.at[:, :N_GROUPS].set(router_group_w[l]).at[:, N_GROUPS:N_GROUPS + N_EXPERTS].set(router_expert_w[l])
        br = jnp.zeros((1, LANES), F32)
        br = br.at[0, :N_GROUPS].set(router_group_b[l]).at[0, N_GROUPS:N_GROUPS + N_EXPERTS].set(router_expert_b[l])
        wr_hi = wr.astype(BF16)
        wr_lo = (wr - wr_hi.astype(F32)).astype(BF16)
        x1, h2, route, counts = _merge_out_router(
            o_fox.reshape(t, WIDTH), o_rwkv.reshape(t, WIDTH), gate, x, mod,
            w_of.astype(BF16), w_out_rwkv[l].astype(BF16), w_o[l].astype(BF16), norm2_g[l],
            jnp.stack([wr_hi, wr_lo]), br)

        dest, blk_e, nused, blk_valid, rows = _moe_plan(route, counts)
        xs = _sc_scatter_rows(h2, dest, rows)
        ys = _moe_experts(xs, blk_e, nused, blk_valid, exp_w1[l], exp_w3[l], exp_w2[l])
        out = _moe_combine_final(dest, route, x1, mod, final_g, ys, s)
    return out.reshape(b, s, d)
```

```python
import functools

import jax
import jax.numpy as jnp
from jax import lax
from jax.experimental import pallas as pl
from jax.experimental.pallas import tpu as pltpu
from jax.experimental.pallas import tpu_sc as plsc

F32 = jnp.float32
BF16 = jnp.bfloat16
HIGHEST = lax.Precision.HIGHEST

HEADS = 8
HEAD_DIM = 64
WIDTH = HEADS * HEAD_DIM
DECAY_LORA = 64
AAA_LORA = 64
GATE_LORA = 128
SHIFT_WIDTH = 3 * WIDTH + DECAY_LORA + AAA_LORA + GATE_LORA
LN_X_EPS = 64e-5
NORM_EPS = 1e-6
N_GROUPS = 4
EXPERTS_PER_GROUP = 8
N_EXPERTS = N_GROUPS * EXPERTS_PER_GROUP
TOP_K = 2

LANES = 128
FOX_BQ = 1024
FOX_BK = 512
CHUNK = 64
RWKV_TILE = 256
MOE_ROWS = 1024
VMEM_LIMIT = 48 * 1024 * 1024


def _cparams(sem):
    return pltpu.CompilerParams(dimension_semantics=sem, vmem_limit_bytes=VMEM_LIMIT)


def _dot(a, b):
    return jnp.dot(a.astype(BF16), b.astype(BF16), preferred_element_type=F32)


def _dot_nt(a, b):
    return lax.dot_general(a.astype(BF16), b.astype(BF16), (((1,), (1,)), ((), ())),
                           preferred_element_type=F32)


def _dot_tn(a, b):
    return lax.dot_general(a.astype(BF16), b.astype(BF16), (((0,), (0,)), ((), ())),
                           preferred_element_type=F32)


def _pack_pairs(x):
    n = x.shape[1] // 2
    lo = lax.bitcast_convert_type(x[:, :n].astype(BF16).astype(F32), jnp.uint32) >> 16
    hi = lax.bitcast_convert_type(x[:, n:].astype(BF16).astype(F32), jnp.uint32) & jnp.uint32(0xFFFF0000)
    return lo | hi


def _unpack_pairs(u):
    lo = lax.bitcast_convert_type(u << 16, F32)
    hi = lax.bitcast_convert_type(u & jnp.uint32(0xFFFF0000), F32)
    return jnp.concatenate([lo, hi], axis=1)


def _softplus(x):
    return jnp.maximum(x, 0.0) + jnp.log1p(jnp.exp(-jnp.abs(x)))


def _mod_kernel(c_ref, w_ref, b_ref, o_ref):
    c = c_ref[...]
    sc = c * jax.nn.sigmoid(c)
    o_ref[...] = jnp.dot(sc, w_ref[...], precision=HIGHEST, preferred_element_type=F32) + b_ref[...]


def _adaln_mod(c, ada_w, ada_b):
    b, d = c.shape
    n = ada_w.shape[1]
    rows = 8
    cp = jnp.zeros((rows, d), F32).at[:b].set(c)
    tn = 1024
    out = pl.pallas_call(
        _mod_kernel,
        out_shape=jax.ShapeDtypeStruct((rows, n), F32),
        grid=(n // tn,),
        in_specs=[pl.BlockSpec((rows, d), lambda j: (0, 0)),
                  pl.BlockSpec((d, tn), lambda j: (0, j)),
                  pl.BlockSpec((1, tn), lambda j: (0, j))],
        out_specs=pl.BlockSpec((rows, tn), lambda j: (0, j)),
        compiler_params=_cparams(("arbitrary",)),
        name="adaln_mod",
    )(cp, ada_w, ada_b.reshape(1, n))
    return out[:b].reshape(b, 6, d)


def _norm_mod_kernel(x_ref, g_ref, mod_ref, o_ref, *, shift_idx, scale_idx):
    x = x_ref[0]
    ms = jnp.mean(x * x, axis=-1, keepdims=True)
    y = x * lax.rsqrt(ms + NORM_EPS) * g_ref[...]
    scale = mod_ref[0, scale_idx:scale_idx + 1, :]
    shift = mod_ref[0, shift_idx:shift_idx + 1, :]
    o_ref[0] = (y * (1.0 + scale) + shift).astype(o_ref.dtype)


def _norm_mod(x, g, mod, shift_idx, scale_idx, tm=1024):
    b, s, d = x.shape
    return pl.pallas_call(
        functools.partial(_norm_mod_kernel, shift_idx=shift_idx, scale_idx=scale_idx),
        out_shape=jax.ShapeDtypeStruct((b, s, d), BF16),
        grid=(b, s // tm),
        in_specs=[pl.BlockSpec((1, tm, d), lambda i, j: (i, j, 0)),
                  pl.BlockSpec((1, d), lambda i, j: (0, 0)),
                  pl.BlockSpec((1, 6, d), lambda i, j: (i, 0, 0))],
        out_specs=pl.BlockSpec((1, tm, d), lambda i, j: (i, j, 0)),
        compiler_params=_cparams(("arbitrary", "arbitrary")),
        name="norm1_mod",
    )(x, g.reshape(1, d), mod)


def _mm_kernel(a_ref, w_ref, o_ref):
    o_ref[...] = jnp.dot(a_ref[...], w_ref[...], preferred_element_type=F32).astype(o_ref.dtype)


def _matmul(a, w, out_dtype, tm=2048, tn=512, name="proj"):
    t, k = a.shape
    n = w.shape[1]
    tn = min(tn, n)
    return pl.pallas_call(
        _mm_kernel,
        out_shape=jax.ShapeDtypeStruct((t, n), out_dtype),
        grid=(t // tm, n // tn),
        in_specs=[pl.BlockSpec((tm, k), lambda i, j: (i, 0)),
                  pl.BlockSpec((k, tn), lambda i, j: (0, j))],
        out_specs=pl.BlockSpec((tm, tn), lambda i, j: (i, j)),
        compiler_params=_cparams(("arbitrary", "arbitrary")),
        name=name,
    )(a, w)


LOG2E = 1.4426950408889634


def _split3(x):
    hi = x.astype(BF16)
    r1 = x - hi.astype(F32)
    mid = r1.astype(BF16)
    lo = (r1 - mid.astype(F32)).astype(BF16)
    return hi, mid, lo


def _fcum_kernel(h_ref, wf_ref, fb_ref, sel_ref, qk_ref, hsel_ref, o_ref, bounds_ref, carry_ref, *, ts):
    @pl.when(pl.program_id(0) == 0)
    def _():
        carry_ref[...] = jnp.zeros_like(carry_ref)

    ri = lax.broadcasted_iota(jnp.int32, (ts, ts), 0)
    ci = lax.broadcasted_iota(jnp.int32, (ts, ts), 1)
    tri = jnp.where(ri >= ci, 1.0, 0.0).astype(BF16)
    for bi in range(h_ref.shape[0]):
        f = jnp.dot(h_ref[bi], wf_ref[...], preferred_element_type=F32) + fb_ref[...]
        lf = -_softplus(-f)
        cum = carry_ref[bi]
        for piece in _split3(lf):
            cum = cum + jnp.dot(tri, piece, preferred_element_type=F32)
        carry_ref[bi] = cum[ts - 1:ts, :]
        out = jnp.zeros((ts, sel_ref.shape[2]), F32)
        for idx, piece in enumerate(_split3(cum * (-LOG2E))):
            out = out + jnp.dot(piece, sel_ref[idx], preferred_element_type=F32)
        o_ref[bi] = out.astype(o_ref.dtype)

        x = qk_ref[bi].astype(F32)
        ssq = _dot(x * x, hsel_ref[...])
        dots = _dot(x[:, :WIDTH] * x[:, WIDTH:], hsel_ref[0:WIDTH, :])
        sub = lax.broadcasted_iota(jnp.int32, bounds_ref.shape[2:], 0)
        bounds_ref[bi, 0] = jnp.where(sub == 0, jnp.max(ssq, axis=0, keepdims=True),
                                      jnp.min(dots, axis=0, keepdims=True))


def _forget_bias(h, wf, fb, qkv, ts=FOX_BK):
    b, s, d = h.shape
    pairs = HEADS // 2
    hsel = (jnp.arange(2 * WIDTH)[:, None] // HEAD_DIM == jnp.arange(LANES)[None, :]).astype(BF16)
    wf_p = jnp.zeros((d, LANES), F32).at[:, :HEADS].set(wf).astype(BF16)
    fb_p = jnp.zeros((1, LANES), F32).at[0, :HEADS].set(fb)
    hh = jnp.arange(HEADS)
    sel = jnp.zeros((3, LANES, pairs * LANES), F32)
    for piece in range(3):
        sel = sel.at[piece, hh, (hh // 2) * LANES + (hh % 2) * 3 + piece].set(1.0)
    return pl.pallas_call(
        functools.partial(_fcum_kernel, ts=ts),
        out_shape=(jax.ShapeDtypeStruct((b, s, pairs * LANES), BF16),
                   jax.ShapeDtypeStruct((b, s // ts, 8, LANES), F32)),
        grid=(s // ts,),
        in_specs=[pl.BlockSpec((b, ts, d), lambda j: (0, j, 0)),
                  pl.BlockSpec((d, LANES), lambda j: (0, 0)),
                  pl.BlockSpec((1, LANES), lambda j: (0, 0)),
                  pl.BlockSpec((3, LANES, pairs * LANES), lambda j: (0, 0, 0)),
                  pl.BlockSpec((b, ts, 2 * WIDTH), lambda j: (0, j, 0)),
                  pl.BlockSpec((2 * WIDTH, LANES), lambda j: (0, 0))],
        out_specs=(pl.BlockSpec((b, ts, pairs * LANES), lambda j: (0, j, 0)),
                   pl.BlockSpec((b, 1, 8, LANES), lambda j: (0, j, 0, 0))),
        scratch_shapes=[pltpu.VMEM((b, 1, LANES), F32)],
        compiler_params=_cparams(("arbitrary",)),
        name="forget_bias",
    )(h, wf_p, fb_p, sel.astype(BF16), qkv, hsel)


def _fox_kernel(first_ref, q_ref, k_ref, v_ref, a_ref, o_ref, m_ref, acc_ref, *, bq, bk):
    i = pl.program_id(2)
    lane = lax.broadcasted_iota(jnp.int32, (bq, LANES), 1)
    first = lane < HEAD_DIM
    qs = q_ref[0].astype(F32) * (HEAD_DIM ** -0.5 * LOG2E)
    aug0 = jnp.where(lane < 3, 1.0, 0.0)
    aug1 = jnp.where((lane >= 3) & (lane < 6), 1.0, 0.0)
    q01 = jnp.concatenate([jnp.concatenate([jnp.where(first, qs, 0.0), aug0], axis=1),
                           jnp.concatenate([jnp.where(first, 0.0, qs), aug1], axis=1)],
                          axis=0).astype(BF16)
    m_ref[...] = jnp.full_like(m_ref, -jnp.inf)
    acc_ref[...] = jnp.zeros_like(acc_ref)
    lane_k = lax.broadcasted_iota(jnp.int32, (bk, LANES), 1)
    keep_first = jnp.where(lane_k < HEAD_DIM, 1.0, 0.0).astype(BF16)
    keep_second = jnp.where(lane_k < HEAD_DIM, 0.0, 1.0).astype(BF16)

    def both_heads(x, r0):
        if r0 == 0:
            return x[...]
        return jnp.concatenate([x[r0:bq], x[bq + r0:2 * bq]], axis=0)

    def logits(j, r0=0):
        start = pl.multiple_of(j * bk, bk)
        kb = jnp.concatenate([k_ref[0, pl.ds(start, bk), :], a_ref[0, pl.ds(start, bk), :]], axis=1)
        return lax.dot_general(both_heads(q01, r0), kb, (((1,), (1,)), ((), ())),
                               preferred_element_type=F32)

    def consume(j, z, r0=0):
        nr = bq - r0
        start = pl.multiple_of(j * bk, bk)
        vb = v_ref[0, pl.ds(start, bk), :]
        m_prev = both_heads(m_ref, r0)
        m_new = jnp.maximum(m_prev, jnp.max(z, axis=1, keepdims=True))
        alpha = jnp.exp2(m_prev - m_new)
        p = jnp.exp2(z - jnp.concatenate([m_new] * (bk // LANES), axis=1)).astype(BF16)
        pv = jnp.concatenate(
            [jnp.dot(p[:nr], vb * keep_first + keep_second, preferred_element_type=F32),
             jnp.dot(p[nr:], vb * keep_second + keep_first, preferred_element_type=F32)], axis=0)
        acc_new = alpha * both_heads(acc_ref, r0) + pv
        if r0 == 0:
            acc_ref[...] = acc_new
            m_ref[...] = m_new
        else:
            for half, dst in ((slice(0, nr), slice(r0, bq)), (slice(nr, 2 * nr), slice(bq + r0, 2 * bq))):
                acc_ref[dst] = acc_new[half]
                m_ref[dst] = m_new[half]

    per_q = bq // bk
    n_full = i * per_q

    j_first = first_ref[pl.program_id(0), pl.program_id(1), i]
    odd = (n_full - j_first) & 1

    @pl.when(odd == 1)
    def _():
        consume(j_first, logits(j_first))

    def body(step, carry):
        j = j_first + odd + 2 * step
        z_a = logits(j)
        z_b = logits(j + 1)
        consume(j, z_a)
        consume(j + 1, z_b)
        return carry

    lax.fori_loop(0, (n_full - j_first) // 2, body, 0)
    for d in range(per_q):
        r0 = d * bk
        row = lax.broadcasted_iota(jnp.int32, (bq - r0, bk), 0)
        col = lax.broadcasted_iota(jnp.int32, (bq - r0, bk), 1)
        keep = col <= row
        z = logits(n_full + d, r0)
        consume(n_full + d, jnp.where(jnp.concatenate([keep, keep], axis=0), z, -jnp.inf), r0)
    acc = acc_ref[...]
    o = acc / pltpu.roll(acc, HEAD_DIM, 1)
    o_ref[0] = jnp.where(first, o[:bq], o[bq:]).astype(o_ref.dtype)


FOX_ZERO_LOG2 = 136.0


def _fox_first_block(fbias, bounds, bq, bk):
    b, s, _ = fbias.shape
    nq, nk = s // bq, s // bk
    n_heads2 = 2 * HEADS
    nrm = jnp.sqrt(bounds[:, :, 0, :n_heads2]) * 1.01
    q_scale = HEAD_DIM ** -0.5 * LOG2E
    qn = nrm[..., :HEADS] * (q_scale * 1.01)
    kn = nrm[..., HEADS:]
    per_q = bq // bk
    qn_i = qn.reshape(b, nq, per_q, HEADS).max(axis=2)
    kn_i = kn.reshape(b, nq, per_q, HEADS).max(axis=2)
    kn_pre = lax.cummax(kn, axis=1)
    diag = bounds[:, :, 1, :HEADS].reshape(b, nq, per_q, HEADS).min(axis=2) * q_scale
    diag_low = diag - 0.02 * qn_i * kn_i
    pairs = HEADS // 2
    def bias_rows(rows):
        pieces = rows.astype(F32).reshape(b, -1, pairs, LANES)[..., :6].reshape(b, -1, pairs, 2, 3)
        return pieces.sum(-1).reshape(b, -1, HEADS)

    nb_end = bias_rows(fbias[:, bk - 1::bk])
    nb_start = bias_rows(fbias[:, ::bq])
    gap = nb_start[:, :, None, :] - nb_end[:, None, :, :]
    need = qn_i[:, :, None, :] * kn_pre[:, None, :, :] - diag_low[:, :, None, :] + FOX_ZERO_LOG2
    skip = (gap > need).reshape(b, nq, nk, pairs, 2).all(axis=-1)
    n_full = jnp.arange(nq) * per_q
    skip = skip & (jnp.arange(nk)[None, None, :, None] < n_full[None, :, None, None])
    first = jnp.argmin(skip, axis=2)
    return first.transpose(0, 2, 1).astype(jnp.int32)


def _fox_attention(qkv, fbias, bounds, bq=FOX_BQ, bk=FOX_BK):
    b, s, _ = qkv.shape
    pairs = HEADS // 2
    cb = WIDTH // LANES
    first = _fox_first_block(fbias, bounds, bq, bk)
    grid_spec = pltpu.PrefetchScalarGridSpec(
        num_scalar_prefetch=1,
        grid=(b, pairs, s // bq),
        in_specs=[pl.BlockSpec((1, bq, LANES), lambda bi, hp, i, fr: (bi, i, hp)),
                  pl.BlockSpec((1, s, LANES), lambda bi, hp, i, fr: (bi, 0, cb + hp)),
                  pl.BlockSpec((1, s, LANES), lambda bi, hp, i, fr: (bi, 0, 2 * cb + hp)),
                  pl.BlockSpec((1, s, LANES), lambda bi, hp, i, fr: (bi, 0, hp))],
        out_specs=pl.BlockSpec((1, bq, LANES), lambda bi, hp, i, fr: (bi, i, hp)),
        scratch_shapes=[pltpu.VMEM((2 * bq, LANES), F32), pltpu.VMEM((2 * bq, LANES), F32)],
    )
    return pl.pallas_call(
        functools.partial(_fox_kernel, bq=bq, bk=bk),
        out_shape=jax.ShapeDtypeStruct((b, s, WIDTH), BF16),
        grid_spec=grid_spec,
        compiler_params=_cparams(("arbitrary", "arbitrary", "arbitrary")),
        name="fox_attention",
    )(first, qkv, qkv, qkv, fbias)


PAIR = 2 * HEAD_DIM
GROUPS = WIDTH // PAIR


def _group(x, g):
    return x[:, g * PAIR:(g + 1) * PAIR]


def _head_sum(x, bd2):
    return jnp.concatenate([_dot(_group(x, g), bd2) for g in range(GROUPS)], axis=1)


def _head_apply(mats, x, lane_first):
    rows = mats.shape[1]
    outs = []
    for g in range(GROUPS):
        res = _dot(mats[2 * g:2 * g + 2].reshape(2 * rows, rows), _group(x, g))
        outs.append(jnp.where(lane_first, res[:rows], res[rows:]))
    return jnp.concatenate(outs, axis=1)


def _rwkv_kernel(p_ref, mu_ref, w0_ref, a0_ref, kk_ref, ka_ref, rk_ref, lng_ref, lnb_ref,
                 wwa_ref, g2_ref, bd_ref, o_ref, st_ref, prev_ref):
    L = CHUNK

    @pl.when(pl.program_id(1) == 0)
    def _():
        st_ref[...] = jnp.zeros_like(st_ref)
        prev_ref[...] = jnp.zeros_like(prev_ref)

    p = p_ref[0]
    T = p.shape[0]
    rowi = lax.broadcasted_iota(jnp.int32, p.shape, 0)
    prev = jnp.where(rowi == 0, prev_ref[...], pltpu.roll(p, 1, 0))
    prev_ref[...] = p[T - 1:T, :]
    ps = p + (prev - p) * mu_ref[...]
    r = ps[:, 0:WIDTH]
    k = ps[:, WIDTH:2 * WIDTH]
    v = ps[:, 2 * WIDTH:3 * WIDTH]
    wa_in = ps[:, 3 * WIDTH:3 * WIDTH + DECAY_LORA + AAA_LORA]
    gd = ps[:, 3 * WIDTH + DECAY_LORA + AAA_LORA:]
    lane_wa = lax.broadcasted_iota(jnp.int32, wa_in.shape, 1)
    wa_act = jnp.where(lane_wa < DECAY_LORA, jnp.tanh(wa_in), wa_in)
    wa = _dot(wa_act, wwa_ref[...])
    log_w = -_softplus(-(w0_ref[...] + wa[:, :WIDTH])) - 0.5
    lw = -jnp.exp(log_w)
    a = jax.nn.sigmoid(a0_ref[...] + wa[:, WIDTH:])
    out_gate = _dot(jax.nn.sigmoid(gd), g2_ref[...])
    bd = bd_ref[...]
    kk0 = k * kk_ref[...]
    kk = kk0 * lax.rsqrt(jnp.maximum(_head_sum(kk0 * kk0, bd), 1e-24))
    k2 = k * (1.0 + (a - 1.0) * ka_ref[...])
    av = -kk
    bv = kk * a

    n_sub = T // L
    rt_i = lax.broadcasted_iota(jnp.int32, (T, T), 0)
    ct_i = lax.broadcasted_iota(jnp.int32, (T, T), 1)
    tri_tile = (rt_i >= ct_i) & (rt_i // L == ct_i // L)
    cl = _split_dot_left(jnp.where(tri_tile, 1.0, 0.0).astype(BF16), lw)
    cl_end = jnp.concatenate([jnp.broadcast_to(cl[(c + 1) * L - 1:(c + 1) * L, :], (L, WIDTH))
                              for c in range(n_sub)], axis=0)
    at_all = av * jnp.exp(cl - lw)
    rt_all = r * jnp.exp(cl)
    einv = jnp.exp(-cl)
    bt_all = bv * einv
    kt_all = k2 * einv
    edec = jnp.exp(cl_end - cl)
    b_end_all = bv * edec
    k_end_all = k2 * edec

    ri = lax.broadcasted_iota(jnp.int32, (L, L), 0)
    ci = lax.broadcasted_iota(jnp.int32, (L, L), 1)
    tri_incl = ri >= ci
    tri_strict = ri > ci
    eye = jnp.where(ri == ci, 1.0, 0.0)
    lane_first = lax.broadcasted_iota(jnp.int32, (L, PAIR), 1) < HEAD_DIM
    qr = lax.broadcasted_iota(jnp.int32, (PAIR, PAIR), 0) < HEAD_DIM
    qc = lax.broadcasted_iota(jnp.int32, (PAIR, PAIR), 1) < HEAD_DIM
    same_head = qr == qc

    def bmm(x, y):
        return lax.dot_general(x.astype(BF16), y.astype(BF16), (((2,), (1,)), ((0,), (0,))),
                               preferred_element_type=F32)

    def chunk_terms(c):
        rows = slice(c * L, (c + 1) * L)
        at, rt, bt, kt, vc = at_all[rows], rt_all[rows], bt_all[rows], kt_all[rows], v[rows]
        sb_heads, sk_heads = [], []
        for g in range(GROUPS):
            at_g, rt_g = _group(at, g), _group(rt, g)
            lhs = jnp.concatenate([jnp.where(lane_first, at_g, 0.0), jnp.where(lane_first, rt_g, 0.0),
                                   jnp.where(lane_first, 0.0, at_g), jnp.where(lane_first, 0.0, rt_g)],
                                  axis=0).astype(BF16)
            sb_g = _dot_nt(lhs, _group(bt, g))
            sk_g = _dot_nt(lhs, _group(kt, g))
            for hh in range(2):
                sb_heads.append(sb_g[hh * 2 * L:(hh + 1) * 2 * L])
                sk_heads.append(sk_g[hh * 2 * L:(hh + 1) * 2 * L])
        sb = jnp.stack(sb_heads)
        sk = jnp.stack(sk_heads)
        n_ab = jnp.where(tri_strict, sb[:, :L, :], 0.0)
        a_ak = jnp.where(tri_strict, sk[:, :L, :], 0.0)
        a_rb = jnp.where(tri_incl, sb[:, L:, :], 0.0)
        a_rk = jnp.where(tri_incl, sk[:, L:, :], 0.0)
        tinv = eye + n_ab
        pw = bmm(n_ab, n_ab)
        span = 2
        while 2 * span < L:
            both = bmm(jnp.concatenate([tinv, pw], axis=1), pw)
            tinv = tinv + both[:, :L, :]
            pw = both[:, L:, :]
            span *= 2
        tinv = tinv + bmm(tinv, pw)
        av_term = _head_apply(a_ak, vc, lane_first)
        pm = _head_apply(tinv, at, lane_first)
        qm = _head_apply(tinv, av_term, lane_first)
        rkv = _head_apply(a_rk, vc, lane_first)
        return pm, qm, rkv, a_rb

    terms = [chunk_terms(c) for c in range(n_sub)]

    y_chunks = []
    for c in range(n_sub):
        rows = slice(c * L, (c + 1) * L)
        pm, qm, rkv, a_rb = terms[c]
        rt, vc, b_end, k_end = rt_all[rows], v[rows], b_end_all[rows], k_end_all[rows]
        gam_last = jnp.exp(cl[(c + 1) * L - 1:(c + 1) * L, :])
        u_parts, ys_parts = [], []
        for g in range(GROUPS):
            pr = _dot_nt(jnp.concatenate([_group(pm, g), _group(rt, g)], axis=0), st_ref[g])
            u_parts.append(pr[:L] + _group(qm, g))
            ys_parts.append(pr[L:])
        u = jnp.concatenate(u_parts, axis=1)
        y_chunks.append(jnp.concatenate(ys_parts, axis=1) + _head_apply(a_rb, u, lane_first) + rkv)
        for g in range(GROUPS):
            upd = _dot_tn(_group(u, g), _group(b_end, g)) + _dot_tn(_group(vc, g), _group(k_end, g))
            st_ref[g] = st_ref[g] * _group(gam_last, g) + jnp.where(same_head, upd, 0.0)
    y = jnp.concatenate(y_chunks, axis=0)

    inv_n = 1.0 / HEAD_DIM
    mean = _head_sum(y, bd) * inv_n
    dlt = y - mean
    var = _head_sum(dlt * dlt, bd) * inv_n
    yn = dlt * lax.rsqrt(var + LN_X_EPS) * lng_ref[...] + lnb_ref[...]
    bonus = _head_sum(r * k2 * rk_ref[...], bd) * v
    o_ref[0] = ((yn + bonus) * out_gate).astype(o_ref.dtype)


def _split_dot_left(w_bf16, x):
    hi, mid, lo = _split3(x)
    return (jnp.dot(w_bf16, hi, preferred_element_type=F32)
            + jnp.dot(w_bf16, mid, preferred_element_type=F32)
            + jnp.dot(w_bf16, lo, preferred_element_type=F32))


def _rwkv_branch(p_rw, mu, w0, w2, a0, a2, g2, k_k, k_a, r_k, ln_g, ln_b):
    b, s, sw = p_rw.shape
    row = lambda t: t.reshape(1, -1).astype(F32)
    wwa = jnp.zeros((DECAY_LORA + AAA_LORA, 2 * WIDTH), F32)
    wwa = wwa.at[:DECAY_LORA, :WIDTH].set(w2).at[DECAY_LORA:, WIDTH:].set(a2).astype(BF16)
    hid = jnp.arange(PAIR) // HEAD_DIM
    bd = (hid[:, None] == hid[None, :]).astype(BF16)
    const = lambda shape: pl.BlockSpec(shape, lambda i, j: (0,) * len(shape))
    return pl.pallas_call(
        _rwkv_kernel,
        out_shape=jax.ShapeDtypeStruct((b, s, WIDTH), BF16),
        grid=(b, s // RWKV_TILE),
        in_specs=[pl.BlockSpec((1, RWKV_TILE, sw), lambda i, j: (i, j, 0)),
                  const((1, sw)), const((1, WIDTH)), const((1, WIDTH)), const((1, WIDTH)),
                  const((1, WIDTH)), const((1, WIDTH)), const((1, WIDTH)), const((1, WIDTH)),
                  const((DECAY_LORA + AAA_LORA, 2 * WIDTH)), const((GATE_LORA, WIDTH)),
                  const((PAIR, PAIR))],
        out_specs=pl.BlockSpec((1, RWKV_TILE, WIDTH), lambda i, j: (i, j, 0)),
        scratch_shapes=[pltpu.VMEM((GROUPS, PAIR, PAIR), F32), pltpu.VMEM((1, sw), F32)],
        compiler_params=_cparams(("arbitrary", "arbitrary")),
        name="rwkv7_scan",
    )(p_rw, row(mu), row(w0), row(a0), row(k_k), row(k_a), row(r_k), row(ln_g), row(ln_b),
      wwa, g2.astype(BF16), bd)


def _out_kernel(of_ref, orw_ref, gate_ref, x_ref, mod_ref, wof_ref, wor_ref, wo_ref, n2g_ref,
                wr_ref, br_ref, x1_ref, h2_ref, route_ref, counts_ref, cnt_ref):
    d = x_ref.shape[-1]
    gate = 0.5 * jnp.tanh(0.5 * gate_ref[...].astype(F32)) + 0.5
    merged = (gate[:, :d] * jnp.dot(of_ref[...], wof_ref[...], preferred_element_type=F32)
              + gate[:, d:] * jnp.dot(orw_ref[...], wor_ref[...], preferred_element_type=F32))
    gate1 = mod_ref[0, 2:3, :]
    shift2 = mod_ref[0, 3:4, :]
    scale2 = mod_ref[0, 4:5, :]
    x1 = x_ref[...] + gate1 * jnp.dot(merged.astype(BF16), wo_ref[...], preferred_element_type=F32)
    x1_ref[...] = x1
    ms = jnp.mean(x1 * x1, axis=-1, keepdims=True)
    h2 = x1 * lax.rsqrt(ms + NORM_EPS) * n2g_ref[...] * (1.0 + scale2) + shift2
    h2_ref[...] = _pack_pairs(h2)

    h2_hi = h2.astype(BF16)
    h2_lo = (h2 - h2_hi.astype(F32)).astype(BF16)
    logits = (jnp.dot(h2_hi, wr_ref[0], preferred_element_type=F32)
              + jnp.dot(h2_lo, wr_ref[0], preferred_element_type=F32)
              + jnp.dot(h2_hi, wr_ref[1], preferred_element_type=F32)) + br_ref[...]
    lane = lax.broadcasted_iota(jnp.int32, logits.shape, 1)
    neg = -jnp.inf
    big = jnp.int32(LANES)
    gl = jnp.where(lane < N_GROUPS, logits, neg)
    gmax = jnp.max(gl, axis=1, keepdims=True)
    gidx = jnp.min(jnp.where(gl == gmax, lane, big), axis=1, keepdims=True)
    g_p = 1.0 / jnp.sum(jnp.exp(gl - gmax), axis=1, keepdims=True)
    e_lane = lane - N_GROUPS
    in_grp = (e_lane >= 0) & (e_lane < N_EXPERTS) & ((e_lane // EXPERTS_PER_GROUP) == gidx)
    sel = jnp.where(in_grp, logits, neg)
    m1 = jnp.max(sel, axis=1, keepdims=True)
    i1 = jnp.min(jnp.where(sel == m1, lane, big), axis=1, keepdims=True)
    sel2 = jnp.where(lane == i1, neg, sel)
    m2 = jnp.max(sel2, axis=1, keepdims=True)
    i2 = jnp.min(jnp.where(sel2 == m2, lane, big), axis=1, keepdims=True)
    e21 = jnp.exp(m2 - m1)
    w_first = g_p / (1.0 + e21)
    w_second = g_p * e21 / (1.0 + e21)
    @pl.when(pl.program_id(0) == 0)
    def _():
        cnt_ref[...] = jnp.zeros_like(cnt_ref)

    tm = logits.shape[0]
    oh1 = lane == i1
    oh2 = lane == i2
    both = jnp.where(oh1 | oh2, 1.0, 0.0)
    before = (lax.broadcasted_iota(jnp.int32, (tm, tm), 0)
              > lax.broadcasted_iota(jnp.int32, (tm, tm), 1))
    seen = jnp.dot(jnp.where(before, 1.0, 0.0).astype(BF16), both.astype(BF16),
                   preferred_element_type=F32) + cnt_ref[...]
    rank1 = jnp.sum(jnp.where(oh1, seen, 0.0), axis=1, keepdims=True)
    rank2 = jnp.sum(jnp.where(oh2, seen, 0.0), axis=1, keepdims=True)
    cnt_ref[...] = cnt_ref[...] + jnp.sum(both, axis=0, keepdims=True)
    counts_ref[...] = jnp.broadcast_to(cnt_ref[...], counts_ref.shape)

    route = jnp.where(lane == 0, (i1 - N_GROUPS).astype(F32),
                      jnp.where(lane == 1, (i2 - N_GROUPS).astype(F32),
                                jnp.where(lane == 2, w_first,
                                          jnp.where(lane == 3, w_second,
                                                    jnp.where(lane == 4, rank1,
                                                              jnp.where(lane == 5, rank2, 0.0))))))
    route_ref[...] = route


def _merge_out_router(o_fox, o_rw, gate, x, mod, wof, wor, wo, n2g, wr, br, tm=512):
    b, s, d = x.shape
    t = b * s
    spb = s // tm
    rowspec = lambda w: pl.BlockSpec((tm, w), lambda i: (i, 0))
    const = lambda shape: pl.BlockSpec(shape, lambda i: (0,) * len(shape))
    return pl.pallas_call(
        _out_kernel,
        out_shape=(jax.ShapeDtypeStruct((t, d), F32), jax.ShapeDtypeStruct((t, d // 2), jnp.uint32),
                   jax.ShapeDtypeStruct((t, LANES), F32), jax.ShapeDtypeStruct((8, LANES), F32)),
        grid=(t // tm,),
        in_specs=[rowspec(WIDTH), rowspec(WIDTH), rowspec(2 * d), rowspec(d),
                  pl.BlockSpec((1, 6, d), lambda i: (i // spb, 0, 0)),
                  const((WIDTH, d)), const((WIDTH, d)), const((d, d)), const((1, d)),
                  const((2, d, LANES)), const((1, LANES))],
        out_specs=(rowspec(d), rowspec(d // 2), rowspec(LANES), const((8, LANES))),
        scratch_shapes=[pltpu.VMEM((1, LANES), F32)],
        compiler_params=_cparams(("arbitrary",)),
        name="merge_out_router",
    )(o_fox, o_rw, gate, x.reshape(t, d), mod, wof, wor, wo, n2g.reshape(1, d), wr, br)


SC_CORES = 2
SC_SUBCORES = 16
SC_CHUNK = 128


def _sc_scatter_rows(src, idx, n_rows):
    t, width = src.shape
    per_worker = t // (SC_CORES * SC_SUBCORES)
    n_chunks = per_worker // SC_CHUNK
    assert n_chunks * SC_CHUNK * SC_CORES * SC_SUBCORES == t and idx.shape[0] == TOP_K * t
    mesh = plsc.VectorSubcoreMesh(core_axis_name="c", subcore_axis_name="s")

    @functools.partial(
        pl.kernel, mesh=mesh,
        out_type=jax.ShapeDtypeStruct((n_rows, width), src.dtype),
        scratch_types=[pltpu.VMEM((SC_CHUNK,), jnp.int32) for _ in range(TOP_K)]
        + [pltpu.VMEM((SC_CHUNK, width), src.dtype), pltpu.SemaphoreType.DMA])
    def scatter(src_hbm, idx_hbm, out_hbm, *scratch):
        idx_v, rows_v, sem = scratch[:TOP_K], scratch[TOP_K], scratch[TOP_K + 1]
        worker = lax.axis_index("s") * SC_CORES + lax.axis_index("c")

        @pl.loop(0, n_chunks)
        def _(ci):
            off = pl.multiple_of(worker * per_worker + ci * SC_CHUNK, SC_CHUNK)
            pltpu.sync_copy(src_hbm.at[pl.ds(off, SC_CHUNK)], rows_v)
            for kk in range(TOP_K):
                pltpu.sync_copy(idx_hbm.at[pl.ds(kk * t + off, SC_CHUNK)], idx_v[kk])
            for kk in range(TOP_K):
                pltpu.async_copy(rows_v, out_hbm.at[idx_v[kk]], sem).wait()

    return scatter(src, idx)


def _expert_kernel(blk_e_ref, nused_ref, valid_ref, xs_ref, w1_ref, w3_ref, w2_ref, o_ref,
                   w1b_ref, w3b_ref, w2b_ref):
    i = pl.program_id(0)
    live = i * MOE_ROWS < nused_ref[0]
    new_expert = jnp.logical_or(i == 0, blk_e_ref[i] != blk_e_ref[jnp.maximum(i - 1, 0)])

    @pl.when(new_expert)
    def _():
        w1b_ref[...] = w1_ref[0].astype(BF16)
        w3b_ref[...] = w3_ref[0].astype(BF16)
        w2b_ref[...] = w2_ref[0].astype(BF16)

    @pl.when(live)
    def _():
        row = lax.broadcasted_iota(jnp.int32, xs_ref.shape, 0)
        xs = jnp.where(row < valid_ref[i], xs_ref[...], jnp.uint32(0))
        xb = _unpack_pairs(xs).astype(BF16)
        h1 = jnp.dot(xb, w1b_ref[...], preferred_element_type=F32)
        h3 = jnp.dot(xb, w3b_ref[...], preferred_element_type=F32)
        hh = (h1 * jax.nn.sigmoid(h1)) * h3
        o_ref[...] = _pack_pairs(jnp.dot(hh.astype(BF16), w2b_ref[...], preferred_element_type=F32))

    @pl.when(jnp.logical_not(live))
    def _():
        o_ref[...] = jnp.zeros_like(o_ref)


def _moe_experts(xs, blk_e, nused, blk_valid, w1, w3, w2):
    rows, dp = xs.shape
    _, d, de = w1.shape
    grid_spec = pltpu.PrefetchScalarGridSpec(
        num_scalar_prefetch=3,
        grid=(rows // MOE_ROWS,),
        in_specs=[pl.BlockSpec((MOE_ROWS, dp), lambda i, be, nu, va: (i, 0)),
                  pl.BlockSpec((1, d, de), lambda i, be, nu, va: (be[i], 0, 0)),
                  pl.BlockSpec((1, d, de), lambda i, be, nu, va: (be[i], 0, 0)),
                  pl.BlockSpec((1, de, d), lambda i, be, nu, va: (be[i], 0, 0))],
        out_specs=pl.BlockSpec((MOE_ROWS, dp), lambda i, be, nu, va: (i, 0)),
        scratch_shapes=[pltpu.VMEM((d, de), BF16), pltpu.VMEM((d, de), BF16), pltpu.VMEM((de, d), BF16)],
    )
    return pl.pallas_call(
        _expert_kernel,
        out_shape=jax.ShapeDtypeStruct((rows, dp), xs.dtype),
        grid_spec=grid_spec,
        compiler_params=_cparams(("arbitrary",)),
        name="moe_experts",
    )(blk_e, nused, blk_valid, xs, w1, w3, w2)


def _sc_gather_rows(table, idx):
    n_idx = idx.shape[0]
    width = table.shape[1]
    per_worker = n_idx // (SC_CORES * SC_SUBCORES)
    n_chunks = per_worker // SC_CHUNK
    assert n_chunks * SC_CHUNK * SC_CORES * SC_SUBCORES == n_idx
    mesh = plsc.VectorSubcoreMesh(core_axis_name="c", subcore_axis_name="s")

    @functools.partial(
        pl.kernel, mesh=mesh,
        out_type=jax.ShapeDtypeStruct((n_idx, width), table.dtype),
        scratch_types=[pltpu.VMEM((SC_CHUNK,), jnp.int32), pltpu.VMEM((SC_CHUNK, width), table.dtype),
                       pltpu.SemaphoreType.DMA])
    def gather(table_hbm, idx_hbm, out_hbm, idx_v, rows_v, sem):
        worker = lax.axis_index("s") * SC_CORES + lax.axis_index("c")

        @pl.loop(0, n_chunks)
        def _(ci):
            off = pl.multiple_of(worker * per_worker + ci * SC_CHUNK, SC_CHUNK)
            pltpu.sync_copy(idx_hbm.at[pl.ds(off, SC_CHUNK)], idx_v)
            pltpu.async_copy(table_hbm.at[idx_v], rows_v, sem).wait()
            pltpu.sync_copy(rows_v, out_hbm.at[pl.ds(off, SC_CHUNK)])

    return gather(table, idx)


def _final_kernel(route_ref, x1_ref, mod_ref, fg_ref, y0_ref, y1_ref, o_ref):
    route = route_ref[...]
    y = route[:, 2:3] * _unpack_pairs(y0_ref[...]) + route[:, 3:4] * _unpack_pairs(y1_ref[...])
    gate2 = mod_ref[0, 5:6, :]
    x2 = x1_ref[...] + gate2 * y
    ms = jnp.mean(x2 * x2, axis=-1, keepdims=True)
    o_ref[...] = x2 * lax.rsqrt(ms + NORM_EPS) * fg_ref[...]


def _moe_combine_final(dest_slots, route, x1, mod, final_g, ys, s, tm=512):
    t, d = x1.shape
    spb = s // tm
    picked = _sc_gather_rows(ys, dest_slots)
    tiles = t // tm
    return pl.pallas_call(
        _final_kernel,
        out_shape=jax.ShapeDtypeStruct((t, d), F32),
        grid=(tiles,),
        in_specs=[pl.BlockSpec((tm, LANES), lambda i: (i, 0)),
                  pl.BlockSpec((tm, d), lambda i: (i, 0)),
                  pl.BlockSpec((1, 6, d), lambda i: (i // spb, 0, 0)),
                  pl.BlockSpec((1, d), lambda i: (0, 0)),
                  pl.BlockSpec((tm, picked.shape[1]), lambda i: (i, 0)),
                  pl.BlockSpec((tm, picked.shape[1]), lambda i: (tiles + i, 0))],
        out_specs=pl.BlockSpec((tm, d), lambda i: (i, 0)),
        compiler_params=_cparams(("arbitrary",)),
        name="moe_combine_final",
    )(route, x1, mod, final_g.reshape(1, d), picked, picked)


def _moe_plan(route, counts):
    t = route.shape[0]
    m = t * TOP_K
    flat_e = route[:, :TOP_K].astype(jnp.int32).T.reshape(m)
    rank = route[:, 4:4 + TOP_K].astype(jnp.int32).T.reshape(m)
    counts = counts[0, N_GROUPS:N_GROUPS + N_EXPERTS].astype(jnp.int32)
    padded = (counts + MOE_ROWS - 1) // MOE_ROWS * MOE_ROWS
    pad_end = jnp.cumsum(padded)
    pad_start = pad_end - padded
    experts = jnp.arange(N_EXPERTS, dtype=jnp.int32)
    start_of = jnp.sum(jnp.where(flat_e[:, None] == experts[None, :], pad_start[None, :], 0), axis=1)
    dest = (start_of + rank).astype(jnp.int32)
    n_blocks = m // MOE_ROWS + N_EXPERTS
    blk_start = jnp.arange(n_blocks, dtype=jnp.int32) * MOE_ROWS
    blk_e = jnp.minimum(jnp.sum(pad_end[None, :] <= blk_start[:, None], axis=1), N_EXPERTS - 1).astype(jnp.int32)
    nused = pad_end[-1:].astype(jnp.int32)
    blk_valid = jnp.clip(counts[blk_e] - (blk_start - pad_start[blk_e]), 0, MOE_ROWS).astype(jnp.int32)
    return dest, blk_e, nused, blk_valid, n_blocks * MOE_ROWS


def kernel(x, c, ada_w, ada_b, norm1_g, w_in, fox_forget_b, shift_mu, rwkv_w0, rwkv_w2, rwkv_a0, rwkv_a2, rwkv_g2, rwkv_k_k, rwkv_k_a, rwkv_r_k, ln_x_g, ln_x_b, w_out_fox, w_out_rwkv, w_o, norm2_g, router_group_w, router_group_b, router_expert_w, router_expert_b, exp_w1, exp_w3, exp_w2, final_g):
    b, s, d = x.shape
    t = b * s
    assert ada_w.shape[0] == 1, "the final norm is fused into the last layer's combine; one layer is laid out"
    for l in range(1):
        mod = _adaln_mod(c, ada_w[l], ada_b[l])

        h = _norm_mod(x, norm1_g[l], mod, shift_idx=0, scale_idx=1)
        h2d = h.reshape(t, d)
        w = w_in[l]
        o_f = 3 * WIDTH
        o_rw = o_f + HEADS
        o_g = o_rw + SHIFT_WIDTH
        perm = jnp.argsort(fox_forget_b[l])
        by_head = lambda m: m.reshape(d, HEADS, HEAD_DIM)[:, perm]
        wq = by_head(w[:, :WIDTH]).reshape(d, WIDTH)
        wk = by_head(w[:, WIDTH:2 * WIDTH]).reshape(d, WIDTH)
        wv = by_head(w[:, 2 * WIDTH:o_f]).reshape(d, WIDTH)
        qkv = _matmul(h2d, jnp.concatenate([wq, wk, wv], axis=1).astype(BF16), BF16, tn=768, name="proj_qkv")
        p_rw = _matmul(h2d, w[:, o_rw:o_g].astype(BF16), F32, tn=896, name="proj_rwkv")
        gate = _matmul(h2d, w[:, o_g:].astype(BF16), BF16, tn=1024, name="proj_gate")
        qkv = qkv.reshape(b, s, 3 * WIDTH)
        fbias, bounds = _forget_bias(h, w[:, o_f:o_rw][:, perm], fox_forget_b[l][perm], qkv)
        o_fox = _fox_attention(qkv, fbias, bounds)
        w_of = w_out_fox[l].reshape(HEADS, HEAD_DIM, d)[perm].reshape(WIDTH, d)
        o_rwkv = _rwkv_branch(p_rw.reshape(b, s, SHIFT_WIDTH), shift_mu[l], rwkv_w0[l], rwkv_w2[l],
                              rwkv_a0[l], rwkv_a2[l], rwkv_g2[l], rwkv_k_k[l], rwkv_k_a[l],
                              rwkv_r_k[l], ln_x_g[l], ln_x_b[l])

        wr = jnp.zeros((d, LANES), F32)
        wr = wr.at[:, :N_GROUPS].set(router_group_w[l]).at[:, N_GROUPS:N_GROUPS + N_EXPERTS].set(router_expert_w[l])
        br = jnp.zeros((1, LANES), F32)
        br = br.at[0, :N_GROUPS].set(router_group_b[l]).at[0, N_GROUPS:N_GROUPS + N_EXPERTS].set(router_expert_b[l])
        wr_hi = wr.astype(BF16)
        wr_lo = (wr - wr_hi.astype(F32)).astype(BF16)
        x1, h2, route, counts = _merge_out_router(
            o_fox.reshape(t, WIDTH), o_rwkv.reshape(t, WIDTH), gate, x, mod,
            w_of.astype(BF16), w_out_rwkv[l].astype(BF16), w_o[l].astype(BF16), norm2_g[l],
            jnp.stack([wr_hi, wr_lo]), br)

        dest, blk_e, nused, blk_valid, rows = _moe_plan(route, counts)
        xs = _sc_scatter_rows(h2, dest, rows)
        ys = _moe_experts(xs, blk_e, nused, blk_valid, exp_w1[l], exp_w3[l], exp_w2[l])
        out = _moe_combine_final(dest, route, x1, mod, final_g, ys, s)
    return out.reshape(b, s, d)
```

```python
import functools

import jax
import jax.numpy as jnp
from jax import lax
from jax.experimental import pallas as pl
from jax.experimental.pallas import tpu as pltpu
from jax.experimental.pallas import tpu_sc as plsc

F32 = jnp.float32
BF16 = jnp.bfloat16
HIGHEST = lax.Precision.HIGHEST

HEADS = 8
HEAD_DIM = 64
WIDTH = HEADS * HEAD_DIM
DECAY_LORA = 64
AAA_LORA = 64
GATE_LORA = 128
SHIFT_WIDTH = 3 * WIDTH + DECAY_LORA + AAA_LORA + GATE_LORA
LN_X_EPS = 64e-5
NORM_EPS = 1e-6
N_GROUPS = 4
EXPERTS_PER_GROUP = 8
N_EXPERTS = N_GROUPS * EXPERTS_PER_GROUP
TOP_K = 2

LANES = 128
FOX_BQ = 1024
FOX_BK = 512
CHUNK = 64
RWKV_TILE = 256
MOE_ROWS = 1024
VMEM_LIMIT = 48 * 1024 * 1024


def _cparams(sem):
    return pltpu.CompilerParams(dimension_semantics=sem, vmem_limit_bytes=VMEM_LIMIT)


def _dot(a, b):
    return jnp.dot(a.astype(BF16), b.astype(BF16), preferred_element_type=F32)


def _dot_nt(a, b):
    return lax.dot_general(a.astype(BF16), b.astype(BF16), (((1,), (1,)), ((), ())),
                           preferred_element_type=F32)


def _dot_tn(a, b):
    return lax.dot_general(a.astype(BF16), b.astype(BF16), (((0,), (0,)), ((), ())),
                           preferred_element_type=F32)


def _pack_pairs(x):
    n = x.shape[1] // 2
    lo = lax.bitcast_convert_type(x[:, :n].astype(BF16).astype(F32), jnp.uint32) >> 16
    hi = lax.bitcast_convert_type(x[:, n:].astype(BF16).astype(F32), jnp.uint32) & jnp.uint32(0xFFFF0000)
    return lo | hi


def _unpack_pairs(u):
    lo = lax.bitcast_convert_type(u << 16, F32)
    hi = lax.bitcast_convert_type(u & jnp.uint32(0xFFFF0000), F32)
    return jnp.concatenate([lo, hi], axis=1)


def _softplus(x):
    return jnp.maximum(x, 0.0) + jnp.log1p(jnp.exp(-jnp.abs(x)))


def _mod_kernel(c_ref, w_ref, b_ref, o_ref):
    c = c_ref[...]
    sc = c * jax.nn.sigmoid(c)
    o_ref[...] = jnp.dot(sc, w_ref[...], precision=HIGHEST, preferred_element_type=F32) + b_ref[...]


def _adaln_mod(c, ada_w, ada_b):
    b, d = c.shape
    n = ada_w.shape[1]
    rows = 8
    cp = jnp.zeros((rows, d), F32).at[:b].set(c)
    tn = 1024
    out = pl.pallas_call(
        _mod_kernel,
        out_shape=jax.ShapeDtypeStruct((rows, n), F32),
        grid=(n // tn,),
        in_specs=[pl.BlockSpec((rows, d), lambda j: (0, 0)),
                  pl.BlockSpec((d, tn), lambda j: (0, j)),
                  pl.BlockSpec((1, tn), lambda j: (0, j))],
        out_specs=pl.BlockSpec((rows, tn), lambda j: (0, j)),
        compiler_params=_cparams(("arbitrary",)),
        name="adaln_mod",
    )(cp, ada_w, ada_b.reshape(1, n))
    return out[:b].reshape(b, 6, d)


def _norm_mod_kernel(x_ref, g_ref, mod_ref, o_ref, *, shift_idx, scale_idx):
    x = x_ref[0]
    ms = jnp.mean(x * x, axis=-1, keepdims=True)
    y = x * lax.rsqrt(ms + NORM_EPS) * g_ref[...]
    scale = mod_ref[0, scale_idx:scale_idx + 1, :]
    shift = mod_ref[0, shift_idx:shift_idx + 1, :]
    o_ref[0] = (y * (1.0 + scale) + shift).astype(o_ref.dtype)


def _norm_mod(x, g, mod, shift_idx, scale_idx, tm=1024):
    b, s, d = x.shape
    return pl.pallas_call(
        functools.partial(_norm_mod_kernel, shift_idx=shift_idx, scale_idx=scale_idx),
        out_shape=jax.ShapeDtypeStruct((b, s, d), BF16),
        grid=(b, s // tm),
        in_specs=[pl.BlockSpec((1, tm, d), lambda i, j: (i, j, 0)),
                  pl.BlockSpec((1, d), lambda i, j: (0, 0)),
                  pl.BlockSpec((1, 6, d), lambda i, j: (i, 0, 0))],
        out_specs=pl.BlockSpec((1, tm, d), lambda i, j: (i, j, 0)),
        compiler_params=_cparams(("arbitrary", "arbitrary")),
        name="norm1_mod",
    )(x, g.reshape(1, d), mod)


def _mm_kernel(a_ref, w_ref, o_ref):
    o_ref[...] = jnp.dot(a_ref[...], w_ref[...], preferred_element_type=F32).astype(o_ref.dtype)


def _matmul(a, w, out_dtype, tm=2048, tn=512, name="proj"):
    t, k = a.shape
    n = w.shape[1]
    tn = min(tn, n)
    return pl.pallas_call(
        _mm_kernel,
        out_shape=jax.ShapeDtypeStruct((t, n), out_dtype),
        grid=(t // tm, n // tn),
        in_specs=[pl.BlockSpec((tm, k), lambda i, j: (i, 0)),
                  pl.BlockSpec((k, tn), lambda i, j: (0, j))],
        out_specs=pl.BlockSpec((tm, tn), lambda i, j: (i, j)),
        compiler_params=_cparams(("arbitrary", "arbitrary")),
        name=name,
    )(a, w)


LOG2E = 1.4426950408889634


def _split3(x):
    hi = x.astype(BF16)
    r1 = x - hi.astype(F32)
    mid = r1.astype(BF16)
    lo = (r1 - mid.astype(F32)).astype(BF16)
    return hi, mid, lo


def _fcum_kernel(h_ref, wf_ref, fb_ref, sel_ref, qk_ref, hsel_ref, o_ref, bounds_ref, carry_ref, *, ts):
    @pl.when(pl.program_id(0) == 0)
    def _():
        carry_ref[...] = jnp.zeros_like(carry_ref)

    ri = lax.broadcasted_iota(jnp.int32, (ts, ts), 0)
    ci = lax.broadcasted_iota(jnp.int32, (ts, ts), 1)
    tri = jnp.where(ri >= ci, 1.0, 0.0).astype(BF16)
    for bi in range(h_ref.shape[0]):
        f = jnp.dot(h_ref[bi], wf_ref[...], preferred_element_type=F32) + fb_ref[...]
        lf = -_softplus(-f)
        cum = carry_ref[bi]
        for piece in _split3(lf):
            cum = cum + jnp.dot(tri, piece, preferred_element_type=F32)
        carry_ref[bi] = cum[ts - 1:ts, :]
        out = jnp.zeros((ts, sel_ref.shape[2]), F32)
        for idx, piece in enumerate(_split3(cum * (-LOG2E))):
            out = out + jnp.dot(piece, sel_ref[idx], preferred_element_type=F32)
        o_ref[bi] = out.astype(o_ref.dtype)

        x = qk_ref[bi].astype(F32)
        ssq = _dot(x * x, hsel_ref[...])
        dots = _dot(x[:, :WIDTH] * x[:, WIDTH:], hsel_ref[0:WIDTH, :])
        sub = lax.broadcasted_iota(jnp.int32, bounds_ref.shape[2:], 0)
        bounds_ref[bi, 0] = jnp.where(sub == 0, jnp.max(ssq, axis=0, keepdims=True),
                                      jnp.min(dots, axis=0, keepdims=True))


def _forget_bias(h, wf, fb, qkv, ts=FOX_BK):
    b, s, d = h.shape
    pairs = HEADS // 2
    hsel = (jnp.arange(2 * WIDTH)[:, None] // HEAD_DIM == jnp.arange(LANES)[None, :]).astype(BF16)
    wf_p = jnp.zeros((d, LANES), F32).at[:, :HEADS].set(wf).astype(BF16)
    fb_p = jnp.zeros((1, LANES), F32).at[0, :HEADS].set(fb)
    hh = jnp.arange(HEADS)
    sel = jnp.zeros((3, LANES, pairs * LANES), F32)
    for piece in range(3):
        sel = sel.at[piece, hh, (hh // 2) * LANES + (hh % 2) * 3 + piece].set(1.0)
    return pl.pallas_call(
        functools.partial(_fcum_kernel, ts=ts),
        out_shape=(jax.ShapeDtypeStruct((b, s, pairs * LANES), BF16),
                   jax.ShapeDtypeStruct((b, s // ts, 8, LANES), F32)),
        grid=(s // ts,),
        in_specs=[pl.BlockSpec((b, ts, d), lambda j: (0, j, 0)),
                  pl.BlockSpec((d, LANES), lambda j: (0, 0)),
                  pl.BlockSpec((1, LANES), lambda j: (0, 0)),
                  pl.BlockSpec((3, LANES, pairs * LANES), lambda j: (0, 0, 0)),
                  pl.BlockSpec((b, ts, 2 * WIDTH), lambda j: (0, j, 0)),
                  pl.BlockSpec((2 * WIDTH, LANES), lambda j: (0, 0))],
        out_specs=(pl.BlockSpec((b, ts, pairs * LANES), lambda j: (0, j, 0)),
                   pl.BlockSpec((b, 1, 8, LANES), lambda j: (0, j, 0, 0))),
        scratch_shapes=[pltpu.VMEM((b, 1, LANES), F32)],
        compiler_params=_cparams(("arbitrary",)),
        name="forget_bias",
    )(h, wf_p, fb_p, sel.astype(BF16), qkv, hsel)


def _fox_kernel(first_ref, q_ref, k_ref, v_ref, a_ref, o_ref, m_ref, acc_ref, *, bq, bk):
    i = pl.program_id(2)
    lane = lax.broadcasted_iota(jnp.int32, (bq, LANES), 1)
    first = lane < HEAD_DIM
    qs = q_ref[0].astype(F32) * (HEAD_DIM ** -0.5 * LOG2E)
    aug0 = jnp.where(lane < 3, 1.0, 0.0)
    aug1 = jnp.where((lane >= 3) & (lane < 6), 1.0, 0.0)
    q01 = jnp.concatenate([jnp.concatenate([jnp.where(first, qs, 0.0), aug0], axis=1),
                           jnp.concatenate([jnp.where(first, 0.0, qs), aug1], axis=1)],
                          axis=0).astype(BF16)
    m_ref[...] = jnp.full_like(m_ref, -jnp.inf)
    acc_ref[...] = jnp.zeros_like(acc_ref)
    lane_k = lax.broadcasted_iota(jnp.int32, (bk, LANES), 1)
    keep_first = jnp.where(lane_k < HEAD_DIM, 1.0, 0.0).astype(BF16)
    keep_second = jnp.where(lane_k < HEAD_DIM, 0.0, 1.0).astype(BF16)

    def both_heads(x, r0):
        if r0 == 0:
            return x[...]
        return jnp.concatenate([x[r0:bq], x[bq + r0:2 * bq]], axis=0)

    def logits(j, r0=0):
        start = pl.multiple_of(j * bk, bk)
        kb = jnp.concatenate([k_ref[0, pl.ds(start, bk), :], a_ref[0, pl.ds(start, bk), :]], axis=1)
        return lax.dot_general(both_heads(q01, r0), kb, (((1,), (1,)), ((), ())),
                               preferred_element_type=F32)

    def consume(j, z, r0=0):
        nr = bq - r0
        start = pl.multiple_of(j * bk, bk)
        vb = v_ref[0, pl.ds(start, bk), :]
        m_prev = both_heads(m_ref, r0)
        m_new = jnp.maximum(m_prev, jnp.max(z, axis=1, keepdims=True))
        alpha = jnp.exp2(m_prev - m_new)
        p = jnp.exp2(z - jnp.concatenate([m_new] * (bk // LANES), axis=1)).astype(BF16)
        pv = jnp.concatenate(
            [jnp.dot(p[:nr], vb * keep_first + keep_second, preferred_element_type=F32),
             jnp.dot(p[nr:], vb * keep_second + keep_first, preferred_element_type=F32)], axis=0)
        acc_new = alpha * both_heads(acc_ref, r0) + pv
        if r0 == 0:
            acc_ref[...] = acc_new
            m_ref[...] = m_new
        else:
            for half, dst in ((slice(0, nr), slice(r0, bq)), (slice(nr, 2 * nr), slice(bq + r0, 2 * bq))):
                acc_ref[dst] = acc_new[half]
                m_ref[dst] = m_new[half]

    per_q = bq // bk
    n_full = i * per_q

    j_first = first_ref[pl.program_id(0), pl.program_id(1), i]
    odd = (n_full - j_first) & 1

    @pl.when(odd == 1)
    def _():
        consume(j_first, logits(j_first))

    def body(step, carry):
        j = j_first + odd + 2 * step
        z_a = logits(j)
        z_b = logits(j + 1)
        consume(j, z_a)
        consume(j + 1, z_b)
        return carry

    lax.fori_loop(0, (n_full - j_first) // 2, body, 0)
    for d in range(per_q):
        r0 = d * bk
        row = lax.broadcasted_iota(jnp.int32, (bq - r0, bk), 0)
        col = lax.broadcasted_iota(jnp.int32, (bq - r0, bk), 1)
        keep = col <= row
        z = logits(n_full + d, r0)
        consume(n_full + d, jnp.where(jnp.concatenate([keep, keep], axis=0), z, -jnp.inf), r0)
    acc = acc_ref[...]
    o = acc / pltpu.roll(acc, HEAD_DIM, 1)
    o_ref[0] = jnp.where(first, o[:bq], o[bq:]).astype(o_ref.dtype)


FOX_ZERO_LOG2 = 136.0


def _fox_first_block(fbias, bounds, bq, bk):
    b, s, _ = fbias.shape
    nq, nk = s // bq, s // bk
    n_heads2 = 2 * HEADS
    nrm = jnp.sqrt(bounds[:, :, 0, :n_heads2]) * 1.01
    q_scale = HEAD_DIM ** -0.5 * LOG2E
    qn = nrm[..., :HEADS] * (q_scale * 1.01)
    kn = nrm[..., HEADS:]
    per_q = bq // bk
    qn_i = qn.reshape(b, nq, per_q, HEADS).max(axis=2)
    kn_i = kn.reshape(b, nq, per_q, HEADS).max(axis=2)
    kn_pre = lax.cummax(kn, axis=1)
    diag = bounds[:, :, 1, :HEADS].reshape(b, nq, per_q, HEADS).min(axis=2) * q_scale
    diag_low = diag - 0.02 * qn_i * kn_i
    pairs = HEADS // 2
    def bias_rows(rows):
        pieces = rows.astype(F32).reshape(b, -1, pairs, LANES)[..., :6].reshape(b, -1, pairs, 2, 3)
        return pieces.sum(-1).reshape(b, -1, HEADS)

    nb_end = bias_rows(fbias[:, bk - 1::bk])
    nb_start = bias_rows(fbias[:, ::bq])
    gap = nb_start[:, :, None, :] - nb_end[:, None, :, :]
    need = qn_i[:, :, None, :] * kn_pre[:, None, :, :] - diag_low[:, :, None, :] + FOX_ZERO_LOG2
    skip = (gap > need).reshape(b, nq, nk, pairs, 2).all(axis=-1)
    n_full = jnp.arange(nq) * per_q
    skip = skip & (jnp.arange(nk)[None, None, :, None] < n_full[None, :, None, None])
    first = jnp.argmin(skip, axis=2)
    return first.transpose(0, 2, 1).astype(jnp.int32)


def _fox_attention(qkv, fbias, bounds, bq=FOX_BQ, bk=FOX_BK):
    b, s, _ = qkv.shape
    pairs = HEADS // 2
    cb = WIDTH // LANES
    first = _fox_first_block(fbias, bounds, bq, bk)
    grid_spec = pltpu.PrefetchScalarGridSpec(
        num_scalar_prefetch=1,
        grid=(b, pairs, s // bq),
        in_specs=[pl.BlockSpec((1, bq, LANES), lambda bi, hp, i, fr: (bi, i, hp)),
                  pl.BlockSpec((1, s, LANES), lambda bi, hp, i, fr: (bi, 0, cb + hp)),
                  pl.BlockSpec((1, s, LANES), lambda bi, hp, i, fr: (bi, 0, 2 * cb + hp)),
                  pl.BlockSpec((1, s, LANES), lambda bi, hp, i, fr: (bi, 0, hp))],
        out_specs=pl.BlockSpec((1, bq, LANES), lambda bi, hp, i, fr: (bi, i, hp)),
        scratch_shapes=[pltpu.VMEM((2 * bq, LANES), F32), pltpu.VMEM((2 * bq, LANES), F32)],
    )
    return pl.pallas_call(
        functools.partial(_fox_kernel, bq=bq, bk=bk),
        out_shape=jax.ShapeDtypeStruct((b, s, WIDTH), BF16),
        grid_spec=grid_spec,
        compiler_params=_cparams(("arbitrary", "arbitrary", "arbitrary")),
        name="fox_attention",
    )(first, qkv, qkv, qkv, fbias)


PAIR = 2 * HEAD_DIM
GROUPS = WIDTH // PAIR


def _group(x, g):
    return x[:, g * PAIR:(g + 1) * PAIR]


def _head_sum(x, bd2):
    return jnp.concatenate([_dot(_group(x, g), bd2) for g in range(GROUPS)], axis=1)


def _head_apply(mats, x, lane_first):
    rows = mats.shape[1]
    outs = []
    for g in range(GROUPS):
        res = _dot(mats[2 * g:2 * g + 2].reshape(2 * rows, rows), _group(x, g))
        outs.append(jnp.where(lane_first, res[:rows], res[rows:]))
    return jnp.concatenate(outs, axis=1)


def _rwkv_kernel(p_ref, mu_ref, w0_ref, a0_ref, kk_ref, ka_ref, rk_ref, lng_ref, lnb_ref,
                 wwa_ref, g2_ref, bd_ref, o_ref, st_ref, prev_ref):
    L = CHUNK

    @pl.when(pl.program_id(1) == 0)
    def _():
        st_ref[...] = jnp.zeros_like(st_ref)
        prev_ref[...] = jnp.zeros_like(prev_ref)

    p = p_ref[0]
    T = p.shape[0]
    rowi = lax.broadcasted_iota(jnp.int32, p.shape, 0)
    prev = jnp.where(rowi == 0, prev_ref[...], pltpu.roll(p, 1, 0))
    prev_ref[...] = p[T - 1:T, :]
    ps = p + (prev - p) * mu_ref[...]
    r = ps[:, 0:WIDTH]
    k = ps[:, WIDTH:2 * WIDTH]
    v = ps[:, 2 * WIDTH:3 * WIDTH]
    wa_in = ps[:, 3 * WIDTH:3 * WIDTH + DECAY_LORA + AAA_LORA]
    gd = ps[:, 3 * WIDTH + DECAY_LORA + AAA_LORA:]
    lane_wa = lax.broadcasted_iota(jnp.int32, wa_in.shape, 1)
    wa_act = jnp.where(lane_wa < DECAY_LORA, jnp.tanh(wa_in), wa_in)
    wa = _dot(wa_act, wwa_ref[...])
    log_w = -_softplus(-(w0_ref[...] + wa[:, :WIDTH])) - 0.5
    lw = -jnp.exp(log_w)
    a = jax.nn.sigmoid(a0_ref[...] + wa[:, WIDTH:])
    out_gate = _dot(jax.nn.sigmoid(gd), g2_ref[...])
    bd = bd_ref[...]
    kk0 = k * kk_ref[...]
    kk = kk0 * lax.rsqrt(jnp.maximum(_head_sum(kk0 * kk0, bd), 1e-24))
    k2 = k * (1.0 + (a - 1.0) * ka_ref[...])
    av = -kk
    bv = kk * a

    n_sub = T // L
    rt_i = lax.broadcasted_iota(jnp.int32, (T, T), 0)
    ct_i = lax.broadcasted_iota(jnp.int32, (T, T), 1)
    tri_tile = (rt_i >= ct_i) & (rt_i // L == ct_i // L)
    cl = _split_dot_left(jnp.where(tri_tile, 1.0, 0.0).astype(BF16), lw)
    cl_end = jnp.concatenate([jnp.broadcast_to(cl[(c + 1) * L - 1:(c + 1) * L, :], (L, WIDTH))
                              for c in range(n_sub)], axis=0)
    at_all = av * jnp.exp(cl - lw)
    rt_all = r * jnp.exp(cl)
    einv = jnp.exp(-cl)
    bt_all = bv * einv
    kt_all = k2 * einv
    edec = jnp.exp(cl_end - cl)
    b_end_all = bv * edec
    k_end_all = k2 * edec

    ri = lax.broadcasted_iota(jnp.int32, (L, L), 0)
    ci = lax.broadcasted_iota(jnp.int32, (L, L), 1)
    tri_incl = ri >= ci
    tri_strict = ri > ci
    eye = jnp.where(ri == ci, 1.0, 0.0)
    lane_first = lax.broadcasted_iota(jnp.int32, (L, PAIR), 1) < HEAD_DIM
    qr = lax.broadcasted_iota(jnp.int32, (PAIR, PAIR), 0) < HEAD_DIM
    qc = lax.broadcasted_iota(jnp.int32, (PAIR, PAIR), 1) < HEAD_DIM
    same_head = qr == qc

    def bmm(x, y):
        return lax.dot_general(x.astype(BF16), y.astype(BF16), (((2,), (1,)), ((0,), (0,))),
                               preferred_element_type=F32)

    def chunk_terms(c):
        rows = slice(c * L, (c + 1) * L)
        at, rt, bt, kt, vc = at_all[rows], rt_all[rows], bt_all[rows], kt_all[rows], v[rows]
        sb_heads, sk_heads = [], []
        for g in range(GROUPS):
            at_g, rt_g = _group(at, g), _group(rt, g)
            lhs = jnp.concatenate([jnp.where(lane_first, at_g, 0.0), jnp.where(lane_first, rt_g, 0.0),
                                   jnp.where(lane_first, 0.0, at_g), jnp.where(lane_first, 0.0, rt_g)],
                                  axis=0).astype(BF16)
            sb_g = _dot_nt(lhs, _group(bt, g))
            sk_g = _dot_nt(lhs, _group(kt, g))
            for hh in range(2):
                sb_heads.append(sb_g[hh * 2 * L:(hh + 1) * 2 * L])
                sk_heads.append(sk_g[hh * 2 * L:(hh + 1) * 2 * L])
        sb = jnp.stack(sb_heads)
        sk = jnp.stack(sk_heads)
        n_ab = jnp.where(tri_strict, sb[:, :L, :], 0.0)
        a_ak = jnp.where(tri_strict, sk[:, :L, :], 0.0)
        a_rb = jnp.where(tri_incl, sb[:, L:, :], 0.0)
        a_rk = jnp.where(tri_incl, sk[:, L:, :], 0.0)
        tinv = eye + n_ab
        pw = bmm(n_ab, n_ab)
        span = 2
        while 2 * span < L:
            both = bmm(jnp.concatenate([tinv, pw], axis=1), pw)
            tinv = tinv + both[:, :L, :]
            pw = both[:, L:, :]
            span *= 2
        tinv = tinv + bmm(tinv, pw)
        av_term = _head_apply(a_ak, vc, lane_first)
        pm = _head_apply(tinv, at, lane_first)
        qm = _head_apply(tinv, av_term, lane_first)
        rkv = _head_apply(a_rk, vc, lane_first)
        return pm, qm, rkv, a_rb

    terms = [chunk_terms(c) for c in range(n_sub)]

    y_chunks = []
    for c in range(n_sub):
        rows = slice(c * L, (c + 1) * L)
        pm, qm, rkv, a_rb = terms[c]
        rt, vc, b_end, k_end = rt_all[rows], v[rows], b_end_all[rows], k_end_all[rows]
        gam_last = jnp.exp(cl[(c + 1) * L - 1:(c + 1) * L, :])
        u_parts, ys_parts = [], []
        for g in range(GROUPS):
            pr = _dot_nt(jnp.concatenate([_group(pm, g), _group(rt, g)], axis=0), st_ref[g])
            u_parts.append(pr[:L] + _group(qm, g))
            ys_parts.append(pr[L:])
        u = jnp.concatenate(u_parts, axis=1)
        y_chunks.append(jnp.concatenate(ys_parts, axis=1) + _head_apply(a_rb, u, lane_first) + rkv)
        for g in range(GROUPS):
            upd = _dot_tn(_group(u, g), _group(b_end, g)) + _dot_tn(_group(vc, g), _group(k_end, g))
            st_ref[g] = st_ref[g] * _group(gam_last, g) + jnp.where(same_head, upd, 0.0)
    y = jnp.concatenate(y_chunks, axis=0)

    inv_n = 1.0 / HEAD_DIM
    mean = _head_sum(y, bd) * inv_n
    dlt = y - mean
    var = _head_sum(dlt * dlt, bd) * inv_n
    yn = dlt * lax.rsqrt(var + LN_X_EPS) * lng_ref[...] + lnb_ref[...]
    bonus = _head_sum(r * k2 * rk_ref[...], bd) * v
    o_ref[0] = ((yn + bonus) * out_gate).astype(o_ref.dtype)


def _split_dot_left(w_bf16, x):
    hi, mid, lo = _split3(x)
    return (jnp.dot(w_bf16, hi, preferred_element_type=F32)
            + jnp.dot(w_bf16, mid, preferred_element_type=F32)
            + jnp.dot(w_bf16, lo, preferred_element_type=F32))


def _rwkv_branch(p_rw, mu, w0, w2, a0, a2, g2, k_k, k_a, r_k, ln_g, ln_b):
    b, s, sw = p_rw.shape
    row = lambda t: t.reshape(1, -1).astype(F32)
    wwa = jnp.zeros((DECAY_LORA + AAA_LORA, 2 * WIDTH), F32)
    wwa = wwa.at[:DECAY_LORA, :WIDTH].set(w2).at[DECAY_LORA:, WIDTH:].set(a2).astype(BF16)
    hid = jnp.arange(PAIR) // HEAD_DIM
    bd = (hid[:, None] == hid[None, :]).astype(BF16)
    const = lambda shape: pl.BlockSpec(shape, lambda i, j: (0,) * len(shape))
    return pl.pallas_call(
        _rwkv_kernel,
        out_shape=jax.ShapeDtypeStruct((b, s, WIDTH), BF16),
        grid=(b, s // RWKV_TILE),
        in_specs=[pl.BlockSpec((1, RWKV_TILE, sw), lambda i, j: (i, j, 0)),
                  const((1, sw)), const((1, WIDTH)), const((1, WIDTH)), const((1, WIDTH)),
                  const((1, WIDTH)), const((1, WIDTH)), const((1, WIDTH)), const((1, WIDTH)),
                  const((DECAY_LORA + AAA_LORA, 2 * WIDTH)), const((GATE_LORA, WIDTH)),
                  const((PAIR, PAIR))],
        out_specs=pl.BlockSpec((1, RWKV_TILE, WIDTH), lambda i, j: (i, j, 0)),
        scratch_shapes=[pltpu.VMEM((GROUPS, PAIR, PAIR), F32), pltpu.VMEM((1, sw), F32)],
        compiler_params=_cparams(("arbitrary", "arbitrary")),
        name="rwkv7_scan",
    )(p_rw, row(mu), row(w0), row(a0), row(k_k), row(k_a), row(r_k), row(ln_g), row(ln_b),
      wwa, g2.astype(BF16), bd)


def _out_kernel(of_ref, orw_ref, gate_ref, x_ref, mod_ref, wof_ref, wor_ref, wo_ref, n2g_ref,
                wr_ref, br_ref, x1_ref, h2_ref, route_ref, counts_ref, cnt_ref):
    d = x_ref.shape[-1]
    gate = 0.5 * jnp.tanh(0.5 * gate_ref[...].astype(F32)) + 0.5
    merged = (gate[:, :d] * jnp.dot(of_ref[...], wof_ref[...], preferred_element_type=F32)
              + gate[:, d:] * jnp.dot(orw_ref[...], wor_ref[...], preferred_element_type=F32))
    gate1 = mod_ref[0, 2:3, :]
    shift2 = mod_ref[0, 3:4, :]
    scale2 = mod_ref[0, 4:5, :]
    x1 = x_ref[...] + gate1 * jnp.dot(merged.astype(BF16), wo_ref[...], preferred_element_type=F32)
    x1_ref[...] = x1
    ms = jnp.mean(x1 * x1, axis=-1, keepdims=True)
    h2 = x1 * lax.rsqrt(ms + NORM_EPS) * n2g_ref[...] * (1.0 + scale2) + shift2
    h2_ref[...] = _pack_pairs(h2)

    h2_hi = h2.astype(BF16)
    h2_lo = (h2 - h2_hi.astype(F32)).astype(BF16)
    logits = (jnp.dot(h2_hi, wr_ref[0], preferred_element_type=F32)
              + jnp.dot(h2_lo, wr_ref[0], preferred_element_type=F32)
              + jnp.dot(h2_hi, wr_ref[1], preferred_element_type=F32)) + br_ref[...]
    lane = lax.broadcasted_iota(jnp.int32, logits.shape, 1)
    neg = -jnp.inf
    big = jnp.int32(LANES)
    gl = jnp.where(lane < N_GROUPS, logits, neg)
    gmax = jnp.max(gl, axis=1, keepdims=True)
    gidx = jnp.min(jnp.where(gl == gmax, lane, big), axis=1, keepdims=True)
    g_p = 1.0 / jnp.sum(jnp.exp(gl - gmax), axis=1, keepdims=True)
    e_lane = lane - N_GROUPS
    in_grp = (e_lane >= 0) & (e_lane < N_EXPERTS) & ((e_lane // EXPERTS_PER_GROUP) == gidx)
    sel = jnp.where(in_grp, logits, neg)
    m1 = jnp.max(sel, axis=1, keepdims=True)
    i1 = jnp.min(jnp.where(sel == m1, lane, big), axis=1, keepdims=True)
    sel2 = jnp.where(lane == i1, neg, sel)
    m2 = jnp.max(sel2, axis=1, keepdims=True)
    i2 = jnp.min(jnp.where(sel2 == m2, lane, big), axis=1, keepdims=True)
    e21 = jnp.exp(m2 - m1)
    w_first = g_p / (1.0 + e21)
    w_second = g_p * e21 / (1.0 + e21)
    @pl.when(pl.program_id(0) == 0)
    def _():
        cnt_ref[...] = jnp.zeros_like(cnt_ref)

    tm = logits.shape[0]
    oh1 = lane == i1
    oh2 = lane == i2
    both = jnp.where(oh1 | oh2, 1.0, 0.0)
    before = (lax.broadcasted_iota(jnp.int32, (tm, tm), 0)
              > lax.broadcasted_iota(jnp.int32, (tm, tm), 1))
    seen = jnp.dot(jnp.where(before, 1.0, 0.0).astype(BF16), both.astype(BF16),
                   preferred_element_type=F32) + cnt_ref[...]
    rank1 = jnp.sum(jnp.where(oh1, seen, 0.0), axis=1, keepdims=True)
    rank2 = jnp.sum(jnp.where(oh2, seen, 0.0), axis=1, keepdims=True)
    cnt_ref[...] = cnt_ref[...] + jnp.sum(both, axis=0, keepdims=True)
    counts_ref[...] = jnp.broadcast_to(cnt_ref[...], counts_ref.shape)

    route = jnp.where(lane == 0, (i1 - N_GROUPS).astype(F32),
                      jnp.where(lane == 1, (i2 - N_GROUPS).astype(F32),
                                jnp.where(lane == 2, w_first,
                                          jnp.where(lane == 3, w_second,
                                                    jnp.where(lane == 4, rank1,
                                                              jnp.where(lane == 5, rank2, 0.0))))))
    route_ref[...] = route


def _merge_out_router(o_fox, o_rw, gate, x, mod, wof, wor, wo, n2g, wr, br, tm=512):
    b, s, d = x.shape
    t = b * s
    spb = s // tm
    rowspec = lambda w: pl.BlockSpec((tm, w), lambda i: (i, 0))
    const = lambda shape: pl.BlockSpec(shape, lambda i: (0,) * len(shape))
    return pl.pallas_call(
        _out_kernel,
        out_shape=(jax.ShapeDtypeStruct((t, d), F32), jax.ShapeDtypeStruct((t, d // 2), jnp.uint32),
                   jax.ShapeDtypeStruct((t, LANES), F32), jax.ShapeDtypeStruct((8, LANES), F32)),
        grid=(t // tm,),
        in_specs=[rowspec(WIDTH), rowspec(WIDTH), rowspec(2 * d), rowspec(d),
                  pl.BlockSpec((1, 6, d), lambda i: (i // spb, 0, 0)),
                  const((WIDTH, d)), const((WIDTH, d)), const((d, d)), const((1, d)),
                  const((2, d, LANES)), const((1, LANES))],
        out_specs=(rowspec(d), rowspec(d // 2), rowspec(LANES), const((8, LANES))),
        scratch_shapes=[pltpu.VMEM((1, LANES), F32)],
        compiler_params=_cparams(("arbitrary",)),
        name="merge_out_router",
    )(o_fox, o_rw, gate, x.reshape(t, d), mod, wof, wor, wo, n2g.reshape(1, d), wr, br)


SC_CORES = 2
SC_SUBCORES = 16
SC_CHUNK = 128


def _sc_scatter_rows(src, idx, n_rows):
    t, width = src.shape
    per_worker = t // (SC_CORES * SC_SUBCORES)
    n_chunks = per_worker // SC_CHUNK
    assert n_chunks * SC_CHUNK * SC_CORES * SC_SUBCORES == t and idx.shape[0] == TOP_K * t
    mesh = plsc.VectorSubcoreMesh(core_axis_name="c", subcore_axis_name="s")

    @functools.partial(
        pl.kernel, mesh=mesh,
        out_type=jax.ShapeDtypeStruct((n_rows, width), src.dtype),
        scratch_types=[pltpu.VMEM((SC_CHUNK,), jnp.int32) for _ in range(TOP_K)]
        + [pltpu.VMEM((SC_CHUNK, width), src.dtype), pltpu.SemaphoreType.DMA])
    def scatter(src_hbm, idx_hbm, out_hbm, *scratch):
        idx_v, rows_v, sem = scratch[:TOP_K], scratch[TOP_K], scratch[TOP_K + 1]
        worker = lax.axis_index("s") * SC_CORES + lax.axis_index("c")

        @pl.loop(0, n_chunks)
        def _(ci):
            off = pl.multiple_of(worker * per_worker + ci * SC_CHUNK, SC_CHUNK)
            pltpu.sync_copy(src_hbm.at[pl.ds(off, SC_CHUNK)], rows_v)
            for kk in range(TOP_K):
                pltpu.sync_copy(idx_hbm.at[pl.ds(kk * t + off, SC_CHUNK)], idx_v[kk])
            for kk in range(TOP_K):
                pltpu.async_copy(rows_v, out_hbm.at[idx_v[kk]], sem).wait()

    return scatter(src, idx)


def _expert_kernel(blk_e_ref, nused_ref, valid_ref, xs_ref, w1_ref, w3_ref, w2_ref, o_ref,
                   w1b_ref, w3b_ref, w2b_ref):
    i = pl.program_id(0)
    live = i * MOE_ROWS < nused_ref[0]
    new_expert = jnp.logical_or(i == 0, blk_e_ref[i] != blk_e_ref[jnp.maximum(i - 1, 0)])

    @pl.when(new_expert)
    def _():
        w1b_ref[...] = w1_ref[0].astype(BF16)
        w3b_ref[...] = w3_ref[0].astype(BF16)
        w2b_ref[...] = w2_ref[0].astype(BF16)

    @pl.when(live)
    def _():
        row = lax.broadcasted_iota(jnp.int32, xs_ref.shape, 0)
        xs = jnp.where(row < valid_ref[i], xs_ref[...], jnp.uint32(0))
        xb = _unpack_pairs(xs).astype(BF16)
        h1 = jnp.dot(xb, w1b_ref[...], preferred_element_type=F32)
        h3 = jnp.dot(xb, w3b_ref[...], preferred_element_type=F32)
        hh = (h1 * jax.nn.sigmoid(h1)) * h3
        o_ref[...] = _pack_pairs(jnp.dot(hh.astype(BF16), w2b_ref[...], preferred_element_type=F32))

    @pl.when(jnp.logical_not(live))
    def _():
        o_ref[...] = jnp.zeros_like(o_ref)


def _moe_experts(xs, blk_e, nused, blk_valid, w1, w3, w2):
    rows, dp = xs.shape
    _, d, de = w1.shape
    grid_spec = pltpu.PrefetchScalarGridSpec(
        num_scalar_prefetch=3,
        grid=(rows // MOE_ROWS,),
        in_specs=[pl.BlockSpec((MOE_ROWS, dp), lambda i, be, nu, va: (i, 0)),
                  pl.BlockSpec((1, d, de), lambda i, be, nu, va: (be[i], 0, 0)),
                  pl.BlockSpec((1, d, de), lambda i, be, nu, va: (be[i], 0, 0)),
                  pl.BlockSpec((1, de, d), lambda i, be, nu, va: (be[i], 0, 0))],
        out_specs=pl.BlockSpec((MOE_ROWS, dp), lambda i, be, nu, va: (i, 0)),
        scratch_shapes=[pltpu.VMEM((d, de), BF16), pltpu.VMEM((d, de), BF16), pltpu.VMEM((de, d), BF16)],
    )
    return pl.pallas_call(
        _expert_kernel,
        out_shape=jax.ShapeDtypeStruct((rows, dp), xs.dtype),
        grid_spec=grid_spec,
        compiler_params=_cparams(("arbitrary",)),
        name="moe_experts",
    )(blk_e, nused, blk_valid, xs, w1, w3, w2)


def _sc_gather_rows(table, idx):
    n_idx = idx.shape[0]
    width = table.shape[1]
    per_worker = n_idx // (SC_CORES * SC_SUBCORES)
    n_chunks = per_worker // SC_CHUNK
    assert n_chunks * SC_CHUNK * SC_CORES * SC_SUBCORES == n_idx
    mesh = plsc.VectorSubcoreMesh(core_axis_name="c", subcore_axis_name="s")

    @functools.partial(
        pl.kernel, mesh=mesh,
        out_type=jax.ShapeDtypeStruct((n_idx, width), table.dtype),
        scratch_types=[pltpu.VMEM((SC_CHUNK,), jnp.int32), pltpu.VMEM((SC_CHUNK, width), table.dtype),
                       pltpu.SemaphoreType.DMA])
    def gather(table_hbm, idx_hbm, out_hbm, idx_v, rows_v, sem):
        worker = lax.axis_index("s") * SC_CORES + lax.axis_index("c")

        @pl.loop(0, n_chunks)
        def _(ci):
            off = pl.multiple_of(worker * per_worker + ci * SC_CHUNK, SC_CHUNK)
            pltpu.sync_copy(idx_hbm.at[pl.ds(off, SC_CHUNK)], idx_v)
            pltpu.async_copy(table_hbm.at[idx_v], rows_v, sem).wait()
            pltpu.sync_copy(rows_v, out_hbm.at[pl.ds(off, SC_CHUNK)])

    return gather(table, idx)


def _final_kernel(route_ref, x1_ref, mod_ref, fg_ref, y0_ref, y1_ref, o_ref):
    route = route_ref[...]
    y = route[:, 2:3] * _unpack_pairs(y0_ref[...]) + route[:, 3:4] * _unpack_pairs(y1_ref[...])
    gate2 = mod_ref[0, 5:6, :]
    x2 = x1_ref[...] + gate2 * y
    ms = jnp.mean(x2 * x2, axis=-1, keepdims=True)
    o_ref[...] = x2 * lax.rsqrt(ms + NORM_EPS) * fg_ref[...]


def _moe_combine_final(dest_slots, route, x1, mod, final_g, ys, s, tm=512):
    t, d = x1.shape
    spb = s // tm
    picked = _sc_gather_rows(ys, dest_slots)
    tiles = t // tm
    return pl.pallas_call(
        _final_kernel,
        out_shape=jax.ShapeDtypeStruct((t, d), F32),
        grid=(tiles,),
        in_specs=[pl.BlockSpec((tm, LANES), lambda i: (i, 0)),
                  pl.BlockSpec((tm, d), lambda i: (i, 0)),
                  pl.BlockSpec((1, 6, d), lambda i: (i // spb, 0, 0)),
                  pl.BlockSpec((1, d), lambda i: (0, 0)),
                  pl.BlockSpec((tm, picked.shape[1]), lambda i: (i, 0)),
                  pl.BlockSpec((tm, picked.shape[1]), lambda i: (tiles + i, 0))],
        out_specs=pl.BlockSpec((tm, d), lambda i: (i, 0)),
        compiler_params=_cparams(("arbitrary",)),
        name="moe_combine_final",
    )(route, x1, mod, final_g.reshape(1, d), picked, picked)


def _moe_plan(route, counts):
    t = route.shape[0]
    m = t * TOP_K
    flat_e = route[:, :TOP_K].astype(jnp.int32).T.reshape(m)
    rank = route[:, 4:4 + TOP_K].astype(jnp.int32).T.reshape(m)
    counts = counts[0, N_GROUPS:N_GROUPS + N_EXPERTS].astype(jnp.int32)
    padded = (counts + MOE_ROWS - 1) // MOE_ROWS * MOE_ROWS
    pad_end = jnp.cumsum(padded)
    pad_start = pad_end - padded
    experts = jnp.arange(N_EXPERTS, dtype=jnp.int32)
    start_of = jnp.sum(jnp.where(flat_e[:, None] == experts[None, :], pad_start[None, :], 0), axis=1)
    dest = (start_of + rank).astype(jnp.int32)
    n_blocks = m // MOE_ROWS + N_EXPERTS
    blk_start = jnp.arange(n_blocks, dtype=jnp.int32) * MOE_ROWS
    blk_e = jnp.minimum(jnp.sum(pad_end[None, :] <= blk_start[:, None], axis=1), N_EXPERTS - 1).astype(jnp.int32)
    nused = pad_end[-1:].astype(jnp.int32)
    blk_valid = jnp.clip(counts[blk_e] - (blk_start - pad_start[blk_e]), 0, MOE_ROWS).astype(jnp.int32)
    return dest, blk_e, nused, blk_valid, n_blocks * MOE_ROWS


def kernel(x, c, ada_w, ada_b, norm1_g, w_in, fox_forget_b, shift_mu, rwkv_w0, rwkv_w2, rwkv_a0, rwkv_a2, rwkv_g2, rwkv_k_k, rwkv_k_a, rwkv_r_k, ln_x_g, ln_x_b, w_out_fox, w_out_rwkv, w_o, norm2_g, router_group_w, router_group_b, router_expert_w, router_expert_b, exp_w1, exp_w3, exp_w2, final_g):
    b, s, d = x.shape
    t = b * s
    assert ada_w.shape[0] == 1, "the final norm is fused into the last layer's combine; one layer is laid out"
    for l in range(1):
        mod = _adaln_mod(c, ada_w[l], ada_b[l])

        h = _norm_mod(x, norm1_g[l], mod, shift_idx=0, scale_idx=1)
        h2d = h.reshape(t, d)
        w = w_in[l]
        o_f = 3 * WIDTH
        o_rw = o_f + HEADS
        o_g = o_rw + SHIFT_WIDTH
        perm = jnp.argsort(fox_forget_b[l])
        by_head = lambda m: m.reshape(d, HEADS, HEAD_DIM)[:, perm]
        wq = by_head(w[:, :WIDTH]).reshape(d, WIDTH)
        wk = by_head(w[:, WIDTH:2 * WIDTH]).reshape(d, WIDTH)
        wv = by_head(w[:, 2 * WIDTH:o_f]).reshape(d, WIDTH)
        qkv = _matmul(h2d, jnp.concatenate([wq, wk, wv], axis=1).astype(BF16), BF16, tn=768, name="proj_qkv")
        p_rw = _matmul(h2d, w[:, o_rw:o_g].astype(BF16), F32, tn=SHIFT_WIDTH, name="proj_rwkv")
        gate = _matmul(h2d, w[:, o_g:].astype(BF16), BF16, tn=1024, name="proj_gate")
        qkv = qkv.reshape(b, s, 3 * WIDTH)
        fbias, bounds = _forget_bias(h, w[:, o_f:o_rw][:, perm], fox_forget_b[l][perm], qkv)
        o_fox = _fox_attention(qkv, fbias, bounds)
        w_of = w_out_fox[l].reshape(HEADS, HEAD_DIM, d)[perm].reshape(WIDTH, d)
        o_rwkv = _rwkv_branch(p_rw.reshape(b, s, SHIFT_WIDTH), shift_mu[l], rwkv_w0[l], rwkv_w2[l],
                              rwkv_a0[l], rwkv_a2[l], rwkv_g2[l], rwkv_k_k[l], rwkv_k_a[l],
                              rwkv_r_k[l], ln_x_g[l], ln_x_b[l])

        wr = jnp.zeros((d, LANES), F32)
        wr = wr.at[:, :N_GROUPS].set(router_group_w[l]).at[:, N_GROUPS:N_GROUPS + N_EXPERTS].set(router_expert_w[l])
        br = jnp.zeros((1, LANES), F32)
        br = br.at[0, :N_GROUPS].set(router_group_b[l]).at[0, N_GROUPS:N_GROUPS + N_EXPERTS].set(router_expert_b[l])
        wr_hi = wr.astype(BF16)
        wr_lo = (wr - wr_hi.astype(F32)).astype(BF16)
        x1, h2, route, counts = _merge_out_router(
            o_fox.reshape(t, WIDTH), o_rwkv.reshape(t, WIDTH), gate, x, mod,
            w_of.astype(BF16), w_out_rwkv[l].astype(BF16), w_o[l].astype(BF16), norm2_g[l],
            jnp.stack([wr_hi, wr_lo]), br)

        dest, blk_e, nused, blk_valid, rows = _moe_plan(route, counts)
        xs = _sc_scatter_rows(h2, dest, rows)
        ys = _moe_experts(xs, blk_e, nused, blk_valid, exp_w1[l], exp_w3[l], exp_w2[l])
        out = _moe_combine_final(dest, route, x1, mod, final_g, ys, s)
    return out.reshape(b, s, d)
```
